```python
import jax, jax.numpy as jnp
from jax import lax
import numpy as np

D_MODEL = 1024
BATCH = 2
SEQ = 8192
DEPTH = 1
DEC_BATCH = 128
DEC_SEQ = 4
PAST_LEN = 8192
PAGE_SIZE = 128

N_HEADS = 8
HEAD_DIM = 64
N_KV_HEADS = 2
GROUP = N_HEADS // N_KV_HEADS
NSA_WIDTH = N_HEADS * HEAD_DIM
KV_WIDTH = N_KV_HEADS * HEAD_DIM
CMP_LEN = 32
CMP_STRIDE = 16
CMP_HIDDEN = 128
SLC_BLK = 64
TOP_N = 16
WINDOW = 512
Q_BLOCK = 128
ROPE_THETA = 10000.0
ATTN_SCALE = HEAD_DIM ** -0.5
FORCE_BONUS = 1e4
NEG_INF = -1e30
GMLP_WIDTH = 512
GMLP_GROUPS = 8
GMLP_GROUP_DIM = GMLP_WIDTH // GMLP_GROUPS
GMLP_CHUNK = 128
N_MEM = 256
CROSS_HEADS = 4
CROSS_HEAD_DIM = D_MODEL // CROSS_HEADS
CROSS_SCALE = CROSS_HEAD_DIM ** -0.5
D_FF = -(-8 * D_MODEL // (3 * 256)) * 256
NORM_EPS = 1e-6
IN_SIZES = (NSA_WIDTH, KV_WIDTH, KV_WIDTH, KV_WIDTH, KV_WIDTH, KV_WIDTH, KV_WIDTH, N_HEADS * 3,
            GMLP_WIDTH, GMLP_WIDTH, D_MODEL, D_MODEL)
IN_WIDTH = sum(IN_SIZES)
IN_SPLITS = tuple(sum(IN_SIZES[:i + 1]) for i in range(len(IN_SIZES) - 1))

kernel_name = 'nsa_gmlp_hybrid_step'


def rms_norm(x, g):
    xf = x.astype(jnp.float32)
    y = xf * lax.rsqrt(jnp.mean(xf * xf, axis=-1, keepdims=True) + NORM_EPS)
    return (y * g.astype(jnp.float32)).astype(x.dtype)


def layer_norm(x, g, b):
    xf = x.astype(jnp.float32)
    xc = xf - jnp.mean(xf, axis=-1, keepdims=True)
    var = jnp.mean(xc * xc, axis=-1, keepdims=True)
    return (xc * lax.rsqrt(var + NORM_EPS) * g.astype(jnp.float32) + b.astype(jnp.float32)).astype(x.dtype)


def rope(x, pos):
    half = HEAD_DIM // 2
    inv_freq = ROPE_THETA ** (-jnp.arange(half, dtype=jnp.float32) / half)
    ang = pos.astype(jnp.float32)[:, None] * inv_freq[None, :]
    shape = (pos.shape[0],) + (1,) * (x.ndim - 3) + (half,)
    cos, sin = jnp.cos(ang).reshape(shape), jnp.sin(ang).reshape(shape)
    xf = x.astype(jnp.float32)
    x1, x2 = xf[..., :half], xf[..., half:]
    return jnp.concatenate([x1 * cos - x2 * sin, x2 * cos + x1 * sin], axis=-1).astype(x.dtype)


def mixer_inputs(h, w_in, pos):
    B, S, _ = h.shape
    q, kc, vc, ks, vs, kw, vw, gn, u, v, ga, gb = jnp.split(h @ w_in, IN_SPLITS, axis=-1)
    heads = lambda t: t.reshape(B, S, N_KV_HEADS, HEAD_DIM)
    q = rope(q.reshape(B, S, N_KV_HEADS, GROUP, HEAD_DIM), pos)
    gn = jax.nn.sigmoid(gn.reshape(B, S, N_KV_HEADS, GROUP, 3))
    return (q, rope(heads(kc), pos), heads(vc), rope(heads(ks), pos), heads(vs),
            rope(heads(kw), pos), heads(vw), gn, jax.nn.gelu(u), jax.nn.gelu(v), ga, gb)


def masked_softmax(s, mask):
    s = jnp.where(mask, s, NEG_INF)
    e = jnp.where(mask, jnp.exp(s - jnp.max(s, axis=-1, keepdims=True)), 0.0)
    return e / jnp.maximum(jnp.sum(e, axis=-1, keepdims=True), 1e-30)


def attend_shared(q, k, v, mask):
    s = jnp.einsum('bqhgd,bkhd->bhgqk', q, k).astype(jnp.float32) * ATTN_SCALE
    p = masked_softmax(s, mask)
    return jnp.einsum('bhgqk,bkhd->bqhgd', p.astype(v.dtype), v), p


def attend_gathered(q, k, v, mask):
    s = jnp.einsum('bqhgd,bqhkd->bqhgk', q, k).astype(jnp.float32) * ATTN_SCALE
    p = masked_softmax(s, mask[:, :, :, None, :])
    return jnp.einsum('bqhgk,bqhkd->bqhgd', p.astype(v.dtype), v)


def cmp_chunk_proj(rows, w1):
    B, L = rows.shape[:2]
    c = rows.reshape(B, L // CMP_STRIDE, CMP_STRIDE, N_KV_HEADS, HEAD_DIM)
    lead = jnp.einsum('bcshd,sdf->bchf', c, w1[:CMP_STRIDE])
    trail = jnp.einsum('bcshd,sdf->bchf', c, w1[CMP_STRIDE:])
    return lead, trail


def cmp_finish(lead, trail, pe, w1, w2):
    pre = lead[:, :-1] + trail[:, 1:] + jnp.einsum('sd,sdf->f', pe, w1)
    return jnp.einsum('bnhf,fd->bnhd', jax.nn.gelu(pre), w2)


def cmp_to_slc_weights(n_cmp, n_slc):
    cs = jnp.arange(n_cmp)[:, None] * CMP_STRIDE
    ss = jnp.arange(n_slc)[None, :] * SLC_BLK
    shared = jnp.minimum(cs + CMP_LEN, ss + SLC_BLK) - jnp.maximum(cs, ss)
    return jnp.maximum(shared, 0).astype(jnp.float32) / CMP_LEN


def nsa_attend(q, qpos, gn, ck, cv, cend, n_slc, gather_sel, kw, vw, kwpos):
    B, Q = q.shape[:2]
    o_cmp, p_cmp = attend_shared(q, ck, cv, cend[None, :] <= qpos[:, None])
    score = jnp.einsum('bhqn,nm->bqhm', p_cmp.sum(axis=2), cmp_to_slc_weights(ck.shape[1], n_slc))
    blk = jnp.arange(n_slc)
    qblk = (qpos // SLC_BLK)[:, None]
    forced = (blk == 0) | (blk == qblk) | (blk == qblk - 1)
    score = jnp.where((blk <= qblk)[:, None], score + FORCE_BONUS * forced[:, None], NEG_INF)
    _, idx = lax.top_k(score, min(TOP_N, n_slc))
    k_sel, v_sel = gather_sel(idx)
    kpos = (idx[..., None] * SLC_BLK + jnp.arange(SLC_BLK)).reshape(B, Q, N_KV_HEADS, -1)
    o_sel = attend_gathered(q, k_sel, v_sel, kpos <= qpos[None, :, None, None])
    wmask = ((kwpos[None, :] >= 0) & (kwpos[None, :] <= qpos[:, None])
             & (kwpos[None, :] > qpos[:, None] - WINDOW))
    o_win, _ = attend_shared(q, kw, vw, wmask)
    o = gn[..., 0:1] * o_cmp + gn[..., 1:2] * o_sel + gn[..., 2:3] * o_win
    return o.reshape(B, Q, NSA_WIDTH)


def nsa_prompt(q, gn, kc, vc, ks, vs, kw, vw, pe, w1, w2):
    B, S = q.shape[:2]
    ck = cmp_finish(*cmp_chunk_proj(kc, w1[0]), pe[0], w1[0], w2[0])
    cv = cmp_finish(*cmp_chunk_proj(vc, w1[1]), pe[1], w1[1], w2[1])
    cend = jnp.arange(ck.shape[1], dtype=jnp.int32) * CMP_STRIDE + (CMP_LEN - 1)
    n_slc = S // SLC_BLK
    kb = ks.reshape(B, n_slc, SLC_BLK, N_KV_HEADS, HEAD_DIM)
    vb = vs.reshape(B, n_slc, SLC_BLK, N_KV_HEADS, HEAD_DIM)
    bi = jnp.arange(B)[:, None, None, None, None]
    hi = jnp.arange(N_KV_HEADS)[None, None, :, None, None]
    r = jnp.arange(SLC_BLK)

    def gather_sel(idx):
        i5 = idx[..., None]
        shape = idx.shape[:3] + (-1, HEAD_DIM)
        return kb[bi, i5, r, hi].reshape(shape), vb[bi, i5, r, hi].reshape(shape)

    kw_pad = jnp.pad(kw, ((0, 0), (WINDOW, 0), (0, 0), (0, 0)))
    vw_pad = jnp.pad(vw, ((0, 0), (WINDOW, 0), (0, 0), (0, 0)))

    def query_block(q0):
        qpos = q0 + jnp.arange(Q_BLOCK, dtype=jnp.int32)
        take = lambda t, n: lax.dynamic_slice_in_dim(t, q0, n, axis=1)
        kwpos = q0 - WINDOW + jnp.arange(WINDOW + Q_BLOCK, dtype=jnp.int32)
        return nsa_attend(take(q, Q_BLOCK), qpos, take(gn, Q_BLOCK), ck, cv, cend, n_slc, gather_sel,
                          take(kw_pad, WINDOW + Q_BLOCK), take(vw_pad, WINDOW + Q_BLOCK), kwpos)

    o = lax.map(query_block, jnp.arange(0, S, Q_BLOCK, dtype=jnp.int32))
    return jnp.swapaxes(o, 0, 1).reshape(B, S, NSA_WIDTH)


def nsa_sample(l, q, gn, kc, vc, ks, vs, kw, vw, pool_ck, pool_cv, pool_sk, pool_sv,
               page_table, win_k, win_v, pe, w1, w2):
    DB, DS = q.shape[:2]
    past = page_table.shape[1] * PAGE_SIZE
    qpos = past + jnp.arange(DS, dtype=jnp.int32)
    n_new_rows = (DS // CMP_STRIDE) * CMP_STRIDE

    def compressed(pool, new_rows, j):
        rows = pool[l, page_table].reshape(DB, past, N_KV_HEADS, HEAD_DIM)
        lead, trail = cmp_chunk_proj(rows, w1[j])
        if n_new_rows:
            lead_n, trail_n = cmp_chunk_proj(new_rows[:, :n_new_rows], w1[j])
            lead = jnp.concatenate([lead, lead_n], axis=1)
            trail = jnp.concatenate([trail, trail_n], axis=1)
        return cmp_finish(lead, trail, pe[j], w1[j], w2[j])

    ck, cv = compressed(pool_ck, kc, 0), compressed(pool_cv, vc, 1)
    cend = jnp.arange(ck.shape[1], dtype=jnp.int32) * CMP_STRIDE + (CMP_LEN - 1)

    n_past_blk = past // SLC_BLK
    n_slc = -(-(past + DS) // SLC_BLK)
    n_new_blk = n_slc - n_past_blk
    pad = ((0, 0), (0, n_new_blk * SLC_BLK - DS), (0, 0), (0, 0))
    kb_new = jnp.pad(ks, pad).reshape(DB, n_new_blk, SLC_BLK, N_KV_HEADS, HEAD_DIM)
    vb_new = jnp.pad(vs, pad).reshape(DB, n_new_blk, SLC_BLK, N_KV_HEADS, HEAD_DIM)
    bi = jnp.arange(DB)[:, None, None, None, None]
    hi = jnp.arange(N_KV_HEADS)[None, None, :, None, None]
    r = jnp.arange(SLC_BLK)

    def gather_sel(idx):
        i5 = idx[..., None]
        row = jnp.minimum(i5, n_past_blk - 1) * SLC_BLK + r
        phys, off = page_table[bi, row // PAGE_SIZE], row % PAGE_SIZE
        nb = jnp.clip(i5 - n_past_blk, 0, n_new_blk - 1)
        in_past = (i5 < n_past_blk)[..., None]
        shape = idx.shape[:3] + (-1, HEAD_DIM)
        pick = lambda pool, new_blocks: jnp.where(
            in_past, pool[l, phys, off, hi], new_blocks[bi, nb, r, hi]).reshape(shape)
        return pick(pool_sk, kb_new), pick(pool_sv, vb_new)

    kwin = jnp.concatenate([win_k, kw], axis=1)
    vwin = jnp.concatenate([win_v, vw], axis=1)
    wb = win_k.shape[1]
    kwpos = past - wb + jnp.arange(wb + DS, dtype=jnp.int32)
    o = nsa_attend(q, qpos, gn, ck, cv, cend, n_slc, gather_sel, kwin, vwin, kwpos)
    return o, kwin[:, DS:], vwin[:, DS:]


def gmlp_spatial(u, v, ln_g, ln_b, ws, bs):
    B, S, _ = u.shape
    L = min(S, GMLP_CHUNK)
    vn = layer_norm(v, ln_g, ln_b).reshape(B, S // L, L, GMLP_GROUPS, GMLP_GROUP_DIM)
    w = jnp.tril(ws[:, :L, :L])
    s = jnp.einsum('gij,bcjgd->bcigd', w, vn) + bs[:, :L].T[None, None, :, :, None]
    return u * s.reshape(B, S, GMLP_WIDTH)


def merge_branches(o_a, o_b, ga, gb, w_a, w_b, w_out):
    m = jax.nn.sigmoid(ga) * (o_a @ w_a) + jax.nn.sigmoid(gb) * (o_b @ w_b)
    return m @ w_out


def memory_kv(mem, g_mem, w_ck, w_cv):
    B, M, _ = mem.shape
    hm = rms_norm(mem, g_mem)
    return ((hm @ w_ck).reshape(B, M, CROSS_HEADS, CROSS_HEAD_DIM),
            (hm @ w_cv).reshape(B, M, CROSS_HEADS, CROSS_HEAD_DIM))


def cross_attend(h, mk, mv, w_cq, w_co):
    B, S, _ = h.shape
    q = (h @ w_cq).reshape(B, S, CROSS_HEADS, CROSS_HEAD_DIM)
    s = jnp.einsum('bqhd,bmhd->bhqm', q, mk).astype(jnp.float32) * CROSS_SCALE
    p = jax.nn.softmax(s, axis=-1).astype(mv.dtype)
    return jnp.einsum('bhqm,bmhd->bqhd', p, mv).reshape(B, S, D_MODEL) @ w_co


def swiglu(h, w_gate, w_up, w_down):
    return (jax.nn.silu(h @ w_gate) * (h @ w_up)) @ w_down


def setup_inputs(seed: int = 0) -> dict:
    key = jax.random.key(seed)
    keys = iter(jax.random.split(key, 48))
    nrm = lambda shape, scale=1.0: scale * jax.random.normal(next(keys), shape, jnp.float32)
    gain = lambda shape: 1.0 + 0.02 * jax.random.normal(next(keys), shape, jnp.float32)
    n_pages = PAST_LEN // PAGE_SIZE
    n_used = DEC_BATCH * n_pages
    n_pool = n_used + n_used // 4
    wbuf = min(WINDOW, PAST_LEN)
    pool_shape = (DEPTH, n_pool, PAGE_SIZE, N_KV_HEADS, HEAD_DIM)
    win_shape = (DEPTH, DEC_BATCH, wbuf, N_KV_HEADS, HEAD_DIM)
    mem_shape = (DEPTH, DEC_BATCH, N_MEM, CROSS_HEADS, CROSS_HEAD_DIM)
    page_table = jax.random.permutation(next(keys), n_pool)[:n_used].reshape(DEC_BATCH, n_pages).astype(jnp.int32)
    dd = D_MODEL ** -0.5
    return {
        'x_prompt': nrm((BATCH, SEQ, D_MODEL)),
        'x_sample': nrm((DEC_BATCH, DEC_SEQ, D_MODEL)),
        'mem_prompt': nrm((BATCH, N_MEM, D_MODEL)),
        'cache_cmp_k': nrm(pool_shape),
        'cache_cmp_v': nrm(pool_shape),
        'cache_slc_k': nrm(pool_shape),
        'cache_slc_v': nrm(pool_shape),
        'page_table': page_table,
        'state_win_k': nrm(win_shape),
        'state_win_v': nrm(win_shape),
        'cache_mem_k': nrm(mem_shape),
        'cache_mem_v': nrm(mem_shape),
        'g_mix': gain((DEPTH, D_MODEL)),
        'w_in': nrm((DEPTH, D_MODEL, IN_WIDTH), dd),
        'cmp_pe': nrm((DEPTH, 2, CMP_LEN, HEAD_DIM), 0.1),
        'cmp_w1': nrm((DEPTH, 2, CMP_LEN, HEAD_DIM, CMP_HIDDEN), (CMP_LEN * HEAD_DIM) ** -0.5),
        'cmp_w2': nrm((DEPTH, 2, CMP_HIDDEN, HEAD_DIM), CMP_HIDDEN ** -0.5),
        'gmlp_ln_g': gain((DEPTH, GMLP_WIDTH)),
        'gmlp_ln_b': nrm((DEPTH, GMLP_WIDTH), 0.02),
        'gmlp_ws': nrm((DEPTH, GMLP_GROUPS, GMLP_CHUNK, GMLP_CHUNK), GMLP_CHUNK ** -0.5),
        'gmlp_bs': gain((DEPTH, GMLP_GROUPS, GMLP_CHUNK)),
        'w_branch_a': nrm((DEPTH, NSA_WIDTH, D_MODEL), NSA_WIDTH ** -0.5),
        'w_branch_b': nrm((DEPTH, GMLP_WIDTH, D_MODEL), GMLP_WIDTH ** -0.5),
        'w_out': nrm((DEPTH, D_MODEL, D_MODEL), dd),
        'g_cross': gain((DEPTH, D_MODEL)),
        'g_mem': gain((DEPTH, D_MODEL)),
        'w_cq': nrm((DEPTH, D_MODEL, D_MODEL), dd),
        'w_ck': nrm((DEPTH, D_MODEL, D_MODEL), dd),
        'w_cv': nrm((DEPTH, D_MODEL, D_MODEL), dd),
        'w_co': nrm((DEPTH, D_MODEL, D_MODEL), dd),
        'g_ffn': gain((DEPTH, D_MODEL)),
        'w_gate': nrm((DEPTH, D_MODEL, D_FF), dd),
        'w_up': nrm((DEPTH, D_MODEL, D_FF), dd),
        'w_down': nrm((DEPTH, D_FF, D_MODEL), D_FF ** -0.5),
        'g_final': gain((D_MODEL,)),
    }


def reference(x_prompt, x_sample, mem_prompt, cache_cmp_k, cache_cmp_v, cache_slc_k, cache_slc_v,
              page_table, state_win_k, state_win_v, cache_mem_k, cache_mem_v,
              g_mix, w_in, cmp_pe, cmp_w1, cmp_w2, gmlp_ln_g, gmlp_ln_b, gmlp_ws, gmlp_bs,
              w_branch_a, w_branch_b, w_out, g_cross, g_mem, w_cq, w_ck, w_cv, w_co,
              g_ffn, w_gate, w_up, w_down, g_final):
    S, DS = x_prompt.shape[1], x_sample.shape[1]
    pos_p = jnp.arange(S, dtype=jnp.int32)
    pos_s = page_table.shape[1] * PAGE_SIZE + jnp.arange(DS, dtype=jnp.int32)
    wb_p = min(WINDOW, S)
    xp, xs = x_prompt, x_sample
    per_layer = []
    for l in range(DEPTH):
        gm = (gmlp_ln_g[l], gmlp_ln_b[l], gmlp_ws[l], gmlp_bs[l])
        mw = (w_branch_a[l], w_branch_b[l], w_out[l])
        q, kc, vc, ks, vs, kw, vw, gn, u, v, ga, gb = mixer_inputs(rms_norm(xp, g_mix[l]), w_in[l], pos_p)
        o_a = nsa_prompt(q, gn, kc, vc, ks, vs, kw, vw, cmp_pe[l], cmp_w1[l], cmp_w2[l])
        xp = xp + merge_branches(o_a, gmlp_spatial(u, v, *gm), ga, gb, *mw)
        mk, mv = memory_kv(mem_prompt, g_mem[l], w_ck[l], w_cv[l])
        xp = xp + cross_attend(rms_norm(xp, g_cross[l]), mk, mv, w_cq[l], w_co[l])
        xp = xp + swiglu(rms_norm(xp, g_ffn[l]), w_gate[l], w_up[l], w_down[l])
        prompt_new = (kc, vc, ks, vs, kw[:, S - wb_p:], vw[:, S - wb_p:], mk, mv)
        q, kc, vc, ks, vs, kw, vw, gn, u, v, ga, gb = mixer_inputs(rms_norm(xs, g_mix[l]), w_in[l], pos_s)
        o_a, win_k, win_v = nsa_sample(l, q, gn, kc, vc, ks, vs, kw, vw, cache_cmp_k, cache_cmp_v,
                                       cache_slc_k, cache_slc_v, page_table, state_win_k[l], state_win_v[l],
                                       cmp_pe[l], cmp_w1[l], cmp_w2[l])
        xs = xs + merge_branches(o_a, gmlp_spatial(u, v, *gm), ga, gb, *mw)
        xs = xs + cross_attend(rms_norm(xs, g_cross[l]), cache_mem_k[l], cache_mem_v[l], w_cq[l], w_co[l])
        xs = xs + swiglu(rms_norm(xs, g_ffn[l]), w_gate[l], w_up[l], w_down[l])
        per_layer.append(prompt_new + (kc, vc, ks, vs, win_k, win_v, v))
    (ck_p, cv_p, sk_p, sv_p, wk_p, wv_p, mk_p, mv_p,
     ck_s, cv_s, sk_s, sv_s, wk_s, wv_s, gv_s) = [jnp.stack(t) for t in zip(*per_layer)]
    y_prompt = rms_norm(xp, g_final)
    y_sample = rms_norm(xs, g_final)
    return (y_prompt, y_sample, ck_p, cv_p, sk_p, sv_p, wk_p, wv_p, mk_p, mv_p,
            ck_s, cv_s, sk_s, sv_s, wk_s, wv_s, gv_s)
```

```python
import functools

import jax
import jax.numpy as jnp
import numpy as np
from jax import lax
from jax.experimental import pallas as pl
from jax.experimental.pallas import tpu as pltpu

D_MODEL = 1024
N_HEADS = 8
HEAD_DIM = 64
N_KV_HEADS = 2
GROUP = N_HEADS // N_KV_HEADS
NSA_WIDTH = N_HEADS * HEAD_DIM
KV_WIDTH = N_KV_HEADS * HEAD_DIM
CMP_LEN = 32
CMP_STRIDE = 16
CMP_HIDDEN = 128
SLC_BLK = 64
TOP_N = 16
WINDOW = 512
Q_BLOCK = 128
ROPE_THETA = 10000.0
ATTN_SCALE = HEAD_DIM ** -0.5
FORCE_BONUS = 1e4
NEG_INF = -1e30
GMLP_WIDTH = 512
GMLP_GROUPS = 8
GMLP_GROUP_DIM = GMLP_WIDTH // GMLP_GROUPS
GMLP_CHUNK = 128
N_MEM = 256
CROSS_HEADS = 4
CROSS_HEAD_DIM = D_MODEL // CROSS_HEADS
CROSS_SCALE = CROSS_HEAD_DIM ** -0.5
D_FF = -(-8 * D_MODEL // (3 * 256)) * 256
NORM_EPS = 1e-6
PAGE_SIZE = 128
IN_SIZES = (NSA_WIDTH, KV_WIDTH, KV_WIDTH, KV_WIDTH, KV_WIDTH, KV_WIDTH, KV_WIDTH, N_HEADS * 3,
            GMLP_WIDTH, GMLP_WIDTH, D_MODEL, D_MODEL)
IN_SPLITS = tuple(sum(IN_SIZES[:i + 1]) for i in range(len(IN_SIZES) - 1))

VMEM_LIMIT_BYTES = 48 * 1024 * 1024
ROW_TILE = 512
FF_CHUNK = 256


def _rms(x, g):
    return x * lax.rsqrt(jnp.mean(x * x, axis=-1, keepdims=True) + NORM_EPS) * g


def _bdot(a, b):
    return jnp.dot(a.astype(jnp.bfloat16), b, preferred_element_type=jnp.float32)


def _ffn_kernel(x_ref, gf_ref, wg_ref, wu_ref, wd_ref, gfin_ref, y_ref, acc_ref):
    x = x_ref[...]
    h = _rms(x, gf_ref[...]).astype(jnp.bfloat16)
    acc_ref[...] = x
    for c in range(D_FF // FF_CHUNK):
        sl = slice(c * FF_CHUNK, (c + 1) * FF_CHUNK)
        a = jnp.dot(h, wg_ref[:, sl], preferred_element_type=jnp.float32)
        b = jnp.dot(h, wu_ref[:, sl], preferred_element_type=jnp.float32)
        t = (a * jax.nn.sigmoid(a) * b).astype(jnp.bfloat16)
        acc_ref[...] += jnp.dot(t, wd_ref[sl, :], preferred_element_type=jnp.float32)
    y_ref[...] = _rms(acc_ref[...], gfin_ref[...])


def ffn_final(x, g_ffn, w_gate, w_up, w_down, g_final):
    m = x.shape[0]
    tm = min(ROW_TILE, m)
    const = lambda i: (0, 0)
    wspec = lambda shape: pl.BlockSpec(shape, const, pipeline_mode=pl.Buffered(1))
    return pl.pallas_call(
        _ffn_kernel,
        grid=(m // tm,),
        in_specs=[
            pl.BlockSpec((tm, D_MODEL), lambda i: (i, 0)),
            wspec((1, D_MODEL)),
            wspec((D_MODEL, D_FF)),
            wspec((D_MODEL, D_FF)),
            wspec((D_FF, D_MODEL)),
            wspec((1, D_MODEL)),
        ],
        out_specs=pl.BlockSpec((tm, D_MODEL), lambda i: (i, 0)),
        out_shape=jax.ShapeDtypeStruct((m, D_MODEL), jnp.float32),
        scratch_shapes=[pltpu.VMEM((tm, D_MODEL), jnp.float32)],
        compiler_params=pltpu.CompilerParams(
            dimension_semantics=("arbitrary",), vmem_limit_bytes=VMEM_LIMIT_BYTES),
        name="ffn_final",
    )(x, g_ffn.reshape(1, -1), w_gate.astype(jnp.bfloat16), w_up.astype(jnp.bfloat16),
      w_down.astype(jnp.bfloat16), g_final.reshape(1, -1))


def rms_norm(x, g):
    xf = x.astype(jnp.float32)
    y = xf * lax.rsqrt(jnp.mean(xf * xf, axis=-1, keepdims=True) + NORM_EPS)
    return (y * g.astype(jnp.float32)).astype(x.dtype)


def layer_norm(x, g, b):
    xf = x.astype(jnp.float32)
    xc = xf - jnp.mean(xf, axis=-1, keepdims=True)
    var = jnp.mean(xc * xc, axis=-1, keepdims=True)
    return (xc * lax.rsqrt(var + NORM_EPS) * g.astype(jnp.float32) + b.astype(jnp.float32)).astype(x.dtype)


def rope(x, pos):
    half = HEAD_DIM // 2
    inv_freq = ROPE_THETA ** (-jnp.arange(half, dtype=jnp.float32) / half)
    ang = pos.astype(jnp.float32)[:, None] * inv_freq[None, :]
    shape = (pos.shape[0],) + (1,) * (x.ndim - 3) + (half,)
    cos, sin = jnp.cos(ang).reshape(shape), jnp.sin(ang).reshape(shape)
    xf = x.astype(jnp.float32)
    x1, x2 = xf[..., :half], xf[..., half:]
    return jnp.concatenate([x1 * cos - x2 * sin, x2 * cos + x1 * sin], axis=-1).astype(x.dtype)


def mixer_inputs(h, w_in, pos):
    B, S, _ = h.shape
    q, kc, vc, ks, vs, kw, vw, gn, u, v, ga, gb = jnp.split(h @ w_in, IN_SPLITS, axis=-1)
    heads = lambda t: t.reshape(B, S, N_KV_HEADS, HEAD_DIM)
    q = rope(q.reshape(B, S, N_KV_HEADS, GROUP, HEAD_DIM), pos)
    gn = jax.nn.sigmoid(gn.reshape(B, S, N_KV_HEADS, GROUP, 3))
    return (q, rope(heads(kc), pos), heads(vc), rope(heads(ks), pos), heads(vs),
            rope(heads(kw), pos), heads(vw), gn, jax.nn.gelu(u), jax.nn.gelu(v), ga, gb)


def masked_softmax(s, mask):
    s = jnp.where(mask, s, NEG_INF)
    e = jnp.where(mask, jnp.exp(s - jnp.max(s, axis=-1, keepdims=True)), 0.0)
    return e / jnp.maximum(jnp.sum(e, axis=-1, keepdims=True), 1e-30)


def attend_shared(q, k, v, mask):
    s = jnp.einsum('bqhgd,bkhd->bhgqk', q, k).astype(jnp.float32) * ATTN_SCALE
    p = masked_softmax(s, mask)
    return jnp.einsum('bhgqk,bkhd->bqhgd', p.astype(v.dtype), v), p


def attend_gathered(q, k, v, mask):
    s = jnp.einsum('bqhgd,bqhkd->bqhgk', q, k).astype(jnp.float32) * ATTN_SCALE
    p = masked_softmax(s, mask[:, :, :, None, :])
    return jnp.einsum('bqhgk,bqhkd->bqhgd', p.astype(v.dtype), v)


def cmp_chunk_proj(rows, w1):
    B, L = rows.shape[:2]
    c = rows.reshape(B, L // CMP_STRIDE, CMP_STRIDE, N_KV_HEADS, HEAD_DIM)
    lead = jnp.einsum('bcshd,sdf->bchf', c, w1[:CMP_STRIDE])
    trail = jnp.einsum('bcshd,sdf->bchf', c, w1[CMP_STRIDE:])
    return lead, trail


def cmp_finish(lead, trail, pe, w1, w2):
    pre = lead[:, :-1] + trail[:, 1:] + jnp.einsum('sd,sdf->f', pe, w1)
    return jnp.einsum('bnhf,fd->bnhd', jax.nn.gelu(pre), w2)


def cmp_to_slc_weights(n_cmp, n_slc):
    cs = jnp.arange(n_cmp)[:, None] * CMP_STRIDE
    ss = jnp.arange(n_slc)[None, :] * SLC_BLK
    shared = jnp.minimum(cs + CMP_LEN, ss + SLC_BLK) - jnp.maximum(cs, ss)
    return jnp.maximum(shared, 0).astype(jnp.float32) / CMP_LEN


def nsa_attend(q, qpos, gn, ck, cv, cend, n_slc, gather_sel, kw, vw, kwpos):
    B, Q = q.shape[:2]
    o_cmp, p_cmp = attend_shared(q, ck, cv, cend[None, :] <= qpos[:, None])
    score = jnp.einsum('bhqn,nm->bqhm', p_cmp.sum(axis=2), cmp_to_slc_weights(ck.shape[1], n_slc))
    blk = jnp.arange(n_slc)
    qblk = (qpos // SLC_BLK)[:, None]
    forced = (blk == 0) | (blk == qblk) | (blk == qblk - 1)
    score = jnp.where((blk <= qblk)[:, None], score + FORCE_BONUS * forced[:, None], NEG_INF)
    _, idx = lax.top_k(score, min(TOP_N, n_slc))
    k_sel, v_sel = gather_sel(idx)
    kpos = (idx[..., None] * SLC_BLK + jnp.arange(SLC_BLK)).reshape(B, Q, N_KV_HEADS, -1)
    o_sel = attend_gathered(q, k_sel, v_sel, kpos <= qpos[None, :, None, None])
    wmask = ((kwpos[None, :] >= 0) & (kwpos[None, :] <= qpos[:, None])
             & (kwpos[None, :] > qpos[:, None] - WINDOW))
    o_win, _ = attend_shared(q, kw, vw, wmask)
    o = gn[..., 0:1] * o_cmp + gn[..., 1:2] * o_sel + gn[..., 2:3] * o_win
    return o.reshape(B, Q, NSA_WIDTH)


def nsa_prompt(q, gn, kc, vc, ks, vs, kw, vw, pe, w1, w2):
    B, S = q.shape[:2]
    ck = cmp_finish(*cmp_chunk_proj(kc, w1[0]), pe[0], w1[0], w2[0])
    cv = cmp_finish(*cmp_chunk_proj(vc, w1[1]), pe[1], w1[1], w2[1])
    cend = jnp.arange(ck.shape[1], dtype=jnp.int32) * CMP_STRIDE + (CMP_LEN - 1)
    n_slc = S // SLC_BLK
    kb = ks.reshape(B, n_slc, SLC_BLK, N_KV_HEADS, HEAD_DIM)
    vb = vs.reshape(B, n_slc, SLC_BLK, N_KV_HEADS, HEAD_DIM)
    bi = jnp.arange(B)[:, None, None, None, None]
    hi = jnp.arange(N_KV_HEADS)[None, None, :, None, None]
    r = jnp.arange(SLC_BLK)

    def gather_sel(idx):
        i5 = idx[..., None]
        shape = idx.shape[:3] + (-1, HEAD_DIM)
        return kb[bi, i5, r, hi].reshape(shape), vb[bi, i5, r, hi].reshape(shape)

    kw_pad = jnp.pad(kw, ((0, 0), (WINDOW, 0), (0, 0), (0, 0)))
    vw_pad = jnp.pad(vw, ((0, 0), (WINDOW, 0), (0, 0), (0, 0)))

    def query_block(q0):
        qpos = q0 + jnp.arange(Q_BLOCK, dtype=jnp.int32)
        take = lambda t, n: lax.dynamic_slice_in_dim(t, q0, n, axis=1)
        kwpos = q0 - WINDOW + jnp.arange(WINDOW + Q_BLOCK, dtype=jnp.int32)
        return nsa_attend(take(q, Q_BLOCK), qpos, take(gn, Q_BLOCK), ck, cv, cend, n_slc, gather_sel,
                          take(kw_pad, WINDOW + Q_BLOCK), take(vw_pad, WINDOW + Q_BLOCK), kwpos)

    o = lax.map(query_block, jnp.arange(0, S, Q_BLOCK, dtype=jnp.int32))
    return jnp.swapaxes(o, 0, 1).reshape(B, S, NSA_WIDTH)


def nsa_sample(l, q, gn, kc, vc, ks, vs, kw, vw, pool_ck, pool_cv, pool_sk, pool_sv,
               page_table, win_k, win_v, pe, w1, w2):
    DB, DS = q.shape[:2]
    past = page_table.shape[1] * PAGE_SIZE
    qpos = past + jnp.arange(DS, dtype=jnp.int32)
    n_new_rows = (DS // CMP_STRIDE) * CMP_STRIDE

    def compressed(pool, new_rows, j):
        rows = pool[l, page_table].reshape(DB, past, N_KV_HEADS, HEAD_DIM)
        lead, trail = cmp_chunk_proj(rows, w1[j])
        if n_new_rows:
            lead_n, trail_n = cmp_chunk_proj(new_rows[:, :n_new_rows], w1[j])
            lead = jnp.concatenate([lead, lead_n], axis=1)
            trail = jnp.concatenate([trail, trail_n], axis=1)
        return cmp_finish(lead, trail, pe[j], w1[j], w2[j])

    ck, cv = compressed(pool_ck, kc, 0), compressed(pool_cv, vc, 1)
    cend = jnp.arange(ck.shape[1], dtype=jnp.int32) * CMP_STRIDE + (CMP_LEN - 1)

    n_past_blk = past // SLC_BLK
    n_slc = -(-(past + DS) // SLC_BLK)
    n_new_blk = n_slc - n_past_blk
    pad = ((0, 0), (0, n_new_blk * SLC_BLK - DS), (0, 0), (0, 0))
    kb_new = jnp.pad(ks, pad).reshape(DB, n_new_blk, SLC_BLK, N_KV_HEADS, HEAD_DIM)
    vb_new = jnp.pad(vs, pad).reshape(DB, n_new_blk, SLC_BLK, N_KV_HEADS, HEAD_DIM)
    bi = jnp.arange(DB)[:, None, None, None, None]
    hi = jnp.arange(N_KV_HEADS)[None, None, :, None, None]
    r = jnp.arange(SLC_BLK)

    def gather_sel(idx):
        i5 = idx[..., None]
        row = jnp.minimum(i5, n_past_blk - 1) * SLC_BLK + r
        phys, off = page_table[bi, row // PAGE_SIZE], row % PAGE_SIZE
        nb = jnp.clip(i5 - n_past_blk, 0, n_new_blk - 1)
        in_past = (i5 < n_past_blk)[..., None]
        shape = idx.shape[:3] + (-1, HEAD_DIM)
        pick = lambda pool, new_blocks: jnp.where(
            in_past, pool[l, phys, off, hi], new_blocks[bi, nb, r, hi]).reshape(shape)
        return pick(pool_sk, kb_new), pick(pool_sv, vb_new)

    kwin = jnp.concatenate([win_k, kw], axis=1)
    vwin = jnp.concatenate([win_v, vw], axis=1)
    wb = win_k.shape[1]
    kwpos = past - wb + jnp.arange(wb + DS, dtype=jnp.int32)
    o = nsa_attend(q, qpos, gn, ck, cv, cend, n_slc, gather_sel, kwin, vwin, kwpos)
    return o, kwin[:, DS:], vwin[:, DS:]


def gmlp_spatial(u, v, ln_g, ln_b, ws, bs):
    B, S, _ = u.shape
    L = min(S, GMLP_CHUNK)
    vn = layer_norm(v, ln_g, ln_b).reshape(B, S // L, L, GMLP_GROUPS, GMLP_GROUP_DIM)
    w = jnp.tril(ws[:, :L, :L])
    s = jnp.einsum('gij,bcjgd->bcigd', w, vn) + bs[:, :L].T[None, None, :, :, None]
    return u * s.reshape(B, S, GMLP_WIDTH)


def merge_branches(o_a, o_b, ga, gb, w_a, w_b, w_out):
    m = jax.nn.sigmoid(ga) * (o_a @ w_a) + jax.nn.sigmoid(gb) * (o_b @ w_b)
    return m @ w_out


def memory_kv(mem, g_mem, w_ck, w_cv):
    B, M, _ = mem.shape
    hm = rms_norm(mem, g_mem)
    return ((hm @ w_ck).reshape(B, M, CROSS_HEADS, CROSS_HEAD_DIM),
            (hm @ w_cv).reshape(B, M, CROSS_HEADS, CROSS_HEAD_DIM))


def cross_attend(h, mk, mv, w_cq, w_co):
    B, S, _ = h.shape
    q = (h @ w_cq).reshape(B, S, CROSS_HEADS, CROSS_HEAD_DIM)
    s = jnp.einsum('bqhd,bmhd->bhqm', q, mk).astype(jnp.float32) * CROSS_SCALE
    p = jax.nn.softmax(s, axis=-1).astype(mv.dtype)
    return jnp.einsum('bhqm,bmhd->bqhd', p, mv).reshape(B, S, D_MODEL) @ w_co


def kernel(x_prompt, x_sample, mem_prompt, cache_cmp_k, cache_cmp_v, cache_slc_k, cache_slc_v,
           page_table, state_win_k, state_win_v, cache_mem_k, cache_mem_v,
           g_mix, w_in, cmp_pe, cmp_w1, cmp_w2, gmlp_ln_g, gmlp_ln_b, gmlp_ws, gmlp_bs,
           w_branch_a, w_branch_b, w_out, g_cross, g_mem, w_cq, w_ck, w_cv, w_co,
           g_ffn, w_gate, w_up, w_down, g_final):
    S, DS = x_prompt.shape[1], x_sample.shape[1]
    pos_p = jnp.arange(S, dtype=jnp.int32)
    pos_s = page_table.shape[1] * PAGE_SIZE + jnp.arange(DS, dtype=jnp.int32)
    wb_p = min(WINDOW, S)
    xp, xs = x_prompt, x_sample
    l = 0
    gm = (gmlp_ln_g[l], gmlp_ln_b[l], gmlp_ws[l], gmlp_bs[l])
    mw = (w_branch_a[l], w_branch_b[l], w_out[l])
    q, kc, vc, ks, vs, kw, vw, gn, u, v, ga, gb = mixer_inputs(rms_norm(xp, g_mix[l]), w_in[l], pos_p)
    o_a = nsa_prompt(q, gn, kc, vc, ks, vs, kw, vw, cmp_pe[l], cmp_w1[l], cmp_w2[l])
    xp = xp + merge_branches(o_a, gmlp_spatial(u, v, *gm), ga, gb, *mw)
    mk, mv = memory_kv(mem_prompt, g_mem[l], w_ck[l], w_cv[l])
    xp = xp + cross_attend(rms_norm(xp, g_cross[l]), mk, mv, w_cq[l], w_co[l])
    prompt_new = (kc, vc, ks, vs, kw[:, S - wb_p:], vw[:, S - wb_p:], mk, mv)
    q, kc, vc, ks, vs, kw, vw, gn, u, v, ga, gb = mixer_inputs(rms_norm(xs, g_mix[l]), w_in[l], pos_s)
    o_a, win_k, win_v = nsa_sample(l, q, gn, kc, vc, ks, vs, kw, vw, cache_cmp_k, cache_cmp_v,
                                   cache_slc_k, cache_slc_v, page_table, state_win_k[l], state_win_v[l],
                                   cmp_pe[l], cmp_w1[l], cmp_w2[l])
    xs = xs + merge_branches(o_a, gmlp_spatial(u, v, *gm), ga, gb, *mw)
    xs = xs + cross_attend(rms_norm(xs, g_cross[l]), cache_mem_k[l], cache_mem_v[l], w_cq[l], w_co[l])
    sample_new = (kc, vc, ks, vs, win_k, win_v, v)

    ffn = functools.partial(ffn_final, g_ffn=g_ffn[l], w_gate=w_gate[l], w_up=w_up[l],
                            w_down=w_down[l], g_final=g_final)
    y_prompt = ffn(xp.reshape(-1, D_MODEL)).reshape(xp.shape)
    y_sample = ffn(xs.reshape(-1, D_MODEL)).reshape(xs.shape)
    outs = prompt_new + sample_new
    return (y_prompt, y_sample) + tuple(t[None] for t in outs)
```

```python
import functools

import jax
import jax.numpy as jnp
import numpy as np
from jax import lax
from jax.experimental import pallas as pl
from jax.experimental.pallas import tpu as pltpu

D_MODEL = 1024
N_HEADS = 8
HEAD_DIM = 64
N_KV_HEADS = 2
GROUP = N_HEADS // N_KV_HEADS
NSA_WIDTH = N_HEADS * HEAD_DIM
KV_WIDTH = N_KV_HEADS * HEAD_DIM
CMP_LEN = 32
CMP_STRIDE = 16
CMP_HIDDEN = 128
SLC_BLK = 64
SLC_SHIFT = 6
TOP_N = 16
WINDOW = 512
ROPE_THETA = 10000.0
ATTN_SCALE = HEAD_DIM ** -0.5
FORCE_BONUS = 1e4
NEG_INF = -1e30
GMLP_WIDTH = 512
GMLP_GROUPS = 8
GMLP_GROUP_DIM = GMLP_WIDTH // GMLP_GROUPS
GMLP_CHUNK = 128
N_MEM = 256
CROSS_HEADS = 4
CROSS_HEAD_DIM = D_MODEL // CROSS_HEADS
CROSS_SCALE = CROSS_HEAD_DIM ** -0.5
D_FF = -(-8 * D_MODEL // (3 * 256)) * 256
NORM_EPS = 1e-6
PAGE_SIZE = 128
IN_SIZES = (NSA_WIDTH, KV_WIDTH, KV_WIDTH, KV_WIDTH, KV_WIDTH, KV_WIDTH, KV_WIDTH, N_HEADS * 3,
            GMLP_WIDTH, GMLP_WIDTH, D_MODEL, D_MODEL)
IN_SPLITS = tuple(sum(IN_SIZES[:i + 1]) for i in range(len(IN_SIZES) - 1))

LANES = 128
SUBLANES = 8
VMEM_LIMIT_BYTES = 48 * 1024 * 1024
PAGED_VMEM_LIMIT_BYTES = 56 * 1024 * 1024
SEL_TILE = 512
SAMPLE_KEY_CHUNK = 2048
CHUNKS_PER_PAGE = PAGE_SIZE // CMP_STRIDE
N_BRANCH = 3
ROW_TILE = 512
FF_CHUNK = 256

_NT = (((1,), (1,)), ((), ()))


def _dot_nt(a, b):
    return lax.dot_general(a, b, _NT, preferred_element_type=jnp.float32)


def _dot(a, b):
    return jnp.dot(a, b, preferred_element_type=jnp.float32)


def _split_bf16(x):
    hi = x.astype(jnp.bfloat16)
    lo = (x - hi.astype(jnp.float32)).astype(jnp.bfloat16)
    return hi, lo


def _iota(shape, dim):
    return lax.broadcasted_iota(jnp.int32, shape, dim)


def _cmp_kernel(pt_ref, pool_k, pool_v, wbig_ref, w2big_ref, peterm_ref, ck_ref, cv_ref,
                buf, sem, *, n_pages):
    b = pl.program_id(0)
    nb = pl.num_programs(0)
    slot = b % 2
    n_chunks = n_pages * CHUNKS_PER_PAGE
    pools = (pool_k, pool_v)

    def page_copy(bb, sl, j, p):
        return pltpu.make_async_copy(pools[p].at[pt_ref[bb, j]],
                                     buf.at[sl, p, pl.ds(j * PAGE_SIZE, PAGE_SIZE)], sem.at[sl, p])

    def start(bb, sl):
        for p in range(2):
            for j in range(n_pages):
                page_copy(bb, sl, j, p).start()

    @pl.when(b == 0)
    def _():
        start(0, 0)

    @pl.when(b + 1 < nb)
    def _():
        start(b + 1, 1 - slot)

    for p in range(2):
        for j in range(n_pages):
            page_copy(b, slot, j, p).wait()

    for p, out_ref in ((0, ck_ref), (1, cv_ref)):
        rows_ref = buf.at[slot, p]
        acc = jnp.zeros((n_chunks, 4 * CMP_HIDDEN), jnp.float32)
        for sp in range(CMP_STRIDE // 2):
            a0 = rows_ref[pl.ds(2 * sp, n_chunks, stride=CMP_STRIDE), :]
            a1 = rows_ref[pl.ds(2 * sp + 1, n_chunks, stride=CMP_STRIDE), :]
            lhs = jnp.concatenate([a0, a1], axis=1).astype(jnp.bfloat16)
            acc = acc + _dot(lhs, wbig_ref[p, sp])
        lead = acc[:, :2 * CMP_HIDDEN]
        trail_next = pltpu.roll(acc[:, 2 * CMP_HIDDEN:], n_chunks - 1, axis=0)
        pre = lead + trail_next + peterm_ref[p]
        out = _dot(jax.nn.gelu(pre).astype(jnp.bfloat16), w2big_ref[p])
        out_ref[0] = out.astype(out_ref.dtype)


def cmp_weights(cmp_pe, cmp_w1, cmp_w2):
    eye = jnp.eye(N_KV_HEADS, dtype=jnp.float32)
    w1r = cmp_w1.reshape(2, 2, CMP_STRIDE // 2, 2, HEAD_DIM, CMP_HIDDEN)
    wbig = jnp.einsum('jtpsdf,hk->jpshdtkf', w1r, eye)
    wbig = wbig.reshape(2, CMP_STRIDE // 2, 2 * KV_WIDTH, 4 * CMP_HIDDEN).astype(jnp.bfloat16)
    w2big = jnp.einsum('jfd,hk->jhfkd', cmp_w2, eye).reshape(2, 2 * CMP_HIDDEN, KV_WIDTH)
    pe_term = jnp.einsum('jsd,jsdf->jf', cmp_pe, cmp_w1)
    peterm = jnp.tile(pe_term[:, None, :], (1, 1, N_KV_HEADS))
    return wbig, w2big.astype(jnp.bfloat16), peterm


def cmp_blocks(pool_k, pool_v, page_table, wbig, w2big, peterm):
    n_seq, n_pages = page_table.shape
    n_chunks = n_pages * CHUNKS_PER_PAGE
    const3 = lambda b, pt: (0, 0, 0)
    out = jax.ShapeDtypeStruct((n_seq, n_chunks, KV_WIDTH), jnp.bfloat16)
    grid_spec = pltpu.PrefetchScalarGridSpec(
        num_scalar_prefetch=1,
        grid=(n_seq,),
        in_specs=[
            pl.BlockSpec(memory_space=pl.ANY),
            pl.BlockSpec(memory_space=pl.ANY),
            pl.BlockSpec(wbig.shape, lambda b, pt: (0, 0, 0, 0), pipeline_mode=pl.Buffered(1)),
            pl.BlockSpec(w2big.shape, const3, pipeline_mode=pl.Buffered(1)),
            pl.BlockSpec(peterm.shape, const3, pipeline_mode=pl.Buffered(1)),
        ],
        out_specs=[pl.BlockSpec((1, n_chunks, KV_WIDTH), lambda b, pt: (b, 0, 0))] * 2,
        scratch_shapes=[
            pltpu.VMEM((2, 2, n_pages * PAGE_SIZE, KV_WIDTH), jnp.float32),
            pltpu.SemaphoreType.DMA((2, 2)),
        ],
    )
    return pl.pallas_call(
        functools.partial(_cmp_kernel, n_pages=n_pages),
        grid_spec=grid_spec,
        out_shape=[out, out],
        compiler_params=pltpu.CompilerParams(
            dimension_semantics=("arbitrary",), vmem_limit_bytes=PAGED_VMEM_LIMIT_BYTES),
        name="cmp_blocks",
    )(page_table, pool_k, pool_v, wbig, w2big, peterm)


def _padded_queries(q, tq):
    lane = _iota((tq, LANES), 1)
    blocks = []
    for h in range(N_KV_HEADS):
        for g in range(GROUP):
            c = h * GROUP + g
            pair = q[:, LANES * (c // 2):LANES * (c // 2 + 1)]
            if c % 2 != h:
                pair = pltpu.roll(pair, HEAD_DIM, axis=1)
            keep = (lane >= HEAD_DIM * h) & (lane < HEAD_DIM * (h + 1))
            blocks.append(jnp.where(keep, pair, 0.0))
    return jnp.concatenate(blocks, axis=0)


def _masked_softmax(s, valid):
    sm = jnp.where(valid, s, NEG_INF)
    mx = jnp.max(sm, axis=1, keepdims=True)
    e = jnp.where(valid, jnp.exp(sm - mx), 0.0)
    den = jnp.maximum(jnp.sum(e, axis=1, keepdims=True), 1e-30)
    return e / den


def _group_sum(p, tq):
    out = []
    for h in range(N_KV_HEADS):
        r0 = h * GROUP * tq
        acc = p[r0:r0 + tq]
        for g in range(1, GROUP):
            acc = acc + p[r0 + g * tq:r0 + (g + 1) * tq]
        out.append(acc)
    return out


def _combine(branches, gn, gexp_w, tq):
    hi, lo = _split_bf16(gn)
    gexp = _dot(hi, gexp_w) + _dot(lo, gexp_w)
    lane = _iota((tq, LANES), 1)
    cols = []
    for h in range(N_KV_HEADS):
        blks = []
        for g in range(GROUP):
            j = h * GROUP + g
            rows = slice(j * tq, (j + 1) * tq)
            o = None
            for br in range(N_BRANCH):
                gate = gexp[:, (j * N_BRANCH + br) * LANES:(j * N_BRANCH + br + 1) * LANES]
                term = gate * branches[br][rows]
                o = term if o is None else o + term
            blks.append(o)
        for jj in range(GROUP // 2):
            a, b2 = blks[2 * jj], blks[2 * jj + 1]
            if h == 0:
                b2 = pltpu.roll(b2, HEAD_DIM, axis=1)
            else:
                a = pltpu.roll(a, HEAD_DIM, axis=1)
            cols.append(jnp.where(lane < HEAD_DIM, a, b2))
    return jnp.concatenate(cols, axis=1)


def gate_expand_matrix():
    e = np.zeros((LANES, N_HEADS * N_BRANCH * LANES), np.float32)
    for j in range(N_HEADS * N_BRANCH):
        e[j, j * LANES:(j + 1) * LANES] = 1.0
    return jnp.asarray(e, jnp.bfloat16)


def cmp_to_slc_matrix(n_chunks, n_cols):
    cs = np.arange(n_chunks)[:, None] * CMP_STRIDE
    ss = np.arange(n_cols)[None, :] * SLC_BLK
    shared = np.minimum(cs + CMP_LEN, ss + SLC_BLK) - np.maximum(cs, ss)
    w = np.maximum(shared, 0).astype(np.float32) / CMP_LEN
    w[n_chunks - 1] = 0.0
    return w


def _nsa_prompt_kernel(q_ref, gn_ref, ck_ref, cv_ref, ks_ref, vs_ref, kw_ref, vw_ref,
                       wcst_ref, gexp_ref, o_ref, lhs_ref, m_ref, l_ref, acc_ref, *, tq, n_chunks):
    i = pl.program_id(1)
    q0 = i * tq
    qp = _padded_queries(q_ref[0] * ATTN_SCALE, tq)
    qpb = qp.astype(jnp.bfloat16)

    def qpos_of(shape):
        return q0 + (_iota(shape, 0) & (tq - 1))

    s = _dot_nt(qpb, ck_ref[0])
    n_idx = _iota(s.shape, 1)
    valid = (n_idx * CMP_STRIDE + (CMP_LEN - 1) <= qpos_of(s.shape)) & (n_idx < n_chunks - 1)
    p_cmp = _masked_softmax(s, valid)
    o_cmp = _dot(p_cmp.astype(jnp.bfloat16), cv_ref[0])

    blk = _iota((LANES, tq), 0)
    blkf = blk.astype(jnp.float32)
    qblk = (q0 + _iota((LANES, tq), 1)) >> SLC_SHIFT
    forced = (blk == 0) | (blk == qblk) | (blk == qblk - 1)
    wcst = wcst_ref[...]
    nsel = []
    for psum in _group_sum(p_cmp, tq):
        hi, lo = _split_bf16(psum)
        sc = _dot_nt(wcst, hi) + _dot_nt(wcst, lo)
        sc = jnp.where(blk <= qblk, jnp.where(forced, sc + FORCE_BONUS, sc), NEG_INF)
        sel = jnp.zeros_like(sc)
        for _ in range(TOP_N):
            mx = jnp.max(sc, axis=0, keepdims=True)
            first = jnp.min(jnp.where(sc == mx, blkf, float(LANES)), axis=0, keepdims=True)
            hit = blkf == first
            sel = jnp.where(hit, 1.0, sel)
            sc = jnp.where(hit, -jnp.inf, sc)
        nsel.append((1.0 - sel).T.astype(jnp.bfloat16))
    for h in range(N_KV_HEADS):
        r0 = h * GROUP * tq
        lhs_ref[r0:r0 + GROUP * tq, :LANES] = qpb[r0:r0 + GROUP * tq]
        for g in range(GROUP):
            lhs_ref[r0 + g * tq:r0 + (g + 1) * tq, LANES:] = nsel[h]

    m_ref[...] = jnp.full(m_ref.shape, -3e38, jnp.float32)
    l_ref[...] = jnp.zeros(l_ref.shape, jnp.float32)
    acc_ref[...] = jnp.zeros(acc_ref.shape, jnp.float32)
    n_tiles = (q0 + tq + SEL_TILE - 1) // SEL_TILE

    def sel_tile(kt, causal):
        k0 = pl.multiple_of(kt * SEL_TILE, SEL_TILE)
        kblk = (k0 + _iota((SEL_TILE, LANES), 0)) >> SLC_SHIFT
        ebig = jnp.where(kblk == _iota((SEL_TILE, LANES), 1), NEG_INF, 0.0).astype(jnp.bfloat16)
        rhs = jnp.concatenate([ks_ref[0, pl.ds(k0, SEL_TILE), :], ebig], axis=1)
        st = _dot_nt(lhs_ref[...], rhs)
        if causal:
            st = jnp.where(k0 + _iota(st.shape, 1) <= qpos_of(st.shape), st, NEG_INF)
        cols = [st[:, c * LANES:(c + 1) * LANES] for c in range(SEL_TILE // LANES)]
        mt = functools.reduce(jnp.maximum, cols)
        m_prev = m_ref[...]
        m_new = jnp.maximum(m_prev, jnp.max(mt, axis=1, keepdims=True))
        alpha = jnp.exp(m_prev - m_new)
        ps = [jnp.exp(c - m_new) for c in cols]
        l_ref[...] = alpha * l_ref[...] + functools.reduce(jnp.add, ps)
        pb = jnp.concatenate(ps, axis=1).astype(jnp.bfloat16)
        acc_ref[...] = alpha * acc_ref[...] + _dot(pb, vs_ref[0, pl.ds(k0, SEL_TILE), :])
        m_ref[...] = m_new

    def body(kt, carry):
        sel_tile(kt, False)
        return carry

    lax.fori_loop(0, n_tiles - 1, body, 0)
    sel_tile(n_tiles - 1, True)
    o_sel = acc_ref[...] / jnp.sum(l_ref[...], axis=1, keepdims=True)

    w0 = pl.multiple_of(q0, tq)
    sw = _dot_nt(qpb, kw_ref[0, pl.ds(w0, WINDOW + tq), :])
    kpos = q0 - WINDOW + _iota(sw.shape, 1)
    qpos = qpos_of(sw.shape)
    p_win = _masked_softmax(sw, (kpos >= 0) & (kpos <= qpos) & (kpos > qpos - WINDOW))
    o_win = _dot(p_win.astype(jnp.bfloat16), vw_ref[0, pl.ds(w0, WINDOW + tq), :])

    o_ref[0] = _combine((o_cmp, o_sel, o_win), gn_ref[0], gexp_ref[...], tq)


def nsa_prompt_attention(q, gn, ck, cv, ks, vs, kw, vw, *, tq=LANES):
    B, S, _ = q.shape
    n_chunks = ck.shape[1]
    assert S % SEL_TILE == 0 and S // SLC_BLK <= LANES and tq == LANES
    bf = lambda t: t.astype(jnp.bfloat16)
    padw = lambda t: jnp.pad(bf(t), ((0, 0), (WINDOW, 0), (0, 0)))
    gn_p = jnp.pad(gn, ((0, 0), (0, 0), (0, LANES - gn.shape[-1])))
    wcst = jnp.asarray(cmp_to_slc_matrix(n_chunks, LANES).T, jnp.bfloat16)
    gexp = gate_expand_matrix()
    rows = N_HEADS * tq
    per_b = lambda b, i: (b, 0, 0)
    tile = lambda b, i: (b, i, 0)
    const = lambda b, i: (0, 0)
    return pl.pallas_call(
        functools.partial(_nsa_prompt_kernel, tq=tq, n_chunks=n_chunks),
        grid=(B, S // tq),
        in_specs=[
            pl.BlockSpec((1, tq, NSA_WIDTH), tile),
            pl.BlockSpec((1, tq, LANES), tile),
            pl.BlockSpec((1, n_chunks, KV_WIDTH), per_b),
            pl.BlockSpec((1, n_chunks, KV_WIDTH), per_b),
            pl.BlockSpec((1, S, KV_WIDTH), per_b),
            pl.BlockSpec((1, S, KV_WIDTH), per_b),
            pl.BlockSpec((1, S + WINDOW, KV_WIDTH), per_b),
            pl.BlockSpec((1, S + WINDOW, KV_WIDTH), per_b),
            pl.BlockSpec(wcst.shape, const),
            pl.BlockSpec(gexp.shape, const),
        ],
        out_specs=pl.BlockSpec((1, tq, NSA_WIDTH), tile),
        out_shape=jax.ShapeDtypeStruct((B, S, NSA_WIDTH), jnp.float32),
        scratch_shapes=[
            pltpu.VMEM((rows, 2 * LANES), jnp.bfloat16),
            pltpu.VMEM((rows, LANES), jnp.float32),
            pltpu.VMEM((rows, LANES), jnp.float32),
            pltpu.VMEM((rows, LANES), jnp.float32),
        ],
        compiler_params=pltpu.CompilerParams(
            dimension_semantics=("arbitrary", "arbitrary"), vmem_limit_bytes=VMEM_LIMIT_BYTES),
        name="nsa_prompt",
    )(q, gn_p, ck, cv, bf(ks), bf(vs), padw(kw), padw(vw), wcst, gexp)


def _nsa_sample_kernel(pt_ref, q_ref, gn_ref, ck_ref, cv_ref, wk_ref, wv_ref,
                       ksn_ref, vsn_ref, kwn_ref, vwn_ref, pool_k, pool_v, ebig_ref, wcs_ref, gexp_ref,
                       o_ref, nwk_ref, nwv_ref, buf, sem, *, n_pages, n_new, tq):
    b = pl.program_id(0)
    nb = pl.num_programs(0)
    slot = b % 2
    past = n_pages * PAGE_SIZE
    n_chunks = n_pages * CHUNKS_PER_PAGE
    n_past_blk = past // SLC_BLK
    n_slc = -(-(past + n_new) // SLC_BLK)
    wb = wk_ref.shape[1]
    rows = N_HEADS * tq
    pools = (pool_k, pool_v)

    def page_copy(bb, sl, j, p):
        return pltpu.make_async_copy(pools[p].at[pt_ref[bb, j]],
                                     buf.at[sl, p, pl.ds(j * PAGE_SIZE, PAGE_SIZE)], sem.at[sl, p])

    def start(bb, sl):
        for p in range(2):
            for j in range(n_pages):
                page_copy(bb, sl, j, p).start()

    @pl.when(b == 0)
    def _():
        start(0, 0)

    @pl.when(b + 1 < nb)
    def _():
        start(b + 1, 1 - slot)

    qp = _padded_queries(q_ref[0] * ATTN_SCALE, tq)
    qpb = qp.astype(jnp.bfloat16)

    def tok_of(shape):
        return _iota(shape, 0) & (tq - 1)

    def new_rows(ref):
        x = jnp.concatenate([ref[0], jnp.zeros((LANES - tq, KV_WIDTH), jnp.float32)], axis=0)
        return x.astype(jnp.bfloat16)

    s = _dot_nt(qpb, ck_ref[0])
    n_idx = _iota(s.shape, 1)
    valid = (n_idx * CMP_STRIDE + (CMP_LEN - 1) <= past + tok_of(s.shape)) & (n_idx < n_chunks - 1)
    p_cmp = _masked_softmax(s, valid)
    o_cmp = _dot(p_cmp.astype(jnp.bfloat16), cv_ref[0])

    ncol = wcs_ref.shape[1]
    blk = _iota((tq, ncol), 1)
    blkf = blk.astype(jnp.float32)
    qblk = (past + _iota((tq, ncol), 0)) >> SLC_SHIFT
    forced = (blk == 0) | (blk == qblk) | (blk == qblk - 1)
    wcs = wcs_ref[...]
    nsel, sel_new = [], []
    for psum in _group_sum(p_cmp, tq):
        hi, lo = _split_bf16(psum)
        sc = _dot(hi, wcs) + _dot(lo, wcs)
        sc = jnp.where(blk <= qblk, jnp.where(forced, sc + FORCE_BONUS, sc), NEG_INF)
        sc = jnp.where(blk < n_slc, sc, -jnp.inf)
        sel = jnp.zeros_like(sc)
        for _ in range(min(TOP_N, n_slc)):
            mx = jnp.max(sc, axis=1, keepdims=True)
            first = jnp.min(jnp.where(sc == mx, blkf, float(ncol)), axis=1, keepdims=True)
            hit = blkf == first
            sel = jnp.where(hit, 1.0, sel)
            sc = jnp.where(hit, -jnp.inf, sc)
        nsel.append(jnp.concatenate([1.0 - sel[:, :LANES]] * GROUP, axis=0).astype(jnp.bfloat16))
        sel_new.append(jnp.broadcast_to(sel[:, n_past_blk:n_past_blk + 1], (tq, LANES)))
    lhs = jnp.concatenate(
        [jnp.concatenate([qpb[h * GROUP * tq:(h + 1) * GROUP * tq], nsel[h]], axis=1)
         for h in range(N_KV_HEADS)], axis=0)
    sel_new_rows = jnp.concatenate([sel_new[h] for h in range(N_KV_HEADS) for _ in range(GROUP)], axis=0)

    tn = _iota((rows, LANES), 1)
    new_ok = (tn <= tok_of((rows, LANES))) & (tn < n_new)
    s_sn = jnp.where(new_ok & (sel_new_rows > 0.5), _dot_nt(qpb, new_rows(ksn_ref)), NEG_INF)
    s_wn = jnp.where(new_ok, _dot_nt(qpb, new_rows(kwn_ref)), NEG_INF)

    wk = wk_ref[0]
    wv = wv_ref[0]
    s_w = _dot_nt(qpb, wk.astype(jnp.bfloat16))
    kwpos = past - wb + _iota(s_w.shape, 1)
    qpos = past + tok_of(s_w.shape)
    s_w = jnp.where((kwpos >= 0) & (kwpos <= qpos) & (kwpos > qpos - WINDOW), s_w, NEG_INF)
    mx = jnp.maximum(jnp.max(s_w, axis=1, keepdims=True), jnp.max(s_wn, axis=1, keepdims=True))
    e_w = jnp.exp(s_w - mx)
    e_wn = jnp.exp(s_wn - mx)
    den = jnp.sum(e_w, axis=1, keepdims=True) + jnp.sum(e_wn, axis=1, keepdims=True)
    o_win = (_dot(e_w.astype(jnp.bfloat16), wv.astype(jnp.bfloat16))
             + _dot(e_wn.astype(jnp.bfloat16), new_rows(vwn_ref))) / den

    row = _iota((wb, KV_WIDTH), 0)
    for src, new_ref, dst in ((wk, kwn_ref, nwk_ref), (wv, vwn_ref, nwv_ref)):
        tail = jnp.concatenate([jnp.zeros((wb - tq, KV_WIDTH), jnp.float32), new_ref[0]], axis=0)
        dst[0] = jnp.where(row >= wb - n_new, pltpu.roll(tail, n_new, axis=0),
                           pltpu.roll(src, wb - n_new, axis=0))

    for p in range(2):
        for j in range(n_pages):
            page_copy(b, slot, j, p).wait()
    ck_keys = min(SAMPLE_KEY_CHUNK, past)
    s_chunks = []
    for c in range(past // ck_keys):
        kc = buf[slot, 0, pl.ds(c * ck_keys, ck_keys), :].astype(jnp.bfloat16)
        rhs = jnp.concatenate([kc, ebig_ref[pl.ds(c * ck_keys, ck_keys), :]], axis=1)
        s_chunks.append(_dot_nt(lhs, rhs))
    mx = jnp.max(s_sn, axis=1, keepdims=True)
    for sc in s_chunks:
        mx = jnp.maximum(mx, jnp.max(sc, axis=1, keepdims=True))
    e_sn = jnp.exp(s_sn - mx)
    den = jnp.sum(e_sn, axis=1, keepdims=True)
    o_sel = _dot(e_sn.astype(jnp.bfloat16), new_rows(vsn_ref))
    for c, sc in enumerate(s_chunks):
        e = jnp.exp(sc - mx)
        den = den + jnp.sum(e, axis=1, keepdims=True)
        vc = buf[slot, 1, pl.ds(c * ck_keys, ck_keys), :].astype(jnp.bfloat16)
        o_sel = o_sel + _dot(e.astype(jnp.bfloat16), vc)
    o_sel = o_sel / den

    o_ref[0] = _combine((o_cmp, o_sel, o_win), gn_ref[0], gexp_ref[...], tq)


def nsa_sample_attention(q, gn, ck, cv, win_k, win_v, ks_new, vs_new, kw_new, vw_new,
                         pool_k, pool_v, page_table, *, tq=SUBLANES):
    DB, DS, _ = q.shape
    n_pages = page_table.shape[1]
    past = n_pages * PAGE_SIZE
    n_chunks = ck.shape[1]
    wb = win_k.shape[1]
    n_slc = -(-(past + DS) // SLC_BLK)
    assert DS <= tq and DS < CMP_STRIDE and past // SLC_BLK <= LANES and wb % SUBLANES == 0
    assert past % min(SAMPLE_KEY_CHUNK, past) == 0
    ncol = -(-n_slc // LANES) * LANES
    pad_t = lambda t: jnp.pad(t, ((0, 0), (0, tq - DS), (0, 0)))
    gn_p = jnp.pad(gn, ((0, 0), (0, tq - DS), (0, LANES - gn.shape[-1])))
    kb = np.arange(past)[:, None] // SLC_BLK
    ebig = jnp.asarray(np.where(kb == np.arange(LANES)[None, :], NEG_INF, 0.0), jnp.bfloat16)
    wcs = jnp.asarray(cmp_to_slc_matrix(n_chunks, ncol), jnp.bfloat16)
    gexp = gate_expand_matrix()
    per_b = lambda b, pt: (b, 0, 0)
    const = lambda b, pt: (0, 0)
    blk3 = lambda n, w: pl.BlockSpec((1, n, w), per_b)
    grid_spec = pltpu.PrefetchScalarGridSpec(
        num_scalar_prefetch=1,
        grid=(DB,),
        in_specs=[
            blk3(tq, NSA_WIDTH), blk3(tq, LANES), blk3(n_chunks, KV_WIDTH), blk3(n_chunks, KV_WIDTH),
            blk3(wb, KV_WIDTH), blk3(wb, KV_WIDTH),
            blk3(tq, KV_WIDTH), blk3(tq, KV_WIDTH), blk3(tq, KV_WIDTH), blk3(tq, KV_WIDTH),
            pl.BlockSpec(memory_space=pl.ANY), pl.BlockSpec(memory_space=pl.ANY),
            pl.BlockSpec(ebig.shape, const, pipeline_mode=pl.Buffered(1)),
            pl.BlockSpec(wcs.shape, const, pipeline_mode=pl.Buffered(1)),
            pl.BlockSpec(gexp.shape, const, pipeline_mode=pl.Buffered(1)),
        ],
        out_specs=[blk3(tq, NSA_WIDTH), blk3(wb, KV_WIDTH), blk3(wb, KV_WIDTH)],
        scratch_shapes=[
            pltpu.VMEM((2, 2, past, KV_WIDTH), jnp.float32),
            pltpu.SemaphoreType.DMA((2, 2)),
        ],
    )
    o, nwk, nwv = pl.pallas_call(
        functools.partial(_nsa_sample_kernel, n_pages=n_pages, n_new=DS, tq=tq),
        grid_spec=grid_spec,
        out_shape=[jax.ShapeDtypeStruct((DB, tq, NSA_WIDTH), jnp.float32),
                   jax.ShapeDtypeStruct((DB, wb, KV_WIDTH), jnp.float32),
                   jax.ShapeDtypeStruct((DB, wb, KV_WIDTH), jnp.float32)],
        compiler_params=pltpu.CompilerParams(
            dimension_semantics=("arbitrary",), vmem_limit_bytes=PAGED_VMEM_LIMIT_BYTES),
        name="nsa_sample",
    )(page_table, pad_t(q), gn_p, ck, cv, win_k, win_v, pad_t(ks_new), pad_t(vs_new), pad_t(kw_new),
      pad_t(vw_new), pool_k, pool_v, ebig, wcs, gexp)
    return o[:, :DS], nwk, nwv


def _rms(x, g):
    return x * lax.rsqrt(jnp.mean(x * x, axis=-1, keepdims=True) + NORM_EPS) * g


def _ffn_kernel(x_ref, gf_ref, wg_ref, wu_ref, wd_ref, gfin_ref, y_ref, acc_ref):
    x = x_ref[...]
    h = _rms(x, gf_ref[...]).astype(jnp.bfloat16)
    acc_ref[...] = x
    for c in range(D_FF // FF_CHUNK):
        sl = slice(c * FF_CHUNK, (c + 1) * FF_CHUNK)
        a = _dot(h, wg_ref[:, sl])
        b = _dot(h, wu_ref[:, sl])
        t = (a * jax.nn.sigmoid(a) * b).astype(jnp.bfloat16)
        acc_ref[...] += _dot(t, wd_ref[sl, :])
    y_ref[...] = _rms(acc_ref[...], gfin_ref[...])


def ffn_final(x, g_ffn, w_gate, w_up, w_down, g_final):
    m = x.shape[0]
    tm = min(ROW_TILE, m)
    const = lambda i: (0, 0)
    wspec = lambda shape: pl.BlockSpec(shape, const, pipeline_mode=pl.Buffered(1))
    return pl.pallas_call(
        _ffn_kernel,
        grid=(m // tm,),
        in_specs=[
            pl.BlockSpec((tm, D_MODEL), lambda i: (i, 0)),
            wspec((1, D_MODEL)),
            wspec((D_MODEL, D_FF)),
            wspec((D_MODEL, D_FF)),
            wspec((D_FF, D_MODEL)),
            wspec((1, D_MODEL)),
        ],
        out_specs=pl.BlockSpec((tm, D_MODEL), lambda i: (i, 0)),
        out_shape=jax.ShapeDtypeStruct((m, D_MODEL), jnp.float32),
        scratch_shapes=[pltpu.VMEM((tm, D_MODEL), jnp.float32)],
        compiler_params=pltpu.CompilerParams(
            dimension_semantics=("arbitrary",), vmem_limit_bytes=VMEM_LIMIT_BYTES),
        name="ffn_final",
    )(x, g_ffn.reshape(1, -1), w_gate.astype(jnp.bfloat16), w_up.astype(jnp.bfloat16),
      w_down.astype(jnp.bfloat16), g_final.reshape(1, -1))


def rms_norm(x, g):
    xf = x.astype(jnp.float32)
    y = xf * lax.rsqrt(jnp.mean(xf * xf, axis=-1, keepdims=True) + NORM_EPS)
    return (y * g.astype(jnp.float32)).astype(x.dtype)


def layer_norm(x, g, b):
    xf = x.astype(jnp.float32)
    xc = xf - jnp.mean(xf, axis=-1, keepdims=True)
    var = jnp.mean(xc * xc, axis=-1, keepdims=True)
    return (xc * lax.rsqrt(var + NORM_EPS) * g.astype(jnp.float32) + b.astype(jnp.float32)).astype(x.dtype)


def rope(x, pos):
    half = HEAD_DIM // 2
    inv_freq = ROPE_THETA ** (-jnp.arange(half, dtype=jnp.float32) / half)
    ang = pos.astype(jnp.float32)[:, None] * inv_freq[None, :]
    shape = (pos.shape[0],) + (1,) * (x.ndim - 3) + (half,)
    cos, sin = jnp.cos(ang).reshape(shape), jnp.sin(ang).reshape(shape)
    xf = x.astype(jnp.float32)
    x1, x2 = xf[..., :half], xf[..., half:]
    return jnp.concatenate([x1 * cos - x2 * sin, x2 * cos + x1 * sin], axis=-1).astype(x.dtype)


def mixer_inputs(h, w_in, pos):
    B, S, _ = h.shape
    q, kc, vc, ks, vs, kw, vw, gn, u, v, ga, gb = jnp.split(h @ w_in, IN_SPLITS, axis=-1)
    heads = lambda t: t.reshape(B, S, N_KV_HEADS, HEAD_DIM)
    q = rope(q.reshape(B, S, N_KV_HEADS, GROUP, HEAD_DIM), pos)
    gn = jax.nn.sigmoid(gn.reshape(B, S, N_KV_HEADS, GROUP, 3))
    return (q, rope(heads(kc), pos), heads(vc), rope(heads(ks), pos), heads(vs),
            rope(heads(kw), pos), heads(vw), gn, jax.nn.gelu(u), jax.nn.gelu(v), ga, gb)


def gmlp_spatial(u, v, ln_g, ln_b, ws, bs):
    B, S, _ = u.shape
    L = min(S, GMLP_CHUNK)
    vn = layer_norm(v, ln_g, ln_b).reshape(B, S // L, L, GMLP_GROUPS, GMLP_GROUP_DIM)
    w = jnp.tril(ws[:, :L, :L])
    s = jnp.einsum('gij,bcjgd->bcigd', w, vn) + bs[:, :L].T[None, None, :, :, None]
    return u * s.reshape(B, S, GMLP_WIDTH)


def merge_branches(o_a, o_b, ga, gb, w_a, w_b, w_out):
    m = jax.nn.sigmoid(ga) * (o_a @ w_a) + jax.nn.sigmoid(gb) * (o_b @ w_b)
    return m @ w_out


def memory_kv(mem, g_mem, w_ck, w_cv):
    B, M, _ = mem.shape
    hm = rms_norm(mem, g_mem)
    return ((hm @ w_ck).reshape(B, M, CROSS_HEADS, CROSS_HEAD_DIM),
            (hm @ w_cv).reshape(B, M, CROSS_HEADS, CROSS_HEAD_DIM))


def cross_attend(h, mk, mv, w_cq, w_co):
    B, S, _ = h.shape
    q = (h @ w_cq).reshape(B, S, CROSS_HEADS, CROSS_HEAD_DIM)
    s = jnp.einsum('bqhd,bmhd->bhqm', q, mk).astype(jnp.float32) * CROSS_SCALE
    p = jax.nn.softmax(s, axis=-1).astype(mv.dtype)
    return jnp.einsum('bhqm,bmhd->bqhd', p, mv).reshape(B, S, D_MODEL) @ w_co


def kernel(x_prompt, x_sample, mem_prompt, cache_cmp_k, cache_cmp_v, cache_slc_k, cache_slc_v,
           page_table, state_win_k, state_win_v, cache_mem_k, cache_mem_v,
           g_mix, w_in, cmp_pe, cmp_w1, cmp_w2, gmlp_ln_g, gmlp_ln_b, gmlp_ws, gmlp_bs,
           w_branch_a, w_branch_b, w_out, g_cross, g_mem, w_cq, w_ck, w_cv, w_co,
           g_ffn, w_gate, w_up, w_down, g_final):
    B, S, _ = x_prompt.shape
    DB, DS, _ = x_sample.shape
    n_pages = page_table.shape[1]
    pos_p = jnp.arange(S, dtype=jnp.int32)
    pos_s = n_pages * PAGE_SIZE + jnp.arange(DS, dtype=jnp.int32)
    wb_p = min(WINDOW, S)
    xp, xs = x_prompt, x_sample
    l = 0
    gm = (gmlp_ln_g[l], gmlp_ln_b[l], gmlp_ws[l], gmlp_bs[l])
    mw = (w_branch_a[l], w_branch_b[l], w_out[l])
    cmpw = cmp_weights(cmp_pe[l], cmp_w1[l], cmp_w2[l])
    rows = lambda t: t.reshape(t.shape[0], t.shape[1], KV_WIDTH)
    as_pool = lambda t: t.reshape(-1, PAGE_SIZE, KV_WIDTH)

    q, kc, vc, ks, vs, kw, vw, gn, u, v, ga, gb = mixer_inputs(rms_norm(xp, g_mix[l]), w_in[l], pos_p)
    pt_p = jnp.arange(B * S // PAGE_SIZE, dtype=jnp.int32).reshape(B, S // PAGE_SIZE)
    ck_p, cv_p = cmp_blocks(as_pool(kc), as_pool(vc), pt_p, *cmpw)
    o_a = nsa_prompt_attention(q.reshape(B, S, NSA_WIDTH), gn.reshape(B, S, N_HEADS * N_BRANCH), ck_p, cv_p,
                               rows(ks), rows(vs), rows(kw), rows(vw))
    xp = xp + merge_branches(o_a, gmlp_spatial(u, v, *gm), ga, gb, *mw)
    mk, mv = memory_kv(mem_prompt, g_mem[l], w_ck[l], w_cv[l])
    xp = xp + cross_attend(rms_norm(xp, g_cross[l]), mk, mv, w_cq[l], w_co[l])
    prompt_new = (kc, vc, ks, vs, kw[:, S - wb_p:], vw[:, S - wb_p:], mk, mv)

    q, kc, vc, ks, vs, kw, vw, gn, u, v, ga, gb = mixer_inputs(rms_norm(xs, g_mix[l]), w_in[l], pos_s)
    ck_s, cv_s = cmp_blocks(as_pool(cache_cmp_k[l]), as_pool(cache_cmp_v[l]), page_table, *cmpw)
    o_a, win_k, win_v = nsa_sample_attention(
        q.reshape(DB, DS, NSA_WIDTH), gn.reshape(DB, DS, N_HEADS * N_BRANCH), ck_s, cv_s,
        rows(state_win_k[l]), rows(state_win_v[l]), rows(ks), rows(vs), rows(kw), rows(vw),
        as_pool(cache_slc_k[l]), as_pool(cache_slc_v[l]), page_table)
    xs = xs + merge_branches(o_a, gmlp_spatial(u, v, *gm), ga, gb, *mw)
    xs = xs + cross_attend(rms_norm(xs, g_cross[l]), cache_mem_k[l], cache_mem_v[l], w_cq[l], w_co[l])
    heads = lambda t: t.reshape(t.shape[0], t.shape[1], N_KV_HEADS, HEAD_DIM)
    sample_new = (kc, vc, ks, vs, heads(win_k), heads(win_v), v)

    ffn = functools.partial(ffn_final, g_ffn=g_ffn[l], w_gate=w_gate[l], w_up=w_up[l],
                            w_down=w_down[l], g_final=g_final)
    y_prompt = ffn(xp.reshape(-1, D_MODEL)).reshape(xp.shape)
    y_sample = ffn(xs.reshape(-1, D_MODEL)).reshape(xs.shape)
    outs = prompt_new + sample_new
    return (y_prompt, y_sample) + tuple(t[None] for t in outs)
```

```python
import functools

import jax
import jax.numpy as jnp
import numpy as np
from jax import lax
from jax.experimental import pallas as pl
from jax.experimental.pallas import tpu as pltpu

D_MODEL = 1024
N_HEADS = 8
HEAD_DIM = 64
N_KV_HEADS = 2
GROUP = N_HEADS // N_KV_HEADS
NSA_WIDTH = N_HEADS * HEAD_DIM
KV_WIDTH = N_KV_HEADS * HEAD_DIM
CMP_LEN = 32
CMP_STRIDE = 16
CMP_HIDDEN = 128
SLC_BLK = 64
SLC_SHIFT = 6
TOP_N = 16
WINDOW = 512
ROPE_THETA = 10000.0
ATTN_SCALE = HEAD_DIM ** -0.5
FORCE_BONUS = 1e4
NEG_INF = -1e30
N_BRANCH = 3
GMLP_WIDTH = 512
GMLP_GROUPS = 8
GMLP_GROUP_DIM = GMLP_WIDTH // GMLP_GROUPS
GMLP_CHUNK = 128
N_MEM = 256
CROSS_HEADS = 4
CROSS_HEAD_DIM = D_MODEL // CROSS_HEADS
CROSS_SCALE = CROSS_HEAD_DIM ** -0.5
D_FF = -(-8 * D_MODEL // (3 * 256)) * 256
NORM_EPS = 1e-6
PAGE_SIZE = 128
CHUNKS_PER_PAGE = PAGE_SIZE // CMP_STRIDE
N_KV_GROUPS = 6

LANES = 128
SUBLANES = 8
VMEM_LIMIT_BYTES = 48 * 1024 * 1024
PAGED_VMEM_LIMIT_BYTES = 56 * 1024 * 1024
SEL_TILE = 512
SAMPLE_KEY_CHUNK = 2048
ROW_TILE = 512
FF_CHUNK = 256
REST_WIDTHS = (LANES, GMLP_WIDTH, GMLP_WIDTH, D_MODEL, D_MODEL)

_NT = (((1,), (1,)), ((), ()))


def _dot_nt(a, b):
    return lax.dot_general(a, b, _NT, preferred_element_type=jnp.float32)


def _dot(a, b):
    return jnp.dot(a, b, preferred_element_type=jnp.float32)


def _split_bf16(x):
    hi = x.astype(jnp.bfloat16)
    lo = (x - hi.astype(jnp.float32)).astype(jnp.bfloat16)
    return hi, lo


def _iota(shape, dim):
    return lax.broadcasted_iota(jnp.int32, shape, dim)


def _rms(x, g):
    return x * lax.rsqrt(jnp.mean(x * x, axis=-1, keepdims=True) + NORM_EPS) * g


def _cmp_from_transposed(xt_ref, p, xrows, wbig_ref, w2big_ref, peterm_ref, out_ref, n_rows):
    n_chunks = n_rows // CMP_STRIDE
    for j in range(n_rows // LANES):
        xrows[j * LANES:(j + 1) * LANES, :] = xt_ref[:, j * LANES:(j + 1) * LANES].T
    acc = jnp.zeros((n_chunks, 4 * CMP_HIDDEN), jnp.float32)
    for sp in range(CMP_STRIDE // 2):
        a0 = xrows[pl.ds(2 * sp, n_chunks, stride=CMP_STRIDE), :]
        a1 = xrows[pl.ds(2 * sp + 1, n_chunks, stride=CMP_STRIDE), :]
        lhs = jnp.concatenate([a0, a1], axis=1).astype(jnp.bfloat16)
        acc = acc + _dot(lhs, wbig_ref[p, sp])
    lead = acc[:, :2 * CMP_HIDDEN]
    trail_next = pltpu.roll(acc[:, 2 * CMP_HIDDEN:], n_chunks - 1, axis=0)
    pre = lead + trail_next + peterm_ref[p]
    out = _dot(jax.nn.gelu(pre).astype(jnp.bfloat16), w2big_ref[p])
    out_ref[0] = out.astype(out_ref.dtype)


def _page_gather(pt_ref, pools, buf, sem, n_pages):
    b = pl.program_id(0)
    nb = pl.num_programs(0)
    slot = b % 2

    def page_copy(bb, sl, j, p):
        return pltpu.make_async_copy(pools[p].at[pt_ref[bb, j]],
                                     buf.at[sl, p, :, pl.ds(j * PAGE_SIZE, PAGE_SIZE)], sem.at[sl, p])

    def start(bb, sl):
        for p in range(len(pools)):
            for j in range(n_pages):
                page_copy(bb, sl, j, p).start()

    @pl.when(b == 0)
    def _():
        start(0, 0)

    @pl.when(b + 1 < nb)
    def _():
        start(b + 1, 1 - slot)

    def wait():
        for p in range(len(pools)):
            for j in range(n_pages):
                page_copy(b, slot, j, p).wait()

    return wait


def _cmp_paged_kernel(pt_ref, pool_k, pool_v, wbig_ref, w2big_ref, peterm_ref, ck_ref, cv_ref,
                      buf, xrows, sem, *, n_pages):
    wait = _page_gather(pt_ref, (pool_k, pool_v), buf, sem, n_pages)
    wait()
    slot = pl.program_id(0) % 2
    for p, out_ref in ((0, ck_ref), (1, cv_ref)):
        _cmp_from_transposed(buf.at[slot, p], p, xrows, wbig_ref, w2big_ref, peterm_ref, out_ref,
                             n_pages * PAGE_SIZE)


def _cmp_dense_kernel(kt_ref, vt_ref, wbig_ref, w2big_ref, peterm_ref, ck_ref, cv_ref, xrows):
    for p, (src, out_ref) in enumerate(((kt_ref, ck_ref), (vt_ref, cv_ref))):
        _cmp_from_transposed(src.at[0], p, xrows, wbig_ref, w2big_ref, peterm_ref, out_ref, src.shape[2])


def cmp_weights(cmp_pe, cmp_w1, cmp_w2):
    eye = jnp.eye(N_KV_HEADS, dtype=jnp.float32)
    w1r = cmp_w1.reshape(2, 2, CMP_STRIDE // 2, 2, HEAD_DIM, CMP_HIDDEN)
    wbig = jnp.einsum('jtpsdf,hk->jpshdtkf', w1r, eye)
    wbig = wbig.reshape(2, CMP_STRIDE // 2, 2 * KV_WIDTH, 4 * CMP_HIDDEN).astype(jnp.bfloat16)
    w2big = jnp.einsum('jfd,hk->jhfkd', cmp_w2, eye).reshape(2, 2 * CMP_HIDDEN, KV_WIDTH)
    pe_term = jnp.einsum('jsd,jsdf->jf', cmp_pe, cmp_w1)
    peterm = jnp.tile(pe_term[:, None, :], (1, 1, N_KV_HEADS))
    return wbig, w2big.astype(jnp.bfloat16), peterm


def _cmp_weight_specs(wbig, w2big, peterm, nargs):
    c3 = (lambda b: (0, 0, 0)) if nargs == 1 else (lambda b, pt: (0, 0, 0))
    c4 = (lambda b: (0, 0, 0, 0)) if nargs == 1 else (lambda b, pt: (0, 0, 0, 0))
    one = pl.Buffered(1)
    return [pl.BlockSpec(wbig.shape, c4, pipeline_mode=one),
            pl.BlockSpec(w2big.shape, c3, pipeline_mode=one),
            pl.BlockSpec(peterm.shape, c3, pipeline_mode=one)]


def cmp_blocks_paged(pool_k, pool_v, page_table, wbig, w2big, peterm):
    n_seq, n_pages = page_table.shape
    n_rows = n_pages * PAGE_SIZE
    n_chunks = n_rows // CMP_STRIDE
    out = jax.ShapeDtypeStruct((n_seq, n_chunks, KV_WIDTH), jnp.bfloat16)
    grid_spec = pltpu.PrefetchScalarGridSpec(
        num_scalar_prefetch=1,
        grid=(n_seq,),
        in_specs=[pl.BlockSpec(memory_space=pl.ANY), pl.BlockSpec(memory_space=pl.ANY)]
        + _cmp_weight_specs(wbig, w2big, peterm, 2),
        out_specs=[pl.BlockSpec((1, n_chunks, KV_WIDTH), lambda b, pt: (b, 0, 0))] * 2,
        scratch_shapes=[
            pltpu.VMEM((2, 2, KV_WIDTH, n_rows), jnp.float32),
            pltpu.VMEM((n_rows, KV_WIDTH), jnp.float32),
            pltpu.SemaphoreType.DMA((2, 2)),
        ],
    )
    return pl.pallas_call(
        functools.partial(_cmp_paged_kernel, n_pages=n_pages),
        grid_spec=grid_spec,
        out_shape=[out, out],
        compiler_params=pltpu.CompilerParams(
            dimension_semantics=("arbitrary",), vmem_limit_bytes=PAGED_VMEM_LIMIT_BYTES),
        name="cmp_blocks_paged",
    )(page_table, pool_k, pool_v, wbig, w2big, peterm)


def cmp_blocks_dense(kt, vt, wbig, w2big, peterm):
    B, _, S = kt.shape
    n_chunks = S // CMP_STRIDE
    out = jax.ShapeDtypeStruct((B, n_chunks, KV_WIDTH), jnp.bfloat16)
    per_b = lambda b: (b, 0, 0)
    return pl.pallas_call(
        _cmp_dense_kernel,
        grid=(B,),
        in_specs=[pl.BlockSpec((1, KV_WIDTH, S), per_b), pl.BlockSpec((1, KV_WIDTH, S), per_b)]
        + _cmp_weight_specs(wbig, w2big, peterm, 1),
        out_specs=[pl.BlockSpec((1, n_chunks, KV_WIDTH), per_b)] * 2,
        out_shape=[out, out],
        scratch_shapes=[pltpu.VMEM((S, KV_WIDTH), jnp.float32)],
        compiler_params=pltpu.CompilerParams(
            dimension_semantics=("arbitrary",), vmem_limit_bytes=VMEM_LIMIT_BYTES),
        name="cmp_blocks_dense",
    )(kt, vt, wbig, w2big, peterm)


def _padded_queries(q, tq):
    lane = _iota((tq, LANES), 1)
    blocks = []
    for h in range(N_KV_HEADS):
        for g in range(GROUP):
            c = h * GROUP + g
            pair = q[:, LANES * (c // 2):LANES * (c // 2 + 1)]
            if c % 2 != h:
                pair = pltpu.roll(pair, HEAD_DIM, axis=1)
            keep = (lane >= HEAD_DIM * h) & (lane < HEAD_DIM * (h + 1))
            blocks.append(jnp.where(keep, pair, 0.0))
    return jnp.concatenate(blocks, axis=0)


def _masked_softmax(s, valid):
    sm = jnp.where(valid, s, NEG_INF)
    mx = jnp.max(sm, axis=1, keepdims=True)
    e = jnp.where(valid, jnp.exp(sm - mx), 0.0)
    den = jnp.maximum(jnp.sum(e, axis=1, keepdims=True), 1e-30)
    return e / den


def _group_sum(p, tq):
    out = []
    for h in range(N_KV_HEADS):
        r0 = h * GROUP * tq
        acc = p[r0:r0 + tq]
        for g in range(1, GROUP):
            acc = acc + p[r0 + g * tq:r0 + (g + 1) * tq]
        out.append(acc)
    return out


def _combine(branches, gn, gexp_w, tq):
    hi, lo = _split_bf16(gn)
    gexp = _dot(hi, gexp_w) + _dot(lo, gexp_w)
    lane = _iota((tq, LANES), 1)
    cols = []
    for h in range(N_KV_HEADS):
        blks = []
        for g in range(GROUP):
            j = h * GROUP + g
            rows = slice(j * tq, (j + 1) * tq)
            o = None
            for br in range(N_BRANCH):
                gate = gexp[:, (j * N_BRANCH + br) * LANES:(j * N_BRANCH + br + 1) * LANES]
                term = gate * branches[br][rows]
                o = term if o is None else o + term
            blks.append(o)
        for jj in range(GROUP // 2):
            a, b2 = blks[2 * jj], blks[2 * jj + 1]
            if h == 0:
                b2 = pltpu.roll(b2, HEAD_DIM, axis=1)
            else:
                a = pltpu.roll(a, HEAD_DIM, axis=1)
            cols.append(jnp.where(lane < HEAD_DIM, a, b2))
    return jnp.concatenate(cols, axis=1)


def gate_expand_matrix():
    e = np.zeros((LANES, N_HEADS * N_BRANCH * LANES), np.float32)
    for j in range(N_HEADS * N_BRANCH):
        e[j, j * LANES:(j + 1) * LANES] = 1.0
    return jnp.asarray(e, jnp.bfloat16)


def cmp_to_slc_matrix(n_chunks, n_cols):
    cs = np.arange(n_chunks)[:, None] * CMP_STRIDE
    ss = np.arange(n_cols)[None, :] * SLC_BLK
    shared = np.minimum(cs + CMP_LEN, ss + SLC_BLK) - np.maximum(cs, ss)
    w = np.maximum(shared, 0).astype(np.float32) / CMP_LEN
    w[n_chunks - 1] = 0.0
    return w


def _nsa_prompt_kernel(q_ref, gn_ref, ck_ref, cv_ref, ks_ref, vs_ref, kw_ref, vw_ref,
                       wcst_ref, gexp_ref, o_ref, lhs_ref, m_ref, l_ref, acc_ref, *, tq, n_chunks):
    i = pl.program_id(1)
    q0 = i * tq
    qp = _padded_queries(q_ref[0] * ATTN_SCALE, tq)
    qpb = qp.astype(jnp.bfloat16)

    def qpos_of(shape):
        return q0 + (_iota(shape, 0) & (tq - 1))

    s = _dot_nt(qpb, ck_ref[0])
    n_idx = _iota(s.shape, 1)
    valid = (n_idx * CMP_STRIDE + (CMP_LEN - 1) <= qpos_of(s.shape)) & (n_idx < n_chunks - 1)
    p_cmp = _masked_softmax(s, valid)
    o_cmp = _dot(p_cmp.astype(jnp.bfloat16), cv_ref[0])

    blk = _iota((LANES, tq), 0)
    blkf = blk.astype(jnp.float32)
    qblk = (q0 + _iota((LANES, tq), 1)) >> SLC_SHIFT
    forced = (blk == 0) | (blk == qblk) | (blk == qblk - 1)
    wcst = wcst_ref[...]
    nsel = []
    for psum in _group_sum(p_cmp, tq):
        hi, lo = _split_bf16(psum)
        sc = _dot_nt(wcst, hi) + _dot_nt(wcst, lo)
        sc = jnp.where(blk <= qblk, jnp.where(forced, sc + FORCE_BONUS, sc), NEG_INF)
        sel = jnp.zeros_like(sc)
        for _ in range(TOP_N):
            mx = jnp.max(sc, axis=0, keepdims=True)
            first = jnp.min(jnp.where(sc == mx, blkf, float(LANES)), axis=0, keepdims=True)
            hit = blkf == first
            sel = jnp.where(hit, 1.0, sel)
            sc = jnp.where(hit, -jnp.inf, sc)
        nsel.append((1.0 - sel).T.astype(jnp.bfloat16))
    for h in range(N_KV_HEADS):
        r0 = h * GROUP * tq
        lhs_ref[r0:r0 + GROUP * tq, :LANES] = qpb[r0:r0 + GROUP * tq]
        for g in range(GROUP):
            lhs_ref[r0 + g * tq:r0 + (g + 1) * tq, LANES:] = nsel[h]

    m_ref[...] = jnp.full(m_ref.shape, -3e38, jnp.float32)
    l_ref[...] = jnp.zeros(l_ref.shape, jnp.float32)
    acc_ref[...] = jnp.zeros(acc_ref.shape, jnp.float32)
    n_tiles = (q0 + tq + SEL_TILE - 1) // SEL_TILE

    def sel_tile(kt, causal):
        k0 = pl.multiple_of(kt * SEL_TILE, SEL_TILE)
        kblk = (k0 + _iota((LANES, SEL_TILE), 1)) >> SLC_SHIFT
        ebig = jnp.where(kblk == _iota((LANES, SEL_TILE), 0), NEG_INF, 0.0).astype(jnp.bfloat16)
        rhs = jnp.concatenate([ks_ref[0, :, pl.ds(k0, SEL_TILE)], ebig], axis=0)
        st = _dot(lhs_ref[...], rhs)
        if causal:
            st = jnp.where(k0 + _iota(st.shape, 1) <= qpos_of(st.shape), st, NEG_INF)
        cols = [st[:, c * LANES:(c + 1) * LANES] for c in range(SEL_TILE // LANES)]
        mt = functools.reduce(jnp.maximum, cols)
        m_prev = m_ref[...]
        m_new = jnp.maximum(m_prev, jnp.max(mt, axis=1, keepdims=True))
        alpha = jnp.exp(m_prev - m_new)
        ps = [jnp.exp(c - m_new) for c in cols]
        l_ref[...] = alpha * l_ref[...] + functools.reduce(jnp.add, ps)
        pb = jnp.concatenate(ps, axis=1).astype(jnp.bfloat16)
        acc_ref[...] = alpha * acc_ref[...] + _dot_nt(pb, vs_ref[0, :, pl.ds(k0, SEL_TILE)])
        m_ref[...] = m_new

    def body(kt, carry):
        sel_tile(kt, False)
        return carry

    lax.fori_loop(0, n_tiles - 1, body, 0)
    sel_tile(n_tiles - 1, True)
    o_sel = acc_ref[...] / jnp.sum(l_ref[...], axis=1, keepdims=True)

    w0 = pl.multiple_of(jnp.maximum(q0 - WINDOW, 0), tq)
    sw = _dot(qpb, kw_ref[0, :, pl.ds(w0, WINDOW + tq)])
    kpos = w0 + _iota(sw.shape, 1)
    qpos = qpos_of(sw.shape)
    p_win = _masked_softmax(sw, (kpos <= qpos) & (kpos > qpos - WINDOW))
    o_win = _dot_nt(p_win.astype(jnp.bfloat16), vw_ref[0, :, pl.ds(w0, WINDOW + tq)])

    o_ref[0] = _combine((o_cmp, o_sel, o_win), gn_ref[0], gexp_ref[...], tq)


def nsa_prompt_attention(q, gn, ck, cv, ks, vs, kw, vw, *, tq=LANES):
    B, S, _ = q.shape
    n_chunks = ck.shape[1]
    assert S % SEL_TILE == 0 and S // SLC_BLK <= LANES and tq == LANES and S >= WINDOW + tq
    wcst = jnp.asarray(cmp_to_slc_matrix(n_chunks, LANES).T, jnp.bfloat16)
    gexp = gate_expand_matrix()
    rows = N_HEADS * tq
    per_b = lambda b, i: (b, 0, 0)
    tile = lambda b, i: (b, i, 0)
    const = lambda b, i: (0, 0)
    return pl.pallas_call(
        functools.partial(_nsa_prompt_kernel, tq=tq, n_chunks=n_chunks),
        grid=(B, S // tq),
        in_specs=[
            pl.BlockSpec((1, tq, NSA_WIDTH), tile),
            pl.BlockSpec((1, tq, LANES), tile),
            pl.BlockSpec((1, n_chunks, KV_WIDTH), per_b),
            pl.BlockSpec((1, n_chunks, KV_WIDTH), per_b),
            pl.BlockSpec((1, KV_WIDTH, S), per_b),
            pl.BlockSpec((1, KV_WIDTH, S), per_b),
            pl.BlockSpec((1, KV_WIDTH, S), per_b),
            pl.BlockSpec((1, KV_WIDTH, S), per_b),
            pl.BlockSpec(wcst.shape, const),
            pl.BlockSpec(gexp.shape, const),
        ],
        out_specs=pl.BlockSpec((1, tq, NSA_WIDTH), tile),
        out_shape=jax.ShapeDtypeStruct((B, S, NSA_WIDTH), jnp.float32),
        scratch_shapes=[
            pltpu.VMEM((rows, 2 * LANES), jnp.bfloat16),
            pltpu.VMEM((rows, LANES), jnp.float32),
            pltpu.VMEM((rows, LANES), jnp.float32),
            pltpu.VMEM((rows, LANES), jnp.float32),
        ],
        compiler_params=pltpu.CompilerParams(
            dimension_semantics=("arbitrary", "arbitrary"), vmem_limit_bytes=VMEM_LIMIT_BYTES),
        name="nsa_prompt",
    )(q, gn, ck, cv, ks, vs, kw, vw, wcst, gexp)


def _nsa_sample_kernel(pt_ref, q_ref, gn_ref, ck_ref, cv_ref, wk_ref, wv_ref,
                       ksn_ref, vsn_ref, kwn_ref, vwn_ref, pool_k, pool_v, ebig_ref, wcs_ref, gexp_ref,
                       o_ref, nwk_ref, nwv_ref, buf, sem, *, n_pages, n_new, tq):
    slot = pl.program_id(0) % 2
    past = n_pages * PAGE_SIZE
    n_chunks = n_pages * CHUNKS_PER_PAGE
    n_past_blk = past // SLC_BLK
    n_slc = -(-(past + n_new) // SLC_BLK)
    wb = wk_ref.shape[2]
    rows = N_HEADS * tq
    wait_pages = _page_gather(pt_ref, (pool_k, pool_v), buf, sem, n_pages)

    qp = _padded_queries(q_ref[0] * ATTN_SCALE, tq)
    qpb = qp.astype(jnp.bfloat16)

    def tok_of(shape):
        return _iota(shape, 0) & (tq - 1)

    def new_rows(ref):
        x = jnp.concatenate([ref[0], jnp.zeros((LANES - tq, KV_WIDTH), jnp.float32)], axis=0)
        return x.astype(jnp.bfloat16)

    s = _dot_nt(qpb, ck_ref[0])
    n_idx = _iota(s.shape, 1)
    valid = (n_idx * CMP_STRIDE + (CMP_LEN - 1) <= past + tok_of(s.shape)) & (n_idx < n_chunks - 1)
    p_cmp = _masked_softmax(s, valid)
    o_cmp = _dot(p_cmp.astype(jnp.bfloat16), cv_ref[0])

    ncol = wcs_ref.shape[1]
    blk = _iota((tq, ncol), 1)
    blkf = blk.astype(jnp.float32)
    qblk = (past + _iota((tq, ncol), 0)) >> SLC_SHIFT
    forced = (blk == 0) | (blk == qblk) | (blk == qblk - 1)
    wcs = wcs_ref[...]
    nsel, sel_new = [], []
    for psum in _group_sum(p_cmp, tq):
        hi, lo = _split_bf16(psum)
        sc = _dot(hi, wcs) + _dot(lo, wcs)
        sc = jnp.where(blk <= qblk, jnp.where(forced, sc + FORCE_BONUS, sc), NEG_INF)
        sc = jnp.where(blk < n_slc, sc, -jnp.inf)
        sel = jnp.zeros_like(sc)
        for _ in range(min(TOP_N, n_slc)):
            mx = jnp.max(sc, axis=1, keepdims=True)
            first = jnp.min(jnp.where(sc == mx, blkf, float(ncol)), axis=1, keepdims=True)
            hit = blkf == first
            sel = jnp.where(hit, 1.0, sel)
            sc = jnp.where(hit, -jnp.inf, sc)
        nsel.append(jnp.concatenate([1.0 - sel[:, :LANES]] * GROUP, axis=0).astype(jnp.bfloat16))
        sel_new.append(jnp.broadcast_to(sel[:, n_past_blk:n_past_blk + 1], (tq, LANES)))
    lhs = jnp.concatenate(
        [jnp.concatenate([qpb[h * GROUP * tq:(h + 1) * GROUP * tq], nsel[h]], axis=1)
         for h in range(N_KV_HEADS)], axis=0)
    sel_new_rows = jnp.concatenate([sel_new[h] for h in range(N_KV_HEADS) for _ in range(GROUP)], axis=0)

    tn = _iota((rows, LANES), 1)
    new_ok = (tn <= tok_of((rows, LANES))) & (tn < n_new)
    s_sn = jnp.where(new_ok & (sel_new_rows > 0.5), _dot_nt(qpb, new_rows(ksn_ref)), NEG_INF)
    s_wn = jnp.where(new_ok, _dot_nt(qpb, new_rows(kwn_ref)), NEG_INF)

    wk = wk_ref[0]
    wv = wv_ref[0]
    s_w = _dot(qpb, wk.astype(jnp.bfloat16))
    kwpos = past - wb + _iota(s_w.shape, 1)
    qpos = past + tok_of(s_w.shape)
    s_w = jnp.where((kwpos >= 0) & (kwpos <= qpos) & (kwpos > qpos - WINDOW), s_w, NEG_INF)
    mx = jnp.maximum(jnp.max(s_w, axis=1, keepdims=True), jnp.max(s_wn, axis=1, keepdims=True))
    e_w = jnp.exp(s_w - mx)
    e_wn = jnp.exp(s_wn - mx)
    den = jnp.sum(e_w, axis=1, keepdims=True) + jnp.sum(e_wn, axis=1, keepdims=True)
    o_win = (_dot_nt(e_w.astype(jnp.bfloat16), wv.astype(jnp.bfloat16))
             + _dot(e_wn.astype(jnp.bfloat16), new_rows(vwn_ref))) / den

    col = _iota((KV_WIDTH, wb), 1)
    for src, new_ref, dst in ((wk, kwn_ref, nwk_ref), (wv, vwn_ref, nwv_ref)):
        new_t = jnp.concatenate([new_ref[0], jnp.zeros((LANES - tq, KV_WIDTH), jnp.float32)], axis=0).T
        tail = jnp.concatenate([jnp.zeros((KV_WIDTH, wb - LANES), jnp.float32), new_t], axis=1)
        dst[0] = jnp.where(col >= wb - n_new, pltpu.roll(tail, LANES - n_new, axis=1),
                           pltpu.roll(src, wb - n_new, axis=1))

    wait_pages()
    ck_keys = min(SAMPLE_KEY_CHUNK, past)
    s_chunks = []
    for c in range(past // ck_keys):
        kc = buf[slot, 0, :, pl.ds(c * ck_keys, ck_keys)].astype(jnp.bfloat16)
        rhs = jnp.concatenate([kc, ebig_ref[:, pl.ds(c * ck_keys, ck_keys)]], axis=0)
        s_chunks.append(_dot(lhs, rhs))
    mx = jnp.max(s_sn, axis=1, keepdims=True)
    for sc in s_chunks:
        mx = jnp.maximum(mx, jnp.max(sc, axis=1, keepdims=True))
    e_sn = jnp.exp(s_sn - mx)
    den = jnp.sum(e_sn, axis=1, keepdims=True)
    o_sel = _dot(e_sn.astype(jnp.bfloat16), new_rows(vsn_ref))
    for c, sc in enumerate(s_chunks):
        e = jnp.exp(sc - mx)
        den = den + jnp.sum(e, axis=1, keepdims=True)
        vc = buf[slot, 1, :, pl.ds(c * ck_keys, ck_keys)].astype(jnp.bfloat16)
        o_sel = o_sel + _dot_nt(e.astype(jnp.bfloat16), vc)
    o_sel = o_sel / den

    o_ref[0] = _combine((o_cmp, o_sel, o_win), gn_ref[0], gexp_ref[...], tq)


def nsa_sample_attention(q, gn, ck, cv, win_k, win_v, ks_new, vs_new, kw_new, vw_new,
                         pool_k, pool_v, page_table, *, tq=SUBLANES):
    DB, DS, _ = q.shape
    n_pages = page_table.shape[1]
    past = n_pages * PAGE_SIZE
    n_chunks = ck.shape[1]
    wb = win_k.shape[2]
    n_slc = -(-(past + DS) // SLC_BLK)
    assert DS <= tq and DS < CMP_STRIDE and past // SLC_BLK <= LANES and wb % LANES == 0
    assert past % min(SAMPLE_KEY_CHUNK, past) == 0
    ncol = -(-n_slc // LANES) * LANES
    pad_t = lambda t: jnp.pad(t, ((0, 0), (0, tq - DS), (0, 0)))
    kb = np.arange(past)[None, :] // SLC_BLK
    ebig = jnp.asarray(np.where(kb == np.arange(LANES)[:, None], NEG_INF, 0.0), jnp.bfloat16)
    wcs = jnp.asarray(cmp_to_slc_matrix(n_chunks, ncol), jnp.bfloat16)
    gexp = gate_expand_matrix()
    per_b = lambda b, pt: (b, 0, 0)
    const = lambda b, pt: (0, 0)
    blk3 = lambda n, w: pl.BlockSpec((1, n, w), per_b)
    grid_spec = pltpu.PrefetchScalarGridSpec(
        num_scalar_prefetch=1,
        grid=(DB,),
        in_specs=[
            blk3(tq, NSA_WIDTH), blk3(tq, LANES), blk3(n_chunks, KV_WIDTH), blk3(n_chunks, KV_WIDTH),
            blk3(KV_WIDTH, wb), blk3(KV_WIDTH, wb),
            blk3(tq, KV_WIDTH), blk3(tq, KV_WIDTH), blk3(tq, KV_WIDTH), blk3(tq, KV_WIDTH),
            pl.BlockSpec(memory_space=pl.ANY), pl.BlockSpec(memory_space=pl.ANY),
            pl.BlockSpec(ebig.shape, const, pipeline_mode=pl.Buffered(1)),
            pl.BlockSpec(wcs.shape, const, pipeline_mode=pl.Buffered(1)),
            pl.BlockSpec(gexp.shape, const, pipeline_mode=pl.Buffered(1)),
        ],
        out_specs=[blk3(tq, NSA_WIDTH), blk3(KV_WIDTH, wb), blk3(KV_WIDTH, wb)],
        scratch_shapes=[
            pltpu.VMEM((2, 2, KV_WIDTH, past), jnp.float32),
            pltpu.SemaphoreType.DMA((2, 2)),
        ],
    )
    o, nwk, nwv = pl.pallas_call(
        functools.partial(_nsa_sample_kernel, n_pages=n_pages, n_new=DS, tq=tq),
        grid_spec=grid_spec,
        out_shape=[jax.ShapeDtypeStruct((DB, tq, NSA_WIDTH), jnp.float32),
                   jax.ShapeDtypeStruct((DB, KV_WIDTH, wb), jnp.float32),
                   jax.ShapeDtypeStruct((DB, KV_WIDTH, wb), jnp.float32)],
        compiler_params=pltpu.CompilerParams(
            dimension_semantics=("arbitrary",), vmem_limit_bytes=PAGED_VMEM_LIMIT_BYTES),
        name="nsa_sample",
    )(page_table, pad_t(q), pad_t(gn), ck, cv, win_k, win_v, pad_t(ks_new), pad_t(vs_new), pad_t(kw_new),
      pad_t(vw_new), pool_k, pool_v, ebig, wcs, gexp)
    return o[:, :DS], nwk, nwv


def rope_tables(pos):
    half = HEAD_DIM // 2
    inv_freq = ROPE_THETA ** (-jnp.arange(half, dtype=jnp.float32) / half)
    ang = pos.astype(jnp.float32)[:, None] * inv_freq[None, :]
    cos, sin = jnp.cos(ang), jnp.sin(ang)
    cos_t = jnp.tile(cos, (1, LANES // half))
    sin_t = jnp.tile(jnp.concatenate([-sin, sin], axis=1), (1, LANES // HEAD_DIM))
    return cos_t, sin_t


def _rope_lanes(y, cos, sin):
    first_half = (_iota((y.shape[0], LANES), 1) & (HEAD_DIM - 1)) < HEAD_DIM // 2
    cols = []
    for c in range(y.shape[1] // LANES):
        yc = y[:, c * LANES:(c + 1) * LANES]
        rot = jnp.where(first_half, pltpu.roll(yc, LANES - HEAD_DIM // 2, axis=1),
                        pltpu.roll(yc, HEAD_DIM // 2, axis=1))
        cols.append(yc * cos + rot * sin)
    return cols[0] if len(cols) == 1 else jnp.concatenate(cols, axis=1)


def _rope_sublanes(y, cos_t, sin_t):
    hh = HEAD_DIM // 2
    rot = jnp.concatenate([y[hh:2 * hh], y[:hh], y[3 * hh:], y[2 * hh:3 * hh]], axis=0)
    return y * cos_t + rot * sin_t


def _proj_kernel(x_ref, g_ref, wq_ref, wkv_ref, wrest_ref, cos_ref, sin_ref, cost_ref, sint_ref, *out_refs,
                 kv_transposed):
    h = _rms(x_ref[...], g_ref[...]).astype(jnp.bfloat16)
    q_ref = out_refs[0]
    kv_refs = out_refs[1:1 + N_KV_GROUPS]
    pos = 1 + N_KV_GROUPS
    if kv_transposed:
        kvb_refs = out_refs[pos:pos + 4]
        pos += 4
    rest_refs = out_refs[pos:]
    cos, sin = cos_ref[...], sin_ref[...]
    q_ref[...] = _rope_lanes(_dot(h, wq_ref[...]), cos, sin)
    if kv_transposed:
        yt = _dot_nt(wkv_ref[...], h)
        for gi in range(N_KV_GROUPS):
            blk = yt[gi * KV_WIDTH:(gi + 1) * KV_WIDTH]
            if gi % 2 == 0:
                blk = _rope_sublanes(blk, cost_ref[...], sint_ref[...])
            kv_refs[gi][0] = blk
            if gi >= 2:
                kvb_refs[gi - 2][0] = blk.astype(jnp.bfloat16)
    else:
        y = _dot(h, wkv_ref[...])
        for gi in range(N_KV_GROUPS):
            blk = y[:, gi * KV_WIDTH:(gi + 1) * KV_WIDTH]
            if gi % 2 == 0:
                blk = _rope_lanes(blk, cos, sin)
            kv_refs[gi][...] = blk
    acts = (jax.nn.sigmoid, jax.nn.gelu, jax.nn.gelu, None, None)
    c0 = 0
    for ref, width, act in zip(rest_refs, REST_WIDTHS, acts):
        for c in range(0, width, GMLP_WIDTH):
            w = min(GMLP_WIDTH, width - c)
            y = _dot(h, wrest_ref[:, c0 + c:c0 + c + w])
            ref[:, c:c + w] = y if act is None else act(y)
        c0 += width


def proj_weights(w_in, kv_transposed):
    wq = w_in[:, :NSA_WIDTH]
    wkv = w_in[:, NSA_WIDTH:NSA_WIDTH + N_KV_GROUPS * KV_WIDTH]
    c = NSA_WIDTH + N_KV_GROUPS * KV_WIDTH
    n_gate = N_HEADS * N_BRANCH
    wgn = jnp.pad(w_in[:, c:c + n_gate], ((0, 0), (0, LANES - n_gate)))
    wrest = jnp.concatenate([wgn, w_in[:, c + n_gate:]], axis=1)
    if kv_transposed:
        wkv = wkv.T
    return wq.astype(jnp.bfloat16), wkv.astype(jnp.bfloat16), wrest.astype(jnp.bfloat16)


def mixer_proj(x, g, w_in, pos, *, batch=None):
    M = x.shape[0]
    tm = min(ROW_TILE, M)
    kv_t = batch is not None
    S = M // batch if kv_t else M
    assert M % tm == 0 and S % tm == 0
    n_s = S // tm
    wq, wkv, wrest = proj_weights(w_in, kv_t)
    cos, sin = rope_tables(pos)
    one = pl.Buffered(1)
    const = lambda i: (0, 0)
    row = lambda i: (i, 0)
    tab = lambda i: (i % n_s, 0)
    tab_t = lambda i: (0, i % n_s)
    f32 = jnp.float32
    out_shape = [jax.ShapeDtypeStruct((M, NSA_WIDTH), f32)]
    out_specs = [pl.BlockSpec((tm, NSA_WIDTH), row)]
    if kv_t:
        kvt = lambda i: (i // n_s, 0, i % n_s)
        out_shape += [jax.ShapeDtypeStruct((batch, KV_WIDTH, S), f32)] * N_KV_GROUPS
        out_shape += [jax.ShapeDtypeStruct((batch, KV_WIDTH, S), jnp.bfloat16)] * 4
        out_specs += [pl.BlockSpec((1, KV_WIDTH, tm), kvt)] * (N_KV_GROUPS + 4)
    else:
        out_shape += [jax.ShapeDtypeStruct((M, KV_WIDTH), f32)] * N_KV_GROUPS
        out_specs += [pl.BlockSpec((tm, KV_WIDTH), row)] * N_KV_GROUPS
    for w in REST_WIDTHS:
        out_shape.append(jax.ShapeDtypeStruct((M, w), f32))
        out_specs.append(pl.BlockSpec((tm, w), row))
    return pl.pallas_call(
        functools.partial(_proj_kernel, kv_transposed=kv_t),
        grid=(M // tm,),
        in_specs=[
            pl.BlockSpec((tm, D_MODEL), row),
            pl.BlockSpec((1, D_MODEL), const, pipeline_mode=one),
            pl.BlockSpec(wq.shape, const, pipeline_mode=one),
            pl.BlockSpec(wkv.shape, const, pipeline_mode=one),
            pl.BlockSpec(wrest.shape, const, pipeline_mode=one),
            pl.BlockSpec((tm, LANES), tab),
            pl.BlockSpec((tm, LANES), tab),
            pl.BlockSpec((LANES, tm), tab_t),
            pl.BlockSpec((LANES, tm), tab_t),
        ],
        out_specs=out_specs,
        out_shape=out_shape,
        compiler_params=pltpu.CompilerParams(
            dimension_semantics=("arbitrary",), vmem_limit_bytes=VMEM_LIMIT_BYTES),
        name="mixer_proj",
    )(x, g.reshape(1, -1), wq, wkv, wrest, cos, sin, cos.T, sin.T)


def _merge_kernel(oa_ref, u_ref, v_ref, ga_ref, gb_ref, x_ref, lng_ref, lnb_ref, wsp_ref, bsx_ref,
                  wa_ref, wb_ref, wout_ref, gc_ref, wcq_ref, x1_ref, hq_ref):
    tm = x_ref.shape[0]
    v = v_ref[...]
    vc = v - jnp.mean(v, axis=-1, keepdims=True)
    var = jnp.mean(vc * vc, axis=-1, keepdims=True)
    vn = vc * lax.rsqrt(var + NORM_EPS) * lng_ref[...] + lnb_ref[...]
    lane = _iota((GMLP_CHUNK, LANES), 1)
    chunks = []
    for c in range(tm // GMLP_CHUNK):
        cols = []
        for pr in range(GMLP_GROUPS // 2):
            pair = vn[c * GMLP_CHUNK:(c + 1) * GMLP_CHUNK, pr * LANES:(pr + 1) * LANES]
            rhs = jnp.concatenate([jnp.where(lane < GMLP_GROUP_DIM, pair, 0.0),
                                   jnp.where(lane >= GMLP_GROUP_DIM, pair, 0.0)], axis=0).astype(jnp.bfloat16)
            cols.append(_dot(wsp_ref[pr], rhs))
        chunks.append(jnp.concatenate(cols, axis=1) + bsx_ref[...])
    o_b = u_ref[...] * jnp.concatenate(chunks, axis=0)
    m = (jax.nn.sigmoid(ga_ref[...]) * _dot(oa_ref[...].astype(jnp.bfloat16), wa_ref[...])
         + jax.nn.sigmoid(gb_ref[...]) * _dot(o_b.astype(jnp.bfloat16), wb_ref[...]))
    x1 = x_ref[...] + _dot(m.astype(jnp.bfloat16), wout_ref[...])
    x1_ref[...] = x1
    hq_ref[...] = _dot(_rms(x1, gc_ref[...]).astype(jnp.bfloat16), wcq_ref[...])


def gmlp_weights(ws, bs, chunk_len):
    L = chunk_len
    w = jnp.tril(ws[:, :L, :L])
    reps = GMLP_CHUNK // L
    w = jnp.einsum('gij,ab->gaibj', w, jnp.eye(reps, dtype=w.dtype)).reshape(GMLP_GROUPS, GMLP_CHUNK, GMLP_CHUNK)
    wsp = jnp.concatenate([w[0::2], w[1::2]], axis=2)
    bsx = jnp.tile(jnp.repeat(bs[:, :L].T, GMLP_GROUP_DIM, axis=1), (reps, 1))
    return wsp.astype(jnp.bfloat16), bsx


def merge_and_cross_q(o_a, u, v, ga, gb, x, ln_g, ln_b, wsp, bsx, w_a, w_b, w_out, g_cross, w_cq):
    M = x.shape[0]
    tm = min(ROW_TILE, M)
    one = pl.Buffered(1)
    row = lambda i: (i, 0)
    cw = lambda shape: pl.BlockSpec(shape, lambda i: (0,) * len(shape), pipeline_mode=one)
    b16 = lambda t: t.astype(jnp.bfloat16)
    out = jax.ShapeDtypeStruct((M, D_MODEL), jnp.float32)
    return pl.pallas_call(
        _merge_kernel,
        grid=(M // tm,),
        in_specs=[
            pl.BlockSpec((tm, NSA_WIDTH), row), pl.BlockSpec((tm, GMLP_WIDTH), row),
            pl.BlockSpec((tm, GMLP_WIDTH), row), pl.BlockSpec((tm, D_MODEL), row),
            pl.BlockSpec((tm, D_MODEL), row), pl.BlockSpec((tm, D_MODEL), row),
            cw((1, GMLP_WIDTH)), cw((1, GMLP_WIDTH)), cw(wsp.shape), cw(bsx.shape),
            cw(w_a.shape), cw(w_b.shape), cw(w_out.shape), cw((1, D_MODEL)), cw(w_cq.shape),
        ],
        out_specs=[pl.BlockSpec((tm, D_MODEL), row)] * 2,
        out_shape=[out, out],
        compiler_params=pltpu.CompilerParams(
            dimension_semantics=("arbitrary",), vmem_limit_bytes=VMEM_LIMIT_BYTES),
        name="merge_branches",
    )(o_a, u, v, ga, gb, x, ln_g.reshape(1, -1), ln_b.reshape(1, -1), wsp, bsx,
      b16(w_a), b16(w_b), b16(w_out), g_cross.reshape(1, -1), b16(w_cq))


def _memkv_kernel(x_ref, g_ref, wk_ref, wv_ref, k_ref, v_ref):
    h = _rms(x_ref[...], g_ref[...]).astype(jnp.bfloat16)
    k_ref[...] = _dot(h, wk_ref[...])
    v_ref[...] = _dot(h, wv_ref[...])


def memory_kv(mem, g_mem, w_ck, w_cv):
    M = mem.shape[0]
    tm = min(ROW_TILE, M)
    one = pl.Buffered(1)
    row = lambda i: (i, 0)
    cw = lambda shape: pl.BlockSpec(shape, lambda i: (0, 0), pipeline_mode=one)
    out = jax.ShapeDtypeStruct((M, D_MODEL), jnp.float32)
    return pl.pallas_call(
        _memkv_kernel,
        grid=(M // tm,),
        in_specs=[pl.BlockSpec((tm, D_MODEL), row), cw((1, D_MODEL)), cw(w_ck.shape), cw(w_cv.shape)],
        out_specs=[pl.BlockSpec((tm, D_MODEL), row)] * 2,
        out_shape=[out, out],
        compiler_params=pltpu.CompilerParams(
            dimension_semantics=("arbitrary",), vmem_limit_bytes=VMEM_LIMIT_BYTES),
        name="memory_kv",
    )(mem, g_mem.reshape(1, -1), w_ck.astype(jnp.bfloat16), w_cv.astype(jnp.bfloat16))


def _cross_kernel(q_ref, mk_ref, mv_ref, o_ref):
    q = q_ref[0] * CROSS_SCALE
    outs = []
    for h in range(CROSS_HEADS):
        sl = slice(h * CROSS_HEAD_DIM, (h + 1) * CROSS_HEAD_DIM)
        s = _dot_nt(q[:, sl].astype(jnp.bfloat16), mk_ref[0, :, sl].astype(jnp.bfloat16))
        e = jnp.exp(s - jnp.max(s, axis=1, keepdims=True))
        p = e / jnp.sum(e, axis=1, keepdims=True)
        outs.append(_dot(p.astype(jnp.bfloat16), mv_ref[0, :, sl].astype(jnp.bfloat16)))
    o_ref[0] = jnp.concatenate(outs, axis=1)


def cross_attention(hq, mk, mv, *, rows):
    NB, R, _ = hq.shape
    return pl.pallas_call(
        _cross_kernel,
        grid=(NB, R // rows),
        in_specs=[pl.BlockSpec((1, rows, D_MODEL), lambda b, i: (b, i, 0)),
                  pl.BlockSpec((1, N_MEM, D_MODEL), lambda b, i: (b, 0, 0)),
                  pl.BlockSpec((1, N_MEM, D_MODEL), lambda b, i: (b, 0, 0))],
        out_specs=pl.BlockSpec((1, rows, D_MODEL), lambda b, i: (b, i, 0)),
        out_shape=jax.ShapeDtypeStruct((NB, R, D_MODEL), jnp.float32),
        compiler_params=pltpu.CompilerParams(
            dimension_semantics=("arbitrary", "arbitrary"), vmem_limit_bytes=VMEM_LIMIT_BYTES),
        name="cross_attention",
    )(hq, mk, mv)


def _tail_kernel(x_ref, o_ref, wco_ref, gf_ref, wg_ref, wu_ref, wd_ref, gfin_ref, y_ref, acc_ref):
    x = x_ref[...] + _dot(o_ref[...].astype(jnp.bfloat16), wco_ref[...])
    h = _rms(x, gf_ref[...]).astype(jnp.bfloat16)
    acc_ref[...] = x
    for c in range(D_FF // FF_CHUNK):
        sl = slice(c * FF_CHUNK, (c + 1) * FF_CHUNK)
        a = _dot(h, wg_ref[:, sl])
        b = _dot(h, wu_ref[:, sl])
        t = (a * jax.nn.sigmoid(a) * b).astype(jnp.bfloat16)
        acc_ref[...] += _dot(t, wd_ref[sl, :])
    y_ref[...] = _rms(acc_ref[...], gfin_ref[...])


def tail(x1, o, w_co, g_ffn, w_gate, w_up, w_down, g_final):
    M = x1.shape[0]
    tm = min(ROW_TILE, M)
    row = lambda i: (i, 0)
    cw = lambda shape: pl.BlockSpec(shape, lambda i: (0, 0), pipeline_mode=pl.Buffered(1))
    b16 = lambda t: t.astype(jnp.bfloat16)
    return pl.pallas_call(
        _tail_kernel,
        grid=(M // tm,),
        in_specs=[pl.BlockSpec((tm, D_MODEL), row), pl.BlockSpec((tm, D_MODEL), row),
                  cw(w_co.shape), cw((1, D_MODEL)), cw(w_gate.shape), cw(w_up.shape), cw(w_down.shape),
                  cw((1, D_MODEL))],
        out_specs=pl.BlockSpec((tm, D_MODEL), row),
        out_shape=jax.ShapeDtypeStruct((M, D_MODEL), jnp.float32),
        scratch_shapes=[pltpu.VMEM((tm, D_MODEL), jnp.float32)],
        compiler_params=pltpu.CompilerParams(
            dimension_semantics=("arbitrary",), vmem_limit_bytes=VMEM_LIMIT_BYTES),
        name="tail",
    )(x1, o, b16(w_co), g_ffn.reshape(1, -1), b16(w_gate), b16(w_up), b16(w_down), g_final.reshape(1, -1))


def kernel(x_prompt, x_sample, mem_prompt, cache_cmp_k, cache_cmp_v, cache_slc_k, cache_slc_v,
           page_table, state_win_k, state_win_v, cache_mem_k, cache_mem_v,
           g_mix, w_in, cmp_pe, cmp_w1, cmp_w2, gmlp_ln_g, gmlp_ln_b, gmlp_ws, gmlp_bs,
           w_branch_a, w_branch_b, w_out, g_cross, g_mem, w_cq, w_ck, w_cv, w_co,
           g_ffn, w_gate, w_up, w_down, g_final):
    B, S, _ = x_prompt.shape
    DB, DS, _ = x_sample.shape
    n_pages = page_table.shape[1]
    assert g_mix.shape[0] == 1, "single-layer step"
    l = 0
    cmpw = cmp_weights(cmp_pe[l], cmp_w1[l], cmp_w2[l])
    merge_w = (gmlp_ln_g[l], gmlp_ln_b[l])
    merge_tail = (w_branch_a[l], w_branch_b[l], w_out[l], g_cross[l], w_cq[l])
    ffn_w = (w_co[l], g_ffn[l], w_gate[l], w_up[l], w_down[l], g_final)
    to_t = lambda t: jnp.transpose(t, (0, 2, 3, 1)).reshape(t.shape[0], KV_WIDTH, t.shape[1])
    from_t = lambda t: jnp.transpose(t.reshape(t.shape[0], N_KV_HEADS, HEAD_DIM, t.shape[2]), (0, 3, 1, 2))
    mem3 = lambda t: t.reshape(-1, N_MEM, D_MODEL)

    xp = x_prompt.reshape(B * S, D_MODEL)
    (q, kct, vct, kst, vst, kwt, vwt, kst_b, vst_b, kwt_b, vwt_b, gn, u, v, ga, gb) = mixer_proj(
        xp, g_mix[l], w_in[l], jnp.arange(S, dtype=jnp.int32), batch=B)
    ck, cv = cmp_blocks_dense(kct, vct, *cmpw)
    o_a = nsa_prompt_attention(q.reshape(B, S, NSA_WIDTH), gn.reshape(B, S, LANES), ck, cv,
                               kst_b, vst_b, kwt_b, vwt_b)
    x1, hq = merge_and_cross_q(o_a.reshape(B * S, NSA_WIDTH), u, v, ga, gb, xp, *merge_w,
                               *gmlp_weights(gmlp_ws[l], gmlp_bs[l], min(S, GMLP_CHUNK)), *merge_tail)
    mk, mv = memory_kv(mem_prompt.reshape(B * N_MEM, D_MODEL), g_mem[l], w_ck[l], w_cv[l])
    o = cross_attention(hq.reshape(B, S, D_MODEL), mem3(mk), mem3(mv), rows=min(S, ROW_TILE))
    y_prompt = tail(x1, o.reshape(B * S, D_MODEL), *ffn_w).reshape(B, S, D_MODEL)
    wb_p = min(WINDOW, S)
    mem5 = lambda t: t.reshape(B, N_MEM, CROSS_HEADS, CROSS_HEAD_DIM)
    prompt_new = (from_t(kct), from_t(vct), from_t(kst), from_t(vst),
                  from_t(kwt[:, :, S - wb_p:]), from_t(vwt[:, :, S - wb_p:]), mem5(mk), mem5(mv))

    xs = x_sample.reshape(DB * DS, D_MODEL)
    pos_s = n_pages * PAGE_SIZE + jnp.arange(DS, dtype=jnp.int32)
    (q, kc, vc, ks, vs, kw, vw, gn, u, v, ga, gb) = mixer_proj(xs, g_mix[l], w_in[l], jnp.tile(pos_s, DB))
    ck, cv = cmp_blocks_paged(to_t(cache_cmp_k[l]), to_t(cache_cmp_v[l]), page_table, *cmpw)
    per_seq = lambda t: t.reshape(DB, DS, -1)
    o_a, win_kt, win_vt = nsa_sample_attention(
        per_seq(q), per_seq(gn), ck, cv, to_t(state_win_k[l]), to_t(state_win_v[l]),
        per_seq(ks), per_seq(vs), per_seq(kw), per_seq(vw),
        to_t(cache_slc_k[l]), to_t(cache_slc_v[l]), page_table)
    x1, hq = merge_and_cross_q(o_a.reshape(DB * DS, NSA_WIDTH), u, v, ga, gb, xs, *merge_w,
                               *gmlp_weights(gmlp_ws[l], gmlp_bs[l], min(DS, GMLP_CHUNK)), *merge_tail)
    o = cross_attention(per_seq(hq), mem3(cache_mem_k[l]), mem3(cache_mem_v[l]), rows=DS)
    y_sample = tail(x1, o.reshape(DB * DS, D_MODEL), *ffn_w).reshape(DB, DS, D_MODEL)
    heads = lambda t: t.reshape(DB, DS, N_KV_HEADS, HEAD_DIM)
    sample_new = (heads(kc), heads(vc), heads(ks), heads(vs), from_t(win_kt), from_t(win_vt), per_seq(v))

    return (y_prompt, y_sample) + tuple(t[None] for t in prompt_new + sample_new)
```

```python
import functools

import jax
import jax.numpy as jnp
import numpy as np
from jax import lax
from jax.experimental import pallas as pl
from jax.experimental.pallas import tpu as pltpu

D_MODEL = 1024
N_HEADS = 8
HEAD_DIM = 64
N_KV_HEADS = 2
GROUP = N_HEADS // N_KV_HEADS
NSA_WIDTH = N_HEADS * HEAD_DIM
KV_WIDTH = N_KV_HEADS * HEAD_DIM
CMP_LEN = 32
CMP_STRIDE = 16
CMP_HIDDEN = 128
SLC_BLK = 64
SLC_SHIFT = 6
TOP_N = 16
WINDOW = 512
ROPE_THETA = 10000.0
ATTN_SCALE = HEAD_DIM ** -0.5
FORCE_BONUS = 1e4
NEG_INF = -1e30
N_BRANCH = 3
GMLP_WIDTH = 512
GMLP_GROUPS = 8
GMLP_GROUP_DIM = GMLP_WIDTH // GMLP_GROUPS
GMLP_CHUNK = 128
N_MEM = 256
CROSS_HEADS = 4
CROSS_HEAD_DIM = D_MODEL // CROSS_HEADS
CROSS_SCALE = CROSS_HEAD_DIM ** -0.5
D_FF = -(-8 * D_MODEL // (3 * 256)) * 256
NORM_EPS = 1e-6
PAGE_SIZE = 128
CHUNKS_PER_PAGE = PAGE_SIZE // CMP_STRIDE
N_KV_GROUPS = 6

LANES = 128
SUBLANES = 8
VMEM_LIMIT_BYTES = 48 * 1024 * 1024
PAGED_VMEM_LIMIT_BYTES = 56 * 1024 * 1024
SEL_TILE = 512
SAMPLE_KEY_CHUNK = 2048
ROW_TILE = 512
FF_CHUNK = 256
GATE_ROWS = 32

_NT = (((1,), (1,)), ((), ()))


def _dot_nt(a, b):
    return lax.dot_general(a, b, _NT, preferred_element_type=jnp.float32)


def _dot(a, b):
    return jnp.dot(a, b, preferred_element_type=jnp.float32)


def _split_bf16(x):
    hi = x.astype(jnp.bfloat16)
    lo = (x - hi.astype(jnp.float32)).astype(jnp.bfloat16)
    return hi, lo


def _iota(shape, dim):
    return lax.broadcasted_iota(jnp.int32, shape, dim)


def _rms(x, g):
    return x * lax.rsqrt(jnp.mean(x * x, axis=-1, keepdims=True) + NORM_EPS) * g


def _cmp_from_transposed(xt_ref, p, xrows, wbig_ref, w2big_ref, peterm_ref, out_ref, n_rows, transpose_out=False):
    n_chunks = n_rows // CMP_STRIDE
    for j in range(n_rows // LANES):
        xrows[j * LANES:(j + 1) * LANES, :] = xt_ref[:, j * LANES:(j + 1) * LANES].T
    acc = jnp.zeros((n_chunks, 4 * CMP_HIDDEN), jnp.float32)
    for sp in range(CMP_STRIDE // 2):
        a0 = xrows[pl.ds(2 * sp, n_chunks, stride=CMP_STRIDE), :]
        a1 = xrows[pl.ds(2 * sp + 1, n_chunks, stride=CMP_STRIDE), :]
        lhs = jnp.concatenate([a0, a1], axis=1).astype(jnp.bfloat16)
        acc = acc + _dot(lhs, wbig_ref[p, sp])
    lead = acc[:, :2 * CMP_HIDDEN]
    trail_next = pltpu.roll(acc[:, 2 * CMP_HIDDEN:], n_chunks - 1, axis=0)
    pre = lead + trail_next + peterm_ref[p]
    out = _dot(jax.nn.gelu(pre).astype(jnp.bfloat16), w2big_ref[p])
    out_ref[0] = (out.T if transpose_out else out).astype(out_ref.dtype)


def _page_gather(pt_ref, pools, buf, sem, n_pages):
    b = pl.program_id(0)
    nb = pl.num_programs(0)
    slot = b % 2

    def page_copy(bb, sl, j, p):
        return pltpu.make_async_copy(pools[p].at[pt_ref[bb, j]],
                                     buf.at[sl, p, :, pl.ds(j * PAGE_SIZE, PAGE_SIZE)], sem.at[sl, p])

    def start(bb, sl):
        for p in range(len(pools)):
            for j in range(n_pages):
                page_copy(bb, sl, j, p).start()

    @pl.when(b == 0)
    def _():
        start(0, 0)

    @pl.when(b + 1 < nb)
    def _():
        start(b + 1, 1 - slot)

    def wait():
        for p in range(len(pools)):
            for j in range(n_pages):
                page_copy(b, slot, j, p).wait()

    return wait


def _cmp_paged_kernel(pt_ref, pool_k, pool_v, wbig_ref, w2big_ref, peterm_ref, ck_ref, cv_ref,
                      buf, xrows, sem, *, n_pages):
    wait = _page_gather(pt_ref, (pool_k, pool_v), buf, sem, n_pages)
    wait()
    slot = pl.program_id(0) % 2
    for p, out_ref in ((0, ck_ref), (1, cv_ref)):
        _cmp_from_transposed(buf.at[slot, p], p, xrows, wbig_ref, w2big_ref, peterm_ref, out_ref,
                             n_pages * PAGE_SIZE)


def _cmp_dense_kernel(kt_ref, vt_ref, wbig_ref, w2big_ref, peterm_ref, ck_ref, cvt_ref, xrows):
    for p, (src, out_ref) in enumerate(((kt_ref, ck_ref), (vt_ref, cvt_ref))):
        _cmp_from_transposed(src.at[0], p, xrows, wbig_ref, w2big_ref, peterm_ref, out_ref, src.shape[2],
                             transpose_out=(p == 1))


def cmp_weights(cmp_pe, cmp_w1, cmp_w2):
    eye = jnp.eye(N_KV_HEADS, dtype=jnp.float32)
    w1r = cmp_w1.reshape(2, 2, CMP_STRIDE // 2, 2, HEAD_DIM, CMP_HIDDEN)
    wbig = jnp.einsum('jtpsdf,hk->jpshdtkf', w1r, eye)
    wbig = wbig.reshape(2, CMP_STRIDE // 2, 2 * KV_WIDTH, 4 * CMP_HIDDEN).astype(jnp.bfloat16)
    w2big = jnp.einsum('jfd,hk->jhfkd', cmp_w2, eye).reshape(2, 2 * CMP_HIDDEN, KV_WIDTH)
    pe_term = jnp.einsum('jsd,jsdf->jf', cmp_pe, cmp_w1)
    peterm = jnp.tile(pe_term[:, None, :], (1, 1, N_KV_HEADS))
    return wbig, w2big.astype(jnp.bfloat16), peterm


def _cmp_weight_specs(wbig, w2big, peterm, nargs):
    c3 = (lambda b: (0, 0, 0)) if nargs == 1 else (lambda b, pt: (0, 0, 0))
    c4 = (lambda b: (0, 0, 0, 0)) if nargs == 1 else (lambda b, pt: (0, 0, 0, 0))
    one = pl.Buffered(1)
    return [pl.BlockSpec(wbig.shape, c4, pipeline_mode=one),
            pl.BlockSpec(w2big.shape, c3, pipeline_mode=one),
            pl.BlockSpec(peterm.shape, c3, pipeline_mode=one)]


def cmp_blocks_paged(pool_k, pool_v, page_table, wbig, w2big, peterm):
    n_seq, n_pages = page_table.shape
    n_rows = n_pages * PAGE_SIZE
    n_chunks = n_rows // CMP_STRIDE
    out = jax.ShapeDtypeStruct((n_seq, n_chunks, KV_WIDTH), jnp.bfloat16)
    grid_spec = pltpu.PrefetchScalarGridSpec(
        num_scalar_prefetch=1,
        grid=(n_seq,),
        in_specs=[pl.BlockSpec(memory_space=pl.ANY), pl.BlockSpec(memory_space=pl.ANY)]
        + _cmp_weight_specs(wbig, w2big, peterm, 2),
        out_specs=[pl.BlockSpec((1, n_chunks, KV_WIDTH), lambda b, pt: (b, 0, 0))] * 2,
        scratch_shapes=[
            pltpu.VMEM((2, 2, KV_WIDTH, n_rows), jnp.float32),
            pltpu.VMEM((n_rows, KV_WIDTH), jnp.float32),
            pltpu.SemaphoreType.DMA((2, 2)),
        ],
    )
    return pl.pallas_call(
        functools.partial(_cmp_paged_kernel, n_pages=n_pages),
        grid_spec=grid_spec,
        out_shape=[out, out],
        compiler_params=pltpu.CompilerParams(
            dimension_semantics=("arbitrary",), vmem_limit_bytes=PAGED_VMEM_LIMIT_BYTES),
        name="cmp_blocks_paged",
    )(page_table, pool_k, pool_v, wbig, w2big, peterm)


def cmp_blocks_dense(kt, vt, wbig, w2big, peterm):
    B, _, S = kt.shape
    n_chunks = S // CMP_STRIDE
    per_b = lambda b: (b, 0, 0)
    return pl.pallas_call(
        _cmp_dense_kernel,
        grid=(B,),
        in_specs=[pl.BlockSpec((1, KV_WIDTH, S), per_b), pl.BlockSpec((1, KV_WIDTH, S), per_b)]
        + _cmp_weight_specs(wbig, w2big, peterm, 1),
        out_specs=[pl.BlockSpec((1, n_chunks, KV_WIDTH), per_b), pl.BlockSpec((1, KV_WIDTH, n_chunks), per_b)],
        out_shape=[jax.ShapeDtypeStruct((B, n_chunks, KV_WIDTH), jnp.bfloat16),
                   jax.ShapeDtypeStruct((B, KV_WIDTH, n_chunks), jnp.bfloat16)],
        scratch_shapes=[pltpu.VMEM((S, KV_WIDTH), jnp.float32)],
        compiler_params=pltpu.CompilerParams(
            dimension_semantics=("arbitrary",), vmem_limit_bytes=VMEM_LIMIT_BYTES),
        name="cmp_blocks_dense",
    )(kt, vt, wbig, w2big, peterm)


def cmp_to_slc_matrix(n_chunks, n_cols):
    cs = np.arange(n_chunks)[:, None] * CMP_STRIDE
    ss = np.arange(n_cols)[None, :] * SLC_BLK
    shared = np.minimum(cs + CMP_LEN, ss + SLC_BLK) - np.maximum(cs, ss)
    w = np.maximum(shared, 0).astype(np.float32) / CMP_LEN
    w[n_chunks - 1] = 0.0
    return w


def _softmax_cols(s, valid):
    sm = jnp.where(valid, s, NEG_INF)
    mx = jnp.max(sm, axis=0, keepdims=True)
    e = jnp.where(valid, jnp.exp(sm - mx), 0.0)
    den = jnp.maximum(jnp.sum(e, axis=0, keepdims=True), 1e-30)
    return e * (1.0 / den)


def _nsa_prompt_kernel(qt_ref, gnt_ref, ck_ref, cvt_ref, ks_ref, vst_ref, kw_ref, vwt_ref,
                       wcst_ref, o_ref, lhs_ref, m_ref, l_ref, acc_ref, *, tq, n_chunks):
    i = pl.program_id(1)
    q0 = i * tq

    def qpos_of(shape):
        return q0 + (_iota(shape, 1) & (tq - 1))

    zeros = jnp.zeros((HEAD_DIM, tq), jnp.bfloat16)
    for j in range(N_HEADS):
        blk = qt_ref[0, j * HEAD_DIM:(j + 1) * HEAD_DIM, :]
        pad = [blk, zeros] if j < GROUP else [zeros, blk]
        lhs_ref[:LANES, j * tq:(j + 1) * tq] = jnp.concatenate(pad, axis=0)
    qp = lhs_ref[:LANES, :]

    s = _dot(ck_ref[0], qp)
    n_idx = _iota(s.shape, 0)
    valid = (n_idx * CMP_STRIDE + (CMP_LEN - 1) <= qpos_of(s.shape)) & (n_idx < n_chunks - 1)
    p_cmp = _softmax_cols(s, valid)
    o_cmp = _dot(cvt_ref[0], p_cmp.astype(jnp.bfloat16))

    blk = _iota((LANES, tq), 0)
    blkf = blk.astype(jnp.float32)
    qblk = (q0 + _iota((LANES, tq), 1)) >> SLC_SHIFT
    forced = (blk == 0) | (blk == qblk) | (blk == qblk - 1)
    wcst = wcst_ref[...]
    for h in range(N_KV_HEADS):
        c0 = h * GROUP * tq
        psum = p_cmp[:, c0:c0 + tq]
        for g in range(1, GROUP):
            psum = psum + p_cmp[:, c0 + g * tq:c0 + (g + 1) * tq]
        hi, lo = _split_bf16(psum)
        sc = _dot(wcst, hi) + _dot(wcst, lo)
        sc = jnp.where(blk <= qblk, jnp.where(forced, sc + FORCE_BONUS, sc), NEG_INF)
        sel = jnp.zeros_like(sc)
        for _ in range(TOP_N):
            mx = jnp.max(sc, axis=0, keepdims=True)
            first = jnp.min(jnp.where(sc == mx, blkf, float(LANES)), axis=0, keepdims=True)
            hit = blkf == first
            sel = jnp.where(hit, 1.0, sel)
            sc = jnp.where(hit, -jnp.inf, sc)
        nsel = (1.0 - sel).astype(jnp.bfloat16)
        for g in range(GROUP):
            lhs_ref[LANES:, c0 + g * tq:c0 + (g + 1) * tq] = nsel

    m_ref[...] = jnp.full(m_ref.shape, -3e38, jnp.float32)
    l_ref[...] = jnp.zeros(l_ref.shape, jnp.float32)
    acc_ref[...] = jnp.zeros(acc_ref.shape, jnp.float32)
    n_tiles = (q0 + tq + SEL_TILE - 1) // SEL_TILE

    def sel_tile(kt, causal):
        k0 = pl.multiple_of(kt * SEL_TILE, SEL_TILE)
        kblk = (k0 + _iota((SEL_TILE, LANES), 0)) >> SLC_SHIFT
        ebig = jnp.where(kblk == _iota((SEL_TILE, LANES), 1), NEG_INF, 0.0).astype(jnp.bfloat16)
        keys = jnp.concatenate([ks_ref[0, pl.ds(k0, SEL_TILE), :], ebig], axis=1)
        st = _dot(keys, lhs_ref[...])
        if causal:
            st = jnp.where(k0 + _iota(st.shape, 0) <= qpos_of(st.shape), st, NEG_INF)
        m_prev = m_ref[...]
        m_new = jnp.maximum(m_prev, jnp.max(st, axis=0, keepdims=True))
        alpha = jnp.exp(m_prev - m_new)
        p = jnp.exp(st - m_new)
        l_ref[...] = alpha * l_ref[...] + jnp.sum(p, axis=0, keepdims=True)
        acc_ref[...] = alpha * acc_ref[...] + _dot(vst_ref[0, :, pl.ds(k0, SEL_TILE)], p.astype(jnp.bfloat16))
        m_ref[...] = m_new

    def body(kt, carry):
        sel_tile(kt, False)
        return carry

    lax.fori_loop(0, n_tiles - 1, body, 0)
    sel_tile(n_tiles - 1, True)
    o_sel = acc_ref[...] * (1.0 / l_ref[...])

    w0 = pl.multiple_of(jnp.maximum(q0 - WINDOW, 0), tq)
    sw = _dot(kw_ref[0, pl.ds(w0, WINDOW + tq), :], qp)
    kpos = w0 + _iota(sw.shape, 0)
    qpos = qpos_of(sw.shape)
    p_win = _softmax_cols(sw, (kpos <= qpos) & (kpos > qpos - WINDOW))
    o_win = _dot(vwt_ref[0, :, pl.ds(w0, WINDOW + tq)], p_win.astype(jnp.bfloat16))

    gates = gnt_ref[0]
    for pair in range(N_HEADS // 2):
        rows = []
        for j in (2 * pair, 2 * pair + 1):
            r0 = (j // GROUP) * HEAD_DIM
            acc = None
            for br, ob in enumerate((o_cmp, o_sel, o_win)):
                gate = gates[j * N_BRANCH + br:j * N_BRANCH + br + 1, :]
                term = gate * ob[r0:r0 + HEAD_DIM, j * tq:(j + 1) * tq]
                acc = term if acc is None else acc + term
            rows.append(acc)
        o_ref[0, :, pair * LANES:(pair + 1) * LANES] = jnp.concatenate(rows, axis=0).T


def nsa_prompt_attention(qt, gnt, ck, cvt, ks, vst, kw, vwt, *, tq=LANES):
    B, _, S = qt.shape
    n_chunks = ck.shape[1]
    assert S % SEL_TILE == 0 and S // SLC_BLK <= LANES and tq == LANES and S >= WINDOW + tq
    wcst = jnp.asarray(cmp_to_slc_matrix(n_chunks, LANES).T, jnp.bfloat16)
    cols = N_HEADS * tq
    per_b = lambda b, i: (b, 0, 0)
    tile_t = lambda b, i: (b, 0, i)
    const = lambda b, i: (0, 0)
    return pl.pallas_call(
        functools.partial(_nsa_prompt_kernel, tq=tq, n_chunks=n_chunks),
        grid=(B, S // tq),
        in_specs=[
            pl.BlockSpec((1, NSA_WIDTH, tq), tile_t),
            pl.BlockSpec((1, gnt.shape[1], tq), tile_t),
            pl.BlockSpec((1, n_chunks, KV_WIDTH), per_b),
            pl.BlockSpec((1, KV_WIDTH, n_chunks), per_b),
            pl.BlockSpec((1, S, KV_WIDTH), per_b),
            pl.BlockSpec((1, KV_WIDTH, S), per_b),
            pl.BlockSpec((1, S, KV_WIDTH), per_b),
            pl.BlockSpec((1, KV_WIDTH, S), per_b),
            pl.BlockSpec(wcst.shape, const),
        ],
        out_specs=pl.BlockSpec((1, tq, NSA_WIDTH), lambda b, i: (b, i, 0)),
        out_shape=jax.ShapeDtypeStruct((B, S, NSA_WIDTH), jnp.float32),
        scratch_shapes=[
            pltpu.VMEM((2 * LANES, cols), jnp.bfloat16),
            pltpu.VMEM((1, cols), jnp.float32),
            pltpu.VMEM((1, cols), jnp.float32),
            pltpu.VMEM((LANES, cols), jnp.float32),
        ],
        compiler_params=pltpu.CompilerParams(
            dimension_semantics=("arbitrary", "arbitrary"), vmem_limit_bytes=VMEM_LIMIT_BYTES),
        name="nsa_prompt",
    )(qt, gnt, ck, cvt, ks, vst, kw, vwt, wcst)


def _padded_queries(q, tq):
    lane = _iota((tq, LANES), 1)
    blocks = []
    for h in range(N_KV_HEADS):
        for g in range(GROUP):
            c = h * GROUP + g
            pair = q[:, LANES * (c // 2):LANES * (c // 2 + 1)]
            if c % 2 != h:
                pair = pltpu.roll(pair, HEAD_DIM, axis=1)
            keep = (lane >= HEAD_DIM * h) & (lane < HEAD_DIM * (h + 1))
            blocks.append(jnp.where(keep, pair, 0.0))
    return jnp.concatenate(blocks, axis=0)


def _masked_softmax(s, valid):
    sm = jnp.where(valid, s, NEG_INF)
    mx = jnp.max(sm, axis=1, keepdims=True)
    e = jnp.where(valid, jnp.exp(sm - mx), 0.0)
    den = jnp.maximum(jnp.sum(e, axis=1, keepdims=True), 1e-30)
    return e / den


def _group_sum(p, tq):
    out = []
    for h in range(N_KV_HEADS):
        r0 = h * GROUP * tq
        acc = p[r0:r0 + tq]
        for g in range(1, GROUP):
            acc = acc + p[r0 + g * tq:r0 + (g + 1) * tq]
        out.append(acc)
    return out


def _combine(branches, gn, gexp_w, tq):
    hi, lo = _split_bf16(gn)
    gexp = _dot(hi, gexp_w) + _dot(lo, gexp_w)
    lane = _iota((tq, LANES), 1)
    cols = []
    for h in range(N_KV_HEADS):
        blks = []
        for g in range(GROUP):
            j = h * GROUP + g
            rows = slice(j * tq, (j + 1) * tq)
            o = None
            for br in range(N_BRANCH):
                gate = gexp[:, (j * N_BRANCH + br) * LANES:(j * N_BRANCH + br + 1) * LANES]
                term = gate * branches[br][rows]
                o = term if o is None else o + term
            blks.append(o)
        for jj in range(GROUP // 2):
            a, b2 = blks[2 * jj], blks[2 * jj + 1]
            if h == 0:
                b2 = pltpu.roll(b2, HEAD_DIM, axis=1)
            else:
                a = pltpu.roll(a, HEAD_DIM, axis=1)
            cols.append(jnp.where(lane < HEAD_DIM, a, b2))
    return jnp.concatenate(cols, axis=1)


def gate_expand_matrix():
    e = np.zeros((LANES, N_HEADS * N_BRANCH * LANES), np.float32)
    for j in range(N_HEADS * N_BRANCH):
        e[j, j * LANES:(j + 1) * LANES] = 1.0
    return jnp.asarray(e, jnp.bfloat16)


def _nsa_sample_kernel(pt_ref, q_ref, gn_ref, ck_ref, cv_ref, wk_ref, wv_ref,
                       ksn_ref, vsn_ref, kwn_ref, vwn_ref, pool_k, pool_v, ebig_ref, wcst_ref, gexp_ref,
                       o_ref, nwk_ref, nwv_ref, buf, sem, *, n_pages, n_new, tq):
    slot = pl.program_id(0) % 2
    past = n_pages * PAGE_SIZE
    n_chunks = n_pages * CHUNKS_PER_PAGE
    n_past_blk = past // SLC_BLK
    n_slc = -(-(past + n_new) // SLC_BLK)
    wb = wk_ref.shape[2]
    rows = N_HEADS * tq
    wait_pages = _page_gather(pt_ref, (pool_k, pool_v), buf, sem, n_pages)

    qp = _padded_queries(q_ref[0] * ATTN_SCALE, tq)
    qpb = qp.astype(jnp.bfloat16)

    def tok_of(shape):
        return _iota(shape, 0) & (tq - 1)

    def new_rows(ref):
        x = jnp.concatenate([ref[0], jnp.zeros((LANES - tq, KV_WIDTH), jnp.float32)], axis=0)
        return x.astype(jnp.bfloat16)

    s = _dot_nt(qpb, ck_ref[0])
    n_idx = _iota(s.shape, 1)
    valid = (n_idx * CMP_STRIDE + (CMP_LEN - 1) <= past + tok_of(s.shape)) & (n_idx < n_chunks - 1)
    p_cmp = _masked_softmax(s, valid)
    o_cmp = _dot(p_cmp.astype(jnp.bfloat16), cv_ref[0])

    ncol = wcst_ref.shape[0]
    psum = jnp.concatenate(_group_sum(p_cmp, tq)
                           + [jnp.zeros((LANES - N_KV_HEADS * tq, n_chunks), jnp.float32)], axis=0)
    hi, lo = _split_bf16(psum)
    wcst = wcst_ref[...]
    sc = _dot_nt(wcst, hi) + _dot_nt(wcst, lo)
    blk = _iota((ncol, LANES), 0)
    blkf = blk.astype(jnp.float32)
    qblk = (past + (_iota((ncol, LANES), 1) & (tq - 1))) >> SLC_SHIFT
    forced = (blk == 0) | (blk == qblk) | (blk == qblk - 1)
    sc = jnp.where(blk <= qblk, jnp.where(forced, sc + FORCE_BONUS, sc), NEG_INF)
    sc = jnp.where(blk < n_slc, sc, -jnp.inf)
    sel_t = jnp.zeros_like(sc)
    for _ in range(min(TOP_N, n_slc)):
        mx = jnp.max(sc, axis=0, keepdims=True)
        first = jnp.min(jnp.where(sc == mx, blkf, float(ncol)), axis=0, keepdims=True)
        hit = blkf == first
        sel_t = jnp.where(hit, 1.0, sel_t)
        sc = jnp.where(hit, -jnp.inf, sc)
    sel = sel_t.T
    nsel, sel_new = [], []
    for h in range(N_KV_HEADS):
        sel_h = sel[h * tq:(h + 1) * tq]
        nsel.append(jnp.concatenate([1.0 - sel_h[:, :LANES]] * GROUP, axis=0).astype(jnp.bfloat16))
        sel_new.append(jnp.broadcast_to(sel_h[:, n_past_blk:n_past_blk + 1], (tq, LANES)))
    lhs = jnp.concatenate(
        [jnp.concatenate([qpb[h * GROUP * tq:(h + 1) * GROUP * tq], nsel[h]], axis=1)
         for h in range(N_KV_HEADS)], axis=0)
    sel_new_rows = jnp.concatenate([sel_new[h] for h in range(N_KV_HEADS) for _ in range(GROUP)], axis=0)

    tn = _iota((rows, LANES), 1)
    new_ok = (tn <= tok_of((rows, LANES))) & (tn < n_new)
    s_sn = jnp.where(new_ok & (sel_new_rows > 0.5), _dot_nt(qpb, new_rows(ksn_ref)), NEG_INF)
    s_wn = jnp.where(new_ok, _dot_nt(qpb, new_rows(kwn_ref)), NEG_INF)

    wk = wk_ref[0]
    wv = wv_ref[0]
    s_w = _dot(qpb, wk.astype(jnp.bfloat16))
    kwpos = past - wb + _iota(s_w.shape, 1)
    qpos = past + tok_of(s_w.shape)
    s_w = jnp.where((kwpos >= 0) & (kwpos <= qpos) & (kwpos > qpos - WINDOW), s_w, NEG_INF)
    mx = jnp.maximum(jnp.max(s_w, axis=1, keepdims=True), jnp.max(s_wn, axis=1, keepdims=True))
    e_w = jnp.exp(s_w - mx)
    e_wn = jnp.exp(s_wn - mx)
    den = jnp.sum(e_w, axis=1, keepdims=True) + jnp.sum(e_wn, axis=1, keepdims=True)
    o_win = (_dot_nt(e_w.astype(jnp.bfloat16), wv.astype(jnp.bfloat16))
             + _dot(e_wn.astype(jnp.bfloat16), new_rows(vwn_ref))) / den

    col = _iota((KV_WIDTH, wb), 1)
    for src, new_ref, dst in ((wk, kwn_ref, nwk_ref), (wv, vwn_ref, nwv_ref)):
        new_t = jnp.concatenate([new_ref[0], jnp.zeros((LANES - tq, KV_WIDTH), jnp.float32)], axis=0).T
        tail = jnp.concatenate([jnp.zeros((KV_WIDTH, wb - LANES), jnp.float32), new_t], axis=1)
        dst[0] = jnp.where(col >= wb - n_new, pltpu.roll(tail, LANES - n_new, axis=1),
                           pltpu.roll(src, wb - n_new, axis=1))

    wait_pages()
    ck_keys = min(SAMPLE_KEY_CHUNK, past)
    s_chunks = []
    for c in range(past // ck_keys):
        kc = buf[slot, 0, :, pl.ds(c * ck_keys, ck_keys)].astype(jnp.bfloat16)
        rhs = jnp.concatenate([kc, ebig_ref[:, pl.ds(c * ck_keys, ck_keys)]], axis=0)
        s_chunks.append(_dot(lhs, rhs))
    mx = jnp.max(s_sn, axis=1, keepdims=True)
    for sc in s_chunks:
        mx = jnp.maximum(mx, jnp.max(sc, axis=1, keepdims=True))
    e_sn = jnp.exp(s_sn - mx)
    den = jnp.sum(e_sn, axis=1, keepdims=True)
    o_sel = _dot(e_sn.astype(jnp.bfloat16), new_rows(vsn_ref))
    for c, sc in enumerate(s_chunks):
        e = jnp.exp(sc - mx)
        den = den + jnp.sum(e, axis=1, keepdims=True)
        vc = buf[slot, 1, :, pl.ds(c * ck_keys, ck_keys)].astype(jnp.bfloat16)
        o_sel = o_sel + _dot_nt(e.astype(jnp.bfloat16), vc)
    o_sel = o_sel / den

    o_ref[0] = _combine((o_cmp, o_sel, o_win), gn_ref[0], gexp_ref[...], tq)


def nsa_sample_attention(q, gn, ck, cv, win_k, win_v, ks_new, vs_new, kw_new, vw_new,
                         pool_k, pool_v, page_table, *, tq=SUBLANES):
    DB, DS, _ = q.shape
    n_pages = page_table.shape[1]
    past = n_pages * PAGE_SIZE
    n_chunks = ck.shape[1]
    wb = win_k.shape[2]
    n_slc = -(-(past + DS) // SLC_BLK)
    assert DS <= tq and DS < CMP_STRIDE and past // SLC_BLK <= LANES and wb % LANES == 0
    assert past % min(SAMPLE_KEY_CHUNK, past) == 0
    ncol = -(-n_slc // LANES) * LANES
    pad_t = lambda t: jnp.pad(t, ((0, 0), (0, tq - DS), (0, 0)))
    kb = np.arange(past)[None, :] // SLC_BLK
    ebig = jnp.asarray(np.where(kb == np.arange(LANES)[:, None], NEG_INF, 0.0), jnp.bfloat16)
    wcst = jnp.asarray(cmp_to_slc_matrix(n_chunks, ncol).T, jnp.bfloat16)
    gexp = gate_expand_matrix()
    per_b = lambda b, pt: (b, 0, 0)
    const = lambda b, pt: (0, 0)
    blk3 = lambda n, w: pl.BlockSpec((1, n, w), per_b)
    grid_spec = pltpu.PrefetchScalarGridSpec(
        num_scalar_prefetch=1,
        grid=(DB,),
        in_specs=[
            blk3(tq, NSA_WIDTH), blk3(tq, LANES), blk3(n_chunks, KV_WIDTH), blk3(n_chunks, KV_WIDTH),
            blk3(KV_WIDTH, wb), blk3(KV_WIDTH, wb),
            blk3(tq, KV_WIDTH), blk3(tq, KV_WIDTH), blk3(tq, KV_WIDTH), blk3(tq, KV_WIDTH),
            pl.BlockSpec(memory_space=pl.ANY), pl.BlockSpec(memory_space=pl.ANY),
            pl.BlockSpec(ebig.shape, const, pipeline_mode=pl.Buffered(1)),
            pl.BlockSpec(wcst.shape, const, pipeline_mode=pl.Buffered(1)),
            pl.BlockSpec(gexp.shape, const, pipeline_mode=pl.Buffered(1)),
        ],
        out_specs=[blk3(tq, NSA_WIDTH), blk3(KV_WIDTH, wb), blk3(KV_WIDTH, wb)],
        scratch_shapes=[
            pltpu.VMEM((2, 2, KV_WIDTH, past), jnp.float32),
            pltpu.SemaphoreType.DMA((2, 2)),
        ],
    )
    o, nwk, nwv = pl.pallas_call(
        functools.partial(_nsa_sample_kernel, n_pages=n_pages, n_new=DS, tq=tq),
        grid_spec=grid_spec,
        out_shape=[jax.ShapeDtypeStruct((DB, tq, NSA_WIDTH), jnp.float32),
                   jax.ShapeDtypeStruct((DB, KV_WIDTH, wb), jnp.float32),
                   jax.ShapeDtypeStruct((DB, KV_WIDTH, wb), jnp.float32)],
        compiler_params=pltpu.CompilerParams(
            dimension_semantics=("arbitrary",), vmem_limit_bytes=PAGED_VMEM_LIMIT_BYTES),
        name="nsa_sample",
    )(page_table, pad_t(q), pad_t(gn), ck, cv, win_k, win_v, pad_t(ks_new), pad_t(vs_new), pad_t(kw_new),
      pad_t(vw_new), pool_k, pool_v, ebig, wcst, gexp)
    return o[:, :DS], nwk, nwv


def rope_tables(pos):
    half = HEAD_DIM // 2
    inv_freq = ROPE_THETA ** (-jnp.arange(half, dtype=jnp.float32) / half)
    ang = pos.astype(jnp.float32)[:, None] * inv_freq[None, :]
    cos, sin = jnp.cos(ang), jnp.sin(ang)
    cos_t = jnp.tile(cos, (1, LANES // half))
    sin_t = jnp.tile(jnp.concatenate([-sin, sin], axis=1), (1, LANES // HEAD_DIM))
    return cos_t, sin_t


def _rope_lanes(y, cos, sin):
    first_half = (_iota((y.shape[0], LANES), 1) & (HEAD_DIM - 1)) < HEAD_DIM // 2
    cols = []
    for c in range(y.shape[1] // LANES):
        yc = y[:, c * LANES:(c + 1) * LANES]
        rot = jnp.where(first_half, pltpu.roll(yc, LANES - HEAD_DIM // 2, axis=1),
                        pltpu.roll(yc, HEAD_DIM // 2, axis=1))
        cols.append(yc * cos + rot * sin)
    return cols[0] if len(cols) == 1 else jnp.concatenate(cols, axis=1)


def _rope_sublanes(y, cos_t, sin_t):
    hh = HEAD_DIM // 2
    out = []
    for c in range(y.shape[0] // LANES):
        yc = y[c * LANES:(c + 1) * LANES]
        rot = jnp.concatenate([yc[hh:2 * hh], yc[:hh], yc[3 * hh:], yc[2 * hh:3 * hh]], axis=0)
        out.append(yc * cos_t + rot * sin_t)
    return out[0] if len(out) == 1 else jnp.concatenate(out, axis=0)


_REST_ACTS = {"u": jax.nn.gelu, "v": jax.nn.gelu, "gn": jax.nn.sigmoid}


def _proj_rest(h, wrest_ref, refs, names):
    c0 = 0
    for ref, name in zip(refs, names):
        width = ref.shape[-1]
        act = _REST_ACTS.get(name)
        for c in range(0, width, GMLP_WIDTH):
            w = min(GMLP_WIDTH, width - c)
            y = _dot(h, wrest_ref[:, c0 + c:c0 + c + w])
            ref[:, c:c + w] = y if act is None else act(y)
        c0 += width


def _proj_rows_kernel(x_ref, g_ref, wq_ref, wkv_ref, wrest_ref, cos_ref, sin_ref, q_ref, *out_refs):
    h = _rms(x_ref[...], g_ref[...]).astype(jnp.bfloat16)
    cos, sin = cos_ref[...], sin_ref[...]
    q_ref[...] = _rope_lanes(_dot(h, wq_ref[...]), cos, sin)
    y = _dot(h, wkv_ref[...])
    for gi in range(N_KV_GROUPS):
        blk = y[:, gi * KV_WIDTH:(gi + 1) * KV_WIDTH]
        out_refs[gi][...] = _rope_lanes(blk, cos, sin) if gi % 2 == 0 else blk
    _proj_rest(h, wrest_ref, out_refs[N_KV_GROUPS:], ("gn", "u", "v", "ga", "gb"))


def _proj_seq_kernel(x_ref, g_ref, wqt_ref, wkvt_ref, wkrow_ref, wgnt_ref, wrest_ref,
                     cos_ref, sin_ref, cost_ref, sint_ref,
                     qt_ref, kct_ref, vct_ref, kst_ref, vst_ref, kwt_ref, vwt_ref,
                     ks_b_ref, vst_b_ref, kw_b_ref, vwt_b_ref, gnt_ref, u_ref, v_ref, ga_ref, gb_ref):
    h = _rms(x_ref[...], g_ref[...]).astype(jnp.bfloat16)
    cos_t, sin_t = cost_ref[...], sint_ref[...]
    qt = _rope_sublanes(_dot_nt(wqt_ref[...], h), cos_t, sin_t)
    qt_ref[0] = (qt * ATTN_SCALE).astype(jnp.bfloat16)
    yt = _dot_nt(wkvt_ref[...], h)
    kv_refs = (kct_ref, vct_ref, kst_ref, vst_ref, kwt_ref, vwt_ref)
    for gi in range(N_KV_GROUPS):
        blk = yt[gi * KV_WIDTH:(gi + 1) * KV_WIDTH]
        if gi % 2 == 0:
            blk = _rope_sublanes(blk, cos_t, sin_t)
        kv_refs[gi][0] = blk
        if gi == 3:
            vst_b_ref[0] = blk.astype(jnp.bfloat16)
        if gi == 5:
            vwt_b_ref[0] = blk.astype(jnp.bfloat16)
    yk = _rope_lanes(_dot(h, wkrow_ref[...]), cos_ref[...], sin_ref[...])
    ks_b_ref[...] = yk[:, :KV_WIDTH].astype(jnp.bfloat16)
    kw_b_ref[...] = yk[:, KV_WIDTH:].astype(jnp.bfloat16)
    gnt_ref[0] = jax.nn.sigmoid(_dot_nt(wgnt_ref[...], h))
    _proj_rest(h, wrest_ref, (u_ref, v_ref, ga_ref, gb_ref), ("u", "v", "ga", "gb"))


def _split_w_in(w_in):
    sizes = (NSA_WIDTH,) + (KV_WIDTH,) * N_KV_GROUPS + (N_HEADS * N_BRANCH,)
    offs = np.cumsum((0,) + sizes)
    wq = w_in[:, offs[0]:offs[1]]
    wkv = w_in[:, offs[1]:offs[7]]
    wgn = w_in[:, offs[7]:offs[8]]
    wrest = w_in[:, offs[8]:]
    return wq, wkv, wgn, wrest


def mixer_proj_rows(x, g, w_in, pos):
    M = x.shape[0]
    tm = min(ROW_TILE, M)
    assert M % tm == 0
    wq, wkv, wgn, wrest = _split_w_in(w_in)
    wrest = jnp.concatenate([jnp.pad(wgn, ((0, 0), (0, LANES - wgn.shape[1]))), wrest], axis=1)
    cos, sin = rope_tables(pos)
    b16 = lambda t: t.astype(jnp.bfloat16)
    row = lambda i: (i, 0)
    cw = lambda shape: pl.BlockSpec(shape, lambda i: (0, 0), pipeline_mode=pl.Buffered(1))
    widths = (NSA_WIDTH,) + (KV_WIDTH,) * N_KV_GROUPS + (LANES, GMLP_WIDTH, GMLP_WIDTH, D_MODEL, D_MODEL)
    return pl.pallas_call(
        _proj_rows_kernel,
        grid=(M // tm,),
        in_specs=[pl.BlockSpec((tm, D_MODEL), row), cw((1, D_MODEL)), cw(wq.shape), cw(wkv.shape),
                  cw(wrest.shape), pl.BlockSpec((tm, LANES), row), pl.BlockSpec((tm, LANES), row)],
        out_specs=[pl.BlockSpec((tm, w), row) for w in widths],
        out_shape=[jax.ShapeDtypeStruct((M, w), jnp.float32) for w in widths],
        compiler_params=pltpu.CompilerParams(
            dimension_semantics=("arbitrary",), vmem_limit_bytes=VMEM_LIMIT_BYTES),
        name="mixer_proj_rows",
    )(x, g.reshape(1, -1), b16(wq), b16(wkv), b16(wrest), cos, sin)


def mixer_proj_seq(x, g, w_in, batch):
    M = x.shape[0]
    S = M // batch
    tm = min(ROW_TILE, S)
    assert S % tm == 0
    n_s = S // tm
    wq, wkv, wgn, wrest = _split_w_in(w_in)
    wkrow = jnp.concatenate([wkv[:, 2 * KV_WIDTH:3 * KV_WIDTH], wkv[:, 4 * KV_WIDTH:5 * KV_WIDTH]], axis=1)
    wgnt = jnp.pad(wgn, ((0, 0), (0, GATE_ROWS - wgn.shape[1]))).T
    cos, sin = rope_tables(jnp.arange(S, dtype=jnp.int32))
    b16 = lambda t: t.astype(jnp.bfloat16)
    f32 = jnp.float32
    row = lambda i: (i, 0)
    tab = lambda i: (i % n_s, 0)
    tab_t = lambda i: (0, i % n_s)
    seq_t = lambda i: (i // n_s, 0, i % n_s)
    cw = lambda shape: pl.BlockSpec(shape, lambda i: (0, 0), pipeline_mode=pl.Buffered(1))
    t_spec = lambda rows: pl.BlockSpec((1, rows, tm), seq_t)
    t_shape = lambda rows, dt: jax.ShapeDtypeStruct((batch, rows, S), dt)
    r_spec = lambda w: pl.BlockSpec((tm, w), row)
    r_shape = lambda w, dt: jax.ShapeDtypeStruct((M, w), dt)
    out_specs = ([t_spec(NSA_WIDTH)] + [t_spec(KV_WIDTH)] * N_KV_GROUPS
                 + [r_spec(KV_WIDTH), t_spec(KV_WIDTH), r_spec(KV_WIDTH), t_spec(KV_WIDTH), t_spec(GATE_ROWS)]
                 + [r_spec(w) for w in (GMLP_WIDTH, GMLP_WIDTH, D_MODEL, D_MODEL)])
    out_shape = ([t_shape(NSA_WIDTH, jnp.bfloat16)] + [t_shape(KV_WIDTH, f32)] * N_KV_GROUPS
                 + [r_shape(KV_WIDTH, jnp.bfloat16), t_shape(KV_WIDTH, jnp.bfloat16),
                    r_shape(KV_WIDTH, jnp.bfloat16), t_shape(KV_WIDTH, jnp.bfloat16), t_shape(GATE_ROWS, f32)]
                 + [r_shape(w, f32) for w in (GMLP_WIDTH, GMLP_WIDTH, D_MODEL, D_MODEL)])
    return pl.pallas_call(
        _proj_seq_kernel,
        grid=(M // tm,),
        in_specs=[pl.BlockSpec((tm, D_MODEL), row), cw((1, D_MODEL)), cw((NSA_WIDTH, D_MODEL)),
                  cw((N_KV_GROUPS * KV_WIDTH, D_MODEL)), cw(wkrow.shape), cw(wgnt.shape), cw(wrest.shape),
                  pl.BlockSpec((tm, LANES), tab), pl.BlockSpec((tm, LANES), tab),
                  pl.BlockSpec((LANES, tm), tab_t), pl.BlockSpec((LANES, tm), tab_t)],
        out_specs=out_specs,
        out_shape=out_shape,
        compiler_params=pltpu.CompilerParams(
            dimension_semantics=("arbitrary",), vmem_limit_bytes=VMEM_LIMIT_BYTES),
        name="mixer_proj_seq",
    )(x, g.reshape(1, -1), b16(wq.T), b16(wkv.T), b16(wkrow), b16(wgnt), b16(wrest), cos, sin, cos.T, sin.T)


def _merge_kernel(oa_ref, u_ref, v_ref, ga_ref, gb_ref, x_ref, lng_ref, lnb_ref, wsp_ref, bsx_ref,
                  wa_ref, wb_ref, wout_ref, gc_ref, wcq_ref, x1_ref, hq_ref):
    tm = x_ref.shape[0]
    v = v_ref[...]
    vc = v - jnp.mean(v, axis=-1, keepdims=True)
    var = jnp.mean(vc * vc, axis=-1, keepdims=True)
    vn = vc * lax.rsqrt(var + NORM_EPS) * lng_ref[...] + lnb_ref[...]
    lane = _iota((GMLP_CHUNK, LANES), 1)
    chunks = []
    for c in range(tm // GMLP_CHUNK):
        cols = []
        for pr in range(GMLP_GROUPS // 2):
            pair = vn[c * GMLP_CHUNK:(c + 1) * GMLP_CHUNK, pr * LANES:(pr + 1) * LANES]
            rhs = jnp.concatenate([jnp.where(lane < GMLP_GROUP_DIM, pair, 0.0),
                                   jnp.where(lane >= GMLP_GROUP_DIM, pair, 0.0)], axis=0).astype(jnp.bfloat16)
            cols.append(_dot(wsp_ref[pr], rhs))
        chunks.append(jnp.concatenate(cols, axis=1) + bsx_ref[...])
    o_b = u_ref[...] * jnp.concatenate(chunks, axis=0)
    m = (jax.nn.sigmoid(ga_ref[...]) * _dot(oa_ref[...].astype(jnp.bfloat16), wa_ref[...])
         + jax.nn.sigmoid(gb_ref[...]) * _dot(o_b.astype(jnp.bfloat16), wb_ref[...]))
    x1 = x_ref[...] + _dot(m.astype(jnp.bfloat16), wout_ref[...])
    x1_ref[...] = x1
    hq_ref[...] = _dot(_rms(x1, gc_ref[...]).astype(jnp.bfloat16), wcq_ref[...])


def gmlp_weights(ws, bs, chunk_len):
    L = chunk_len
    w = jnp.tril(ws[:, :L, :L])
    reps = GMLP_CHUNK // L
    w = jnp.einsum('gij,ab->gaibj', w, jnp.eye(reps, dtype=w.dtype)).reshape(GMLP_GROUPS, GMLP_CHUNK, GMLP_CHUNK)
    wsp = jnp.concatenate([w[0::2], w[1::2]], axis=2)
    bsx = jnp.tile(jnp.repeat(bs[:, :L].T, GMLP_GROUP_DIM, axis=1), (reps, 1))
    return wsp.astype(jnp.bfloat16), bsx


def merge_and_cross_q(o_a, u, v, ga, gb, x, ln_g, ln_b, wsp, bsx, w_a, w_b, w_out, g_cross, w_cq):
    M = x.shape[0]
    tm = min(ROW_TILE, M)
    one = pl.Buffered(1)
    row = lambda i: (i, 0)
    cw = lambda shape: pl.BlockSpec(shape, lambda i: (0,) * len(shape), pipeline_mode=one)
    b16 = lambda t: t.astype(jnp.bfloat16)
    out = jax.ShapeDtypeStruct((M, D_MODEL), jnp.float32)
    return pl.pallas_call(
        _merge_kernel,
        grid=(M // tm,),
        in_specs=[
            pl.BlockSpec((tm, NSA_WIDTH), row), pl.BlockSpec((tm, GMLP_WIDTH), row),
            pl.BlockSpec((tm, GMLP_WIDTH), row), pl.BlockSpec((tm, D_MODEL), row),
            pl.BlockSpec((tm, D_MODEL), row), pl.BlockSpec((tm, D_MODEL), row),
            cw((1, GMLP_WIDTH)), cw((1, GMLP_WIDTH)), cw(wsp.shape), cw(bsx.shape),
            cw(w_a.shape), cw(w_b.shape), cw(w_out.shape), cw((1, D_MODEL)), cw(w_cq.shape),
        ],
        out_specs=[pl.BlockSpec((tm, D_MODEL), row)] * 2,
        out_shape=[out, out],
        compiler_params=pltpu.CompilerParams(
            dimension_semantics=("arbitrary",), vmem_limit_bytes=VMEM_LIMIT_BYTES),
        name="merge_branches",
    )(o_a, u, v, ga, gb, x, ln_g.reshape(1, -1), ln_b.reshape(1, -1), wsp, bsx,
      b16(w_a), b16(w_b), b16(w_out), g_cross.reshape(1, -1), b16(w_cq))


def _memkv_kernel(x_ref, g_ref, wk_ref, wv_ref, k_ref, v_ref):
    h = _rms(x_ref[...], g_ref[...]).astype(jnp.bfloat16)
    k_ref[...] = _dot(h, wk_ref[...])
    v_ref[...] = _dot(h, wv_ref[...])


def memory_kv(mem, g_mem, w_ck, w_cv):
    M = mem.shape[0]
    tm = min(ROW_TILE, M)
    one = pl.Buffered(1)
    row = lambda i: (i, 0)
    cw = lambda shape: pl.BlockSpec(shape, lambda i: (0, 0), pipeline_mode=one)
    out = jax.ShapeDtypeStruct((M, D_MODEL), jnp.float32)
    return pl.pallas_call(
        _memkv_kernel,
        grid=(M // tm,),
        in_specs=[pl.BlockSpec((tm, D_MODEL), row), cw((1, D_MODEL)), cw(w_ck.shape), cw(w_cv.shape)],
        out_specs=[pl.BlockSpec((tm, D_MODEL), row)] * 2,
        out_shape=[out, out],
        compiler_params=pltpu.CompilerParams(
            dimension_semantics=("arbitrary",), vmem_limit_bytes=VMEM_LIMIT_BYTES),
        name="memory_kv",
    )(mem, g_mem.reshape(1, -1), w_ck.astype(jnp.bfloat16), w_cv.astype(jnp.bfloat16))


def _cross_kernel(q_ref, mk_ref, mv_ref, o_ref):
    q = q_ref[0] * CROSS_SCALE
    outs = []
    for h in range(CROSS_HEADS):
        sl = slice(h * CROSS_HEAD_DIM, (h + 1) * CROSS_HEAD_DIM)
        s = _dot_nt(q[:, sl].astype(jnp.bfloat16), mk_ref[0, :, sl].astype(jnp.bfloat16))
        e = jnp.exp(s - jnp.max(s, axis=1, keepdims=True))
        p = e / jnp.sum(e, axis=1, keepdims=True)
        outs.append(_dot(p.astype(jnp.bfloat16), mv_ref[0, :, sl].astype(jnp.bfloat16)))
    o_ref[0] = jnp.concatenate(outs, axis=1)


def cross_attention(hq, mk, mv, *, rows):
    NB, R, _ = hq.shape
    return pl.pallas_call(
        _cross_kernel,
        grid=(NB, R // rows),
        in_specs=[pl.BlockSpec((1, rows, D_MODEL), lambda b, i: (b, i, 0)),
                  pl.BlockSpec((1, N_MEM, D_MODEL), lambda b, i: (b, 0, 0)),
                  pl.BlockSpec((1, N_MEM, D_MODEL), lambda b, i: (b, 0, 0))],
        out_specs=pl.BlockSpec((1, rows, D_MODEL), lambda b, i: (b, i, 0)),
        out_shape=jax.ShapeDtypeStruct((NB, R, D_MODEL), jnp.float32),
        compiler_params=pltpu.CompilerParams(
            dimension_semantics=("arbitrary", "arbitrary"), vmem_limit_bytes=VMEM_LIMIT_BYTES),
        name="cross_attention",
    )(hq, mk, mv)


def _cross_heads_kernel(q_ref, mk_ref, mv_ref, o_ref, *, rows):
    heads = [slice(h * CROSS_HEAD_DIM, (h + 1) * CROSS_HEAD_DIM) for h in range(CROSS_HEADS)]
    q = jnp.concatenate([q_ref[0] * CROSS_SCALE, jnp.zeros((SUBLANES - rows, D_MODEL), jnp.float32)], axis=0)
    s = jnp.concatenate([_dot_nt(q[:, sl].astype(jnp.bfloat16), mk_ref[0, :, sl].astype(jnp.bfloat16))
                         for sl in heads], axis=0)
    e = jnp.exp(s - jnp.max(s, axis=1, keepdims=True))
    p = (e / jnp.sum(e, axis=1, keepdims=True)).astype(jnp.bfloat16)
    o = jnp.concatenate([_dot(p[h * SUBLANES:(h + 1) * SUBLANES], mv_ref[0, :, sl].astype(jnp.bfloat16))
                         for h, sl in enumerate(heads)], axis=1)
    o_ref[0] = o[:rows]


def cross_attention_cached(hq, mk, mv):
    NB, R, _ = hq.shape
    mem_spec = pl.BlockSpec((1, N_MEM, D_MODEL), lambda b: (b, 0, 0))
    return pl.pallas_call(
        functools.partial(_cross_heads_kernel, rows=R),
        grid=(NB,),
        in_specs=[pl.BlockSpec((1, R, D_MODEL), lambda b: (b, 0, 0)), mem_spec, mem_spec],
        out_specs=pl.BlockSpec((1, R, D_MODEL), lambda b: (b, 0, 0)),
        out_shape=jax.ShapeDtypeStruct((NB, R, D_MODEL), jnp.float32),
        compiler_params=pltpu.CompilerParams(
            dimension_semantics=("arbitrary",), vmem_limit_bytes=VMEM_LIMIT_BYTES),
        name="cross_attention_cached",
    )(hq, mk, mv)


def _tail_kernel(x_ref, o_ref, wco_ref, gf_ref, wg_ref, wu_ref, wd_ref, gfin_ref, y_ref, acc_ref):
    x = x_ref[...] + _dot(o_ref[...].astype(jnp.bfloat16), wco_ref[...])
    h = _rms(x, gf_ref[...]).astype(jnp.bfloat16)
    acc_ref[...] = x
    for c in range(D_FF // FF_CHUNK):
        sl = slice(c * FF_CHUNK, (c + 1) * FF_CHUNK)
        a = _dot(h, wg_ref[:, sl])
        b = _dot(h, wu_ref[:, sl])
        t = (a * jax.nn.sigmoid(a) * b).astype(jnp.bfloat16)
        acc_ref[...] += _dot(t, wd_ref[sl, :])
    y_ref[...] = _rms(acc_ref[...], gfin_ref[...])


def tail(x1, o, w_co, g_ffn, w_gate, w_up, w_down, g_final):
    M = x1.shape[0]
    tm = min(ROW_TILE, M)
    row = lambda i: (i, 0)
    cw = lambda shape: pl.BlockSpec(shape, lambda i: (0, 0), pipeline_mode=pl.Buffered(1))
    b16 = lambda t: t.astype(jnp.bfloat16)
    return pl.pallas_call(
        _tail_kernel,
        grid=(M // tm,),
        in_specs=[pl.BlockSpec((tm, D_MODEL), row), pl.BlockSpec((tm, D_MODEL), row),
                  cw(w_co.shape), cw((1, D_MODEL)), cw(w_gate.shape), cw(w_up.shape), cw(w_down.shape),
                  cw((1, D_MODEL))],
        out_specs=pl.BlockSpec((tm, D_MODEL), row),
        out_shape=jax.ShapeDtypeStruct((M, D_MODEL), jnp.float32),
        scratch_shapes=[pltpu.VMEM((tm, D_MODEL), jnp.float32)],
        compiler_params=pltpu.CompilerParams(
            dimension_semantics=("arbitrary",), vmem_limit_bytes=VMEM_LIMIT_BYTES),
        name="tail",
    )(x1, o, b16(w_co), g_ffn.reshape(1, -1), b16(w_gate), b16(w_up), b16(w_down), g_final.reshape(1, -1))


def kernel(x_prompt, x_sample, mem_prompt, cache_cmp_k, cache_cmp_v, cache_slc_k, cache_slc_v,
           page_table, state_win_k, state_win_v, cache_mem_k, cache_mem_v,
           g_mix, w_in, cmp_pe, cmp_w1, cmp_w2, gmlp_ln_g, gmlp_ln_b, gmlp_ws, gmlp_bs,
           w_branch_a, w_branch_b, w_out, g_cross, g_mem, w_cq, w_ck, w_cv, w_co,
           g_ffn, w_gate, w_up, w_down, g_final):
    B, S, _ = x_prompt.shape
    DB, DS, _ = x_sample.shape
    n_pages = page_table.shape[1]
    assert g_mix.shape[0] == 1, "single-layer step"
    l = 0
    cmpw = cmp_weights(cmp_pe[l], cmp_w1[l], cmp_w2[l])
    merge_w = (gmlp_ln_g[l], gmlp_ln_b[l])
    merge_tail = (w_branch_a[l], w_branch_b[l], w_out[l], g_cross[l], w_cq[l])
    ffn_w = (w_co[l], g_ffn[l], w_gate[l], w_up[l], w_down[l], g_final)
    to_t = lambda t: jnp.transpose(t, (0, 2, 3, 1)).reshape(t.shape[0], KV_WIDTH, t.shape[1])
    from_t = lambda t: jnp.transpose(t.reshape(t.shape[0], N_KV_HEADS, HEAD_DIM, t.shape[2]), (0, 3, 1, 2))
    mem3 = lambda t: t.reshape(-1, N_MEM, D_MODEL)

    xp = x_prompt.reshape(B * S, D_MODEL)
    (qt, kct, vct, kst, vst, kwt, vwt, ks_b, vst_b, kw_b, vwt_b, gnt, u, v, ga, gb) = mixer_proj_seq(
        xp, g_mix[l], w_in[l], B)
    ck, cvt = cmp_blocks_dense(kct, vct, *cmpw)
    per_b = lambda t: t.reshape(B, S, -1)
    o_a = nsa_prompt_attention(qt, gnt, ck, cvt, per_b(ks_b), vst_b, per_b(kw_b), vwt_b)
    x1, hq = merge_and_cross_q(o_a.reshape(B * S, NSA_WIDTH), u, v, ga, gb, xp, *merge_w,
                               *gmlp_weights(gmlp_ws[l], gmlp_bs[l], min(S, GMLP_CHUNK)), *merge_tail)
    mk, mv = memory_kv(mem_prompt.reshape(B * N_MEM, D_MODEL), g_mem[l], w_ck[l], w_cv[l])
    o = cross_attention(hq.reshape(B, S, D_MODEL), mem3(mk), mem3(mv), rows=min(S, ROW_TILE))
    y_prompt = tail(x1, o.reshape(B * S, D_MODEL), *ffn_w).reshape(B, S, D_MODEL)
    wb_p = min(WINDOW, S)
    mem5 = lambda t: t.reshape(B, N_MEM, CROSS_HEADS, CROSS_HEAD_DIM)
    prompt_new = (from_t(kct), from_t(vct), from_t(kst), from_t(vst),
                  from_t(kwt[:, :, S - wb_p:]), from_t(vwt[:, :, S - wb_p:]), mem5(mk), mem5(mv))

    xs = x_sample.reshape(DB * DS, D_MODEL)
    pos_s = n_pages * PAGE_SIZE + jnp.arange(DS, dtype=jnp.int32)
    (q, kc, vc, ks, vs, kw, vw, gn, u, v, ga, gb) = mixer_proj_rows(xs, g_mix[l], w_in[l], jnp.tile(pos_s, DB))
    ck, cv = cmp_blocks_paged(to_t(cache_cmp_k[l]), to_t(cache_cmp_v[l]), page_table, *cmpw)
    per_seq = lambda t: t.reshape(DB, DS, -1)
    o_a, win_kt, win_vt = nsa_sample_attention(
        per_seq(q), per_seq(gn), ck, cv, to_t(state_win_k[l]), to_t(state_win_v[l]),
        per_seq(ks), per_seq(vs), per_seq(kw), per_seq(vw),
        to_t(cache_slc_k[l]), to_t(cache_slc_v[l]), page_table)
    x1, hq = merge_and_cross_q(o_a.reshape(DB * DS, NSA_WIDTH), u, v, ga, gb, xs, *merge_w,
                               *gmlp_weights(gmlp_ws[l], gmlp_bs[l], min(DS, GMLP_CHUNK)), *merge_tail)
    o = cross_attention_cached(per_seq(hq), mem3(cache_mem_k[l]), mem3(cache_mem_v[l]))
    y_sample = tail(x1, o.reshape(DB * DS, D_MODEL), *ffn_w).reshape(DB, DS, D_MODEL)
    heads = lambda t: t.reshape(DB, DS, N_KV_HEADS, HEAD_DIM)
    sample_new = (heads(kc), heads(vc), heads(ks), heads(vs), from_t(win_kt), from_t(win_vt), per_seq(v))

    return (y_prompt, y_sample) + tuple(t[None] for t in prompt_new + sample_new)
```

```python
import functools

import jax
import jax.numpy as jnp
import numpy as np
from jax import lax
from jax.experimental import pallas as pl
from jax.experimental.pallas import tpu as pltpu

D_MODEL = 1024
N_HEADS = 8
HEAD_DIM = 64
N_KV_HEADS = 2
GROUP = N_HEADS // N_KV_HEADS
NSA_WIDTH = N_HEADS * HEAD_DIM
KV_WIDTH = N_KV_HEADS * HEAD_DIM
CMP_LEN = 32
CMP_STRIDE = 16
CMP_HIDDEN = 128
SLC_BLK = 64
SLC_SHIFT = 6
TOP_N = 16
WINDOW = 512
ROPE_THETA = 10000.0
ATTN_SCALE = HEAD_DIM ** -0.5
LOG2E = 1.4426950408889634
FORCE_BONUS = 1e4
NEG_INF = -1e30
N_BRANCH = 3
GMLP_WIDTH = 512
GMLP_GROUPS = 8
GMLP_GROUP_DIM = GMLP_WIDTH // GMLP_GROUPS
GMLP_CHUNK = 128
N_MEM = 256
CROSS_HEADS = 4
CROSS_HEAD_DIM = D_MODEL // CROSS_HEADS
CROSS_SCALE = CROSS_HEAD_DIM ** -0.5
D_FF = -(-8 * D_MODEL // (3 * 256)) * 256
NORM_EPS = 1e-6
PAGE_SIZE = 128
CHUNKS_PER_PAGE = PAGE_SIZE // CMP_STRIDE
N_KV_GROUPS = 6

LANES = 128
SUBLANES = 8
VMEM_LIMIT_BYTES = 48 * 1024 * 1024
PAGED_VMEM_LIMIT_BYTES = 56 * 1024 * 1024
SEL_TILE = 512
SAMPLE_KEY_CHUNK = 2048
ROW_TILE = 512
FF_CHUNK = 256

_NT = (((1,), (1,)), ((), ()))


def _dot_nt(a, b):
    return lax.dot_general(a, b, _NT, preferred_element_type=jnp.float32)


def _dot(a, b):
    return jnp.dot(a, b, preferred_element_type=jnp.float32)


def _split_bf16(x):
    hi = x.astype(jnp.bfloat16)
    lo = (x - hi.astype(jnp.float32)).astype(jnp.bfloat16)
    return hi, lo


def _iota(shape, dim):
    return lax.broadcasted_iota(jnp.int32, shape, dim)


def _rms(x, g):
    return x * lax.rsqrt(jnp.mean(x * x, axis=-1, keepdims=True) + NORM_EPS) * g


def _cmp_from_transposed(xt_ref, p, xrows, wbig_ref, w2big_ref, peterm_ref, out_ref, n_rows):
    n_chunks = n_rows // CMP_STRIDE
    for j in range(n_rows // LANES):
        xrows[j * LANES:(j + 1) * LANES, :] = xt_ref[:, j * LANES:(j + 1) * LANES].T
    acc = jnp.zeros((n_chunks, 4 * CMP_HIDDEN), jnp.float32)
    for sp in range(CMP_STRIDE // 2):
        a0 = xrows[pl.ds(2 * sp, n_chunks, stride=CMP_STRIDE), :]
        a1 = xrows[pl.ds(2 * sp + 1, n_chunks, stride=CMP_STRIDE), :]
        lhs = jnp.concatenate([a0, a1], axis=1).astype(jnp.bfloat16)
        acc = acc + _dot(lhs, wbig_ref[p, sp])
    lead = acc[:, :2 * CMP_HIDDEN]
    trail_next = pltpu.roll(acc[:, 2 * CMP_HIDDEN:], n_chunks - 1, axis=0)
    pre = lead + trail_next + peterm_ref[p]
    out = _dot(jax.nn.gelu(pre).astype(jnp.bfloat16), w2big_ref[p])
    out_ref[0] = out.astype(out_ref.dtype)


def _page_gather(pt_ref, pools, buf, sem, n_pages):
    b = pl.program_id(0)
    nb = pl.num_programs(0)
    slot = b % 2

    def page_copy(bb, sl, j, p):
        return pltpu.make_async_copy(pools[p].at[pt_ref[bb, j]],
                                     buf.at[sl, p, :, pl.ds(j * PAGE_SIZE, PAGE_SIZE)], sem.at[sl, p])

    def start(bb, sl):
        for p in range(len(pools)):
            for j in range(n_pages):
                page_copy(bb, sl, j, p).start()

    @pl.when(b == 0)
    def _():
        start(0, 0)

    @pl.when(b + 1 < nb)
    def _():
        start(b + 1, 1 - slot)

    def wait():
        for p in range(len(pools)):
            for j in range(n_pages):
                page_copy(b, slot, j, p).wait()

    return wait


def _cmp_paged_kernel(pt_ref, pool_k, pool_v, wbig_ref, w2big_ref, peterm_ref, ck_ref, cv_ref,
                      buf, xrows, sem, *, n_pages):
    wait = _page_gather(pt_ref, (pool_k, pool_v), buf, sem, n_pages)
    wait()
    slot = pl.program_id(0) % 2
    for p, out_ref in ((0, ck_ref), (1, cv_ref)):
        _cmp_from_transposed(buf.at[slot, p], p, xrows, wbig_ref, w2big_ref, peterm_ref, out_ref,
                             n_pages * PAGE_SIZE)


def _cmp_dense_kernel(kt_ref, vt_ref, wbig_ref, w2big_ref, peterm_ref, ck_ref, cv_ref, xrows):
    for p, (src, out_ref) in enumerate(((kt_ref, ck_ref), (vt_ref, cv_ref))):
        _cmp_from_transposed(src.at[0], p, xrows, wbig_ref, w2big_ref, peterm_ref, out_ref, src.shape[2])


def cmp_weights(cmp_pe, cmp_w1, cmp_w2):
    eye = jnp.eye(N_KV_HEADS, dtype=jnp.float32)
    w1r = cmp_w1.reshape(2, 2, CMP_STRIDE // 2, 2, HEAD_DIM, CMP_HIDDEN)
    wbig = jnp.einsum('jtpsdf,hk->jpshdtkf', w1r, eye)
    wbig = wbig.reshape(2, CMP_STRIDE // 2, 2 * KV_WIDTH, 4 * CMP_HIDDEN).astype(jnp.bfloat16)
    w2big = jnp.einsum('jfd,hk->jhfkd', cmp_w2, eye).reshape(2, 2 * CMP_HIDDEN, KV_WIDTH)
    pe_term = jnp.einsum('jsd,jsdf->jf', cmp_pe, cmp_w1)
    peterm = jnp.tile(pe_term[:, None, :], (1, 1, N_KV_HEADS))
    return wbig, w2big.astype(jnp.bfloat16), peterm


def _cmp_weight_specs(wbig, w2big, peterm, nargs):
    c3 = (lambda b: (0, 0, 0)) if nargs == 1 else (lambda b, pt: (0, 0, 0))
    c4 = (lambda b: (0, 0, 0, 0)) if nargs == 1 else (lambda b, pt: (0, 0, 0, 0))
    one = pl.Buffered(1)
    return [pl.BlockSpec(wbig.shape, c4, pipeline_mode=one),
            pl.BlockSpec(w2big.shape, c3, pipeline_mode=one),
            pl.BlockSpec(peterm.shape, c3, pipeline_mode=one)]


def cmp_blocks_paged(pool_k, pool_v, page_table, wbig, w2big, peterm):
    n_seq, n_pages = page_table.shape
    n_rows = n_pages * PAGE_SIZE
    n_chunks = n_rows // CMP_STRIDE
    out = jax.ShapeDtypeStruct((n_seq, n_chunks, KV_WIDTH), jnp.bfloat16)
    grid_spec = pltpu.PrefetchScalarGridSpec(
        num_scalar_prefetch=1,
        grid=(n_seq,),
        in_specs=[pl.BlockSpec(memory_space=pl.ANY), pl.BlockSpec(memory_space=pl.ANY)]
        + _cmp_weight_specs(wbig, w2big, peterm, 2),
        out_specs=[pl.BlockSpec((1, n_chunks, KV_WIDTH), lambda b, pt: (b, 0, 0))] * 2,
        scratch_shapes=[
            pltpu.VMEM((2, 2, KV_WIDTH, n_rows), jnp.float32),
            pltpu.VMEM((n_rows, KV_WIDTH), jnp.float32),
            pltpu.SemaphoreType.DMA((2, 2)),
        ],
    )
    return pl.pallas_call(
        functools.partial(_cmp_paged_kernel, n_pages=n_pages),
        grid_spec=grid_spec,
        out_shape=[out, out],
        compiler_params=pltpu.CompilerParams(
            dimension_semantics=("arbitrary",), vmem_limit_bytes=PAGED_VMEM_LIMIT_BYTES),
        name="cmp_blocks_paged",
    )(page_table, pool_k, pool_v, wbig, w2big, peterm)


def cmp_blocks_dense(kt, vt, wbig, w2big, peterm):
    B, _, S = kt.shape
    n_chunks = S // CMP_STRIDE
    per_b = lambda b: (b, 0, 0)
    return pl.pallas_call(
        _cmp_dense_kernel,
        grid=(B,),
        in_specs=[pl.BlockSpec((1, KV_WIDTH, S), per_b), pl.BlockSpec((1, KV_WIDTH, S), per_b)]
        + _cmp_weight_specs(wbig, w2big, peterm, 1),
        out_specs=[pl.BlockSpec((1, n_chunks, KV_WIDTH), per_b)] * 2,
        out_shape=[jax.ShapeDtypeStruct((B, n_chunks, KV_WIDTH), jnp.bfloat16)] * 2,
        scratch_shapes=[pltpu.VMEM((S, KV_WIDTH), jnp.float32)],
        compiler_params=pltpu.CompilerParams(
            dimension_semantics=("arbitrary",), vmem_limit_bytes=VMEM_LIMIT_BYTES),
        name="cmp_blocks_dense",
    )(kt, vt, wbig, w2big, peterm)


def cmp_to_slc_matrix(n_chunks, n_cols):
    cs = np.arange(n_chunks)[:, None] * CMP_STRIDE
    ss = np.arange(n_cols)[None, :] * SLC_BLK
    shared = np.minimum(cs + CMP_LEN, ss + SLC_BLK) - np.maximum(cs, ss)
    w = np.maximum(shared, 0).astype(np.float32) / CMP_LEN
    w[n_chunks - 1] = 0.0
    return w


def _nsa_prompt_kernel(q_ref, gn_ref, ck_ref, cv_ref, ks_ref, vs_ref, kw_ref, vw_ref,
                       wcst_ref, gexp_ref, o_ref, lhs_ref, m_ref, l_ref, acc_ref, *, tq, n_chunks):
    i = pl.program_id(1)
    q0 = i * tq
    qpb = _padded_queries(q_ref[0], tq).astype(jnp.bfloat16)

    def qpos_of(shape):
        return q0 + (_iota(shape, 0) & (tq - 1))

    s = _dot_nt(qpb, ck_ref[0])
    n_idx = _iota(s.shape, 1)
    valid = (n_idx * CMP_STRIDE + (CMP_LEN - 1) <= qpos_of(s.shape)) & (n_idx < n_chunks - 1)
    p_cmp = _masked_softmax2(s, valid)
    o_cmp = _dot(p_cmp.astype(jnp.bfloat16), cv_ref[0])

    blk = _iota((LANES, tq), 0)
    blkf = blk.astype(jnp.float32)
    qblk = (q0 + _iota((LANES, tq), 1)) >> SLC_SHIFT
    forced = (blk == 0) | (blk == qblk) | (blk == qblk - 1)
    wcst = wcst_ref[...]
    nsel = []
    for psum in _group_sum(p_cmp, tq):
        hi, lo = _split_bf16(psum)
        sc = _dot_nt(wcst, hi) + _dot_nt(wcst, lo)
        sc = jnp.where(blk <= qblk, jnp.where(forced, sc + FORCE_BONUS, sc), NEG_INF)
        sel = jnp.zeros_like(sc)
        for _ in range(TOP_N):
            mx = jnp.max(sc, axis=0, keepdims=True)
            first = jnp.min(jnp.where(sc == mx, blkf, float(LANES)), axis=0, keepdims=True)
            hit = blkf == first
            sel = jnp.where(hit, 1.0, sel)
            sc = jnp.where(hit, -jnp.inf, sc)
        nsel.append((1.0 - sel).T.astype(jnp.bfloat16))
    for h in range(N_KV_HEADS):
        r0 = h * GROUP * tq
        lhs_ref[r0:r0 + GROUP * tq, :LANES] = qpb[r0:r0 + GROUP * tq]
        for g in range(GROUP):
            lhs_ref[r0 + g * tq:r0 + (g + 1) * tq, LANES:] = nsel[h]

    m_ref[...] = jnp.full(m_ref.shape, -3e38, jnp.float32)
    l_ref[...] = jnp.zeros(l_ref.shape, jnp.float32)
    acc_ref[...] = jnp.zeros(acc_ref.shape, jnp.float32)
    n_tiles = (q0 + tq + SEL_TILE - 1) // SEL_TILE

    def sel_tile(kt, causal):
        k0 = pl.multiple_of(kt * SEL_TILE, SEL_TILE)
        kblk = (k0 + _iota((LANES, SEL_TILE), 1)) >> SLC_SHIFT
        ebig = jnp.where(kblk == _iota((LANES, SEL_TILE), 0), NEG_INF, 0.0).astype(jnp.bfloat16)
        rhs = jnp.concatenate([ks_ref[0, :, pl.ds(k0, SEL_TILE)], ebig], axis=0)
        st = _dot(lhs_ref[...], rhs)
        if causal:
            st = jnp.where(k0 + _iota(st.shape, 1) <= qpos_of(st.shape), st, NEG_INF)
        cols = [st[:, c * LANES:(c + 1) * LANES] for c in range(SEL_TILE // LANES)]
        mt = functools.reduce(jnp.maximum, cols)
        m_prev = m_ref[...]
        m_new = jnp.maximum(m_prev, jnp.max(mt, axis=1, keepdims=True))
        alpha = jnp.exp2(m_prev - m_new)
        ps = [jnp.exp2(c - m_new) for c in cols]
        l_ref[...] = alpha * l_ref[...] + functools.reduce(jnp.add, ps)
        pb = jnp.concatenate(ps, axis=1).astype(jnp.bfloat16)
        acc_ref[...] = alpha * acc_ref[...] + _dot_nt(pb, vs_ref[0, :, pl.ds(k0, SEL_TILE)])
        m_ref[...] = m_new

    def body(kt, carry):
        sel_tile(kt, False)
        return carry

    lax.fori_loop(0, n_tiles - 1, body, 0)
    sel_tile(n_tiles - 1, True)
    o_sel = acc_ref[...] / jnp.sum(l_ref[...], axis=1, keepdims=True)

    w0 = pl.multiple_of(jnp.maximum(q0 - WINDOW, 0), tq)
    sw = _dot(qpb, kw_ref[0, :, pl.ds(w0, WINDOW + tq)])
    kpos = w0 + _iota(sw.shape, 1)
    qpos = qpos_of(sw.shape)
    p_win = _masked_softmax2(sw, (kpos <= qpos) & (kpos > qpos - WINDOW))
    o_win = _dot_nt(p_win.astype(jnp.bfloat16), vw_ref[0, :, pl.ds(w0, WINDOW + tq)])

    o_ref[0] = _combine((o_cmp, o_sel, o_win), gn_ref[0], gexp_ref[...], tq)


def nsa_prompt_attention(q, gn, ck, cv, ks, vs, kw, vw, *, tq=LANES):
    B, S, _ = q.shape
    n_chunks = ck.shape[1]
    assert S % SEL_TILE == 0 and S // SLC_BLK <= LANES and tq == LANES and S >= WINDOW + tq
    wcst = jnp.asarray(cmp_to_slc_matrix(n_chunks, LANES).T, jnp.bfloat16)
    gexp = gate_expand_matrix()
    rows = N_HEADS * tq
    per_b = lambda b, i: (b, 0, 0)
    tile = lambda b, i: (b, i, 0)
    const = lambda b, i: (0, 0)
    return pl.pallas_call(
        functools.partial(_nsa_prompt_kernel, tq=tq, n_chunks=n_chunks),
        grid=(B, S // tq),
        in_specs=[
            pl.BlockSpec((1, tq, NSA_WIDTH), tile),
            pl.BlockSpec((1, tq, LANES), tile),
            pl.BlockSpec((1, n_chunks, KV_WIDTH), per_b),
            pl.BlockSpec((1, n_chunks, KV_WIDTH), per_b),
            pl.BlockSpec((1, KV_WIDTH, S), per_b),
            pl.BlockSpec((1, KV_WIDTH, S), per_b),
            pl.BlockSpec((1, KV_WIDTH, S), per_b),
            pl.BlockSpec((1, KV_WIDTH, S), per_b),
            pl.BlockSpec(wcst.shape, const),
            pl.BlockSpec(gexp.shape, const),
        ],
        out_specs=pl.BlockSpec((1, tq, NSA_WIDTH), tile),
        out_shape=jax.ShapeDtypeStruct((B, S, NSA_WIDTH), jnp.float32),
        scratch_shapes=[
            pltpu.VMEM((rows, 2 * LANES), jnp.bfloat16),
            pltpu.VMEM((rows, LANES), jnp.float32),
            pltpu.VMEM((rows, LANES), jnp.float32),
            pltpu.VMEM((rows, LANES), jnp.float32),
        ],
        compiler_params=pltpu.CompilerParams(
            dimension_semantics=("arbitrary", "arbitrary"), vmem_limit_bytes=VMEM_LIMIT_BYTES),
        name="nsa_prompt",
    )(q, gn, ck, cv, ks, vs, kw, vw, wcst, gexp)


def _padded_queries(q, tq):
    lane = _iota((tq, LANES), 1)
    blocks = []
    for h in range(N_KV_HEADS):
        for g in range(GROUP):
            c = h * GROUP + g
            pair = q[:, LANES * (c // 2):LANES * (c // 2 + 1)]
            if c % 2 != h:
                pair = pltpu.roll(pair, HEAD_DIM, axis=1)
            keep = (lane >= HEAD_DIM * h) & (lane < HEAD_DIM * (h + 1))
            blocks.append(jnp.where(keep, pair, 0.0))
    return jnp.concatenate(blocks, axis=0)


def _masked_softmax(s, valid):
    sm = jnp.where(valid, s, NEG_INF)
    mx = jnp.max(sm, axis=1, keepdims=True)
    e = jnp.where(valid, jnp.exp(sm - mx), 0.0)
    den = jnp.maximum(jnp.sum(e, axis=1, keepdims=True), 1e-30)
    return e / den


def _masked_softmax2(s2, valid):
    sm = jnp.where(valid, s2, NEG_INF)
    mx = jnp.max(sm, axis=1, keepdims=True)
    e = jnp.exp2(sm - mx)
    inv = jnp.where(mx > 0.5 * NEG_INF, 1.0 / jnp.sum(e, axis=1, keepdims=True), 0.0)
    return e * inv


def _group_sum(p, tq):
    out = []
    for h in range(N_KV_HEADS):
        r0 = h * GROUP * tq
        acc = p[r0:r0 + tq]
        for g in range(1, GROUP):
            acc = acc + p[r0 + g * tq:r0 + (g + 1) * tq]
        out.append(acc)
    return out


def _combine(branches, gn, gexp_w, tq):
    hi, lo = _split_bf16(gn)
    gexp = _dot(hi, gexp_w) + _dot(lo, gexp_w)
    lane = _iota((tq, LANES), 1)
    cols = []
    for h in range(N_KV_HEADS):
        blks = []
        for g in range(GROUP):
            j = h * GROUP + g
            rows = slice(j * tq, (j + 1) * tq)
            o = None
            for br in range(N_BRANCH):
                gate = gexp[:, (j * N_BRANCH + br) * LANES:(j * N_BRANCH + br + 1) * LANES]
                term = gate * branches[br][rows]
                o = term if o is None else o + term
            blks.append(o)
        for jj in range(GROUP // 2):
            a, b2 = blks[2 * jj], blks[2 * jj + 1]
            if h == 0:
                b2 = pltpu.roll(b2, HEAD_DIM, axis=1)
            else:
                a = pltpu.roll(a, HEAD_DIM, axis=1)
            cols.append(jnp.where(lane < HEAD_DIM, a, b2))
    return jnp.concatenate(cols, axis=1)


def gate_expand_matrix():
    e = np.zeros((LANES, N_HEADS * N_BRANCH * LANES), np.float32)
    for j in range(N_HEADS * N_BRANCH):
        e[j, j * LANES:(j + 1) * LANES] = 1.0
    return jnp.asarray(e, jnp.bfloat16)


def _nsa_sample_kernel(pt_ref, q_ref, gn_ref, ck_ref, cv_ref, wk_ref, wv_ref,
                       ksn_ref, vsn_ref, kwn_ref, vwn_ref, pool_k, pool_v, ebig_ref, wcst_ref, gexp_ref,
                       o_ref, nwk_ref, nwv_ref, buf, sem, *, n_pages, n_new, tq):
    slot = pl.program_id(0) % 2
    past = n_pages * PAGE_SIZE
    n_chunks = n_pages * CHUNKS_PER_PAGE
    n_past_blk = past // SLC_BLK
    n_slc = -(-(past + n_new) // SLC_BLK)
    wb = wk_ref.shape[2]
    rows = N_HEADS * tq
    wait_pages = _page_gather(pt_ref, (pool_k, pool_v), buf, sem, n_pages)

    qp = _padded_queries(q_ref[0] * ATTN_SCALE, tq)
    qpb = qp.astype(jnp.bfloat16)

    def tok_of(shape):
        return _iota(shape, 0) & (tq - 1)

    def new_rows(ref):
        x = jnp.concatenate([ref[0], jnp.zeros((LANES - tq, KV_WIDTH), jnp.float32)], axis=0)
        return x.astype(jnp.bfloat16)

    s = _dot_nt(qpb, ck_ref[0])
    n_idx = _iota(s.shape, 1)
    valid = (n_idx * CMP_STRIDE + (CMP_LEN - 1) <= past + tok_of(s.shape)) & (n_idx < n_chunks - 1)
    p_cmp = _masked_softmax(s, valid)
    o_cmp = _dot(p_cmp.astype(jnp.bfloat16), cv_ref[0])

    ncol = wcst_ref.shape[0]
    psum = jnp.concatenate(_group_sum(p_cmp, tq)
                           + [jnp.zeros((LANES - N_KV_HEADS * tq, n_chunks), jnp.float32)], axis=0)
    hi, lo = _split_bf16(psum)
    wcst = wcst_ref[...]
    sc = _dot_nt(wcst, hi) + _dot_nt(wcst, lo)
    blk = _iota((ncol, LANES), 0)
    blkf = blk.astype(jnp.float32)
    qblk = (past + (_iota((ncol, LANES), 1) & (tq - 1))) >> SLC_SHIFT
    forced = (blk == 0) | (blk == qblk) | (blk == qblk - 1)
    sc = jnp.where(blk <= qblk, jnp.where(forced, sc + FORCE_BONUS, sc), NEG_INF)
    sc = jnp.where(blk < n_slc, sc, -jnp.inf)
    sel_t = jnp.zeros_like(sc)
    for _ in range(min(TOP_N, n_slc)):
        mx = jnp.max(sc, axis=0, keepdims=True)
        first = jnp.min(jnp.where(sc == mx, blkf, float(ncol)), axis=0, keepdims=True)
        hit = blkf == first
        sel_t = jnp.where(hit, 1.0, sel_t)
        sc = jnp.where(hit, -jnp.inf, sc)
    sel = sel_t.T
    nsel, sel_new = [], []
    for h in range(N_KV_HEADS):
        sel_h = sel[h * tq:(h + 1) * tq]
        nsel.append(jnp.concatenate([1.0 - sel_h[:, :LANES]] * GROUP, axis=0).astype(jnp.bfloat16))
        sel_new.append(jnp.broadcast_to(sel_h[:, n_past_blk:n_past_blk + 1], (tq, LANES)))
    lhs = jnp.concatenate(
        [jnp.concatenate([qpb[h * GROUP * tq:(h + 1) * GROUP * tq], nsel[h]], axis=1)
         for h in range(N_KV_HEADS)], axis=0)
    sel_new_rows = jnp.concatenate([sel_new[h] for h in range(N_KV_HEADS) for _ in range(GROUP)], axis=0)

    tn = _iota((rows, LANES), 1)
    new_ok = (tn <= tok_of((rows, LANES))) & (tn < n_new)
    s_sn = jnp.where(new_ok & (sel_new_rows > 0.5), _dot_nt(qpb, new_rows(ksn_ref)), NEG_INF)
    s_wn = jnp.where(new_ok, _dot_nt(qpb, new_rows(kwn_ref)), NEG_INF)

    wk = wk_ref[0]
    wv = wv_ref[0]
    s_w = _dot(qpb, wk.astype(jnp.bfloat16))
    kwpos = past - wb + _iota(s_w.shape, 1)
    qpos = past + tok_of(s_w.shape)
    s_w = jnp.where((kwpos >= 0) & (kwpos <= qpos) & (kwpos > qpos - WINDOW), s_w, NEG_INF)
    mx = jnp.maximum(jnp.max(s_w, axis=1, keepdims=True), jnp.max(s_wn, axis=1, keepdims=True))
    e_w = jnp.exp(s_w - mx)
    e_wn = jnp.exp(s_wn - mx)
    den = jnp.sum(e_w, axis=1, keepdims=True) + jnp.sum(e_wn, axis=1, keepdims=True)
    o_win = (_dot_nt(e_w.astype(jnp.bfloat16), wv.astype(jnp.bfloat16))
             + _dot(e_wn.astype(jnp.bfloat16), new_rows(vwn_ref))) / den

    col = _iota((KV_WIDTH, wb), 1)
    for src, new_ref, dst in ((wk, kwn_ref, nwk_ref), (wv, vwn_ref, nwv_ref)):
        new_t = jnp.concatenate([new_ref[0], jnp.zeros((LANES - tq, KV_WIDTH), jnp.float32)], axis=0).T
        tail = jnp.concatenate([jnp.zeros((KV_WIDTH, wb - LANES), jnp.float32), new_t], axis=1)
        dst[0] = jnp.where(col >= wb - n_new, pltpu.roll(tail, LANES - n_new, axis=1),
                           pltpu.roll(src, wb - n_new, axis=1))

    wait_pages()
    ck_keys = min(SAMPLE_KEY_CHUNK, past)
    s_chunks = []
    for c in range(past // ck_keys):
        kc = buf[slot, 0, :, pl.ds(c * ck_keys, ck_keys)].astype(jnp.bfloat16)
        rhs = jnp.concatenate([kc, ebig_ref[:, pl.ds(c * ck_keys, ck_keys)]], axis=0)
        s_chunks.append(_dot(lhs, rhs))
    mx = jnp.max(s_sn, axis=1, keepdims=True)
    for sc in s_chunks:
        mx = jnp.maximum(mx, jnp.max(sc, axis=1, keepdims=True))
    e_sn = jnp.exp(s_sn - mx)
    den = jnp.sum(e_sn, axis=1, keepdims=True)
    o_sel = _dot(e_sn.astype(jnp.bfloat16), new_rows(vsn_ref))
    for c, sc in enumerate(s_chunks):
        e = jnp.exp(sc - mx)
        den = den + jnp.sum(e, axis=1, keepdims=True)
        vc = buf[slot, 1, :, pl.ds(c * ck_keys, ck_keys)].astype(jnp.bfloat16)
        o_sel = o_sel + _dot_nt(e.astype(jnp.bfloat16), vc)
    o_sel = o_sel / den

    o_ref[0] = _combine((o_cmp, o_sel, o_win), gn_ref[0], gexp_ref[...], tq)


def nsa_sample_attention(q, gn, ck, cv, win_k, win_v, ks_new, vs_new, kw_new, vw_new,
                         pool_k, pool_v, page_table, *, tq=SUBLANES):
    DB, DS, _ = q.shape
    n_pages = page_table.shape[1]
    past = n_pages * PAGE_SIZE
    n_chunks = ck.shape[1]
    wb = win_k.shape[2]
    n_slc = -(-(past + DS) // SLC_BLK)
    assert DS <= tq and DS < CMP_STRIDE and past // SLC_BLK <= LANES and wb % LANES == 0
    assert past % min(SAMPLE_KEY_CHUNK, past) == 0
    ncol = -(-n_slc // LANES) * LANES
    pad_t = lambda t: jnp.pad(t, ((0, 0), (0, tq - DS), (0, 0)))
    kb = np.arange(past)[None, :] // SLC_BLK
    ebig = jnp.asarray(np.where(kb == np.arange(LANES)[:, None], NEG_INF, 0.0), jnp.bfloat16)
    wcst = jnp.asarray(cmp_to_slc_matrix(n_chunks, ncol).T, jnp.bfloat16)
    gexp = gate_expand_matrix()
    per_b = lambda b, pt: (b, 0, 0)
    const = lambda b, pt: (0, 0)
    blk3 = lambda n, w: pl.BlockSpec((1, n, w), per_b)
    grid_spec = pltpu.PrefetchScalarGridSpec(
        num_scalar_prefetch=1,
        grid=(DB,),
        in_specs=[
            blk3(tq, NSA_WIDTH), blk3(tq, LANES), blk3(n_chunks, KV_WIDTH), blk3(n_chunks, KV_WIDTH),
            blk3(KV_WIDTH, wb), blk3(KV_WIDTH, wb),
            blk3(tq, KV_WIDTH), blk3(tq, KV_WIDTH), blk3(tq, KV_WIDTH), blk3(tq, KV_WIDTH),
            pl.BlockSpec(memory_space=pl.ANY), pl.BlockSpec(memory_space=pl.ANY),
            pl.BlockSpec(ebig.shape, const, pipeline_mode=pl.Buffered(1)),
            pl.BlockSpec(wcst.shape, const, pipeline_mode=pl.Buffered(1)),
            pl.BlockSpec(gexp.shape, const, pipeline_mode=pl.Buffered(1)),
        ],
        out_specs=[blk3(tq, NSA_WIDTH), blk3(KV_WIDTH, wb), blk3(KV_WIDTH, wb)],
        scratch_shapes=[
            pltpu.VMEM((2, 2, KV_WIDTH, past), jnp.float32),
            pltpu.SemaphoreType.DMA((2, 2)),
        ],
    )
    o, nwk, nwv = pl.pallas_call(
        functools.partial(_nsa_sample_kernel, n_pages=n_pages, n_new=DS, tq=tq),
        grid_spec=grid_spec,
        out_shape=[jax.ShapeDtypeStruct((DB, tq, NSA_WIDTH), jnp.float32),
                   jax.ShapeDtypeStruct((DB, KV_WIDTH, wb), jnp.float32),
                   jax.ShapeDtypeStruct((DB, KV_WIDTH, wb), jnp.float32)],
        compiler_params=pltpu.CompilerParams(
            dimension_semantics=("arbitrary",), vmem_limit_bytes=PAGED_VMEM_LIMIT_BYTES),
        name="nsa_sample",
    )(page_table, pad_t(q), pad_t(gn), ck, cv, win_k, win_v, pad_t(ks_new), pad_t(vs_new), pad_t(kw_new),
      pad_t(vw_new), pool_k, pool_v, ebig, wcst, gexp)
    return o[:, :DS], nwk, nwv


def rope_tables(pos):
    half = HEAD_DIM // 2
    inv_freq = ROPE_THETA ** (-jnp.arange(half, dtype=jnp.float32) / half)
    ang = pos.astype(jnp.float32)[:, None] * inv_freq[None, :]
    cos, sin = jnp.cos(ang), jnp.sin(ang)
    cos_t = jnp.tile(cos, (1, LANES // half))
    sin_t = jnp.tile(jnp.concatenate([-sin, sin], axis=1), (1, LANES // HEAD_DIM))
    return cos_t, sin_t


def _rope_lanes(y, cos, sin):
    first_half = (_iota((y.shape[0], LANES), 1) & (HEAD_DIM - 1)) < HEAD_DIM // 2
    cols = []
    for c in range(y.shape[1] // LANES):
        yc = y[:, c * LANES:(c + 1) * LANES]
        rot = jnp.where(first_half, pltpu.roll(yc, LANES - HEAD_DIM // 2, axis=1),
                        pltpu.roll(yc, HEAD_DIM // 2, axis=1))
        cols.append(yc * cos + rot * sin)
    return cols[0] if len(cols) == 1 else jnp.concatenate(cols, axis=1)


def _rope_sublanes(y, cos_t, sin_t):
    hh = HEAD_DIM // 2
    out = []
    for c in range(y.shape[0] // LANES):
        yc = y[c * LANES:(c + 1) * LANES]
        rot = jnp.concatenate([yc[hh:2 * hh], yc[:hh], yc[3 * hh:], yc[2 * hh:3 * hh]], axis=0)
        out.append(yc * cos_t + rot * sin_t)
    return out[0] if len(out) == 1 else jnp.concatenate(out, axis=0)


_REST_ACTS = {"u": jax.nn.gelu, "v": jax.nn.gelu, "gn": jax.nn.sigmoid}


def _proj_rest(h, wrest_ref, refs, names):
    c0 = 0
    for ref, name in zip(refs, names):
        width = ref.shape[-1]
        act = _REST_ACTS.get(name)
        for c in range(0, width, GMLP_WIDTH):
            w = min(GMLP_WIDTH, width - c)
            y = _dot(h, wrest_ref[:, c0 + c:c0 + c + w])
            ref[:, c:c + w] = y if act is None else act(y)
        c0 += width


def _proj_rows_kernel(x_ref, g_ref, wq_ref, wkv_ref, wrest_ref, cos_ref, sin_ref, q_ref, *out_refs):
    h = _rms(x_ref[...], g_ref[...]).astype(jnp.bfloat16)
    cos, sin = cos_ref[...], sin_ref[...]
    q_ref[...] = _rope_lanes(_dot(h, wq_ref[...]), cos, sin)
    y = _dot(h, wkv_ref[...])
    for gi in range(N_KV_GROUPS):
        blk = y[:, gi * KV_WIDTH:(gi + 1) * KV_WIDTH]
        out_refs[gi][...] = _rope_lanes(blk, cos, sin) if gi % 2 == 0 else blk
    _proj_rest(h, wrest_ref, out_refs[N_KV_GROUPS:], ("gn", "u", "v", "ga", "gb"))


def _proj_seq_kernel(x_ref, g_ref, wq_ref, wkvt_ref, wrest_ref, cos_ref, sin_ref, cost_ref, sint_ref,
                     q_ref, kct_ref, vct_ref, kst_ref, vst_ref, kwt_ref, vwt_ref,
                     kst_b_ref, vst_b_ref, kwt_b_ref, vwt_b_ref, gn_ref, u_ref, v_ref, ga_ref, gb_ref):
    h = _rms(x_ref[...], g_ref[...]).astype(jnp.bfloat16)
    q_ref[...] = _rope_lanes(_dot(h, wq_ref[...]), cos_ref[...], sin_ref[...]) * (ATTN_SCALE * LOG2E)
    cos_t, sin_t = cost_ref[...], sint_ref[...]
    yt = _dot_nt(wkvt_ref[...], h)
    kv_refs = (kct_ref, vct_ref, kst_ref, vst_ref, kwt_ref, vwt_ref)
    kvb_refs = (None, None, kst_b_ref, vst_b_ref, kwt_b_ref, vwt_b_ref)
    for gi in range(N_KV_GROUPS):
        blk = yt[gi * KV_WIDTH:(gi + 1) * KV_WIDTH]
        if gi % 2 == 0:
            blk = _rope_sublanes(blk, cos_t, sin_t)
        kv_refs[gi][0] = blk
        if kvb_refs[gi] is not None:
            kvb_refs[gi][0] = blk.astype(jnp.bfloat16)
    _proj_rest(h, wrest_ref, (gn_ref, u_ref, v_ref, ga_ref, gb_ref), ("gn", "u", "v", "ga", "gb"))


def _split_w_in(w_in):
    sizes = (NSA_WIDTH,) + (KV_WIDTH,) * N_KV_GROUPS + (N_HEADS * N_BRANCH,)
    offs = np.cumsum((0,) + sizes)
    wq = w_in[:, offs[0]:offs[1]]
    wkv = w_in[:, offs[1]:offs[7]]
    wgn = w_in[:, offs[7]:offs[8]]
    wrest = w_in[:, offs[8]:]
    return wq, wkv, wgn, wrest


def mixer_proj_rows(x, g, w_in, pos):
    M = x.shape[0]
    tm = min(ROW_TILE, M)
    assert M % tm == 0
    wq, wkv, wgn, wrest = _split_w_in(w_in)
    wrest = jnp.concatenate([jnp.pad(wgn, ((0, 0), (0, LANES - wgn.shape[1]))), wrest], axis=1)
    cos, sin = rope_tables(pos)
    b16 = lambda t: t.astype(jnp.bfloat16)
    row = lambda i: (i, 0)
    cw = lambda shape: pl.BlockSpec(shape, lambda i: (0, 0), pipeline_mode=pl.Buffered(1))
    widths = (NSA_WIDTH,) + (KV_WIDTH,) * N_KV_GROUPS + (LANES, GMLP_WIDTH, GMLP_WIDTH, D_MODEL, D_MODEL)
    return pl.pallas_call(
        _proj_rows_kernel,
        grid=(M // tm,),
        in_specs=[pl.BlockSpec((tm, D_MODEL), row), cw((1, D_MODEL)), cw(wq.shape), cw(wkv.shape),
                  cw(wrest.shape), pl.BlockSpec((tm, LANES), row), pl.BlockSpec((tm, LANES), row)],
        out_specs=[pl.BlockSpec((tm, w), row) for w in widths],
        out_shape=[jax.ShapeDtypeStruct((M, w), jnp.float32) for w in widths],
        compiler_params=pltpu.CompilerParams(
            dimension_semantics=("arbitrary",), vmem_limit_bytes=VMEM_LIMIT_BYTES),
        name="mixer_proj_rows",
    )(x, g.reshape(1, -1), b16(wq), b16(wkv), b16(wrest), cos, sin)


def mixer_proj_seq(x, g, w_in, batch):
    M = x.shape[0]
    S = M // batch
    tm = min(ROW_TILE, S)
    assert S % tm == 0
    n_s = S // tm
    wq, wkv, wgn, wrest = _split_w_in(w_in)
    wrest = jnp.concatenate([jnp.pad(wgn, ((0, 0), (0, LANES - wgn.shape[1]))), wrest], axis=1)
    cos, sin = rope_tables(jnp.arange(S, dtype=jnp.int32))
    b16 = lambda t: t.astype(jnp.bfloat16)
    f32 = jnp.float32
    row = lambda i: (i, 0)
    tab = lambda i: (i % n_s, 0)
    tab_t = lambda i: (0, i % n_s)
    seq_t = lambda i: (i // n_s, 0, i % n_s)
    cw = lambda shape: pl.BlockSpec(shape, lambda i: (0, 0), pipeline_mode=pl.Buffered(1))
    t_spec = pl.BlockSpec((1, KV_WIDTH, tm), seq_t)
    t_shape = lambda dt: jax.ShapeDtypeStruct((batch, KV_WIDTH, S), dt)
    r_widths = (LANES, GMLP_WIDTH, GMLP_WIDTH, D_MODEL, D_MODEL)
    out_specs = ([pl.BlockSpec((tm, NSA_WIDTH), row)] + [t_spec] * (N_KV_GROUPS + 4)
                 + [pl.BlockSpec((tm, w), row) for w in r_widths])
    out_shape = ([jax.ShapeDtypeStruct((M, NSA_WIDTH), f32)] + [t_shape(f32)] * N_KV_GROUPS
                 + [t_shape(jnp.bfloat16)] * 4 + [jax.ShapeDtypeStruct((M, w), f32) for w in r_widths])
    return pl.pallas_call(
        _proj_seq_kernel,
        grid=(M // tm,),
        in_specs=[pl.BlockSpec((tm, D_MODEL), row), cw((1, D_MODEL)), cw(wq.shape),
                  cw((N_KV_GROUPS * KV_WIDTH, D_MODEL)), cw(wrest.shape),
                  pl.BlockSpec((tm, LANES), tab), pl.BlockSpec((tm, LANES), tab),
                  pl.BlockSpec((LANES, tm), tab_t), pl.BlockSpec((LANES, tm), tab_t)],
        out_specs=out_specs,
        out_shape=out_shape,
        compiler_params=pltpu.CompilerParams(
            dimension_semantics=("arbitrary",), vmem_limit_bytes=VMEM_LIMIT_BYTES),
        name="mixer_proj_seq",
    )(x, g.reshape(1, -1), b16(wq), b16(wkv.T), b16(wrest), cos, sin, cos.T, sin.T)


def _merge_kernel(oa_ref, u_ref, v_ref, ga_ref, gb_ref, x_ref, lng_ref, lnb_ref, wsp_ref, bsx_ref,
                  wa_ref, wb_ref, wout_ref, gc_ref, wcq_ref, x1_ref, hq_ref):
    tm = x_ref.shape[0]
    v = v_ref[...]
    vc = v - jnp.mean(v, axis=-1, keepdims=True)
    var = jnp.mean(vc * vc, axis=-1, keepdims=True)
    vn = vc * lax.rsqrt(var + NORM_EPS) * lng_ref[...] + lnb_ref[...]
    lane = _iota((GMLP_CHUNK, LANES), 1)
    chunks = []
    for c in range(tm // GMLP_CHUNK):
        cols = []
        for pr in range(GMLP_GROUPS // 2):
            pair = vn[c * GMLP_CHUNK:(c + 1) * GMLP_CHUNK, pr * LANES:(pr + 1) * LANES]
            rhs = jnp.concatenate([jnp.where(lane < GMLP_GROUP_DIM, pair, 0.0),
                                   jnp.where(lane >= GMLP_GROUP_DIM, pair, 0.0)], axis=0).astype(jnp.bfloat16)
            cols.append(_dot(wsp_ref[pr], rhs))
        chunks.append(jnp.concatenate(cols, axis=1) + bsx_ref[...])
    o_b = u_ref[...] * jnp.concatenate(chunks, axis=0)
    m = (jax.nn.sigmoid(ga_ref[...]) * _dot(oa_ref[...].astype(jnp.bfloat16), wa_ref[...])
         + jax.nn.sigmoid(gb_ref[...]) * _dot(o_b.astype(jnp.bfloat16), wb_ref[...]))
    x1 = x_ref[...] + _dot(m.astype(jnp.bfloat16), wout_ref[...])
    x1_ref[...] = x1
    hq_ref[...] = _dot(_rms(x1, gc_ref[...]).astype(jnp.bfloat16), wcq_ref[...])


def gmlp_weights(ws, bs, chunk_len):
    L = chunk_len
    w = jnp.tril(ws[:, :L, :L])
    reps = GMLP_CHUNK // L
    w = jnp.einsum('gij,ab->gaibj', w, jnp.eye(reps, dtype=w.dtype)).reshape(GMLP_GROUPS, GMLP_CHUNK, GMLP_CHUNK)
    wsp = jnp.concatenate([w[0::2], w[1::2]], axis=2)
    bsx = jnp.tile(jnp.repeat(bs[:, :L].T, GMLP_GROUP_DIM, axis=1), (reps, 1))
    return wsp.astype(jnp.bfloat16), bsx


def merge_and_cross_q(o_a, u, v, ga, gb, x, ln_g, ln_b, wsp, bsx, w_a, w_b, w_out, g_cross, w_cq):
    M = x.shape[0]
    tm = min(ROW_TILE, M)
    one = pl.Buffered(1)
    row = lambda i: (i, 0)
    cw = lambda shape: pl.BlockSpec(shape, lambda i: (0,) * len(shape), pipeline_mode=one)
    b16 = lambda t: t.astype(jnp.bfloat16)
    out = jax.ShapeDtypeStruct((M, D_MODEL), jnp.float32)
    return pl.pallas_call(
        _merge_kernel,
        grid=(M // tm,),
        in_specs=[
            pl.BlockSpec((tm, NSA_WIDTH), row), pl.BlockSpec((tm, GMLP_WIDTH), row),
            pl.BlockSpec((tm, GMLP_WIDTH), row), pl.BlockSpec((tm, D_MODEL), row),
            pl.BlockSpec((tm, D_MODEL), row), pl.BlockSpec((tm, D_MODEL), row),
            cw((1, GMLP_WIDTH)), cw((1, GMLP_WIDTH)), cw(wsp.shape), cw(bsx.shape),
            cw(w_a.shape), cw(w_b.shape), cw(w_out.shape), cw((1, D_MODEL)), cw(w_cq.shape),
        ],
        out_specs=[pl.BlockSpec((tm, D_MODEL), row)] * 2,
        out_shape=[out, out],
        compiler_params=pltpu.CompilerParams(
            dimension_semantics=("arbitrary",), vmem_limit_bytes=VMEM_LIMIT_BYTES),
        name="merge_branches",
    )(o_a, u, v, ga, gb, x, ln_g.reshape(1, -1), ln_b.reshape(1, -1), wsp, bsx,
      b16(w_a), b16(w_b), b16(w_out), g_cross.reshape(1, -1), b16(w_cq))


def _memkv_kernel(x_ref, g_ref, wk_ref, wv_ref, k_ref, v_ref):
    h = _rms(x_ref[...], g_ref[...]).astype(jnp.bfloat16)
    k_ref[...] = _dot(h, wk_ref[...])
    v_ref[...] = _dot(h, wv_ref[...])


def memory_kv(mem, g_mem, w_ck, w_cv):
    M = mem.shape[0]
    tm = min(ROW_TILE, M)
    one = pl.Buffered(1)
    row = lambda i: (i, 0)
    cw = lambda shape: pl.BlockSpec(shape, lambda i: (0, 0), pipeline_mode=one)
    out = jax.ShapeDtypeStruct((M, D_MODEL), jnp.float32)
    return pl.pallas_call(
        _memkv_kernel,
        grid=(M // tm,),
        in_specs=[pl.BlockSpec((tm, D_MODEL), row), cw((1, D_MODEL)), cw(w_ck.shape), cw(w_cv.shape)],
        out_specs=[pl.BlockSpec((tm, D_MODEL), row)] * 2,
        out_shape=[out, out],
        compiler_params=pltpu.CompilerParams(
            dimension_semantics=("arbitrary",), vmem_limit_bytes=VMEM_LIMIT_BYTES),
        name="memory_kv",
    )(mem, g_mem.reshape(1, -1), w_ck.astype(jnp.bfloat16), w_cv.astype(jnp.bfloat16))


def _cross_kernel(q_ref, mk_ref, mv_ref, o_ref):
    q = q_ref[0] * CROSS_SCALE
    outs = []
    for h in range(CROSS_HEADS):
        sl = slice(h * CROSS_HEAD_DIM, (h + 1) * CROSS_HEAD_DIM)
        s = _dot_nt(q[:, sl].astype(jnp.bfloat16), mk_ref[0, :, sl].astype(jnp.bfloat16))
        e = jnp.exp(s - jnp.max(s, axis=1, keepdims=True))
        p = e / jnp.sum(e, axis=1, keepdims=True)
        outs.append(_dot(p.astype(jnp.bfloat16), mv_ref[0, :, sl].astype(jnp.bfloat16)))
    o_ref[0] = jnp.concatenate(outs, axis=1)


def cross_attention(hq, mk, mv, *, rows):
    NB, R, _ = hq.shape
    return pl.pallas_call(
        _cross_kernel,
        grid=(NB, R // rows),
        in_specs=[pl.BlockSpec((1, rows, D_MODEL), lambda b, i: (b, i, 0)),
                  pl.BlockSpec((1, N_MEM, D_MODEL), lambda b, i: (b, 0, 0)),
                  pl.BlockSpec((1, N_MEM, D_MODEL), lambda b, i: (b, 0, 0))],
        out_specs=pl.BlockSpec((1, rows, D_MODEL), lambda b, i: (b, i, 0)),
        out_shape=jax.ShapeDtypeStruct((NB, R, D_MODEL), jnp.float32),
        compiler_params=pltpu.CompilerParams(
            dimension_semantics=("arbitrary", "arbitrary"), vmem_limit_bytes=VMEM_LIMIT_BYTES),
        name="cross_attention",
    )(hq, mk, mv)


def _cross_heads_kernel(q_ref, mk_ref, mv_ref, o_ref, *, rows):
    heads = [slice(h * CROSS_HEAD_DIM, (h + 1) * CROSS_HEAD_DIM) for h in range(CROSS_HEADS)]
    q = jnp.concatenate([q_ref[0] * CROSS_SCALE, jnp.zeros((SUBLANES - rows, D_MODEL), jnp.float32)], axis=0)
    s = jnp.concatenate([_dot_nt(q[:, sl].astype(jnp.bfloat16), mk_ref[0, :, sl].astype(jnp.bfloat16))
                         for sl in heads], axis=0)
    e = jnp.exp(s - jnp.max(s, axis=1, keepdims=True))
    p = (e / jnp.sum(e, axis=1, keepdims=True)).astype(jnp.bfloat16)
    o = jnp.concatenate([_dot(p[h * SUBLANES:(h + 1) * SUBLANES], mv_ref[0, :, sl].astype(jnp.bfloat16))
                         for h, sl in enumerate(heads)], axis=1)
    o_ref[0] = o[:rows]


def cross_attention_cached(hq, mk, mv):
    NB, R, _ = hq.shape
    mem_spec = pl.BlockSpec((1, N_MEM, D_MODEL), lambda b: (b, 0, 0))
    return pl.pallas_call(
        functools.partial(_cross_heads_kernel, rows=R),
        grid=(NB,),
        in_specs=[pl.BlockSpec((1, R, D_MODEL), lambda b: (b, 0, 0)), mem_spec, mem_spec],
        out_specs=pl.BlockSpec((1, R, D_MODEL), lambda b: (b, 0, 0)),
        out_shape=jax.ShapeDtypeStruct((NB, R, D_MODEL), jnp.float32),
        compiler_params=pltpu.CompilerParams(
            dimension_semantics=("arbitrary",), vmem_limit_bytes=VMEM_LIMIT_BYTES),
        name="cross_attention_cached",
    )(hq, mk, mv)


def _tail_kernel(x_ref, o_ref, wco_ref, gf_ref, wg_ref, wu_ref, wd_ref, gfin_ref, y_ref, acc_ref):
    x = x_ref[...] + _dot(o_ref[...].astype(jnp.bfloat16), wco_ref[...])
    h = _rms(x, gf_ref[...]).astype(jnp.bfloat16)
    acc_ref[...] = x
    for c in range(D_FF // FF_CHUNK):
        sl = slice(c * FF_CHUNK, (c + 1) * FF_CHUNK)
        a = _dot(h, wg_ref[:, sl])
        b = _dot(h, wu_ref[:, sl])
        t = (a * jax.nn.sigmoid(a) * b).astype(jnp.bfloat16)
        acc_ref[...] += _dot(t, wd_ref[sl, :])
    y_ref[...] = _rms(acc_ref[...], gfin_ref[...])


def tail(x1, o, w_co, g_ffn, w_gate, w_up, w_down, g_final):
    M = x1.shape[0]
    tm = min(ROW_TILE, M)
    row = lambda i: (i, 0)
    cw = lambda shape: pl.BlockSpec(shape, lambda i: (0, 0), pipeline_mode=pl.Buffered(1))
    b16 = lambda t: t.astype(jnp.bfloat16)
    return pl.pallas_call(
        _tail_kernel,
        grid=(M // tm,),
        in_specs=[pl.BlockSpec((tm, D_MODEL), row), pl.BlockSpec((tm, D_MODEL), row),
                  cw(w_co.shape), cw((1, D_MODEL)), cw(w_gate.shape), cw(w_up.shape), cw(w_down.shape),
                  cw((1, D_MODEL))],
        out_specs=pl.BlockSpec((tm, D_MODEL), row),
        out_shape=jax.ShapeDtypeStruct((M, D_MODEL), jnp.float32),
        scratch_shapes=[pltpu.VMEM((tm, D_MODEL), jnp.float32)],
        compiler_params=pltpu.CompilerParams(
            dimension_semantics=("arbitrary",), vmem_limit_bytes=VMEM_LIMIT_BYTES),
        name="tail",
    )(x1, o, b16(w_co), g_ffn.reshape(1, -1), b16(w_gate), b16(w_up), b16(w_down), g_final.reshape(1, -1))


def kernel(x_prompt, x_sample, mem_prompt, cache_cmp_k, cache_cmp_v, cache_slc_k, cache_slc_v,
           page_table, state_win_k, state_win_v, cache_mem_k, cache_mem_v,
           g_mix, w_in, cmp_pe, cmp_w1, cmp_w2, gmlp_ln_g, gmlp_ln_b, gmlp_ws, gmlp_bs,
           w_branch_a, w_branch_b, w_out, g_cross, g_mem, w_cq, w_ck, w_cv, w_co,
           g_ffn, w_gate, w_up, w_down, g_final):
    B, S, _ = x_prompt.shape
    DB, DS, _ = x_sample.shape
    n_pages = page_table.shape[1]
    assert g_mix.shape[0] == 1, "single-layer step"
    l = 0
    cmpw = cmp_weights(cmp_pe[l], cmp_w1[l], cmp_w2[l])
    merge_w = (gmlp_ln_g[l], gmlp_ln_b[l])
    merge_tail = (w_branch_a[l], w_branch_b[l], w_out[l], g_cross[l], w_cq[l])
    ffn_w = (w_co[l], g_ffn[l], w_gate[l], w_up[l], w_down[l], g_final)
    to_t = lambda t: jnp.transpose(t, (0, 2, 3, 1)).reshape(t.shape[0], KV_WIDTH, t.shape[1])
    from_t = lambda t: jnp.transpose(t.reshape(t.shape[0], N_KV_HEADS, HEAD_DIM, t.shape[2]), (0, 3, 1, 2))
    mem3 = lambda t: t.reshape(-1, N_MEM, D_MODEL)

    xp = x_prompt.reshape(B * S, D_MODEL)
    (q, kct, vct, kst, vst, kwt, vwt, kst_b, vst_b, kwt_b, vwt_b, gn, u, v, ga, gb) = mixer_proj_seq(
        xp, g_mix[l], w_in[l], B)
    ck, cv = cmp_blocks_dense(kct, vct, *cmpw)
    o_a = nsa_prompt_attention(q.reshape(B, S, NSA_WIDTH), gn.reshape(B, S, LANES), ck, cv,
                               kst_b, vst_b, kwt_b, vwt_b)
    x1, hq = merge_and_cross_q(o_a.reshape(B * S, NSA_WIDTH), u, v, ga, gb, xp, *merge_w,
                               *gmlp_weights(gmlp_ws[l], gmlp_bs[l], min(S, GMLP_CHUNK)), *merge_tail)
    mk, mv = memory_kv(mem_prompt.reshape(B * N_MEM, D_MODEL), g_mem[l], w_ck[l], w_cv[l])
    o = cross_attention(hq.reshape(B, S, D_MODEL), mem3(mk), mem3(mv), rows=min(S, ROW_TILE))
    y_prompt = tail(x1, o.reshape(B * S, D_MODEL), *ffn_w).reshape(B, S, D_MODEL)
    wb_p = min(WINDOW, S)
    mem5 = lambda t: t.reshape(B, N_MEM, CROSS_HEADS, CROSS_HEAD_DIM)
    prompt_new = (from_t(kct), from_t(vct), from_t(kst), from_t(vst),
                  from_t(kwt[:, :, S - wb_p:]), from_t(vwt[:, :, S - wb_p:]), mem5(mk), mem5(mv))

    xs = x_sample.reshape(DB * DS, D_MODEL)
    pos_s = n_pages * PAGE_SIZE + jnp.arange(DS, dtype=jnp.int32)
    (q, kc, vc, ks, vs, kw, vw, gn, u, v, ga, gb) = mixer_proj_rows(xs, g_mix[l], w_in[l], jnp.tile(pos_s, DB))
    ck, cv = cmp_blocks_paged(to_t(cache_cmp_k[l]), to_t(cache_cmp_v[l]), page_table, *cmpw)
    per_seq = lambda t: t.reshape(DB, DS, -1)
    o_a, win_kt, win_vt = nsa_sample_attention(
        per_seq(q), per_seq(gn), ck, cv, to_t(state_win_k[l]), to_t(state_win_v[l]),
        per_seq(ks), per_seq(vs), per_seq(kw), per_seq(vw),
        to_t(cache_slc_k[l]), to_t(cache_slc_v[l]), page_table)
    x1, hq = merge_and_cross_q(o_a.reshape(DB * DS, NSA_WIDTH), u, v, ga, gb, xs, *merge_w,
                               *gmlp_weights(gmlp_ws[l], gmlp_bs[l], min(DS, GMLP_CHUNK)), *merge_tail)
    o = cross_attention_cached(per_seq(hq), mem3(cache_mem_k[l]), mem3(cache_mem_v[l]))
    y_sample = tail(x1, o.reshape(DB * DS, D_MODEL), *ffn_w).reshape(DB, DS, D_MODEL)
    heads = lambda t: t.reshape(DB, DS, N_KV_HEADS, HEAD_DIM)
    sample_new = (heads(kc), heads(vc), heads(ks), heads(vs), from_t(win_kt), from_t(win_vt), per_seq(v))

    return (y_prompt, y_sample) + tuple(t[None] for t in prompt_new + sample_new)
```

```python
import functools

import jax
import jax.numpy as jnp
import numpy as np
from jax import lax
from jax.experimental import pallas as pl
from jax.experimental.pallas import tpu as pltpu

D_MODEL = 1024
N_HEADS = 8
HEAD_DIM = 64
N_KV_HEADS = 2
GROUP = N_HEADS // N_KV_HEADS
NSA_WIDTH = N_HEADS * HEAD_DIM
KV_WIDTH = N_KV_HEADS * HEAD_DIM
CMP_LEN = 32
CMP_STRIDE = 16
CMP_HIDDEN = 128
SLC_BLK = 64
SLC_SHIFT = 6
TOP_N = 16
WINDOW = 512
ROPE_THETA = 10000.0
ATTN_SCALE = HEAD_DIM ** -0.5
LOG2E = 1.4426950408889634
FORCE_BONUS = 1e4
NEG_INF = -1e30
N_BRANCH = 3
GMLP_WIDTH = 512
GMLP_GROUPS = 8
GMLP_GROUP_DIM = GMLP_WIDTH // GMLP_GROUPS
GMLP_CHUNK = 128
N_MEM = 256
CROSS_HEADS = 4
CROSS_HEAD_DIM = D_MODEL // CROSS_HEADS
CROSS_SCALE = CROSS_HEAD_DIM ** -0.5
D_FF = -(-8 * D_MODEL // (3 * 256)) * 256
NORM_EPS = 1e-6
PAGE_SIZE = 128
CHUNKS_PER_PAGE = PAGE_SIZE // CMP_STRIDE
N_KV_GROUPS = 6

LANES = 128
SUBLANES = 8
VMEM_LIMIT_BYTES = 48 * 1024 * 1024
PAGED_VMEM_LIMIT_BYTES = 56 * 1024 * 1024
SEL_TILE = 512
SAMPLE_KEY_CHUNK = 2048
SAMPLE_SEQS_PER_STEP = 2
ROW_TILE = 512
FF_CHUNK = 256

_NT = (((1,), (1,)), ((), ()))


def _dot_nt(a, b):
    return lax.dot_general(a, b, _NT, preferred_element_type=jnp.float32)


def _dot(a, b):
    return jnp.dot(a, b, preferred_element_type=jnp.float32)


def _split_bf16(x):
    hi = x.astype(jnp.bfloat16)
    lo = (x - hi.astype(jnp.float32)).astype(jnp.bfloat16)
    return hi, lo


def _iota(shape, dim):
    return lax.broadcasted_iota(jnp.int32, shape, dim)


def _rms(x, g):
    return x * lax.rsqrt(jnp.mean(x * x, axis=-1, keepdims=True) + NORM_EPS) * g


def _cmp_from_transposed(xt_ref, p, xrows, wbig_ref, w2big_ref, peterm_ref, out_ref, n_rows):
    n_chunks = n_rows // CMP_STRIDE
    for j in range(n_rows // LANES):
        xrows[j * LANES:(j + 1) * LANES, :] = xt_ref[:, j * LANES:(j + 1) * LANES].T
    acc = jnp.zeros((n_chunks, 4 * CMP_HIDDEN), jnp.float32)
    for sp in range(CMP_STRIDE // 2):
        a0 = xrows[pl.ds(2 * sp, n_chunks, stride=CMP_STRIDE), :]
        a1 = xrows[pl.ds(2 * sp + 1, n_chunks, stride=CMP_STRIDE), :]
        lhs = jnp.concatenate([a0, a1], axis=1).astype(jnp.bfloat16)
        acc = acc + _dot(lhs, wbig_ref[p, sp])
    lead = acc[:, :2 * CMP_HIDDEN]
    trail_next = pltpu.roll(acc[:, 2 * CMP_HIDDEN:], n_chunks - 1, axis=0)
    pre = lead + trail_next + peterm_ref[p]
    out = _dot(jax.nn.gelu(pre).astype(jnp.bfloat16), w2big_ref[p])
    out_ref[0] = out.astype(out_ref.dtype)


def _page_gather(pt_ref, pools, buf, sem, n_pages, n_seq=1):
    b = pl.program_id(0)
    nb = pl.num_programs(0)
    slot = b % 2

    def copies(bb, sl):
        return [pltpu.make_async_copy(pools[p].at[pt_ref[bb * n_seq + si, j]],
                                      buf.at[sl, si, p, :, pl.ds(j * PAGE_SIZE, PAGE_SIZE)], sem.at[sl, p])
                for si in range(n_seq) for p in range(len(pools)) for j in range(n_pages)]

    @pl.when(b == 0)
    def _():
        for c in copies(0, 0):
            c.start()

    @pl.when(b + 1 < nb)
    def _():
        for c in copies(b + 1, 1 - slot):
            c.start()

    def wait():
        for c in copies(b, slot):
            c.wait()

    return wait


def _cmp_paged_kernel(pt_ref, pool_k, pool_v, wbig_ref, w2big_ref, peterm_ref, ck_ref, cv_ref,
                      buf, xrows, sem, *, n_pages):
    wait = _page_gather(pt_ref, (pool_k, pool_v), buf, sem, n_pages)
    wait()
    slot = pl.program_id(0) % 2
    for p, out_ref in ((0, ck_ref), (1, cv_ref)):
        _cmp_from_transposed(buf.at[slot, 0, p], p, xrows, wbig_ref, w2big_ref, peterm_ref, out_ref,
                             n_pages * PAGE_SIZE)


def _cmp_dense_kernel(kt_ref, vt_ref, wbig_ref, w2big_ref, peterm_ref, ck_ref, cv_ref, xrows):
    for p, (src, out_ref) in enumerate(((kt_ref, ck_ref), (vt_ref, cv_ref))):
        _cmp_from_transposed(src.at[0], p, xrows, wbig_ref, w2big_ref, peterm_ref, out_ref, src.shape[2])


def cmp_weights(cmp_pe, cmp_w1, cmp_w2):
    eye = jnp.eye(N_KV_HEADS, dtype=jnp.float32)
    w1r = cmp_w1.reshape(2, 2, CMP_STRIDE // 2, 2, HEAD_DIM, CMP_HIDDEN)
    wbig = jnp.einsum('jtpsdf,hk->jpshdtkf', w1r, eye)
    wbig = wbig.reshape(2, CMP_STRIDE // 2, 2 * KV_WIDTH, 4 * CMP_HIDDEN).astype(jnp.bfloat16)
    w2big = jnp.einsum('jfd,hk->jhfkd', cmp_w2, eye).reshape(2, 2 * CMP_HIDDEN, KV_WIDTH)
    pe_term = jnp.einsum('jsd,jsdf->jf', cmp_pe, cmp_w1)
    peterm = jnp.tile(pe_term[:, None, :], (1, 1, N_KV_HEADS))
    return wbig, w2big.astype(jnp.bfloat16), peterm


def _cmp_weight_specs(wbig, w2big, peterm, nargs):
    c3 = (lambda b: (0, 0, 0)) if nargs == 1 else (lambda b, pt: (0, 0, 0))
    c4 = (lambda b: (0, 0, 0, 0)) if nargs == 1 else (lambda b, pt: (0, 0, 0, 0))
    one = pl.Buffered(1)
    return [pl.BlockSpec(wbig.shape, c4, pipeline_mode=one),
            pl.BlockSpec(w2big.shape, c3, pipeline_mode=one),
            pl.BlockSpec(peterm.shape, c3, pipeline_mode=one)]


def cmp_blocks_paged(pool_k, pool_v, page_table, wbig, w2big, peterm):
    n_seq, n_pages = page_table.shape
    n_rows = n_pages * PAGE_SIZE
    n_chunks = n_rows // CMP_STRIDE
    out = jax.ShapeDtypeStruct((n_seq, n_chunks, KV_WIDTH), jnp.bfloat16)
    grid_spec = pltpu.PrefetchScalarGridSpec(
        num_scalar_prefetch=1,
        grid=(n_seq,),
        in_specs=[pl.BlockSpec(memory_space=pl.ANY), pl.BlockSpec(memory_space=pl.ANY)]
        + _cmp_weight_specs(wbig, w2big, peterm, 2),
        out_specs=[pl.BlockSpec((1, n_chunks, KV_WIDTH), lambda b, pt: (b, 0, 0))] * 2,
        scratch_shapes=[
            pltpu.VMEM((2, 1, 2, KV_WIDTH, n_rows), jnp.float32),
            pltpu.VMEM((n_rows, KV_WIDTH), jnp.float32),
            pltpu.SemaphoreType.DMA((2, 2)),
        ],
    )
    return pl.pallas_call(
        functools.partial(_cmp_paged_kernel, n_pages=n_pages),
        grid_spec=grid_spec,
        out_shape=[out, out],
        compiler_params=pltpu.CompilerParams(
            dimension_semantics=("arbitrary",), vmem_limit_bytes=PAGED_VMEM_LIMIT_BYTES),
        name="cmp_blocks_paged",
    )(page_table, pool_k, pool_v, wbig, w2big, peterm)


def cmp_blocks_dense(kt, vt, wbig, w2big, peterm):
    B, _, S = kt.shape
    n_chunks = S // CMP_STRIDE
    per_b = lambda b: (b, 0, 0)
    return pl.pallas_call(
        _cmp_dense_kernel,
        grid=(B,),
        in_specs=[pl.BlockSpec((1, KV_WIDTH, S), per_b), pl.BlockSpec((1, KV_WIDTH, S), per_b)]
        + _cmp_weight_specs(wbig, w2big, peterm, 1),
        out_specs=[pl.BlockSpec((1, n_chunks, KV_WIDTH), per_b)] * 2,
        out_shape=[jax.ShapeDtypeStruct((B, n_chunks, KV_WIDTH), jnp.bfloat16)] * 2,
        scratch_shapes=[pltpu.VMEM((S, KV_WIDTH), jnp.float32)],
        compiler_params=pltpu.CompilerParams(
            dimension_semantics=("arbitrary",), vmem_limit_bytes=VMEM_LIMIT_BYTES),
        name="cmp_blocks_dense",
    )(kt, vt, wbig, w2big, peterm)


def cmp_to_slc_matrix(n_chunks, n_cols):
    cs = np.arange(n_chunks)[:, None] * CMP_STRIDE
    ss = np.arange(n_cols)[None, :] * SLC_BLK
    shared = np.minimum(cs + CMP_LEN, ss + SLC_BLK) - np.maximum(cs, ss)
    w = np.maximum(shared, 0).astype(np.float32) / CMP_LEN
    w[n_chunks - 1] = 0.0
    return w


def _nsa_prompt_kernel(q_ref, gn_ref, ck_ref, cv_ref, ks_ref, vs_ref, kw_ref, vw_ref,
                       wcst_ref, gexp_ref, o_ref, lhs_ref, m_ref, l_ref, acc_ref, *, tq, n_chunks):
    i = pl.program_id(1)
    q0 = i * tq
    qpb = _padded_queries(q_ref[0], tq).astype(jnp.bfloat16)

    def qpos_of(shape):
        return q0 + (_iota(shape, 0) & (tq - 1))

    s = _dot_nt(qpb, ck_ref[0])
    n_idx = _iota(s.shape, 1)
    valid = (n_idx * CMP_STRIDE + (CMP_LEN - 1) <= qpos_of(s.shape)) & (n_idx < n_chunks - 1)
    p_cmp = _masked_softmax2(s, valid)
    o_cmp = _dot(p_cmp.astype(jnp.bfloat16), cv_ref[0])

    blk = _iota((LANES, tq), 0)
    blkf = blk.astype(jnp.float32)
    qblk = (q0 + _iota((LANES, tq), 1)) >> SLC_SHIFT
    forced = (blk == 0) | (blk == qblk) | (blk == qblk - 1)
    wcst = wcst_ref[...]
    nsel = []
    for psum in _group_sum(p_cmp, tq):
        hi, lo = _split_bf16(psum)
        sc = _dot_nt(wcst, hi) + _dot_nt(wcst, lo)
        sc = jnp.where(blk <= qblk, jnp.where(forced, sc + FORCE_BONUS, sc), NEG_INF)
        sel = jnp.zeros_like(sc)
        for _ in range(TOP_N):
            mx = jnp.max(sc, axis=0, keepdims=True)
            first = jnp.min(jnp.where(sc == mx, blkf, float(LANES)), axis=0, keepdims=True)
            hit = blkf == first
            sel = jnp.where(hit, 1.0, sel)
            sc = jnp.where(hit, -jnp.inf, sc)
        nsel.append((1.0 - sel).T.astype(jnp.bfloat16))
    for h in range(N_KV_HEADS):
        r0 = h * GROUP * tq
        lhs_ref[r0:r0 + GROUP * tq, :LANES] = qpb[r0:r0 + GROUP * tq]
        for g in range(GROUP):
            lhs_ref[r0 + g * tq:r0 + (g + 1) * tq, LANES:] = nsel[h]

    m_ref[...] = jnp.full(m_ref.shape, -3e38, jnp.float32)
    l_ref[...] = jnp.zeros(l_ref.shape, jnp.float32)
    acc_ref[...] = jnp.zeros(acc_ref.shape, jnp.float32)
    n_tiles = (q0 + tq + SEL_TILE - 1) // SEL_TILE

    def sel_tile(kt, causal):
        k0 = pl.multiple_of(kt * SEL_TILE, SEL_TILE)
        kblk = (k0 + _iota((LANES, SEL_TILE), 1)) >> SLC_SHIFT
        ebig = jnp.where(kblk == _iota((LANES, SEL_TILE), 0), NEG_INF, 0.0).astype(jnp.bfloat16)
        rhs = jnp.concatenate([ks_ref[0, :, pl.ds(k0, SEL_TILE)], ebig], axis=0)
        st = _dot(lhs_ref[...], rhs)
        if causal:
            st = jnp.where(k0 + _iota(st.shape, 1) <= qpos_of(st.shape), st, NEG_INF)
        cols = [st[:, c * LANES:(c + 1) * LANES] for c in range(SEL_TILE // LANES)]
        mt = functools.reduce(jnp.maximum, cols)
        m_prev = m_ref[...]
        m_new = jnp.maximum(m_prev, jnp.max(mt, axis=1, keepdims=True))
        alpha = jnp.exp2(m_prev - m_new)
        ps = [jnp.exp2(c - m_new) for c in cols]
        l_ref[...] = alpha * l_ref[...] + functools.reduce(jnp.add, ps)
        pb = jnp.concatenate(ps, axis=1).astype(jnp.bfloat16)
        acc_ref[...] = alpha * acc_ref[...] + _dot_nt(pb, vs_ref[0, :, pl.ds(k0, SEL_TILE)])
        m_ref[...] = m_new

    def body(kt, carry):
        sel_tile(kt, False)
        return carry

    lax.fori_loop(0, n_tiles - 1, body, 0)
    sel_tile(n_tiles - 1, True)
    o_sel = acc_ref[...] / jnp.sum(l_ref[...], axis=1, keepdims=True)

    w0 = pl.multiple_of(jnp.maximum(q0 - WINDOW, 0), tq)
    sw = _dot(qpb, kw_ref[0, :, pl.ds(w0, WINDOW + tq)])
    kpos = w0 + _iota(sw.shape, 1)
    qpos = qpos_of(sw.shape)
    p_win = _masked_softmax2(sw, (kpos <= qpos) & (kpos > qpos - WINDOW))
    o_win = _dot_nt(p_win.astype(jnp.bfloat16), vw_ref[0, :, pl.ds(w0, WINDOW + tq)])

    o_ref[0] = _combine((o_cmp, o_sel, o_win), gn_ref[0], gexp_ref[...], tq)


def nsa_prompt_attention(q, gn, ck, cv, ks, vs, kw, vw, *, tq=LANES):
    B, S, _ = q.shape
    n_chunks = ck.shape[1]
    assert S % SEL_TILE == 0 and S // SLC_BLK <= LANES and tq == LANES and S >= WINDOW + tq
    wcst = jnp.asarray(cmp_to_slc_matrix(n_chunks, LANES).T, jnp.bfloat16)
    gexp = gate_expand_matrix()
    rows = N_HEADS * tq
    per_b = lambda b, i: (b, 0, 0)
    tile = lambda b, i: (b, i, 0)
    const = lambda b, i: (0, 0)
    return pl.pallas_call(
        functools.partial(_nsa_prompt_kernel, tq=tq, n_chunks=n_chunks),
        grid=(B, S // tq),
        in_specs=[
            pl.BlockSpec((1, tq, NSA_WIDTH), tile),
            pl.BlockSpec((1, tq, LANES), tile),
            pl.BlockSpec((1, n_chunks, KV_WIDTH), per_b),
            pl.BlockSpec((1, n_chunks, KV_WIDTH), per_b),
            pl.BlockSpec((1, KV_WIDTH, S), per_b),
            pl.BlockSpec((1, KV_WIDTH, S), per_b),
            pl.BlockSpec((1, KV_WIDTH, S), per_b),
            pl.BlockSpec((1, KV_WIDTH, S), per_b),
            pl.BlockSpec(wcst.shape, const),
            pl.BlockSpec(gexp.shape, const),
        ],
        out_specs=pl.BlockSpec((1, tq, NSA_WIDTH), tile),
        out_shape=jax.ShapeDtypeStruct((B, S, NSA_WIDTH), jnp.float32),
        scratch_shapes=[
            pltpu.VMEM((rows, 2 * LANES), jnp.bfloat16),
            pltpu.VMEM((rows, LANES), jnp.float32),
            pltpu.VMEM((rows, LANES), jnp.float32),
            pltpu.VMEM((rows, LANES), jnp.float32),
        ],
        compiler_params=pltpu.CompilerParams(
            dimension_semantics=("arbitrary", "arbitrary"), vmem_limit_bytes=VMEM_LIMIT_BYTES),
        name="nsa_prompt",
    )(q, gn, ck, cv, ks, vs, kw, vw, wcst, gexp)


def _padded_queries(q, tq):
    lane = _iota((tq, LANES), 1)
    blocks = []
    for h in range(N_KV_HEADS):
        for g in range(GROUP):
            c = h * GROUP + g
            pair = q[:, LANES * (c // 2):LANES * (c // 2 + 1)]
            if c % 2 != h:
                pair = pltpu.roll(pair, HEAD_DIM, axis=1)
            keep = (lane >= HEAD_DIM * h) & (lane < HEAD_DIM * (h + 1))
            blocks.append(jnp.where(keep, pair, 0.0))
    return jnp.concatenate(blocks, axis=0)


def _masked_softmax(s, valid):
    sm = jnp.where(valid, s, NEG_INF)
    mx = jnp.max(sm, axis=1, keepdims=True)
    e = jnp.where(valid, jnp.exp(sm - mx), 0.0)
    den = jnp.maximum(jnp.sum(e, axis=1, keepdims=True), 1e-30)
    return e / den


def _masked_softmax2(s2, valid):
    sm = jnp.where(valid, s2, NEG_INF)
    mx = jnp.max(sm, axis=1, keepdims=True)
    e = jnp.exp2(sm - mx)
    inv = jnp.where(mx > 0.5 * NEG_INF, 1.0 / jnp.sum(e, axis=1, keepdims=True), 0.0)
    return e * inv


def _group_sum(p, tq):
    out = []
    for h in range(N_KV_HEADS):
        r0 = h * GROUP * tq
        acc = p[r0:r0 + tq]
        for g in range(1, GROUP):
            acc = acc + p[r0 + g * tq:r0 + (g + 1) * tq]
        out.append(acc)
    return out


def _combine(branches, gn, gexp_w, tq):
    hi, lo = _split_bf16(gn)
    gexp = _dot(hi, gexp_w) + _dot(lo, gexp_w)
    lane = _iota((tq, LANES), 1)
    cols = []
    for h in range(N_KV_HEADS):
        blks = []
        for g in range(GROUP):
            j = h * GROUP + g
            rows = slice(j * tq, (j + 1) * tq)
            o = None
            for br in range(N_BRANCH):
                gate = gexp[:, (j * N_BRANCH + br) * LANES:(j * N_BRANCH + br + 1) * LANES]
                term = gate * branches[br][rows]
                o = term if o is None else o + term
            blks.append(o)
        for jj in range(GROUP // 2):
            a, b2 = blks[2 * jj], blks[2 * jj + 1]
            if h == 0:
                b2 = pltpu.roll(b2, HEAD_DIM, axis=1)
            else:
                a = pltpu.roll(a, HEAD_DIM, axis=1)
            cols.append(jnp.where(lane < HEAD_DIM, a, b2))
    return jnp.concatenate(cols, axis=1)


def gate_expand_matrix():
    e = np.zeros((LANES, N_HEADS * N_BRANCH * LANES), np.float32)
    for j in range(N_HEADS * N_BRANCH):
        e[j, j * LANES:(j + 1) * LANES] = 1.0
    return jnp.asarray(e, jnp.bfloat16)


def _nsa_sample_kernel(pt_ref, q_ref, gn_ref, ck_ref, cv_ref, wk_ref, wv_ref,
                       ksn_ref, vsn_ref, kwn_ref, vwn_ref, pool_k, pool_v, ebig_ref, wcst_ref, gexp_ref,
                       o_ref, nwk_ref, nwv_ref, buf, sem, *, n_pages, n_new, tq, n_seq):
    slot = pl.program_id(0) % 2
    past = n_pages * PAGE_SIZE
    n_chunks = n_pages * CHUNKS_PER_PAGE
    n_past_blk = past // SLC_BLK
    n_slc = -(-(past + n_new) // SLC_BLK)
    wb = wk_ref.shape[2]
    rows = N_HEADS * tq
    wait_pages = _page_gather(pt_ref, (pool_k, pool_v), buf, sem, n_pages, n_seq)

    def tok_of(shape):
        return _iota(shape, 0) & (tq - 1)

    def new_rows(ref, si):
        x = jnp.concatenate([ref[si], jnp.zeros((LANES - tq, KV_WIDTH), jnp.float32)], axis=0)
        return x.astype(jnp.bfloat16)

    def before_pages(si):
        qpb = _padded_queries(q_ref[si] * ATTN_SCALE, tq).astype(jnp.bfloat16)

        s = _dot_nt(qpb, ck_ref[si])
        n_idx = _iota(s.shape, 1)
        valid = (n_idx * CMP_STRIDE + (CMP_LEN - 1) <= past + tok_of(s.shape)) & (n_idx < n_chunks - 1)
        p_cmp = _masked_softmax(s, valid)
        o_cmp = _dot(p_cmp.astype(jnp.bfloat16), cv_ref[si])

        ncol = wcst_ref.shape[0]
        psum = jnp.concatenate(_group_sum(p_cmp, tq)
                               + [jnp.zeros((LANES - N_KV_HEADS * tq, n_chunks), jnp.float32)], axis=0)
        hi, lo = _split_bf16(psum)
        wcst = wcst_ref[...]
        sc = _dot_nt(wcst, hi) + _dot_nt(wcst, lo)
        blk = _iota((ncol, LANES), 0)
        blkf = blk.astype(jnp.float32)
        qblk = (past + (_iota((ncol, LANES), 1) & (tq - 1))) >> SLC_SHIFT
        forced = (blk == 0) | (blk == qblk) | (blk == qblk - 1)
        sc = jnp.where(blk <= qblk, jnp.where(forced, sc + FORCE_BONUS, sc), NEG_INF)
        sc = jnp.where(blk < n_slc, sc, -jnp.inf)
        sel_t = jnp.zeros_like(sc)
        for _ in range(min(TOP_N, n_slc)):
            mx = jnp.max(sc, axis=0, keepdims=True)
            first = jnp.min(jnp.where(sc == mx, blkf, float(ncol)), axis=0, keepdims=True)
            hit = blkf == first
            sel_t = jnp.where(hit, 1.0, sel_t)
            sc = jnp.where(hit, -jnp.inf, sc)
        sel = sel_t.T
        nsel, sel_new = [], []
        for h in range(N_KV_HEADS):
            sel_h = sel[h * tq:(h + 1) * tq]
            nsel.append(jnp.concatenate([1.0 - sel_h[:, :LANES]] * GROUP, axis=0).astype(jnp.bfloat16))
            sel_new.append(jnp.broadcast_to(sel_h[:, n_past_blk:n_past_blk + 1], (tq, LANES)))
        lhs = jnp.concatenate(
            [jnp.concatenate([qpb[h * GROUP * tq:(h + 1) * GROUP * tq], nsel[h]], axis=1)
             for h in range(N_KV_HEADS)], axis=0)
        sel_new_rows = jnp.concatenate([sel_new[h] for h in range(N_KV_HEADS) for _ in range(GROUP)], axis=0)

        tn = _iota((rows, LANES), 1)
        new_ok = (tn <= tok_of((rows, LANES))) & (tn < n_new)
        s_sn = jnp.where(new_ok & (sel_new_rows > 0.5), _dot_nt(qpb, new_rows(ksn_ref, si)), NEG_INF)
        s_wn = jnp.where(new_ok, _dot_nt(qpb, new_rows(kwn_ref, si)), NEG_INF)

        wk = wk_ref[si]
        wv = wv_ref[si]
        s_w = _dot(qpb, wk.astype(jnp.bfloat16))
        kwpos = past - wb + _iota(s_w.shape, 1)
        qpos = past + tok_of(s_w.shape)
        s_w = jnp.where((kwpos >= 0) & (kwpos <= qpos) & (kwpos > qpos - WINDOW), s_w, NEG_INF)
        mx = jnp.maximum(jnp.max(s_w, axis=1, keepdims=True), jnp.max(s_wn, axis=1, keepdims=True))
        e_w = jnp.exp(s_w - mx)
        e_wn = jnp.exp(s_wn - mx)
        den = jnp.sum(e_w, axis=1, keepdims=True) + jnp.sum(e_wn, axis=1, keepdims=True)
        o_win = (_dot_nt(e_w.astype(jnp.bfloat16), wv.astype(jnp.bfloat16))
                 + _dot(e_wn.astype(jnp.bfloat16), new_rows(vwn_ref, si))) / den

        col = _iota((KV_WIDTH, wb), 1)
        for src, new_ref, dst in ((wk, kwn_ref, nwk_ref), (wv, vwn_ref, nwv_ref)):
            new_t = jnp.concatenate([new_ref[si], jnp.zeros((LANES - tq, KV_WIDTH), jnp.float32)], axis=0).T
            tail = jnp.concatenate([jnp.zeros((KV_WIDTH, wb - LANES), jnp.float32), new_t], axis=1)
            dst[si] = jnp.where(col >= wb - n_new, pltpu.roll(tail, LANES - n_new, axis=1),
                                pltpu.roll(src, wb - n_new, axis=1))

        return lhs, s_sn, o_cmp, o_win

    def after_pages(si, lhs, s_sn, o_cmp, o_win):
        ck_keys = min(SAMPLE_KEY_CHUNK, past)
        s_chunks = []
        for c in range(past // ck_keys):
            kc = buf[slot, si, 0, :, pl.ds(c * ck_keys, ck_keys)].astype(jnp.bfloat16)
            rhs = jnp.concatenate([kc, ebig_ref[:, pl.ds(c * ck_keys, ck_keys)]], axis=0)
            s_chunks.append(_dot(lhs, rhs))
        mx = jnp.max(s_sn, axis=1, keepdims=True)
        for sc in s_chunks:
            mx = jnp.maximum(mx, jnp.max(sc, axis=1, keepdims=True))
        e_sn = jnp.exp(s_sn - mx)
        den = jnp.sum(e_sn, axis=1, keepdims=True)
        o_sel = _dot(e_sn.astype(jnp.bfloat16), new_rows(vsn_ref, si))
        for c, sc in enumerate(s_chunks):
            e = jnp.exp(sc - mx)
            den = den + jnp.sum(e, axis=1, keepdims=True)
            vc = buf[slot, si, 1, :, pl.ds(c * ck_keys, ck_keys)].astype(jnp.bfloat16)
            o_sel = o_sel + _dot_nt(e.astype(jnp.bfloat16), vc)
        o_sel = o_sel / den

        o_ref[si] = _combine((o_cmp, o_sel, o_win), gn_ref[si], gexp_ref[...], tq)

    staged = [before_pages(si) for si in range(n_seq)]
    wait_pages()
    for si in range(n_seq):
        after_pages(si, *staged[si])


def nsa_sample_attention(q, gn, ck, cv, win_k, win_v, ks_new, vs_new, kw_new, vw_new,
                         pool_k, pool_v, page_table, *, tq=SUBLANES, n_seq=SAMPLE_SEQS_PER_STEP):
    DB, DS, _ = q.shape
    n_pages = page_table.shape[1]
    past = n_pages * PAGE_SIZE
    n_chunks = ck.shape[1]
    wb = win_k.shape[2]
    n_slc = -(-(past + DS) // SLC_BLK)
    assert DS <= tq and DS < CMP_STRIDE and past // SLC_BLK <= LANES and wb % LANES == 0
    assert past % min(SAMPLE_KEY_CHUNK, past) == 0 and DB % n_seq == 0
    ncol = -(-n_slc // LANES) * LANES
    pad_t = lambda t: jnp.pad(t, ((0, 0), (0, tq - DS), (0, 0)))
    kb = np.arange(past)[None, :] // SLC_BLK
    ebig = jnp.asarray(np.where(kb == np.arange(LANES)[:, None], NEG_INF, 0.0), jnp.bfloat16)
    wcst = jnp.asarray(cmp_to_slc_matrix(n_chunks, ncol).T, jnp.bfloat16)
    gexp = gate_expand_matrix()
    per_b = lambda b, pt: (b, 0, 0)
    const = lambda b, pt: (0, 0)
    blk3 = lambda n, w: pl.BlockSpec((n_seq, n, w), per_b)
    grid_spec = pltpu.PrefetchScalarGridSpec(
        num_scalar_prefetch=1,
        grid=(DB // n_seq,),
        in_specs=[
            blk3(tq, NSA_WIDTH), blk3(tq, LANES), blk3(n_chunks, KV_WIDTH), blk3(n_chunks, KV_WIDTH),
            blk3(KV_WIDTH, wb), blk3(KV_WIDTH, wb),
            blk3(tq, KV_WIDTH), blk3(tq, KV_WIDTH), blk3(tq, KV_WIDTH), blk3(tq, KV_WIDTH),
            pl.BlockSpec(memory_space=pl.ANY), pl.BlockSpec(memory_space=pl.ANY),
            pl.BlockSpec(ebig.shape, const, pipeline_mode=pl.Buffered(1)),
            pl.BlockSpec(wcst.shape, const, pipeline_mode=pl.Buffered(1)),
            pl.BlockSpec(gexp.shape, const, pipeline_mode=pl.Buffered(1)),
        ],
        out_specs=[blk3(tq, NSA_WIDTH), blk3(KV_WIDTH, wb), blk3(KV_WIDTH, wb)],
        scratch_shapes=[
            pltpu.VMEM((2, n_seq, 2, KV_WIDTH, past), jnp.float32),
            pltpu.SemaphoreType.DMA((2, 2)),
        ],
    )
    o, nwk, nwv = pl.pallas_call(
        functools.partial(_nsa_sample_kernel, n_pages=n_pages, n_new=DS, tq=tq, n_seq=n_seq),
        grid_spec=grid_spec,
        out_shape=[jax.ShapeDtypeStruct((DB, tq, NSA_WIDTH), jnp.float32),
                   jax.ShapeDtypeStruct((DB, KV_WIDTH, wb), jnp.float32),
                   jax.ShapeDtypeStruct((DB, KV_WIDTH, wb), jnp.float32)],
        compiler_params=pltpu.CompilerParams(
            dimension_semantics=("arbitrary",), vmem_limit_bytes=PAGED_VMEM_LIMIT_BYTES),
        name="nsa_sample",
    )(page_table, pad_t(q), pad_t(gn), ck, cv, win_k, win_v, pad_t(ks_new), pad_t(vs_new), pad_t(kw_new),
      pad_t(vw_new), pool_k, pool_v, ebig, wcst, gexp)
    return o[:, :DS], nwk, nwv


def rope_tables(pos):
    half = HEAD_DIM // 2
    inv_freq = ROPE_THETA ** (-jnp.arange(half, dtype=jnp.float32) / half)
    ang = pos.astype(jnp.float32)[:, None] * inv_freq[None, :]
    cos, sin = jnp.cos(ang), jnp.sin(ang)
    cos_t = jnp.tile(cos, (1, LANES // half))
    sin_t = jnp.tile(jnp.concatenate([-sin, sin], axis=1), (1, LANES // HEAD_DIM))
    return cos_t, sin_t


def _rope_lanes(y, cos, sin):
    first_half = (_iota((y.shape[0], LANES), 1) & (HEAD_DIM - 1)) < HEAD_DIM // 2
    cols = []
    for c in range(y.shape[1] // LANES):
        yc = y[:, c * LANES:(c + 1) * LANES]
        rot = jnp.where(first_half, pltpu.roll(yc, LANES - HEAD_DIM // 2, axis=1),
                        pltpu.roll(yc, HEAD_DIM // 2, axis=1))
        cols.append(yc * cos + rot * sin)
    return cols[0] if len(cols) == 1 else jnp.concatenate(cols, axis=1)


def _rope_sublanes(y, cos_t, sin_t):
    hh = HEAD_DIM // 2
    out = []
    for c in range(y.shape[0] // LANES):
        yc = y[c * LANES:(c + 1) * LANES]
        rot = jnp.concatenate([yc[hh:2 * hh], yc[:hh], yc[3 * hh:], yc[2 * hh:3 * hh]], axis=0)
        out.append(yc * cos_t + rot * sin_t)
    return out[0] if len(out) == 1 else jnp.concatenate(out, axis=0)


_REST_ACTS = {"u": jax.nn.gelu, "v": jax.nn.gelu, "gn": jax.nn.sigmoid}


def _proj_rest(h, wrest_ref, refs, names):
    c0 = 0
    for ref, name in zip(refs, names):
        width = ref.shape[-1]
        act = _REST_ACTS.get(name)
        for c in range(0, width, GMLP_WIDTH):
            w = min(GMLP_WIDTH, width - c)
            y = _dot(h, wrest_ref[:, c0 + c:c0 + c + w])
            ref[:, c:c + w] = y if act is None else act(y)
        c0 += width


def _proj_rows_kernel(x_ref, g_ref, wq_ref, wkv_ref, wrest_ref, cos_ref, sin_ref, q_ref, *out_refs):
    h = _rms(x_ref[...], g_ref[...]).astype(jnp.bfloat16)
    cos, sin = cos_ref[...], sin_ref[...]
    q_ref[...] = _rope_lanes(_dot(h, wq_ref[...]), cos, sin)
    y = _dot(h, wkv_ref[...])
    for gi in range(N_KV_GROUPS):
        blk = y[:, gi * KV_WIDTH:(gi + 1) * KV_WIDTH]
        out_refs[gi][...] = _rope_lanes(blk, cos, sin) if gi % 2 == 0 else blk
    _proj_rest(h, wrest_ref, out_refs[N_KV_GROUPS:], ("gn", "u", "v", "ga", "gb"))


def _proj_seq_kernel(x_ref, g_ref, wq_ref, wkvt_ref, wrest_ref, cos_ref, sin_ref, cost_ref, sint_ref,
                     q_ref, kct_ref, vct_ref, kst_ref, vst_ref, kwt_ref, vwt_ref,
                     kst_b_ref, vst_b_ref, kwt_b_ref, vwt_b_ref, gn_ref, u_ref, v_ref, ga_ref, gb_ref):
    h = _rms(x_ref[...], g_ref[...]).astype(jnp.bfloat16)
    q_ref[...] = _rope_lanes(_dot(h, wq_ref[...]), cos_ref[...], sin_ref[...]) * (ATTN_SCALE * LOG2E)
    cos_t, sin_t = cost_ref[...], sint_ref[...]
    yt = _dot_nt(wkvt_ref[...], h)
    kv_refs = (kct_ref, vct_ref, kst_ref, vst_ref, kwt_ref, vwt_ref)
    kvb_refs = (None, None, kst_b_ref, vst_b_ref, kwt_b_ref, vwt_b_ref)
    for gi in range(N_KV_GROUPS):
        blk = yt[gi * KV_WIDTH:(gi + 1) * KV_WIDTH]
        if gi % 2 == 0:
            blk = _rope_sublanes(blk, cos_t, sin_t)
        kv_refs[gi][0] = blk
        if kvb_refs[gi] is not None:
            kvb_refs[gi][0] = blk.astype(jnp.bfloat16)
    _proj_rest(h, wrest_ref, (gn_ref, u_ref, v_ref, ga_ref, gb_ref), ("gn", "u", "v", "ga", "gb"))


def _split_w_in(w_in):
    sizes = (NSA_WIDTH,) + (KV_WIDTH,) * N_KV_GROUPS + (N_HEADS * N_BRANCH,)
    offs = np.cumsum((0,) + sizes)
    wq = w_in[:, offs[0]:offs[1]]
    wkv = w_in[:, offs[1]:offs[7]]
    wgn = w_in[:, offs[7]:offs[8]]
    wrest = w_in[:, offs[8]:]
    return wq, wkv, wgn, wrest


def mixer_proj_rows(x, g, w_in, pos):
    M = x.shape[0]
    tm = min(ROW_TILE, M)
    assert M % tm == 0
    wq, wkv, wgn, wrest = _split_w_in(w_in)
    wrest = jnp.concatenate([jnp.pad(wgn, ((0, 0), (0, LANES - wgn.shape[1]))), wrest], axis=1)
    cos, sin = rope_tables(pos)
    b16 = lambda t: t.astype(jnp.bfloat16)
    row = lambda i: (i, 0)
    cw = lambda shape: pl.BlockSpec(shape, lambda i: (0, 0), pipeline_mode=pl.Buffered(1))
    widths = (NSA_WIDTH,) + (KV_WIDTH,) * N_KV_GROUPS + (LANES, GMLP_WIDTH, GMLP_WIDTH, D_MODEL, D_MODEL)
    return pl.pallas_call(
        _proj_rows_kernel,
        grid=(M // tm,),
        in_specs=[pl.BlockSpec((tm, D_MODEL), row), cw((1, D_MODEL)), cw(wq.shape), cw(wkv.shape),
                  cw(wrest.shape), pl.BlockSpec((tm, LANES), row), pl.BlockSpec((tm, LANES), row)],
        out_specs=[pl.BlockSpec((tm, w), row) for w in widths],
        out_shape=[jax.ShapeDtypeStruct((M, w), jnp.float32) for w in widths],
        compiler_params=pltpu.CompilerParams(
            dimension_semantics=("arbitrary",), vmem_limit_bytes=VMEM_LIMIT_BYTES),
        name="mixer_proj_rows",
    )(x, g.reshape(1, -1), b16(wq), b16(wkv), b16(wrest), cos, sin)


def mixer_proj_seq(x, g, w_in, batch):
    M = x.shape[0]
    S = M // batch
    tm = min(ROW_TILE, S)
    assert S % tm == 0
    n_s = S // tm
    wq, wkv, wgn, wrest = _split_w_in(w_in)
    wrest = jnp.concatenate([jnp.pad(wgn, ((0, 0), (0, LANES - wgn.shape[1]))), wrest], axis=1)
    cos, sin = rope_tables(jnp.arange(S, dtype=jnp.int32))
    b16 = lambda t: t.astype(jnp.bfloat16)
    f32 = jnp.float32
    row = lambda i: (i, 0)
    tab = lambda i: (i % n_s, 0)
    tab_t = lambda i: (0, i % n_s)
    seq_t = lambda i: (i // n_s, 0, i % n_s)
    cw = lambda shape: pl.BlockSpec(shape, lambda i: (0, 0), pipeline_mode=pl.Buffered(1))
    t_spec = pl.BlockSpec((1, KV_WIDTH, tm), seq_t)
    t_shape = lambda dt: jax.ShapeDtypeStruct((batch, KV_WIDTH, S), dt)
    r_widths = (LANES, GMLP_WIDTH, GMLP_WIDTH, D_MODEL, D_MODEL)
    out_specs = ([pl.BlockSpec((tm, NSA_WIDTH), row)] + [t_spec] * (N_KV_GROUPS + 4)
                 + [pl.BlockSpec((tm, w), row) for w in r_widths])
    out_shape = ([jax.ShapeDtypeStruct((M, NSA_WIDTH), f32)] + [t_shape(f32)] * N_KV_GROUPS
                 + [t_shape(jnp.bfloat16)] * 4 + [jax.ShapeDtypeStruct((M, w), f32) for w in r_widths])
    return pl.pallas_call(
        _proj_seq_kernel,
        grid=(M // tm,),
        in_specs=[pl.BlockSpec((tm, D_MODEL), row), cw((1, D_MODEL)), cw(wq.shape),
                  cw((N_KV_GROUPS * KV_WIDTH, D_MODEL)), cw(wrest.shape),
                  pl.BlockSpec((tm, LANES), tab), pl.BlockSpec((tm, LANES), tab),
                  pl.BlockSpec((LANES, tm), tab_t), pl.BlockSpec((LANES, tm), tab_t)],
        out_specs=out_specs,
        out_shape=out_shape,
        compiler_params=pltpu.CompilerParams(
            dimension_semantics=("arbitrary",), vmem_limit_bytes=VMEM_LIMIT_BYTES),
        name="mixer_proj_seq",
    )(x, g.reshape(1, -1), b16(wq), b16(wkv.T), b16(wrest), cos, sin, cos.T, sin.T)


def _merge_kernel(oa_ref, u_ref, v_ref, ga_ref, gb_ref, x_ref, lng_ref, lnb_ref, wsp_ref, bsx_ref,
                  wa_ref, wb_ref, wout_ref, gc_ref, wcq_ref, x1_ref, hq_ref):
    tm = x_ref.shape[0]
    v = v_ref[...]
    vc = v - jnp.mean(v, axis=-1, keepdims=True)
    var = jnp.mean(vc * vc, axis=-1, keepdims=True)
    vn = vc * lax.rsqrt(var + NORM_EPS) * lng_ref[...] + lnb_ref[...]
    lane = _iota((GMLP_CHUNK, LANES), 1)
    chunks = []
    for c in range(tm // GMLP_CHUNK):
        cols = []
        for pr in range(GMLP_GROUPS // 2):
            pair = vn[c * GMLP_CHUNK:(c + 1) * GMLP_CHUNK, pr * LANES:(pr + 1) * LANES]
            rhs = jnp.concatenate([jnp.where(lane < GMLP_GROUP_DIM, pair, 0.0),
                                   jnp.where(lane >= GMLP_GROUP_DIM, pair, 0.0)], axis=0).astype(jnp.bfloat16)
            cols.append(_dot(wsp_ref[pr], rhs))
        chunks.append(jnp.concatenate(cols, axis=1) + bsx_ref[...])
    o_b = u_ref[...] * jnp.concatenate(chunks, axis=0)
    m = (jax.nn.sigmoid(ga_ref[...]) * _dot(oa_ref[...].astype(jnp.bfloat16), wa_ref[...])
         + jax.nn.sigmoid(gb_ref[...]) * _dot(o_b.astype(jnp.bfloat16), wb_ref[...]))
    x1 = x_ref[...] + _dot(m.astype(jnp.bfloat16), wout_ref[...])
    x1_ref[...] = x1
    hq_ref[...] = _dot(_rms(x1, gc_ref[...]).astype(jnp.bfloat16), wcq_ref[...])


def gmlp_weights(ws, bs, chunk_len):
    L = chunk_len
    w = jnp.tril(ws[:, :L, :L])
    reps = GMLP_CHUNK // L
    w = jnp.einsum('gij,ab->gaibj', w, jnp.eye(reps, dtype=w.dtype)).reshape(GMLP_GROUPS, GMLP_CHUNK, GMLP_CHUNK)
    wsp = jnp.concatenate([w[0::2], w[1::2]], axis=2)
    bsx = jnp.tile(jnp.repeat(bs[:, :L].T, GMLP_GROUP_DIM, axis=1), (reps, 1))
    return wsp.astype(jnp.bfloat16), bsx


def merge_and_cross_q(o_a, u, v, ga, gb, x, ln_g, ln_b, wsp, bsx, w_a, w_b, w_out, g_cross, w_cq):
    M = x.shape[0]
    tm = min(ROW_TILE, M)
    one = pl.Buffered(1)
    row = lambda i: (i, 0)
    cw = lambda shape: pl.BlockSpec(shape, lambda i: (0,) * len(shape), pipeline_mode=one)
    b16 = lambda t: t.astype(jnp.bfloat16)
    out = jax.ShapeDtypeStruct((M, D_MODEL), jnp.float32)
    return pl.pallas_call(
        _merge_kernel,
        grid=(M // tm,),
        in_specs=[
            pl.BlockSpec((tm, NSA_WIDTH), row), pl.BlockSpec((tm, GMLP_WIDTH), row),
            pl.BlockSpec((tm, GMLP_WIDTH), row), pl.BlockSpec((tm, D_MODEL), row),
            pl.BlockSpec((tm, D_MODEL), row), pl.BlockSpec((tm, D_MODEL), row),
            cw((1, GMLP_WIDTH)), cw((1, GMLP_WIDTH)), cw(wsp.shape), cw(bsx.shape),
            cw(w_a.shape), cw(w_b.shape), cw(w_out.shape), cw((1, D_MODEL)), cw(w_cq.shape),
        ],
        out_specs=[pl.BlockSpec((tm, D_MODEL), row)] * 2,
        out_shape=[out, out],
        compiler_params=pltpu.CompilerParams(
            dimension_semantics=("arbitrary",), vmem_limit_bytes=VMEM_LIMIT_BYTES),
        name="merge_branches",
    )(o_a, u, v, ga, gb, x, ln_g.reshape(1, -1), ln_b.reshape(1, -1), wsp, bsx,
      b16(w_a), b16(w_b), b16(w_out), g_cross.reshape(1, -1), b16(w_cq))


def _memkv_kernel(x_ref, g_ref, wk_ref, wv_ref, k_ref, v_ref):
    h = _rms(x_ref[...], g_ref[...]).astype(jnp.bfloat16)
    k_ref[...] = _dot(h, wk_ref[...])
    v_ref[...] = _dot(h, wv_ref[...])


def memory_kv(mem, g_mem, w_ck, w_cv):
    M = mem.shape[0]
    tm = min(ROW_TILE, M)
    one = pl.Buffered(1)
    row = lambda i: (i, 0)
    cw = lambda shape: pl.BlockSpec(shape, lambda i: (0, 0), pipeline_mode=one)
    out = jax.ShapeDtypeStruct((M, D_MODEL), jnp.float32)
    return pl.pallas_call(
        _memkv_kernel,
        grid=(M // tm,),
        in_specs=[pl.BlockSpec((tm, D_MODEL), row), cw((1, D_MODEL)), cw(w_ck.shape), cw(w_cv.shape)],
        out_specs=[pl.BlockSpec((tm, D_MODEL), row)] * 2,
        out_shape=[out, out],
        compiler_params=pltpu.CompilerParams(
            dimension_semantics=("arbitrary",), vmem_limit_bytes=VMEM_LIMIT_BYTES),
        name="memory_kv",
    )(mem, g_mem.reshape(1, -1), w_ck.astype(jnp.bfloat16), w_cv.astype(jnp.bfloat16))


def _cross_kernel(q_ref, mk_ref, mv_ref, o_ref):
    q = q_ref[0] * CROSS_SCALE
    outs = []
    for h in range(CROSS_HEADS):
        sl = slice(h * CROSS_HEAD_DIM, (h + 1) * CROSS_HEAD_DIM)
        s = _dot_nt(q[:, sl].astype(jnp.bfloat16), mk_ref[0, :, sl].astype(jnp.bfloat16))
        e = jnp.exp(s - jnp.max(s, axis=1, keepdims=True))
        p = e / jnp.sum(e, axis=1, keepdims=True)
        outs.append(_dot(p.astype(jnp.bfloat16), mv_ref[0, :, sl].astype(jnp.bfloat16)))
    o_ref[0] = jnp.concatenate(outs, axis=1)


def cross_attention(hq, mk, mv, *, rows):
    NB, R, _ = hq.shape
    return pl.pallas_call(
        _cross_kernel,
        grid=(NB, R // rows),
        in_specs=[pl.BlockSpec((1, rows, D_MODEL), lambda b, i: (b, i, 0)),
                  pl.BlockSpec((1, N_MEM, D_MODEL), lambda b, i: (b, 0, 0)),
                  pl.BlockSpec((1, N_MEM, D_MODEL), lambda b, i: (b, 0, 0))],
        out_specs=pl.BlockSpec((1, rows, D_MODEL), lambda b, i: (b, i, 0)),
        out_shape=jax.ShapeDtypeStruct((NB, R, D_MODEL), jnp.float32),
        compiler_params=pltpu.CompilerParams(
            dimension_semantics=("arbitrary", "arbitrary"), vmem_limit_bytes=VMEM_LIMIT_BYTES),
        name="cross_attention",
    )(hq, mk, mv)


def _cross_heads_kernel(q_ref, mk_hbm, mv_hbm, o_ref, buf, sem, *, rows):
    b = pl.program_id(0)
    nb = pl.num_programs(0)
    slot = b % 2

    def head_copies(bb, sl):
        return [pltpu.make_async_copy(src.at[bb, :, h, :], buf.at[sl, kv, h], sem.at[sl, kv])
                for kv, src in enumerate((mk_hbm, mv_hbm)) for h in range(CROSS_HEADS)]

    @pl.when(b == 0)
    def _():
        for c in head_copies(0, 0):
            c.start()

    @pl.when(b + 1 < nb)
    def _():
        for c in head_copies(b + 1, 1 - slot):
            c.start()

    q = jnp.concatenate([q_ref[0] * CROSS_SCALE, jnp.zeros((SUBLANES - rows, D_MODEL), jnp.float32)], axis=0)
    qh = [q[:, h * CROSS_HEAD_DIM:(h + 1) * CROSS_HEAD_DIM].astype(jnp.bfloat16) for h in range(CROSS_HEADS)]
    for c in head_copies(b, slot):
        c.wait()
    s = jnp.concatenate([_dot_nt(qh[h], buf[slot, 0, h].astype(jnp.bfloat16)) for h in range(CROSS_HEADS)], axis=0)
    e = jnp.exp(s - jnp.max(s, axis=1, keepdims=True))
    p = (e / jnp.sum(e, axis=1, keepdims=True)).astype(jnp.bfloat16)
    o = jnp.concatenate([_dot(p[h * SUBLANES:(h + 1) * SUBLANES], buf[slot, 1, h].astype(jnp.bfloat16))
                         for h in range(CROSS_HEADS)], axis=1)
    o_ref[0] = o[:rows]


def cross_attention_cached(hq, mk, mv):
    NB, R, _ = hq.shape
    any_spec = pl.BlockSpec(memory_space=pl.ANY)
    return pl.pallas_call(
        functools.partial(_cross_heads_kernel, rows=R),
        grid=(NB,),
        in_specs=[pl.BlockSpec((1, R, D_MODEL), lambda b: (b, 0, 0)), any_spec, any_spec],
        out_specs=pl.BlockSpec((1, R, D_MODEL), lambda b: (b, 0, 0)),
        out_shape=jax.ShapeDtypeStruct((NB, R, D_MODEL), jnp.float32),
        scratch_shapes=[pltpu.VMEM((2, 2, CROSS_HEADS, N_MEM, CROSS_HEAD_DIM), jnp.float32),
                        pltpu.SemaphoreType.DMA((2, 2))],
        compiler_params=pltpu.CompilerParams(
            dimension_semantics=("arbitrary",), vmem_limit_bytes=VMEM_LIMIT_BYTES),
        name="cross_attention_cached",
    )(hq, mk, mv)


def _tail_kernel(x_ref, o_ref, wco_ref, gf_ref, wg_ref, wu_ref, wd_ref, gfin_ref, y_ref, acc_ref):
    x = x_ref[...] + _dot(o_ref[...].astype(jnp.bfloat16), wco_ref[...])
    h = _rms(x, gf_ref[...]).astype(jnp.bfloat16)
    acc_ref[...] = x
    for c in range(D_FF // FF_CHUNK):
        sl = slice(c * FF_CHUNK, (c + 1) * FF_CHUNK)
        a = _dot(h, wg_ref[:, sl])
        b = _dot(h, wu_ref[:, sl])
        t = (a * jax.nn.sigmoid(a) * b).astype(jnp.bfloat16)
        acc_ref[...] += _dot(t, wd_ref[sl, :])
    y_ref[...] = _rms(acc_ref[...], gfin_ref[...])


def tail(x1, o, w_co, g_ffn, w_gate, w_up, w_down, g_final):
    M = x1.shape[0]
    tm = min(ROW_TILE, M)
    row = lambda i: (i, 0)
    cw = lambda shape: pl.BlockSpec(shape, lambda i: (0, 0), pipeline_mode=pl.Buffered(1))
    b16 = lambda t: t.astype(jnp.bfloat16)
    return pl.pallas_call(
        _tail_kernel,
        grid=(M // tm,),
        in_specs=[pl.BlockSpec((tm, D_MODEL), row), pl.BlockSpec((tm, D_MODEL), row),
                  cw(w_co.shape), cw((1, D_MODEL)), cw(w_gate.shape), cw(w_up.shape), cw(w_down.shape),
                  cw((1, D_MODEL))],
        out_specs=pl.BlockSpec((tm, D_MODEL), row),
        out_shape=jax.ShapeDtypeStruct((M, D_MODEL), jnp.float32),
        scratch_shapes=[pltpu.VMEM((tm, D_MODEL), jnp.float32)],
        compiler_params=pltpu.CompilerParams(
            dimension_semantics=("arbitrary",), vmem_limit_bytes=VMEM_LIMIT_BYTES),
        name="tail",
    )(x1, o, b16(w_co), g_ffn.reshape(1, -1), b16(w_gate), b16(w_up), b16(w_down), g_final.reshape(1, -1))


def kernel(x_prompt, x_sample, mem_prompt, cache_cmp_k, cache_cmp_v, cache_slc_k, cache_slc_v,
           page_table, state_win_k, state_win_v, cache_mem_k, cache_mem_v,
           g_mix, w_in, cmp_pe, cmp_w1, cmp_w2, gmlp_ln_g, gmlp_ln_b, gmlp_ws, gmlp_bs,
           w_branch_a, w_branch_b, w_out, g_cross, g_mem, w_cq, w_ck, w_cv, w_co,
           g_ffn, w_gate, w_up, w_down, g_final):
    B, S, _ = x_prompt.shape
    DB, DS, _ = x_sample.shape
    n_pages = page_table.shape[1]
    assert g_mix.shape[0] == 1, "single-layer step"
    l = 0
    cmpw = cmp_weights(cmp_pe[l], cmp_w1[l], cmp_w2[l])
    merge_w = (gmlp_ln_g[l], gmlp_ln_b[l])
    merge_tail = (w_branch_a[l], w_branch_b[l], w_out[l], g_cross[l], w_cq[l])
    ffn_w = (w_co[l], g_ffn[l], w_gate[l], w_up[l], w_down[l], g_final)
    to_t = lambda t: jnp.transpose(t, (0, 2, 3, 1)).reshape(t.shape[0], KV_WIDTH, t.shape[1])
    from_t = lambda t: jnp.transpose(t.reshape(t.shape[0], N_KV_HEADS, HEAD_DIM, t.shape[2]), (0, 3, 1, 2))
    mem3 = lambda t: t.reshape(-1, N_MEM, D_MODEL)

    xp = x_prompt.reshape(B * S, D_MODEL)
    (q, kct, vct, kst, vst, kwt, vwt, kst_b, vst_b, kwt_b, vwt_b, gn, u, v, ga, gb) = mixer_proj_seq(
        xp, g_mix[l], w_in[l], B)
    ck, cv = cmp_blocks_dense(kct, vct, *cmpw)
    o_a = nsa_prompt_attention(q.reshape(B, S, NSA_WIDTH), gn.reshape(B, S, LANES), ck, cv,
                               kst_b, vst_b, kwt_b, vwt_b)
    x1, hq = merge_and_cross_q(o_a.reshape(B * S, NSA_WIDTH), u, v, ga, gb, xp, *merge_w,
                               *gmlp_weights(gmlp_ws[l], gmlp_bs[l], min(S, GMLP_CHUNK)), *merge_tail)
    mk, mv = memory_kv(mem_prompt.reshape(B * N_MEM, D_MODEL), g_mem[l], w_ck[l], w_cv[l])
    o = cross_attention(hq.reshape(B, S, D_MODEL), mem3(mk), mem3(mv), rows=min(S, ROW_TILE))
    y_prompt = tail(x1, o.reshape(B * S, D_MODEL), *ffn_w).reshape(B, S, D_MODEL)
    wb_p = min(WINDOW, S)
    mem5 = lambda t: t.reshape(B, N_MEM, CROSS_HEADS, CROSS_HEAD_DIM)
    prompt_new = (from_t(kct), from_t(vct), from_t(kst), from_t(vst),
                  from_t(kwt[:, :, S - wb_p:]), from_t(vwt[:, :, S - wb_p:]), mem5(mk), mem5(mv))

    xs = x_sample.reshape(DB * DS, D_MODEL)
    pos_s = n_pages * PAGE_SIZE + jnp.arange(DS, dtype=jnp.int32)
    (q, kc, vc, ks, vs, kw, vw, gn, u, v, ga, gb) = mixer_proj_rows(xs, g_mix[l], w_in[l], jnp.tile(pos_s, DB))
    ck, cv = cmp_blocks_paged(to_t(cache_cmp_k[l]), to_t(cache_cmp_v[l]), page_table, *cmpw)
    per_seq = lambda t: t.reshape(DB, DS, -1)
    o_a, win_kt, win_vt = nsa_sample_attention(
        per_seq(q), per_seq(gn), ck, cv, to_t(state_win_k[l]), to_t(state_win_v[l]),
        per_seq(ks), per_seq(vs), per_seq(kw), per_seq(vw),
        to_t(cache_slc_k[l]), to_t(cache_slc_v[l]), page_table)
    x1, hq = merge_and_cross_q(o_a.reshape(DB * DS, NSA_WIDTH), u, v, ga, gb, xs, *merge_w,
                               *gmlp_weights(gmlp_ws[l], gmlp_bs[l], min(DS, GMLP_CHUNK)), *merge_tail)
    o = cross_attention_cached(per_seq(hq), cache_mem_k[l], cache_mem_v[l])
    y_sample = tail(x1, o.reshape(DB * DS, D_MODEL), *ffn_w).reshape(DB, DS, D_MODEL)
    heads = lambda t: t.reshape(DB, DS, N_KV_HEADS, HEAD_DIM)
    sample_new = (heads(kc), heads(vc), heads(ks), heads(vs), from_t(win_kt), from_t(win_vt), per_seq(v))

    return (y_prompt, y_sample) + tuple(t[None] for t in prompt_new + sample_new)
```

```python
import functools

import jax
import jax.numpy as jnp
import numpy as np
from jax import lax
from jax.experimental import pallas as pl
from jax.experimental.pallas import tpu as pltpu

D_MODEL = 1024
N_HEADS = 8
HEAD_DIM = 64
N_KV_HEADS = 2
GROUP = N_HEADS // N_KV_HEADS
NSA_WIDTH = N_HEADS * HEAD_DIM
KV_WIDTH = N_KV_HEADS * HEAD_DIM
CMP_LEN = 32
CMP_STRIDE = 16
CMP_HIDDEN = 128
SLC_BLK = 64
SLC_SHIFT = 6
TOP_N = 16
WINDOW = 512
ROPE_THETA = 10000.0
ATTN_SCALE = HEAD_DIM ** -0.5
LOG2E = 1.4426950408889634
FORCE_BONUS = 1e4
NEG_INF = -1e30
N_BRANCH = 3
GMLP_WIDTH = 512
GMLP_GROUPS = 8
GMLP_GROUP_DIM = GMLP_WIDTH // GMLP_GROUPS
GMLP_CHUNK = 128
N_MEM = 256
CROSS_HEADS = 4
CROSS_HEAD_DIM = D_MODEL // CROSS_HEADS
CROSS_SCALE = CROSS_HEAD_DIM ** -0.5
D_FF = -(-8 * D_MODEL // (3 * 256)) * 256
NORM_EPS = 1e-6
PAGE_SIZE = 128
CHUNKS_PER_PAGE = PAGE_SIZE // CMP_STRIDE
N_KV_GROUPS = 6

LANES = 128
SUBLANES = 8
VMEM_LIMIT_BYTES = 48 * 1024 * 1024
PAGED_VMEM_LIMIT_BYTES = 56 * 1024 * 1024
SEL_TILE = 512
SAMPLE_KEY_CHUNK = 2048
SAMPLE_SEQS_PER_STEP = 2
CROSS_SEQS_PER_STEP = 4
ROW_TILE = 512
FF_CHUNK = 256

_NT = (((1,), (1,)), ((), ()))


def _dot_nt(a, b):
    return lax.dot_general(a, b, _NT, preferred_element_type=jnp.float32)


def _dot(a, b):
    return jnp.dot(a, b, preferred_element_type=jnp.float32)


def _split_bf16(x):
    hi = x.astype(jnp.bfloat16)
    lo = (x - hi.astype(jnp.float32)).astype(jnp.bfloat16)
    return hi, lo


def _iota(shape, dim):
    return lax.broadcasted_iota(jnp.int32, shape, dim)


def _rms(x, g):
    return x * lax.rsqrt(jnp.mean(x * x, axis=-1, keepdims=True) + NORM_EPS) * g


def _cmp_from_transposed(xt_ref, p, xs, wbig_ref, w2big_ref, peterm_ref, out_ref, n_rows):
    n_chunks = n_rows // CMP_STRIDE
    pitch = xs.shape[0] // CMP_STRIDE
    per_blk = LANES // CMP_STRIDE
    for j in range(n_rows // LANES):
        blk = xt_ref[:, j * LANES:(j + 1) * LANES].T
        for k in range(LANES // SUBLANES):
            c, s0 = k // 2, (k % 2) * SUBLANES
            xs[pl.ds(s0 * pitch + j * per_blk + c, SUBLANES, stride=pitch), :] = blk[k * SUBLANES:(k + 1) * SUBLANES]
    acc = jnp.zeros((n_chunks, 4 * CMP_HIDDEN), jnp.float32)
    for sp in range(CMP_STRIDE // 2):
        a0 = xs[pl.ds(2 * sp * pitch, n_chunks), :]
        a1 = xs[pl.ds((2 * sp + 1) * pitch, n_chunks), :]
        lhs = jnp.concatenate([a0, a1], axis=1).astype(jnp.bfloat16)
        acc = acc + _dot(lhs, wbig_ref[p, sp])
    lead = acc[:, :2 * CMP_HIDDEN]
    trail_next = pltpu.roll(acc[:, 2 * CMP_HIDDEN:], n_chunks - 1, axis=0)
    pre = lead + trail_next + peterm_ref[p]
    out = _dot(jax.nn.gelu(pre).astype(jnp.bfloat16), w2big_ref[p])
    out_ref[0] = out.astype(out_ref.dtype)


def _regroup_scratch(n_rows):
    n_chunks = n_rows // CMP_STRIDE
    pitch = n_chunks + SUBLANES if n_chunks % 32 == 0 else n_chunks
    return pltpu.VMEM((CMP_STRIDE * pitch, KV_WIDTH), jnp.float32)


def _page_gather(pt_ref, pools, buf, sem, n_pages, n_seq=1):
    b = pl.program_id(0)
    nb = pl.num_programs(0)
    slot = b % 2

    def copies(bb, sl):
        return [pltpu.make_async_copy(pools[p].at[pt_ref[bb * n_seq + si, j]],
                                      buf.at[sl, si, p, :, pl.ds(j * PAGE_SIZE, PAGE_SIZE)], sem.at[sl, p])
                for si in range(n_seq) for p in range(len(pools)) for j in range(n_pages)]

    @pl.when(b == 0)
    def _():
        for c in copies(0, 0):
            c.start()

    @pl.when(b + 1 < nb)
    def _():
        for c in copies(b + 1, 1 - slot):
            c.start()

    def wait():
        for c in copies(b, slot):
            c.wait()

    return wait


def _cmp_paged_kernel(pt_ref, pool_k, pool_v, wbig_ref, w2big_ref, peterm_ref, ck_ref, cv_ref,
                      buf, xs, sem, *, n_pages):
    wait = _page_gather(pt_ref, (pool_k, pool_v), buf, sem, n_pages)
    wait()
    slot = pl.program_id(0) % 2
    for p, out_ref in ((0, ck_ref), (1, cv_ref)):
        _cmp_from_transposed(buf.at[slot, 0, p], p, xs, wbig_ref, w2big_ref, peterm_ref, out_ref,
                             n_pages * PAGE_SIZE)


def _cmp_dense_kernel(kt_ref, vt_ref, wbig_ref, w2big_ref, peterm_ref, ck_ref, cv_ref, xs):
    for p, (src, out_ref) in enumerate(((kt_ref, ck_ref), (vt_ref, cv_ref))):
        _cmp_from_transposed(src.at[0], p, xs, wbig_ref, w2big_ref, peterm_ref, out_ref, src.shape[2])


def cmp_weights(cmp_pe, cmp_w1, cmp_w2):
    eye = jnp.eye(N_KV_HEADS, dtype=jnp.float32)
    w1r = cmp_w1.reshape(2, 2, CMP_STRIDE // 2, 2, HEAD_DIM, CMP_HIDDEN)
    wbig = jnp.einsum('jtpsdf,hk->jpshdtkf', w1r, eye)
    wbig = wbig.reshape(2, CMP_STRIDE // 2, 2 * KV_WIDTH, 4 * CMP_HIDDEN).astype(jnp.bfloat16)
    w2big = jnp.einsum('jfd,hk->jhfkd', cmp_w2, eye).reshape(2, 2 * CMP_HIDDEN, KV_WIDTH)
    pe_term = jnp.einsum('jsd,jsdf->jf', cmp_pe, cmp_w1)
    peterm = jnp.tile(pe_term[:, None, :], (1, 1, N_KV_HEADS))
    return wbig, w2big.astype(jnp.bfloat16), peterm


def _cmp_weight_specs(wbig, w2big, peterm, nargs):
    c3 = (lambda b: (0, 0, 0)) if nargs == 1 else (lambda b, pt: (0, 0, 0))
    c4 = (lambda b: (0, 0, 0, 0)) if nargs == 1 else (lambda b, pt: (0, 0, 0, 0))
    one = pl.Buffered(1)
    return [pl.BlockSpec(wbig.shape, c4, pipeline_mode=one),
            pl.BlockSpec(w2big.shape, c3, pipeline_mode=one),
            pl.BlockSpec(peterm.shape, c3, pipeline_mode=one)]


def cmp_blocks_paged(pool_k, pool_v, page_table, wbig, w2big, peterm):
    n_seq, n_pages = page_table.shape
    n_rows = n_pages * PAGE_SIZE
    n_chunks = n_rows // CMP_STRIDE
    out = jax.ShapeDtypeStruct((n_seq, n_chunks, KV_WIDTH), jnp.bfloat16)
    grid_spec = pltpu.PrefetchScalarGridSpec(
        num_scalar_prefetch=1,
        grid=(n_seq,),
        in_specs=[pl.BlockSpec(memory_space=pl.ANY), pl.BlockSpec(memory_space=pl.ANY)]
        + _cmp_weight_specs(wbig, w2big, peterm, 2),
        out_specs=[pl.BlockSpec((1, n_chunks, KV_WIDTH), lambda b, pt: (b, 0, 0))] * 2,
        scratch_shapes=[
            pltpu.VMEM((2, 1, 2, KV_WIDTH, n_rows), jnp.float32),
            _regroup_scratch(n_rows),
            pltpu.SemaphoreType.DMA((2, 2)),
        ],
    )
    return pl.pallas_call(
        functools.partial(_cmp_paged_kernel, n_pages=n_pages),
        grid_spec=grid_spec,
        out_shape=[out, out],
        compiler_params=pltpu.CompilerParams(
            dimension_semantics=("arbitrary",), vmem_limit_bytes=PAGED_VMEM_LIMIT_BYTES),
        name="cmp_blocks_paged",
    )(page_table, pool_k, pool_v, wbig, w2big, peterm)


def cmp_blocks_dense(kt, vt, wbig, w2big, peterm):
    B, _, S = kt.shape
    n_chunks = S // CMP_STRIDE
    per_b = lambda b: (b, 0, 0)
    return pl.pallas_call(
        _cmp_dense_kernel,
        grid=(B,),
        in_specs=[pl.BlockSpec((1, KV_WIDTH, S), per_b), pl.BlockSpec((1, KV_WIDTH, S), per_b)]
        + _cmp_weight_specs(wbig, w2big, peterm, 1),
        out_specs=[pl.BlockSpec((1, n_chunks, KV_WIDTH), per_b)] * 2,
        out_shape=[jax.ShapeDtypeStruct((B, n_chunks, KV_WIDTH), jnp.bfloat16)] * 2,
        scratch_shapes=[_regroup_scratch(S)],
        compiler_params=pltpu.CompilerParams(
            dimension_semantics=("arbitrary",), vmem_limit_bytes=VMEM_LIMIT_BYTES),
        name="cmp_blocks_dense",
    )(kt, vt, wbig, w2big, peterm)


def cmp_to_slc_matrix(n_chunks, n_cols):
    cs = np.arange(n_chunks)[:, None] * CMP_STRIDE
    ss = np.arange(n_cols)[None, :] * SLC_BLK
    shared = np.minimum(cs + CMP_LEN, ss + SLC_BLK) - np.maximum(cs, ss)
    w = np.maximum(shared, 0).astype(np.float32) / CMP_LEN
    w[n_chunks - 1] = 0.0
    return w


def _nsa_prompt_kernel(q_ref, gn_ref, ck_ref, cv_ref, ks_ref, vs_ref, kw_ref, vw_ref,
                       wcst_ref, gexp_ref, o_ref, lhs_ref, m_ref, l_ref, acc_ref, *, tq, n_chunks):
    i = pl.program_id(1)
    q0 = i * tq
    qpb = _padded_queries(q_ref[0], tq).astype(jnp.bfloat16)

    def qpos_of(shape):
        return q0 + (_iota(shape, 0) & (tq - 1))

    s = _dot_nt(qpb, ck_ref[0])
    n_idx = _iota(s.shape, 1)
    valid = (n_idx * CMP_STRIDE + (CMP_LEN - 1) <= qpos_of(s.shape)) & (n_idx < n_chunks - 1)
    p_cmp = _masked_softmax2(s, valid)
    o_cmp = _dot(p_cmp.astype(jnp.bfloat16), cv_ref[0])

    blk = _iota((LANES, tq), 0)
    blkf = blk.astype(jnp.float32)
    qblk = (q0 + _iota((LANES, tq), 1)) >> SLC_SHIFT
    forced = (blk == 0) | (blk == qblk) | (blk == qblk - 1)
    wcst = wcst_ref[...]
    nsel = []
    for psum in _group_sum(p_cmp, tq):
        hi, lo = _split_bf16(psum)
        sc = _dot_nt(wcst, hi) + _dot_nt(wcst, lo)
        sc = jnp.where(blk <= qblk, jnp.where(forced, sc + FORCE_BONUS, sc), NEG_INF)
        sel = jnp.zeros_like(sc)
        for _ in range(TOP_N):
            mx = jnp.max(sc, axis=0, keepdims=True)
            first = jnp.min(jnp.where(sc == mx, blkf, float(LANES)), axis=0, keepdims=True)
            hit = blkf == first
            sel = jnp.where(hit, 1.0, sel)
            sc = jnp.where(hit, -jnp.inf, sc)
        nsel.append((1.0 - sel).T.astype(jnp.bfloat16))
    for h in range(N_KV_HEADS):
        r0 = h * GROUP * tq
        lhs_ref[r0:r0 + GROUP * tq, :LANES] = qpb[r0:r0 + GROUP * tq]
        for g in range(GROUP):
            lhs_ref[r0 + g * tq:r0 + (g + 1) * tq, LANES:] = nsel[h]

    m_ref[...] = jnp.full(m_ref.shape, -3e38, jnp.float32)
    l_ref[...] = jnp.zeros(l_ref.shape, jnp.float32)
    acc_ref[...] = jnp.zeros(acc_ref.shape, jnp.float32)
    n_tiles = (q0 + tq + SEL_TILE - 1) // SEL_TILE

    def sel_tile(kt, causal):
        k0 = pl.multiple_of(kt * SEL_TILE, SEL_TILE)
        kblk = (k0 + _iota((LANES, SEL_TILE), 1)) >> SLC_SHIFT
        ebig = jnp.where(kblk == _iota((LANES, SEL_TILE), 0), NEG_INF, 0.0).astype(jnp.bfloat16)
        rhs = jnp.concatenate([ks_ref[0, :, pl.ds(k0, SEL_TILE)], ebig], axis=0)
        st = _dot(lhs_ref[...], rhs)
        if causal:
            st = jnp.where(k0 + _iota(st.shape, 1) <= qpos_of(st.shape), st, NEG_INF)
        cols = [st[:, c * LANES:(c + 1) * LANES] for c in range(SEL_TILE // LANES)]
        mt = functools.reduce(jnp.maximum, cols)
        m_prev = m_ref[...]
        m_new = jnp.maximum(m_prev, jnp.max(mt, axis=1, keepdims=True))
        alpha = jnp.exp2(m_prev - m_new)
        ps = [jnp.exp2(c - m_new) for c in cols]
        l_ref[...] = alpha * l_ref[...] + functools.reduce(jnp.add, ps)
        pb = jnp.concatenate(ps, axis=1).astype(jnp.bfloat16)
        acc_ref[...] = alpha * acc_ref[...] + _dot_nt(pb, vs_ref[0, :, pl.ds(k0, SEL_TILE)])
        m_ref[...] = m_new

    def body(kt, carry):
        sel_tile(kt, False)
        return carry

    lax.fori_loop(0, n_tiles - 1, body, 0)
    sel_tile(n_tiles - 1, True)
    o_sel = acc_ref[...] / jnp.sum(l_ref[...], axis=1, keepdims=True)

    w0 = pl.multiple_of(jnp.maximum(q0 - WINDOW, 0), tq)
    sw = _dot(qpb, kw_ref[0, :, pl.ds(w0, WINDOW + tq)])
    kpos = w0 + _iota(sw.shape, 1)
    qpos = qpos_of(sw.shape)
    p_win = _masked_softmax2(sw, (kpos <= qpos) & (kpos > qpos - WINDOW))
    o_win = _dot_nt(p_win.astype(jnp.bfloat16), vw_ref[0, :, pl.ds(w0, WINDOW + tq)])

    o_ref[0] = _combine((o_cmp, o_sel, o_win), gn_ref[0], gexp_ref[...], tq)


def nsa_prompt_attention(q, gn, ck, cv, ks, vs, kw, vw, *, tq=LANES):
    B, S, _ = q.shape
    n_chunks = ck.shape[1]
    assert S % SEL_TILE == 0 and S // SLC_BLK <= LANES and tq == LANES and S >= WINDOW + tq
    wcst = jnp.asarray(cmp_to_slc_matrix(n_chunks, LANES).T, jnp.bfloat16)
    gexp = gate_expand_matrix()
    rows = N_HEADS * tq
    per_b = lambda b, i: (b, 0, 0)
    tile = lambda b, i: (b, i, 0)
    const = lambda b, i: (0, 0)
    return pl.pallas_call(
        functools.partial(_nsa_prompt_kernel, tq=tq, n_chunks=n_chunks),
        grid=(B, S // tq),
        in_specs=[
            pl.BlockSpec((1, tq, NSA_WIDTH), tile),
            pl.BlockSpec((1, tq, LANES), tile),
            pl.BlockSpec((1, n_chunks, KV_WIDTH), per_b),
            pl.BlockSpec((1, n_chunks, KV_WIDTH), per_b),
            pl.BlockSpec((1, KV_WIDTH, S), per_b),
            pl.BlockSpec((1, KV_WIDTH, S), per_b),
            pl.BlockSpec((1, KV_WIDTH, S), per_b),
            pl.BlockSpec((1, KV_WIDTH, S), per_b),
            pl.BlockSpec(wcst.shape, const),
            pl.BlockSpec(gexp.shape, const),
        ],
        out_specs=pl.BlockSpec((1, tq, NSA_WIDTH), tile),
        out_shape=jax.ShapeDtypeStruct((B, S, NSA_WIDTH), jnp.float32),
        scratch_shapes=[
            pltpu.VMEM((rows, 2 * LANES), jnp.bfloat16),
            pltpu.VMEM((rows, LANES), jnp.float32),
            pltpu.VMEM((rows, LANES), jnp.float32),
            pltpu.VMEM((rows, LANES), jnp.float32),
        ],
        compiler_params=pltpu.CompilerParams(
            dimension_semantics=("arbitrary", "arbitrary"), vmem_limit_bytes=VMEM_LIMIT_BYTES),
        name="nsa_prompt",
    )(q, gn, ck, cv, ks, vs, kw, vw, wcst, gexp)


def _padded_queries(q, tq):
    lane = _iota((tq, LANES), 1)
    blocks = []
    for h in range(N_KV_HEADS):
        for g in range(GROUP):
            c = h * GROUP + g
            pair = q[:, LANES * (c // 2):LANES * (c // 2 + 1)]
            if c % 2 != h:
                pair = pltpu.roll(pair, HEAD_DIM, axis=1)
            keep = (lane >= HEAD_DIM * h) & (lane < HEAD_DIM * (h + 1))
            blocks.append(jnp.where(keep, pair, 0.0))
    return jnp.concatenate(blocks, axis=0)


def _masked_softmax(s, valid):
    sm = jnp.where(valid, s, NEG_INF)
    mx = jnp.max(sm, axis=1, keepdims=True)
    e = jnp.where(valid, jnp.exp(sm - mx), 0.0)
    den = jnp.maximum(jnp.sum(e, axis=1, keepdims=True), 1e-30)
    return e / den


def _masked_softmax2(s2, valid):
    sm = jnp.where(valid, s2, NEG_INF)
    mx = jnp.max(sm, axis=1, keepdims=True)
    e = jnp.exp2(sm - mx)
    inv = jnp.where(mx > 0.5 * NEG_INF, 1.0 / jnp.sum(e, axis=1, keepdims=True), 0.0)
    return e * inv


def _group_sum(p, tq):
    out = []
    for h in range(N_KV_HEADS):
        r0 = h * GROUP * tq
        acc = p[r0:r0 + tq]
        for g in range(1, GROUP):
            acc = acc + p[r0 + g * tq:r0 + (g + 1) * tq]
        out.append(acc)
    return out


def _combine(branches, gn, gexp_w, tq):
    hi, lo = _split_bf16(gn)
    gexp = _dot(hi, gexp_w) + _dot(lo, gexp_w)
    lane = _iota((tq, LANES), 1)
    cols = []
    for h in range(N_KV_HEADS):
        blks = []
        for g in range(GROUP):
            j = h * GROUP + g
            rows = slice(j * tq, (j + 1) * tq)
            o = None
            for br in range(N_BRANCH):
                gate = gexp[:, (j * N_BRANCH + br) * LANES:(j * N_BRANCH + br + 1) * LANES]
                term = gate * branches[br][rows]
                o = term if o is None else o + term
            blks.append(o)
        for jj in range(GROUP // 2):
            a, b2 = blks[2 * jj], blks[2 * jj + 1]
            if h == 0:
                b2 = pltpu.roll(b2, HEAD_DIM, axis=1)
            else:
                a = pltpu.roll(a, HEAD_DIM, axis=1)
            cols.append(jnp.where(lane < HEAD_DIM, a, b2))
    return jnp.concatenate(cols, axis=1)


def gate_expand_matrix():
    e = np.zeros((LANES, N_HEADS * N_BRANCH * LANES), np.float32)
    for j in range(N_HEADS * N_BRANCH):
        e[j, j * LANES:(j + 1) * LANES] = 1.0
    return jnp.asarray(e, jnp.bfloat16)


def _nsa_sample_kernel(pt_ref, q_ref, gn_ref, ck_ref, cv_ref, wk_ref, wv_ref,
                       ksn_ref, vsn_ref, kwn_ref, vwn_ref, pool_k, pool_v, ebig_ref, wcst_ref, gexp_ref,
                       o_ref, nwk_ref, nwv_ref, buf, sem, *, n_pages, n_new, tq, n_seq):
    slot = pl.program_id(0) % 2
    past = n_pages * PAGE_SIZE
    n_chunks = n_pages * CHUNKS_PER_PAGE
    n_past_blk = past // SLC_BLK
    n_slc = -(-(past + n_new) // SLC_BLK)
    wb = wk_ref.shape[2]
    rows = N_HEADS * tq
    wait_pages = _page_gather(pt_ref, (pool_k, pool_v), buf, sem, n_pages, n_seq)

    def tok_of(shape):
        return _iota(shape, 0) & (tq - 1)

    def new_rows(ref, si):
        x = jnp.concatenate([ref[si], jnp.zeros((LANES - tq, KV_WIDTH), jnp.float32)], axis=0)
        return x.astype(jnp.bfloat16)

    def before_pages(si):
        qpb = _padded_queries(q_ref[si] * ATTN_SCALE, tq).astype(jnp.bfloat16)

        s = _dot_nt(qpb, ck_ref[si])
        n_idx = _iota(s.shape, 1)
        valid = (n_idx * CMP_STRIDE + (CMP_LEN - 1) <= past + tok_of(s.shape)) & (n_idx < n_chunks - 1)
        p_cmp = _masked_softmax(s, valid)
        o_cmp = _dot(p_cmp.astype(jnp.bfloat16), cv_ref[si])

        ncol = wcst_ref.shape[0]
        psum = jnp.concatenate(_group_sum(p_cmp, tq)
                               + [jnp.zeros((LANES - N_KV_HEADS * tq, n_chunks), jnp.float32)], axis=0)
        hi, lo = _split_bf16(psum)
        wcst = wcst_ref[...]
        sc = _dot_nt(wcst, hi) + _dot_nt(wcst, lo)
        blk = _iota((ncol, LANES), 0)
        blkf = blk.astype(jnp.float32)
        qblk = (past + (_iota((ncol, LANES), 1) & (tq - 1))) >> SLC_SHIFT
        forced = (blk == 0) | (blk == qblk) | (blk == qblk - 1)
        sc = jnp.where(blk <= qblk, jnp.where(forced, sc + FORCE_BONUS, sc), NEG_INF)
        sc = jnp.where(blk < n_slc, sc, -jnp.inf)
        sel_t = jnp.zeros_like(sc)
        for _ in range(min(TOP_N, n_slc)):
            mx = jnp.max(sc, axis=0, keepdims=True)
            first = jnp.min(jnp.where(sc == mx, blkf, float(ncol)), axis=0, keepdims=True)
            hit = blkf == first
            sel_t = jnp.where(hit, 1.0, sel_t)
            sc = jnp.where(hit, -jnp.inf, sc)
        sel = sel_t.T
        nsel, sel_new = [], []
        for h in range(N_KV_HEADS):
            sel_h = sel[h * tq:(h + 1) * tq]
            nsel.append(jnp.concatenate([1.0 - sel_h[:, :LANES]] * GROUP, axis=0).astype(jnp.bfloat16))
            sel_new.append(jnp.broadcast_to(sel_h[:, n_past_blk:n_past_blk + 1], (tq, LANES)))
        lhs = jnp.concatenate(
            [jnp.concatenate([qpb[h * GROUP * tq:(h + 1) * GROUP * tq], nsel[h]], axis=1)
             for h in range(N_KV_HEADS)], axis=0)
        sel_new_rows = jnp.concatenate([sel_new[h] for h in range(N_KV_HEADS) for _ in range(GROUP)], axis=0)

        tn = _iota((rows, LANES), 1)
        new_ok = (tn <= tok_of((rows, LANES))) & (tn < n_new)
        s_sn = jnp.where(new_ok & (sel_new_rows > 0.5), _dot_nt(qpb, new_rows(ksn_ref, si)), NEG_INF)
        s_wn = jnp.where(new_ok, _dot_nt(qpb, new_rows(kwn_ref, si)), NEG_INF)

        wk = wk_ref[si]
        wv = wv_ref[si]
        s_w = _dot(qpb, wk.astype(jnp.bfloat16))
        kwpos = past - wb + _iota(s_w.shape, 1)
        qpos = past + tok_of(s_w.shape)
        s_w = jnp.where((kwpos >= 0) & (kwpos <= qpos) & (kwpos > qpos - WINDOW), s_w, NEG_INF)
        mx = jnp.maximum(jnp.max(s_w, axis=1, keepdims=True), jnp.max(s_wn, axis=1, keepdims=True))
        e_w = jnp.exp(s_w - mx)
        e_wn = jnp.exp(s_wn - mx)
        den = jnp.sum(e_w, axis=1, keepdims=True) + jnp.sum(e_wn, axis=1, keepdims=True)
        o_win = (_dot_nt(e_w.astype(jnp.bfloat16), wv.astype(jnp.bfloat16))
                 + _dot(e_wn.astype(jnp.bfloat16), new_rows(vwn_ref, si))) / den

        col = _iota((KV_WIDTH, wb), 1)
        for src, new_ref, dst in ((wk, kwn_ref, nwk_ref), (wv, vwn_ref, nwv_ref)):
            new_t = jnp.concatenate([new_ref[si], jnp.zeros((LANES - tq, KV_WIDTH), jnp.float32)], axis=0).T
            tail = jnp.concatenate([jnp.zeros((KV_WIDTH, wb - LANES), jnp.float32), new_t], axis=1)
            dst[si] = jnp.where(col >= wb - n_new, pltpu.roll(tail, LANES - n_new, axis=1),
                                pltpu.roll(src, wb - n_new, axis=1))

        return lhs, s_sn, o_cmp, o_win

    def after_pages(si, lhs, s_sn, o_cmp, o_win):
        ck_keys = min(SAMPLE_KEY_CHUNK, past)
        s_chunks = []
        for c in range(past // ck_keys):
            kc = buf[slot, si, 0, :, pl.ds(c * ck_keys, ck_keys)].astype(jnp.bfloat16)
            rhs = jnp.concatenate([kc, ebig_ref[:, pl.ds(c * ck_keys, ck_keys)]], axis=0)
            s_chunks.append(_dot(lhs, rhs))
        mx = jnp.max(s_sn, axis=1, keepdims=True)
        for sc in s_chunks:
            mx = jnp.maximum(mx, jnp.max(sc, axis=1, keepdims=True))
        e_sn = jnp.exp(s_sn - mx)
        den = jnp.sum(e_sn, axis=1, keepdims=True)
        o_sel = _dot(e_sn.astype(jnp.bfloat16), new_rows(vsn_ref, si))
        for c, sc in enumerate(s_chunks):
            e = jnp.exp(sc - mx)
            den = den + jnp.sum(e, axis=1, keepdims=True)
            vc = buf[slot, si, 1, :, pl.ds(c * ck_keys, ck_keys)].astype(jnp.bfloat16)
            o_sel = o_sel + _dot_nt(e.astype(jnp.bfloat16), vc)
        o_sel = o_sel / den

        o_ref[si] = _combine((o_cmp, o_sel, o_win), gn_ref[si], gexp_ref[...], tq)

    staged = [before_pages(si) for si in range(n_seq)]
    wait_pages()
    for si in range(n_seq):
        after_pages(si, *staged[si])


def nsa_sample_attention(q, gn, ck, cv, win_k, win_v, ks_new, vs_new, kw_new, vw_new,
                         pool_k, pool_v, page_table, *, tq=SUBLANES, n_seq=SAMPLE_SEQS_PER_STEP):
    DB, DS, _ = q.shape
    n_pages = page_table.shape[1]
    past = n_pages * PAGE_SIZE
    n_chunks = ck.shape[1]
    wb = win_k.shape[2]
    n_slc = -(-(past + DS) // SLC_BLK)
    assert DS <= tq and DS < CMP_STRIDE and past // SLC_BLK <= LANES and wb % LANES == 0
    assert past % min(SAMPLE_KEY_CHUNK, past) == 0 and DB % n_seq == 0
    ncol = -(-n_slc // LANES) * LANES
    pad_t = lambda t: jnp.pad(t, ((0, 0), (0, tq - DS), (0, 0)))
    kb = np.arange(past)[None, :] // SLC_BLK
    ebig = jnp.asarray(np.where(kb == np.arange(LANES)[:, None], NEG_INF, 0.0), jnp.bfloat16)
    wcst = jnp.asarray(cmp_to_slc_matrix(n_chunks, ncol).T, jnp.bfloat16)
    gexp = gate_expand_matrix()
    per_b = lambda b, pt: (b, 0, 0)
    const = lambda b, pt: (0, 0)
    blk3 = lambda n, w: pl.BlockSpec((n_seq, n, w), per_b)
    grid_spec = pltpu.PrefetchScalarGridSpec(
        num_scalar_prefetch=1,
        grid=(DB // n_seq,),
        in_specs=[
            blk3(tq, NSA_WIDTH), blk3(tq, LANES), blk3(n_chunks, KV_WIDTH), blk3(n_chunks, KV_WIDTH),
            blk3(KV_WIDTH, wb), blk3(KV_WIDTH, wb),
            blk3(tq, KV_WIDTH), blk3(tq, KV_WIDTH), blk3(tq, KV_WIDTH), blk3(tq, KV_WIDTH),
            pl.BlockSpec(memory_space=pl.ANY), pl.BlockSpec(memory_space=pl.ANY),
            pl.BlockSpec(ebig.shape, const, pipeline_mode=pl.Buffered(1)),
            pl.BlockSpec(wcst.shape, const, pipeline_mode=pl.Buffered(1)),
            pl.BlockSpec(gexp.shape, const, pipeline_mode=pl.Buffered(1)),
        ],
        out_specs=[blk3(tq, NSA_WIDTH), blk3(KV_WIDTH, wb), blk3(KV_WIDTH, wb)],
        scratch_shapes=[
            pltpu.VMEM((2, n_seq, 2, KV_WIDTH, past), jnp.float32),
            pltpu.SemaphoreType.DMA((2, 2)),
        ],
    )
    o, nwk, nwv = pl.pallas_call(
        functools.partial(_nsa_sample_kernel, n_pages=n_pages, n_new=DS, tq=tq, n_seq=n_seq),
        grid_spec=grid_spec,
        out_shape=[jax.ShapeDtypeStruct((DB, tq, NSA_WIDTH), jnp.float32),
                   jax.ShapeDtypeStruct((DB, KV_WIDTH, wb), jnp.float32),
                   jax.ShapeDtypeStruct((DB, KV_WIDTH, wb), jnp.float32)],
        compiler_params=pltpu.CompilerParams(
            dimension_semantics=("arbitrary",), vmem_limit_bytes=PAGED_VMEM_LIMIT_BYTES),
        name="nsa_sample",
    )(page_table, pad_t(q), pad_t(gn), ck, cv, win_k, win_v, pad_t(ks_new), pad_t(vs_new), pad_t(kw_new),
      pad_t(vw_new), pool_k, pool_v, ebig, wcst, gexp)
    return o[:, :DS], nwk, nwv


def rope_tables(pos):
    half = HEAD_DIM // 2
    inv_freq = ROPE_THETA ** (-jnp.arange(half, dtype=jnp.float32) / half)
    ang = pos.astype(jnp.float32)[:, None] * inv_freq[None, :]
    cos, sin = jnp.cos(ang), jnp.sin(ang)
    cos_t = jnp.tile(cos, (1, LANES // half))
    sin_t = jnp.tile(jnp.concatenate([-sin, sin], axis=1), (1, LANES // HEAD_DIM))
    return cos_t, sin_t


def _rope_lanes(y, cos, sin):
    first_half = (_iota((y.shape[0], LANES), 1) & (HEAD_DIM - 1)) < HEAD_DIM // 2
    cols = []
    for c in range(y.shape[1] // LANES):
        yc = y[:, c * LANES:(c + 1) * LANES]
        rot = jnp.where(first_half, pltpu.roll(yc, LANES - HEAD_DIM // 2, axis=1),
                        pltpu.roll(yc, HEAD_DIM // 2, axis=1))
        cols.append(yc * cos + rot * sin)
    return cols[0] if len(cols) == 1 else jnp.concatenate(cols, axis=1)


def _rope_sublanes(y, cos_t, sin_t):
    hh = HEAD_DIM // 2
    out = []
    for c in range(y.shape[0] // LANES):
        yc = y[c * LANES:(c + 1) * LANES]
        rot = jnp.concatenate([yc[hh:2 * hh], yc[:hh], yc[3 * hh:], yc[2 * hh:3 * hh]], axis=0)
        out.append(yc * cos_t + rot * sin_t)
    return out[0] if len(out) == 1 else jnp.concatenate(out, axis=0)


_REST_ACTS = {"u": jax.nn.gelu, "v": jax.nn.gelu, "gn": jax.nn.sigmoid}


def _proj_rest(h, wrest_ref, refs, names):
    c0 = 0
    for ref, name in zip(refs, names):
        width = ref.shape[-1]
        act = _REST_ACTS.get(name)
        for c in range(0, width, GMLP_WIDTH):
            w = min(GMLP_WIDTH, width - c)
            y = _dot(h, wrest_ref[:, c0 + c:c0 + c + w])
            ref[:, c:c + w] = y if act is None else act(y)
        c0 += width


def _proj_rows_kernel(x_ref, g_ref, wq_ref, wkv_ref, wrest_ref, cos_ref, sin_ref, q_ref, *out_refs):
    h = _rms(x_ref[...], g_ref[...]).astype(jnp.bfloat16)
    cos, sin = cos_ref[...], sin_ref[...]
    q_ref[...] = _rope_lanes(_dot(h, wq_ref[...]), cos, sin)
    y = _dot(h, wkv_ref[...])
    for gi in range(N_KV_GROUPS):
        blk = y[:, gi * KV_WIDTH:(gi + 1) * KV_WIDTH]
        out_refs[gi][...] = _rope_lanes(blk, cos, sin) if gi % 2 == 0 else blk
    _proj_rest(h, wrest_ref, out_refs[N_KV_GROUPS:], ("gn", "u", "v", "ga", "gb"))


def _proj_seq_kernel(x_ref, g_ref, wq_ref, wkvt_ref, wrest_ref, cos_ref, sin_ref, cost_ref, sint_ref,
                     q_ref, kct_ref, vct_ref, kst_ref, vst_ref, kwt_ref, vwt_ref,
                     kst_b_ref, vst_b_ref, kwt_b_ref, vwt_b_ref, gn_ref, u_ref, v_ref, ga_ref, gb_ref):
    h = _rms(x_ref[...], g_ref[...]).astype(jnp.bfloat16)
    q_ref[...] = _rope_lanes(_dot(h, wq_ref[...]), cos_ref[...], sin_ref[...]) * (ATTN_SCALE * LOG2E)
    cos_t, sin_t = cost_ref[...], sint_ref[...]
    yt = _dot_nt(wkvt_ref[...], h)
    kv_refs = (kct_ref, vct_ref, kst_ref, vst_ref, kwt_ref, vwt_ref)
    kvb_refs = (None, None, kst_b_ref, vst_b_ref, kwt_b_ref, vwt_b_ref)
    for gi in range(N_KV_GROUPS):
        blk = yt[gi * KV_WIDTH:(gi + 1) * KV_WIDTH]
        if gi % 2 == 0:
            blk = _rope_sublanes(blk, cos_t, sin_t)
        kv_refs[gi][0] = blk
        if kvb_refs[gi] is not None:
            kvb_refs[gi][0] = blk.astype(jnp.bfloat16)
    _proj_rest(h, wrest_ref, (gn_ref, u_ref, v_ref, ga_ref, gb_ref), ("gn", "u", "v", "ga", "gb"))


def _split_w_in(w_in):
    sizes = (NSA_WIDTH,) + (KV_WIDTH,) * N_KV_GROUPS + (N_HEADS * N_BRANCH,)
    offs = np.cumsum((0,) + sizes)
    wq = w_in[:, offs[0]:offs[1]]
    wkv = w_in[:, offs[1]:offs[7]]
    wgn = w_in[:, offs[7]:offs[8]]
    wrest = w_in[:, offs[8]:]
    return wq, wkv, wgn, wrest


def mixer_proj_rows(x, g, w_in, pos):
    M = x.shape[0]
    tm = min(ROW_TILE, M)
    assert M % tm == 0
    wq, wkv, wgn, wrest = _split_w_in(w_in)
    wrest = jnp.concatenate([jnp.pad(wgn, ((0, 0), (0, LANES - wgn.shape[1]))), wrest], axis=1)
    cos, sin = rope_tables(pos)
    b16 = lambda t: t.astype(jnp.bfloat16)
    row = lambda i: (i, 0)
    cw = lambda shape: pl.BlockSpec(shape, lambda i: (0, 0), pipeline_mode=pl.Buffered(1))
    widths = (NSA_WIDTH,) + (KV_WIDTH,) * N_KV_GROUPS + (LANES, GMLP_WIDTH, GMLP_WIDTH, D_MODEL, D_MODEL)
    return pl.pallas_call(
        _proj_rows_kernel,
        grid=(M // tm,),
        in_specs=[pl.BlockSpec((tm, D_MODEL), row), cw((1, D_MODEL)), cw(wq.shape), cw(wkv.shape),
                  cw(wrest.shape), pl.BlockSpec((tm, LANES), row), pl.BlockSpec((tm, LANES), row)],
        out_specs=[pl.BlockSpec((tm, w), row) for w in widths],
        out_shape=[jax.ShapeDtypeStruct((M, w), jnp.float32) for w in widths],
        compiler_params=pltpu.CompilerParams(
            dimension_semantics=("arbitrary",), vmem_limit_bytes=VMEM_LIMIT_BYTES),
        name="mixer_proj_rows",
    )(x, g.reshape(1, -1), b16(wq), b16(wkv), b16(wrest), cos, sin)


def mixer_proj_seq(x, g, w_in, batch):
    M = x.shape[0]
    S = M // batch
    tm = min(ROW_TILE, S)
    assert S % tm == 0
    n_s = S // tm
    wq, wkv, wgn, wrest = _split_w_in(w_in)
    wrest = jnp.concatenate([jnp.pad(wgn, ((0, 0), (0, LANES - wgn.shape[1]))), wrest], axis=1)
    cos, sin = rope_tables(jnp.arange(S, dtype=jnp.int32))
    b16 = lambda t: t.astype(jnp.bfloat16)
    f32 = jnp.float32
    row = lambda i: (i, 0)
    tab = lambda i: (i % n_s, 0)
    tab_t = lambda i: (0, i % n_s)
    seq_t = lambda i: (i // n_s, 0, i % n_s)
    cw = lambda shape: pl.BlockSpec(shape, lambda i: (0, 0), pipeline_mode=pl.Buffered(1))
    t_spec = pl.BlockSpec((1, KV_WIDTH, tm), seq_t)
    t_shape = lambda dt: jax.ShapeDtypeStruct((batch, KV_WIDTH, S), dt)
    r_widths = (LANES, GMLP_WIDTH, GMLP_WIDTH, D_MODEL, D_MODEL)
    out_specs = ([pl.BlockSpec((tm, NSA_WIDTH), row)] + [t_spec] * (N_KV_GROUPS + 4)
                 + [pl.BlockSpec((tm, w), row) for w in r_widths])
    out_shape = ([jax.ShapeDtypeStruct((M, NSA_WIDTH), f32)] + [t_shape(f32)] * N_KV_GROUPS
                 + [t_shape(jnp.bfloat16)] * 4 + [jax.ShapeDtypeStruct((M, w), f32) for w in r_widths])
    return pl.pallas_call(
        _proj_seq_kernel,
        grid=(M // tm,),
        in_specs=[pl.BlockSpec((tm, D_MODEL), row), cw((1, D_MODEL)), cw(wq.shape),
                  cw((N_KV_GROUPS * KV_WIDTH, D_MODEL)), cw(wrest.shape),
                  pl.BlockSpec((tm, LANES), tab), pl.BlockSpec((tm, LANES), tab),
                  pl.BlockSpec((LANES, tm), tab_t), pl.BlockSpec((LANES, tm), tab_t)],
        out_specs=out_specs,
        out_shape=out_shape,
        compiler_params=pltpu.CompilerParams(
            dimension_semantics=("arbitrary",), vmem_limit_bytes=VMEM_LIMIT_BYTES),
        name="mixer_proj_seq",
    )(x, g.reshape(1, -1), b16(wq), b16(wkv.T), b16(wrest), cos, sin, cos.T, sin.T)


def _merge_kernel(oa_ref, u_ref, v_ref, ga_ref, gb_ref, x_ref, lng_ref, lnb_ref, wsp_ref, bsx_ref,
                  wa_ref, wb_ref, wout_ref, gc_ref, wcq_ref, x1_ref, hq_ref):
    tm = x_ref.shape[0]
    v = v_ref[...]
    vc = v - jnp.mean(v, axis=-1, keepdims=True)
    var = jnp.mean(vc * vc, axis=-1, keepdims=True)
    vn = vc * lax.rsqrt(var + NORM_EPS) * lng_ref[...] + lnb_ref[...]
    lane = _iota((GMLP_CHUNK, LANES), 1)
    chunks = []
    for c in range(tm // GMLP_CHUNK):
        cols = []
        for pr in range(GMLP_GROUPS // 2):
            pair = vn[c * GMLP_CHUNK:(c + 1) * GMLP_CHUNK, pr * LANES:(pr + 1) * LANES]
            rhs = jnp.concatenate([jnp.where(lane < GMLP_GROUP_DIM, pair, 0.0),
                                   jnp.where(lane >= GMLP_GROUP_DIM, pair, 0.0)], axis=0).astype(jnp.bfloat16)
            cols.append(_dot(wsp_ref[pr], rhs))
        chunks.append(jnp.concatenate(cols, axis=1) + bsx_ref[...])
    o_b = u_ref[...] * jnp.concatenate(chunks, axis=0)
    m = (jax.nn.sigmoid(ga_ref[...]) * _dot(oa_ref[...].astype(jnp.bfloat16), wa_ref[...])
         + jax.nn.sigmoid(gb_ref[...]) * _dot(o_b.astype(jnp.bfloat16), wb_ref[...]))
    x1 = x_ref[...] + _dot(m.astype(jnp.bfloat16), wout_ref[...])
    x1_ref[...] = x1
    hq_ref[...] = _dot(_rms(x1, gc_ref[...]).astype(jnp.bfloat16), wcq_ref[...])


def gmlp_weights(ws, bs, chunk_len):
    L = chunk_len
    w = jnp.tril(ws[:, :L, :L])
    reps = GMLP_CHUNK // L
    w = jnp.einsum('gij,ab->gaibj', w, jnp.eye(reps, dtype=w.dtype)).reshape(GMLP_GROUPS, GMLP_CHUNK, GMLP_CHUNK)
    pairs = w.reshape(GMLP_GROUPS // 2, 2, GMLP_CHUNK, GMLP_CHUNK)
    wsp = jnp.swapaxes(pairs, 1, 2).reshape(GMLP_GROUPS // 2, GMLP_CHUNK, 2 * GMLP_CHUNK)
    bsx = jnp.tile(jnp.repeat(bs[:, :L].T, GMLP_GROUP_DIM, axis=1), (reps, 1))
    return wsp.astype(jnp.bfloat16), bsx


def merge_and_cross_q(o_a, u, v, ga, gb, x, ln_g, ln_b, wsp, bsx, w_a, w_b, w_out, g_cross, w_cq):
    M = x.shape[0]
    tm = min(ROW_TILE, M)
    one = pl.Buffered(1)
    row = lambda i: (i, 0)
    cw = lambda shape: pl.BlockSpec(shape, lambda i: (0,) * len(shape), pipeline_mode=one)
    b16 = lambda t: t.astype(jnp.bfloat16)
    out = jax.ShapeDtypeStruct((M, D_MODEL), jnp.float32)
    return pl.pallas_call(
        _merge_kernel,
        grid=(M // tm,),
        in_specs=[
            pl.BlockSpec((tm, NSA_WIDTH), row), pl.BlockSpec((tm, GMLP_WIDTH), row),
            pl.BlockSpec((tm, GMLP_WIDTH), row), pl.BlockSpec((tm, D_MODEL), row),
            pl.BlockSpec((tm, D_MODEL), row), pl.BlockSpec((tm, D_MODEL), row),
            cw((1, GMLP_WIDTH)), cw((1, GMLP_WIDTH)), cw(wsp.shape), cw(bsx.shape),
            cw(w_a.shape), cw(w_b.shape), cw(w_out.shape), cw((1, D_MODEL)), cw(w_cq.shape),
        ],
        out_specs=[pl.BlockSpec((tm, D_MODEL), row)] * 2,
        out_shape=[out, out],
        compiler_params=pltpu.CompilerParams(
            dimension_semantics=("arbitrary",), vmem_limit_bytes=VMEM_LIMIT_BYTES),
        name="merge_branches",
    )(o_a, u, v, ga, gb, x, ln_g.reshape(1, -1), ln_b.reshape(1, -1), wsp, bsx,
      b16(w_a), b16(w_b), b16(w_out), g_cross.reshape(1, -1), b16(w_cq))


def _memkv_kernel(x_ref, g_ref, wk_ref, wv_ref, k_ref, v_ref):
    h = _rms(x_ref[...], g_ref[...]).astype(jnp.bfloat16)
    k_ref[...] = _dot(h, wk_ref[...])
    v_ref[...] = _dot(h, wv_ref[...])


def memory_kv(mem, g_mem, w_ck, w_cv):
    M = mem.shape[0]
    tm = min(ROW_TILE, M)
    one = pl.Buffered(1)
    row = lambda i: (i, 0)
    cw = lambda shape: pl.BlockSpec(shape, lambda i: (0, 0), pipeline_mode=one)
    out = jax.ShapeDtypeStruct((M, D_MODEL), jnp.float32)
    return pl.pallas_call(
        _memkv_kernel,
        grid=(M // tm,),
        in_specs=[pl.BlockSpec((tm, D_MODEL), row), cw((1, D_MODEL)), cw(w_ck.shape), cw(w_cv.shape)],
        out_specs=[pl.BlockSpec((tm, D_MODEL), row)] * 2,
        out_shape=[out, out],
        compiler_params=pltpu.CompilerParams(
            dimension_semantics=("arbitrary",), vmem_limit_bytes=VMEM_LIMIT_BYTES),
        name="memory_kv",
    )(mem, g_mem.reshape(1, -1), w_ck.astype(jnp.bfloat16), w_cv.astype(jnp.bfloat16))


def _cross_kernel(q_ref, mk_ref, mv_ref, o_ref):
    q = q_ref[0] * CROSS_SCALE
    outs = []
    for h in range(CROSS_HEADS):
        sl = slice(h * CROSS_HEAD_DIM, (h + 1) * CROSS_HEAD_DIM)
        s = _dot_nt(q[:, sl].astype(jnp.bfloat16), mk_ref[0, :, sl].astype(jnp.bfloat16))
        e = jnp.exp(s - jnp.max(s, axis=1, keepdims=True))
        p = e / jnp.sum(e, axis=1, keepdims=True)
        outs.append(_dot(p.astype(jnp.bfloat16), mv_ref[0, :, sl].astype(jnp.bfloat16)))
    o_ref[0] = jnp.concatenate(outs, axis=1)


def cross_attention(hq, mk, mv, *, rows):
    NB, R, _ = hq.shape
    return pl.pallas_call(
        _cross_kernel,
        grid=(NB, R // rows),
        in_specs=[pl.BlockSpec((1, rows, D_MODEL), lambda b, i: (b, i, 0)),
                  pl.BlockSpec((1, N_MEM, D_MODEL), lambda b, i: (b, 0, 0)),
                  pl.BlockSpec((1, N_MEM, D_MODEL), lambda b, i: (b, 0, 0))],
        out_specs=pl.BlockSpec((1, rows, D_MODEL), lambda b, i: (b, i, 0)),
        out_shape=jax.ShapeDtypeStruct((NB, R, D_MODEL), jnp.float32),
        compiler_params=pltpu.CompilerParams(
            dimension_semantics=("arbitrary", "arbitrary"), vmem_limit_bytes=VMEM_LIMIT_BYTES),
        name="cross_attention",
    )(hq, mk, mv)


def _cross_heads_kernel(q_ref, mk_hbm, mv_hbm, o_ref, buf, sem, *, rows, n_seq):
    b = pl.program_id(0)
    nb = pl.num_programs(0)
    slot = b % 2

    def head_copies(bb, sl):
        return [pltpu.make_async_copy(src.at[bb * n_seq + si, :, h, :], buf.at[sl, si, kv, h], sem.at[sl, kv])
                for si in range(n_seq) for kv, src in enumerate((mk_hbm, mv_hbm)) for h in range(CROSS_HEADS)]

    @pl.when(b == 0)
    def _():
        for c in head_copies(0, 0):
            c.start()

    @pl.when(b + 1 < nb)
    def _():
        for c in head_copies(b + 1, 1 - slot):
            c.start()

    pad = jnp.zeros((SUBLANES - rows, D_MODEL), jnp.float32)
    qs = [jnp.concatenate([q_ref[si] * CROSS_SCALE, pad], axis=0) for si in range(n_seq)]
    for c in head_copies(b, slot):
        c.wait()
    for si in range(n_seq):
        qh = [qs[si][:, h * CROSS_HEAD_DIM:(h + 1) * CROSS_HEAD_DIM].astype(jnp.bfloat16) for h in range(CROSS_HEADS)]
        s = jnp.concatenate([_dot_nt(qh[h], buf[slot, si, 0, h].astype(jnp.bfloat16)) for h in range(CROSS_HEADS)], axis=0)
        e = jnp.exp(s - jnp.max(s, axis=1, keepdims=True))
        p = (e / jnp.sum(e, axis=1, keepdims=True)).astype(jnp.bfloat16)
        o = jnp.concatenate([_dot(p[h * SUBLANES:(h + 1) * SUBLANES], buf[slot, si, 1, h].astype(jnp.bfloat16))
                             for h in range(CROSS_HEADS)], axis=1)
        o_ref[si] = o[:rows]


def cross_attention_cached(hq, mk, mv, *, n_seq=CROSS_SEQS_PER_STEP):
    NB, R, _ = hq.shape
    assert NB % n_seq == 0
    any_spec = pl.BlockSpec(memory_space=pl.ANY)
    return pl.pallas_call(
        functools.partial(_cross_heads_kernel, rows=R, n_seq=n_seq),
        grid=(NB // n_seq,),
        in_specs=[pl.BlockSpec((n_seq, R, D_MODEL), lambda b: (b, 0, 0)), any_spec, any_spec],
        out_specs=pl.BlockSpec((n_seq, R, D_MODEL), lambda b: (b, 0, 0)),
        out_shape=jax.ShapeDtypeStruct((NB, R, D_MODEL), jnp.float32),
        scratch_shapes=[pltpu.VMEM((2, n_seq, 2, CROSS_HEADS, N_MEM, CROSS_HEAD_DIM), jnp.float32),
                        pltpu.SemaphoreType.DMA((2, 2))],
        compiler_params=pltpu.CompilerParams(
            dimension_semantics=("arbitrary",), vmem_limit_bytes=VMEM_LIMIT_BYTES),
        name="cross_attention_cached",
    )(hq, mk, mv)


def _tail_kernel(x_ref, o_ref, wco_ref, gf_ref, wg_ref, wu_ref, wd_ref, gfin_ref, y_ref, acc_ref):
    x = x_ref[...] + _dot(o_ref[...].astype(jnp.bfloat16), wco_ref[...])
    h = _rms(x, gf_ref[...]).astype(jnp.bfloat16)
    acc_ref[...] = x
    for c in range(D_FF // FF_CHUNK):
        sl = slice(c * FF_CHUNK, (c + 1) * FF_CHUNK)
        a = _dot(h, wg_ref[:, sl])
        b = _dot(h, wu_ref[:, sl])
        t = (a * jax.nn.sigmoid(a) * b).astype(jnp.bfloat16)
        acc_ref[...] += _dot(t, wd_ref[sl, :])
    y_ref[...] = _rms(acc_ref[...], gfin_ref[...])


def tail(x1, o, w_co, g_ffn, w_gate, w_up, w_down, g_final):
    M = x1.shape[0]
    tm = min(ROW_TILE, M)
    row = lambda i: (i, 0)
    cw = lambda shape: pl.BlockSpec(shape, lambda i: (0, 0), pipeline_mode=pl.Buffered(1))
    b16 = lambda t: t.astype(jnp.bfloat16)
    return pl.pallas_call(
        _tail_kernel,
        grid=(M // tm,),
        in_specs=[pl.BlockSpec((tm, D_MODEL), row), pl.BlockSpec((tm, D_MODEL), row),
                  cw(w_co.shape), cw((1, D_MODEL)), cw(w_gate.shape), cw(w_up.shape), cw(w_down.shape),
                  cw((1, D_MODEL))],
        out_specs=pl.BlockSpec((tm, D_MODEL), row),
        out_shape=jax.ShapeDtypeStruct((M, D_MODEL), jnp.float32),
        scratch_shapes=[pltpu.VMEM((tm, D_MODEL), jnp.float32)],
        compiler_params=pltpu.CompilerParams(
            dimension_semantics=("arbitrary",), vmem_limit_bytes=VMEM_LIMIT_BYTES),
        name="tail",
    )(x1, o, b16(w_co), g_ffn.reshape(1, -1), b16(w_gate), b16(w_up), b16(w_down), g_final.reshape(1, -1))


def kernel(x_prompt, x_sample, mem_prompt, cache_cmp_k, cache_cmp_v, cache_slc_k, cache_slc_v,
           page_table, state_win_k, state_win_v, cache_mem_k, cache_mem_v,
           g_mix, w_in, cmp_pe, cmp_w1, cmp_w2, gmlp_ln_g, gmlp_ln_b, gmlp_ws, gmlp_bs,
           w_branch_a, w_branch_b, w_out, g_cross, g_mem, w_cq, w_ck, w_cv, w_co,
           g_ffn, w_gate, w_up, w_down, g_final):
    B, S, _ = x_prompt.shape
    DB, DS, _ = x_sample.shape
    n_pages = page_table.shape[1]
    assert g_mix.shape[0] == 1, "single-layer step"
    l = 0
    cmpw = cmp_weights(cmp_pe[l], cmp_w1[l], cmp_w2[l])
    merge_w = (gmlp_ln_g[l], gmlp_ln_b[l])
    merge_tail = (w_branch_a[l], w_branch_b[l], w_out[l], g_cross[l], w_cq[l])
    ffn_w = (w_co[l], g_ffn[l], w_gate[l], w_up[l], w_down[l], g_final)
    to_t = lambda t: jnp.transpose(t, (0, 2, 3, 1)).reshape(t.shape[0], KV_WIDTH, t.shape[1])
    from_t = lambda t: jnp.transpose(t.reshape(t.shape[0], N_KV_HEADS, HEAD_DIM, t.shape[2]), (0, 3, 1, 2))
    mem3 = lambda t: t.reshape(-1, N_MEM, D_MODEL)

    xp = x_prompt.reshape(B * S, D_MODEL)
    (q, kct, vct, kst, vst, kwt, vwt, kst_b, vst_b, kwt_b, vwt_b, gn, u, v, ga, gb) = mixer_proj_seq(
        xp, g_mix[l], w_in[l], B)
    ck, cv = cmp_blocks_dense(kct, vct, *cmpw)
    o_a = nsa_prompt_attention(q.reshape(B, S, NSA_WIDTH), gn.reshape(B, S, LANES), ck, cv,
                               kst_b, vst_b, kwt_b, vwt_b)
    x1, hq = merge_and_cross_q(o_a.reshape(B * S, NSA_WIDTH), u, v, ga, gb, xp, *merge_w,
                               *gmlp_weights(gmlp_ws[l], gmlp_bs[l], min(S, GMLP_CHUNK)), *merge_tail)
    mk, mv = memory_kv(mem_prompt.reshape(B * N_MEM, D_MODEL), g_mem[l], w_ck[l], w_cv[l])
    o = cross_attention(hq.reshape(B, S, D_MODEL), mem3(mk), mem3(mv), rows=min(S, ROW_TILE))
    y_prompt = tail(x1, o.reshape(B * S, D_MODEL), *ffn_w).reshape(B, S, D_MODEL)
    wb_p = min(WINDOW, S)
    mem5 = lambda t: t.reshape(B, N_MEM, CROSS_HEADS, CROSS_HEAD_DIM)
    prompt_new = (from_t(kct), from_t(vct), from_t(kst), from_t(vst),
                  from_t(kwt[:, :, S - wb_p:]), from_t(vwt[:, :, S - wb_p:]), mem5(mk), mem5(mv))

    xs = x_sample.reshape(DB * DS, D_MODEL)
    pos_s = n_pages * PAGE_SIZE + jnp.arange(DS, dtype=jnp.int32)
    (q, kc, vc, ks, vs, kw, vw, gn, u, v, ga, gb) = mixer_proj_rows(xs, g_mix[l], w_in[l], jnp.tile(pos_s, DB))
    ck, cv = cmp_blocks_paged(to_t(cache_cmp_k[l]), to_t(cache_cmp_v[l]), page_table, *cmpw)
    per_seq = lambda t: t.reshape(DB, DS, -1)
    o_a, win_kt, win_vt = nsa_sample_attention(
        per_seq(q), per_seq(gn), ck, cv, to_t(state_win_k[l]), to_t(state_win_v[l]),
        per_seq(ks), per_seq(vs), per_seq(kw), per_seq(vw),
        to_t(cache_slc_k[l]), to_t(cache_slc_v[l]), page_table)
    x1, hq = merge_and_cross_q(o_a.reshape(DB * DS, NSA_WIDTH), u, v, ga, gb, xs, *merge_w,
                               *gmlp_weights(gmlp_ws[l], gmlp_bs[l], min(DS, GMLP_CHUNK)), *merge_tail)
    o = cross_attention_cached(per_seq(hq), cache_mem_k[l], cache_mem_v[l])
    y_sample = tail(x1, o.reshape(DB * DS, D_MODEL), *ffn_w).reshape(DB, DS, D_MODEL)
    heads = lambda t: t.reshape(DB, DS, N_KV_HEADS, HEAD_DIM)
    sample_new = (heads(kc), heads(vc), heads(ks), heads(vs), from_t(win_kt), from_t(win_vt), per_seq(v))

    return (y_prompt, y_sample) + tuple(t[None] for t in prompt_new + sample_new)
```

```python
import functools

import jax
import jax.numpy as jnp
import numpy as np
from jax import lax
from jax.experimental import pallas as pl
from jax.experimental.pallas import tpu as pltpu

D_MODEL = 1024
N_HEADS = 8
HEAD_DIM = 64
N_KV_HEADS = 2
GROUP = N_HEADS // N_KV_HEADS
NSA_WIDTH = N_HEADS * HEAD_DIM
KV_WIDTH = N_KV_HEADS * HEAD_DIM
CMP_LEN = 32
CMP_STRIDE = 16
CMP_HIDDEN = 128
SLC_BLK = 64
SLC_SHIFT = 6
TOP_N = 16
WINDOW = 512
ROPE_THETA = 10000.0
ATTN_SCALE = HEAD_DIM ** -0.5
LOG2E = 1.4426950408889634
FORCE_BONUS = 1e4
NEG_INF = -1e30
N_BRANCH = 3
GMLP_WIDTH = 512
GMLP_GROUPS = 8
GMLP_GROUP_DIM = GMLP_WIDTH // GMLP_GROUPS
GMLP_CHUNK = 128
N_MEM = 256
CROSS_HEADS = 4
CROSS_HEAD_DIM = D_MODEL // CROSS_HEADS
CROSS_SCALE = CROSS_HEAD_DIM ** -0.5
D_FF = -(-8 * D_MODEL // (3 * 256)) * 256
NORM_EPS = 1e-6
PAGE_SIZE = 128
CHUNKS_PER_PAGE = PAGE_SIZE // CMP_STRIDE
N_KV_GROUPS = 6

LANES = 128
SUBLANES = 8
VMEM_LIMIT_BYTES = 48 * 1024 * 1024
PAGED_VMEM_LIMIT_BYTES = 56 * 1024 * 1024
SEL_TILE = 1024
SAMPLE_KEY_CHUNK = 2048
SAMPLE_SEQS_PER_STEP = 2
CROSS_SEQS_PER_STEP = 4
ROW_TILE = 512
FF_CHUNK = 256

_NT = (((1,), (1,)), ((), ()))


def _dot_nt(a, b):
    return lax.dot_general(a, b, _NT, preferred_element_type=jnp.float32)


def _dot(a, b):
    return jnp.dot(a, b, preferred_element_type=jnp.float32)


def _split_bf16(x):
    hi = x.astype(jnp.bfloat16)
    lo = (x - hi.astype(jnp.float32)).astype(jnp.bfloat16)
    return hi, lo


def _iota(shape, dim):
    return lax.broadcasted_iota(jnp.int32, shape, dim)


def _rms(x, g):
    return x * lax.rsqrt(jnp.mean(x * x, axis=-1, keepdims=True) + NORM_EPS) * g


def _cmp_from_transposed(xt_ref, p, xs, wbig_ref, w2big_ref, peterm_ref, out_ref, n_rows):
    n_chunks = n_rows // CMP_STRIDE
    pitch = xs.shape[0] // CMP_STRIDE
    per_blk = LANES // CMP_STRIDE
    for j in range(n_rows // LANES):
        blk = xt_ref[:, j * LANES:(j + 1) * LANES].T
        for k in range(LANES // SUBLANES):
            c, s0 = k // 2, (k % 2) * SUBLANES
            xs[pl.ds(s0 * pitch + j * per_blk + c, SUBLANES, stride=pitch), :] = blk[k * SUBLANES:(k + 1) * SUBLANES]
    acc = jnp.zeros((n_chunks, 4 * CMP_HIDDEN), jnp.float32)
    for sp in range(CMP_STRIDE // 2):
        a0 = xs[pl.ds(2 * sp * pitch, n_chunks), :]
        a1 = xs[pl.ds((2 * sp + 1) * pitch, n_chunks), :]
        lhs = jnp.concatenate([a0, a1], axis=1).astype(jnp.bfloat16)
        acc = acc + _dot(lhs, wbig_ref[p, sp])
    lead = acc[:, :2 * CMP_HIDDEN]
    trail_next = pltpu.roll(acc[:, 2 * CMP_HIDDEN:], n_chunks - 1, axis=0)
    pre = lead + trail_next + peterm_ref[p]
    out = _dot(jax.nn.gelu(pre).astype(jnp.bfloat16), w2big_ref[p])
    out_ref[0] = out.astype(out_ref.dtype)


def _regroup_scratch(n_rows):
    n_chunks = n_rows // CMP_STRIDE
    pitch = n_chunks + SUBLANES if n_chunks % 32 == 0 else n_chunks
    return pltpu.VMEM((CMP_STRIDE * pitch, KV_WIDTH), jnp.float32)


def _page_gather(pt_ref, pools, buf, sem, n_pages, n_seq=1):
    b = pl.program_id(0)
    nb = pl.num_programs(0)
    slot = b % 2

    def copies(bb, sl):
        return [pltpu.make_async_copy(pools[p].at[pt_ref[bb * n_seq + si, j]],
                                      buf.at[sl, si, p, :, pl.ds(j * PAGE_SIZE, PAGE_SIZE)], sem.at[sl, p])
                for si in range(n_seq) for p in range(len(pools)) for j in range(n_pages)]

    @pl.when(b == 0)
    def _():
        for c in copies(0, 0):
            c.start()

    @pl.when(b + 1 < nb)
    def _():
        for c in copies(b + 1, 1 - slot):
            c.start()

    def wait():
        for c in copies(b, slot):
            c.wait()

    return wait


def _cmp_paged_kernel(pt_ref, pool_k, pool_v, wbig_ref, w2big_ref, peterm_ref, ck_ref, cv_ref,
                      buf, xs, sem, *, n_pages):
    wait = _page_gather(pt_ref, (pool_k, pool_v), buf, sem, n_pages)
    wait()
    slot = pl.program_id(0) % 2
    for p, out_ref in ((0, ck_ref), (1, cv_ref)):
        _cmp_from_transposed(buf.at[slot, 0, p], p, xs, wbig_ref, w2big_ref, peterm_ref, out_ref,
                             n_pages * PAGE_SIZE)


def _cmp_dense_kernel(kt_ref, vt_ref, wbig_ref, w2big_ref, peterm_ref, ck_ref, cv_ref, xs):
    for p, (src, out_ref) in enumerate(((kt_ref, ck_ref), (vt_ref, cv_ref))):
        _cmp_from_transposed(src.at[0], p, xs, wbig_ref, w2big_ref, peterm_ref, out_ref, src.shape[2])


def cmp_weights(cmp_pe, cmp_w1, cmp_w2):
    eye = jnp.eye(N_KV_HEADS, dtype=jnp.float32)
    w1r = cmp_w1.reshape(2, 2, CMP_STRIDE // 2, 2, HEAD_DIM, CMP_HIDDEN)
    wbig = jnp.einsum('jtpsdf,hk->jpshdtkf', w1r, eye)
    wbig = wbig.reshape(2, CMP_STRIDE // 2, 2 * KV_WIDTH, 4 * CMP_HIDDEN).astype(jnp.bfloat16)
    w2big = jnp.einsum('jfd,hk->jhfkd', cmp_w2, eye).reshape(2, 2 * CMP_HIDDEN, KV_WIDTH)
    pe_term = jnp.einsum('jsd,jsdf->jf', cmp_pe, cmp_w1)
    peterm = jnp.tile(pe_term[:, None, :], (1, 1, N_KV_HEADS))
    return wbig, w2big.astype(jnp.bfloat16), peterm


def _cmp_weight_specs(wbig, w2big, peterm, nargs):
    c3 = (lambda b: (0, 0, 0)) if nargs == 1 else (lambda b, pt: (0, 0, 0))
    c4 = (lambda b: (0, 0, 0, 0)) if nargs == 1 else (lambda b, pt: (0, 0, 0, 0))
    one = pl.Buffered(1)
    return [pl.BlockSpec(wbig.shape, c4, pipeline_mode=one),
            pl.BlockSpec(w2big.shape, c3, pipeline_mode=one),
            pl.BlockSpec(peterm.shape, c3, pipeline_mode=one)]


def cmp_blocks_paged(pool_k, pool_v, page_table, wbig, w2big, peterm):
    n_seq, n_pages = page_table.shape
    n_rows = n_pages * PAGE_SIZE
    n_chunks = n_rows // CMP_STRIDE
    out = jax.ShapeDtypeStruct((n_seq, n_chunks, KV_WIDTH), jnp.bfloat16)
    grid_spec = pltpu.PrefetchScalarGridSpec(
        num_scalar_prefetch=1,
        grid=(n_seq,),
        in_specs=[pl.BlockSpec(memory_space=pl.ANY), pl.BlockSpec(memory_space=pl.ANY)]
        + _cmp_weight_specs(wbig, w2big, peterm, 2),
        out_specs=[pl.BlockSpec((1, n_chunks, KV_WIDTH), lambda b, pt: (b, 0, 0))] * 2,
        scratch_shapes=[
            pltpu.VMEM((2, 1, 2, KV_WIDTH, n_rows), jnp.float32),
            _regroup_scratch(n_rows),
            pltpu.SemaphoreType.DMA((2, 2)),
        ],
    )
    return pl.pallas_call(
        functools.partial(_cmp_paged_kernel, n_pages=n_pages),
        grid_spec=grid_spec,
        out_shape=[out, out],
        compiler_params=pltpu.CompilerParams(
            dimension_semantics=("arbitrary",), vmem_limit_bytes=PAGED_VMEM_LIMIT_BYTES),
        name="cmp_blocks_paged",
    )(page_table, pool_k, pool_v, wbig, w2big, peterm)


def cmp_blocks_dense(kt, vt, wbig, w2big, peterm):
    B, _, S = kt.shape
    n_chunks = S // CMP_STRIDE
    per_b = lambda b: (b, 0, 0)
    return pl.pallas_call(
        _cmp_dense_kernel,
        grid=(B,),
        in_specs=[pl.BlockSpec((1, KV_WIDTH, S), per_b), pl.BlockSpec((1, KV_WIDTH, S), per_b)]
        + _cmp_weight_specs(wbig, w2big, peterm, 1),
        out_specs=[pl.BlockSpec((1, n_chunks, KV_WIDTH), per_b)] * 2,
        out_shape=[jax.ShapeDtypeStruct((B, n_chunks, KV_WIDTH), jnp.bfloat16)] * 2,
        scratch_shapes=[_regroup_scratch(S)],
        compiler_params=pltpu.CompilerParams(
            dimension_semantics=("arbitrary",), vmem_limit_bytes=VMEM_LIMIT_BYTES),
        name="cmp_blocks_dense",
    )(kt, vt, wbig, w2big, peterm)


def cmp_to_slc_matrix(n_chunks, n_cols):
    cs = np.arange(n_chunks)[:, None] * CMP_STRIDE
    ss = np.arange(n_cols)[None, :] * SLC_BLK
    shared = np.minimum(cs + CMP_LEN, ss + SLC_BLK) - np.maximum(cs, ss)
    w = np.maximum(shared, 0).astype(np.float32) / CMP_LEN
    w[n_chunks - 1] = 0.0
    return w


def _nsa_prompt_kernel(q_ref, gn_ref, ck_ref, cv_ref, ks_ref, vs_ref, kw_ref, vw_ref,
                       wcst_ref, gexp_ref, o_ref, lhs_ref, m_ref, l_ref, acc_ref, *, tq, n_chunks):
    i = pl.program_id(1)
    q0 = i * tq
    qpb = _padded_queries(q_ref[0], tq).astype(jnp.bfloat16)

    def qpos_of(shape):
        return q0 + (_iota(shape, 0) & (tq - 1))

    s = _dot_nt(qpb, ck_ref[0])
    n_idx = _iota(s.shape, 1)
    valid = (n_idx * CMP_STRIDE + (CMP_LEN - 1) <= qpos_of(s.shape)) & (n_idx < n_chunks - 1)
    p_cmp = _masked_softmax2(s, valid)
    o_cmp = _dot(p_cmp.astype(jnp.bfloat16), cv_ref[0])

    blk = _iota((LANES, tq), 0)
    blkf = blk.astype(jnp.float32)
    qblk = (q0 + _iota((LANES, tq), 1)) >> SLC_SHIFT
    forced = (blk == 0) | (blk == qblk) | (blk == qblk - 1)
    wcst = wcst_ref[...]
    nsel = []
    for psum in _group_sum(p_cmp, tq):
        hi, lo = _split_bf16(psum)
        sc = _dot_nt(wcst, hi) + _dot_nt(wcst, lo)
        sc = jnp.where(blk <= qblk, jnp.where(forced, sc + FORCE_BONUS, sc), NEG_INF)
        sel = jnp.zeros_like(sc)
        for _ in range(TOP_N):
            mx = jnp.max(sc, axis=0, keepdims=True)
            first = jnp.min(jnp.where(sc == mx, blkf, float(LANES)), axis=0, keepdims=True)
            hit = blkf == first
            sel = jnp.where(hit, 1.0, sel)
            sc = jnp.where(hit, -jnp.inf, sc)
        nsel.append((1.0 - sel).T.astype(jnp.bfloat16))
    for h in range(N_KV_HEADS):
        r0 = h * GROUP * tq
        lhs_ref[r0:r0 + GROUP * tq, :LANES] = qpb[r0:r0 + GROUP * tq]
        for g in range(GROUP):
            lhs_ref[r0 + g * tq:r0 + (g + 1) * tq, LANES:] = nsel[h]

    m_ref[...] = jnp.full(m_ref.shape, -3e38, jnp.float32)
    l_ref[...] = jnp.zeros(l_ref.shape, jnp.float32)
    acc_ref[...] = jnp.zeros(acc_ref.shape, jnp.float32)
    n_tiles = (q0 + tq + SEL_TILE - 1) // SEL_TILE

    def sel_tile(kt, causal):
        k0 = pl.multiple_of(kt * SEL_TILE, SEL_TILE)
        kblk = (k0 + _iota((LANES, SEL_TILE), 1)) >> SLC_SHIFT
        ebig = jnp.where(kblk == _iota((LANES, SEL_TILE), 0), NEG_INF, 0.0).astype(jnp.bfloat16)
        rhs = jnp.concatenate([ks_ref[0, :, pl.ds(k0, SEL_TILE)], ebig], axis=0)
        st = _dot(lhs_ref[...], rhs)
        if causal:
            st = jnp.where(k0 + _iota(st.shape, 1) <= qpos_of(st.shape), st, NEG_INF)
        cols = [st[:, c * LANES:(c + 1) * LANES] for c in range(SEL_TILE // LANES)]
        mt = functools.reduce(jnp.maximum, cols)
        m_prev = m_ref[...]
        m_new = jnp.maximum(m_prev, jnp.max(mt, axis=1, keepdims=True))
        alpha = jnp.exp2(m_prev - m_new)
        ps = [jnp.exp2(c - m_new) for c in cols]
        l_ref[...] = alpha * l_ref[...] + functools.reduce(jnp.add, ps)
        pb = jnp.concatenate(ps, axis=1).astype(jnp.bfloat16)
        acc_ref[...] = alpha * acc_ref[...] + _dot_nt(pb, vs_ref[0, :, pl.ds(k0, SEL_TILE)])
        m_ref[...] = m_new

    def body(kt, carry):
        sel_tile(kt, False)
        return carry

    lax.fori_loop(0, n_tiles - 1, body, 0)
    sel_tile(n_tiles - 1, True)
    o_sel = acc_ref[...] / jnp.sum(l_ref[...], axis=1, keepdims=True)

    w0 = pl.multiple_of(jnp.maximum(q0 - WINDOW, 0), tq)
    sw = _dot(qpb, kw_ref[0, :, pl.ds(w0, WINDOW + tq)])
    kpos = w0 + _iota(sw.shape, 1)
    qpos = qpos_of(sw.shape)
    p_win = _masked_softmax2(sw, (kpos <= qpos) & (kpos > qpos - WINDOW))
    o_win = _dot_nt(p_win.astype(jnp.bfloat16), vw_ref[0, :, pl.ds(w0, WINDOW + tq)])

    o_ref[0] = _combine((o_cmp, o_sel, o_win), gn_ref[0], gexp_ref[...], tq)


def nsa_prompt_attention(q, gn, ck, cv, ks, vs, kw, vw, *, tq=LANES):
    B, S, _ = q.shape
    n_chunks = ck.shape[1]
    assert S % SEL_TILE == 0 and S // SLC_BLK <= LANES and tq == LANES and S >= WINDOW + tq
    wcst = jnp.asarray(cmp_to_slc_matrix(n_chunks, LANES).T, jnp.bfloat16)
    gexp = gate_expand_matrix()
    rows = N_HEADS * tq
    per_b = lambda b, i: (b, 0, 0)
    tile = lambda b, i: (b, i, 0)
    const = lambda b, i: (0, 0)
    return pl.pallas_call(
        functools.partial(_nsa_prompt_kernel, tq=tq, n_chunks=n_chunks),
        grid=(B, S // tq),
        in_specs=[
            pl.BlockSpec((1, tq, NSA_WIDTH), tile),
            pl.BlockSpec((1, tq, LANES), tile),
            pl.BlockSpec((1, n_chunks, KV_WIDTH), per_b),
            pl.BlockSpec((1, n_chunks, KV_WIDTH), per_b),
            pl.BlockSpec((1, KV_WIDTH, S), per_b),
            pl.BlockSpec((1, KV_WIDTH, S), per_b),
            pl.BlockSpec((1, KV_WIDTH, S), per_b),
            pl.BlockSpec((1, KV_WIDTH, S), per_b),
            pl.BlockSpec(wcst.shape, const),
            pl.BlockSpec(gexp.shape, const),
        ],
        out_specs=pl.BlockSpec((1, tq, NSA_WIDTH), tile),
        out_shape=jax.ShapeDtypeStruct((B, S, NSA_WIDTH), jnp.float32),
        scratch_shapes=[
            pltpu.VMEM((rows, 2 * LANES), jnp.bfloat16),
            pltpu.VMEM((rows, LANES), jnp.float32),
            pltpu.VMEM((rows, LANES), jnp.float32),
            pltpu.VMEM((rows, LANES), jnp.float32),
        ],
        compiler_params=pltpu.CompilerParams(
            dimension_semantics=("arbitrary", "arbitrary"), vmem_limit_bytes=VMEM_LIMIT_BYTES),
        name="nsa_prompt",
    )(q, gn, ck, cv, ks, vs, kw, vw, wcst, gexp)


def _padded_queries(q, tq):
    lane = _iota((tq, LANES), 1)
    blocks = []
    for h in range(N_KV_HEADS):
        for g in range(GROUP):
            c = h * GROUP + g
            pair = q[:, LANES * (c // 2):LANES * (c // 2 + 1)]
            if c % 2 != h:
                pair = pltpu.roll(pair, HEAD_DIM, axis=1)
            keep = (lane >= HEAD_DIM * h) & (lane < HEAD_DIM * (h + 1))
            blocks.append(jnp.where(keep, pair, 0.0))
    return jnp.concatenate(blocks, axis=0)


def _masked_softmax(s, valid):
    sm = jnp.where(valid, s, NEG_INF)
    mx = jnp.max(sm, axis=1, keepdims=True)
    e = jnp.where(valid, jnp.exp(sm - mx), 0.0)
    den = jnp.maximum(jnp.sum(e, axis=1, keepdims=True), 1e-30)
    return e / den


def _masked_softmax2(s2, valid):
    sm = jnp.where(valid, s2, NEG_INF)
    mx = jnp.max(sm, axis=1, keepdims=True)
    e = jnp.exp2(sm - mx)
    inv = jnp.where(mx > 0.5 * NEG_INF, 1.0 / jnp.sum(e, axis=1, keepdims=True), 0.0)
    return e * inv


def _group_sum(p, tq):
    out = []
    for h in range(N_KV_HEADS):
        r0 = h * GROUP * tq
        acc = p[r0:r0 + tq]
        for g in range(1, GROUP):
            acc = acc + p[r0 + g * tq:r0 + (g + 1) * tq]
        out.append(acc)
    return out


def _combine(branches, gn, gexp_w, tq):
    hi, lo = _split_bf16(gn)
    gexp = _dot(hi, gexp_w) + _dot(lo, gexp_w)
    lane = _iota((tq, LANES), 1)
    cols = []
    for h in range(N_KV_HEADS):
        blks = []
        for g in range(GROUP):
            j = h * GROUP + g
            rows = slice(j * tq, (j + 1) * tq)
            o = None
            for br in range(N_BRANCH):
                gate = gexp[:, (j * N_BRANCH + br) * LANES:(j * N_BRANCH + br + 1) * LANES]
                term = gate * branches[br][rows]
                o = term if o is None else o + term
            blks.append(o)
        for jj in range(GROUP // 2):
            a, b2 = blks[2 * jj], blks[2 * jj + 1]
            if h == 0:
                b2 = pltpu.roll(b2, HEAD_DIM, axis=1)
            else:
                a = pltpu.roll(a, HEAD_DIM, axis=1)
            cols.append(jnp.where(lane < HEAD_DIM, a, b2))
    return jnp.concatenate(cols, axis=1)


def gate_expand_matrix():
    e = np.zeros((LANES, N_HEADS * N_BRANCH * LANES), np.float32)
    for j in range(N_HEADS * N_BRANCH):
        e[j, j * LANES:(j + 1) * LANES] = 1.0
    return jnp.asarray(e, jnp.bfloat16)


def _nsa_sample_kernel(pt_ref, q_ref, gn_ref, ck_ref, cv_ref, wk_ref, wv_ref,
                       ksn_ref, vsn_ref, kwn_ref, vwn_ref, pool_k, pool_v, ebig_ref, wcst_ref, gexp_ref,
                       o_ref, nwk_ref, nwv_ref, buf, sem, *, n_pages, n_new, tq, n_seq):
    slot = pl.program_id(0) % 2
    past = n_pages * PAGE_SIZE
    n_chunks = n_pages * CHUNKS_PER_PAGE
    n_past_blk = past // SLC_BLK
    n_slc = -(-(past + n_new) // SLC_BLK)
    wb = wk_ref.shape[2]
    rows = N_HEADS * tq
    wait_pages = _page_gather(pt_ref, (pool_k, pool_v), buf, sem, n_pages, n_seq)

    def tok_of(shape):
        return _iota(shape, 0) & (tq - 1)

    def new_rows(ref, si):
        x = jnp.concatenate([ref[si], jnp.zeros((LANES - tq, KV_WIDTH), jnp.float32)], axis=0)
        return x.astype(jnp.bfloat16)

    def compressed(si):
        qpb = _padded_queries(q_ref[si] * ATTN_SCALE, tq).astype(jnp.bfloat16)
        s = _dot_nt(qpb, ck_ref[si])
        n_idx = _iota(s.shape, 1)
        valid = (n_idx * CMP_STRIDE + (CMP_LEN - 1) <= past + tok_of(s.shape)) & (n_idx < n_chunks - 1)
        p_cmp = _masked_softmax(s, valid)
        return qpb, p_cmp, _dot(p_cmp.astype(jnp.bfloat16), cv_ref[si])

    def select(p_cmps):
        ncol = wcst_ref.shape[0]
        groups = [g for p in p_cmps for g in _group_sum(p, tq)]
        psum = jnp.concatenate(groups + [jnp.zeros((LANES - len(groups) * tq, n_chunks), jnp.float32)], axis=0)
        hi, lo = _split_bf16(psum)
        wcst = wcst_ref[...]
        sc = _dot_nt(wcst, hi) + _dot_nt(wcst, lo)
        blk = _iota((ncol, LANES), 0)
        blkf = blk.astype(jnp.float32)
        qblk = (past + (_iota((ncol, LANES), 1) & (tq - 1))) >> SLC_SHIFT
        forced = (blk == 0) | (blk == qblk) | (blk == qblk - 1)
        sc = jnp.where(blk <= qblk, jnp.where(forced, sc + FORCE_BONUS, sc), NEG_INF)
        sc = jnp.where(blk < n_slc, sc, -jnp.inf)
        sel_t = jnp.zeros_like(sc)
        for _ in range(min(TOP_N, n_slc)):
            mx = jnp.max(sc, axis=0, keepdims=True)
            first = jnp.min(jnp.where(sc == mx, blkf, float(ncol)), axis=0, keepdims=True)
            hit = blkf == first
            sel_t = jnp.where(hit, 1.0, sel_t)
            sc = jnp.where(hit, -jnp.inf, sc)
        sel = sel_t.T
        return [[sel[(si * N_KV_HEADS + h) * tq:(si * N_KV_HEADS + h + 1) * tq] for h in range(N_KV_HEADS)]
                for si in range(n_seq)]

    def window_and_operands(si, qpb, sel):
        nsel = [jnp.concatenate([1.0 - sel[h][:, :LANES]] * GROUP, axis=0).astype(jnp.bfloat16) for h in range(N_KV_HEADS)]
        sel_new = [jnp.broadcast_to(sel[h][:, n_past_blk:n_past_blk + 1], (tq, LANES)) for h in range(N_KV_HEADS)]
        lhs = jnp.concatenate(
            [jnp.concatenate([qpb[h * GROUP * tq:(h + 1) * GROUP * tq], nsel[h]], axis=1)
             for h in range(N_KV_HEADS)], axis=0)
        sel_new_rows = jnp.concatenate([sel_new[h] for h in range(N_KV_HEADS) for _ in range(GROUP)], axis=0)

        tn = _iota((rows, LANES), 1)
        new_ok = (tn <= tok_of((rows, LANES))) & (tn < n_new)
        s_sn = jnp.where(new_ok & (sel_new_rows > 0.5), _dot_nt(qpb, new_rows(ksn_ref, si)), NEG_INF)
        s_wn = jnp.where(new_ok, _dot_nt(qpb, new_rows(kwn_ref, si)), NEG_INF)

        wk = wk_ref[si]
        wv = wv_ref[si]
        s_w = _dot(qpb, wk.astype(jnp.bfloat16))
        kwpos = past - wb + _iota(s_w.shape, 1)
        qpos = past + tok_of(s_w.shape)
        s_w = jnp.where((kwpos >= 0) & (kwpos <= qpos) & (kwpos > qpos - WINDOW), s_w, NEG_INF)
        mx = jnp.maximum(jnp.max(s_w, axis=1, keepdims=True), jnp.max(s_wn, axis=1, keepdims=True))
        e_w = jnp.exp(s_w - mx)
        e_wn = jnp.exp(s_wn - mx)
        den = jnp.sum(e_w, axis=1, keepdims=True) + jnp.sum(e_wn, axis=1, keepdims=True)
        o_win = (_dot_nt(e_w.astype(jnp.bfloat16), wv.astype(jnp.bfloat16))
                 + _dot(e_wn.astype(jnp.bfloat16), new_rows(vwn_ref, si))) / den

        col = _iota((KV_WIDTH, wb), 1)
        for src, new_ref, dst in ((wk, kwn_ref, nwk_ref), (wv, vwn_ref, nwv_ref)):
            new_t = jnp.concatenate([new_ref[si], jnp.zeros((LANES - tq, KV_WIDTH), jnp.float32)], axis=0).T
            tail = jnp.concatenate([jnp.zeros((KV_WIDTH, wb - LANES), jnp.float32), new_t], axis=1)
            dst[si] = jnp.where(col >= wb - n_new, pltpu.roll(tail, LANES - n_new, axis=1),
                                pltpu.roll(src, wb - n_new, axis=1))
        return lhs, s_sn, o_win

    def selected_and_combine(si, lhs, s_sn, o_cmp, o_win):
        ck_keys = min(SAMPLE_KEY_CHUNK, past)
        s_chunks = []
        for c in range(past // ck_keys):
            kc = buf[slot, si, 0, :, pl.ds(c * ck_keys, ck_keys)].astype(jnp.bfloat16)
            rhs = jnp.concatenate([kc, ebig_ref[:, pl.ds(c * ck_keys, ck_keys)]], axis=0)
            s_chunks.append(_dot(lhs, rhs))
        mx = jnp.max(s_sn, axis=1, keepdims=True)
        for sc in s_chunks:
            mx = jnp.maximum(mx, jnp.max(sc, axis=1, keepdims=True))
        e_sn = jnp.exp(s_sn - mx)
        den = jnp.sum(e_sn, axis=1, keepdims=True)
        o_sel = _dot(e_sn.astype(jnp.bfloat16), new_rows(vsn_ref, si))
        for c, sc in enumerate(s_chunks):
            e = jnp.exp(sc - mx)
            den = den + jnp.sum(e, axis=1, keepdims=True)
            vc = buf[slot, si, 1, :, pl.ds(c * ck_keys, ck_keys)].astype(jnp.bfloat16)
            o_sel = o_sel + _dot_nt(e.astype(jnp.bfloat16), vc)
        o_sel = o_sel / den

        o_ref[si] = _combine((o_cmp, o_sel, o_win), gn_ref[si], gexp_ref[...], tq)

    seqs = range(n_seq)
    qpb, p_cmp, o_cmp = zip(*[compressed(si) for si in seqs])
    sel = select(p_cmp)
    staged = [window_and_operands(si, qpb[si], sel[si]) for si in seqs]
    wait_pages()
    for si in seqs:
        lhs, s_sn, o_win = staged[si]
        selected_and_combine(si, lhs, s_sn, o_cmp[si], o_win)


def nsa_sample_attention(q, gn, ck, cv, win_k, win_v, ks_new, vs_new, kw_new, vw_new,
                         pool_k, pool_v, page_table, *, tq=SUBLANES, n_seq=SAMPLE_SEQS_PER_STEP):
    DB, DS, _ = q.shape
    n_pages = page_table.shape[1]
    past = n_pages * PAGE_SIZE
    n_chunks = ck.shape[1]
    wb = win_k.shape[2]
    n_slc = -(-(past + DS) // SLC_BLK)
    assert DS <= tq and DS < CMP_STRIDE and past // SLC_BLK <= LANES and wb % LANES == 0
    assert past % min(SAMPLE_KEY_CHUNK, past) == 0 and DB % n_seq == 0 and n_seq * N_KV_HEADS * tq <= LANES
    ncol = -(-n_slc // LANES) * LANES
    pad_t = lambda t: jnp.pad(t, ((0, 0), (0, tq - DS), (0, 0)))
    kb = np.arange(past)[None, :] // SLC_BLK
    ebig = jnp.asarray(np.where(kb == np.arange(LANES)[:, None], NEG_INF, 0.0), jnp.bfloat16)
    wcst = jnp.asarray(cmp_to_slc_matrix(n_chunks, ncol).T, jnp.bfloat16)
    gexp = gate_expand_matrix()
    per_b = lambda b, pt: (b, 0, 0)
    const = lambda b, pt: (0, 0)
    blk3 = lambda n, w: pl.BlockSpec((n_seq, n, w), per_b)
    grid_spec = pltpu.PrefetchScalarGridSpec(
        num_scalar_prefetch=1,
        grid=(DB // n_seq,),
        in_specs=[
            blk3(tq, NSA_WIDTH), blk3(tq, LANES), blk3(n_chunks, KV_WIDTH), blk3(n_chunks, KV_WIDTH),
            blk3(KV_WIDTH, wb), blk3(KV_WIDTH, wb),
            blk3(tq, KV_WIDTH), blk3(tq, KV_WIDTH), blk3(tq, KV_WIDTH), blk3(tq, KV_WIDTH),
            pl.BlockSpec(memory_space=pl.ANY), pl.BlockSpec(memory_space=pl.ANY),
            pl.BlockSpec(ebig.shape, const, pipeline_mode=pl.Buffered(1)),
            pl.BlockSpec(wcst.shape, const, pipeline_mode=pl.Buffered(1)),
            pl.BlockSpec(gexp.shape, const, pipeline_mode=pl.Buffered(1)),
        ],
        out_specs=[blk3(tq, NSA_WIDTH), blk3(KV_WIDTH, wb), blk3(KV_WIDTH, wb)],
        scratch_shapes=[
            pltpu.VMEM((2, n_seq, 2, KV_WIDTH, past), jnp.float32),
            pltpu.SemaphoreType.DMA((2, 2)),
        ],
    )
    o, nwk, nwv = pl.pallas_call(
        functools.partial(_nsa_sample_kernel, n_pages=n_pages, n_new=DS, tq=tq, n_seq=n_seq),
        grid_spec=grid_spec,
        out_shape=[jax.ShapeDtypeStruct((DB, tq, NSA_WIDTH), jnp.float32),
                   jax.ShapeDtypeStruct((DB, KV_WIDTH, wb), jnp.float32),
                   jax.ShapeDtypeStruct((DB, KV_WIDTH, wb), jnp.float32)],
        compiler_params=pltpu.CompilerParams(
            dimension_semantics=("arbitrary",), vmem_limit_bytes=PAGED_VMEM_LIMIT_BYTES),
        name="nsa_sample",
    )(page_table, pad_t(q), pad_t(gn), ck, cv, win_k, win_v, pad_t(ks_new), pad_t(vs_new), pad_t(kw_new),
      pad_t(vw_new), pool_k, pool_v, ebig, wcst, gexp)
    return o[:, :DS], nwk, nwv


def rope_tables(pos):
    half = HEAD_DIM // 2
    inv_freq = ROPE_THETA ** (-jnp.arange(half, dtype=jnp.float32) / half)
    ang = pos.astype(jnp.float32)[:, None] * inv_freq[None, :]
    cos, sin = jnp.cos(ang), jnp.sin(ang)
    cos_t = jnp.tile(cos, (1, LANES // half))
    sin_t = jnp.tile(jnp.concatenate([-sin, sin], axis=1), (1, LANES // HEAD_DIM))
    return cos_t, sin_t


def _rope_lanes(y, cos, sin):
    first_half = (_iota((y.shape[0], LANES), 1) & (HEAD_DIM - 1)) < HEAD_DIM // 2
    cols = []
    for c in range(y.shape[1] // LANES):
        yc = y[:, c * LANES:(c + 1) * LANES]
        rot = jnp.where(first_half, pltpu.roll(yc, LANES - HEAD_DIM // 2, axis=1),
                        pltpu.roll(yc, HEAD_DIM // 2, axis=1))
        cols.append(yc * cos + rot * sin)
    return cols[0] if len(cols) == 1 else jnp.concatenate(cols, axis=1)


def _rope_sublanes(y, cos_t, sin_t):
    hh = HEAD_DIM // 2
    out = []
    for c in range(y.shape[0] // LANES):
        yc = y[c * LANES:(c + 1) * LANES]
        rot = jnp.concatenate([yc[hh:2 * hh], yc[:hh], yc[3 * hh:], yc[2 * hh:3 * hh]], axis=0)
        out.append(yc * cos_t + rot * sin_t)
    return out[0] if len(out) == 1 else jnp.concatenate(out, axis=0)


_REST_ACTS = {"u": jax.nn.gelu, "v": jax.nn.gelu, "gn": jax.nn.sigmoid}


def _proj_rest(h, wrest_ref, refs, names):
    c0 = 0
    for ref, name in zip(refs, names):
        width = ref.shape[-1]
        act = _REST_ACTS.get(name)
        for c in range(0, width, GMLP_WIDTH):
            w = min(GMLP_WIDTH, width - c)
            y = _dot(h, wrest_ref[:, c0 + c:c0 + c + w])
            ref[:, c:c + w] = y if act is None else act(y)
        c0 += width


def _proj_rows_kernel(x_ref, g_ref, wq_ref, wkv_ref, wrest_ref, cos_ref, sin_ref, q_ref, *out_refs):
    h = _rms(x_ref[...], g_ref[...]).astype(jnp.bfloat16)
    cos, sin = cos_ref[...], sin_ref[...]
    q_ref[...] = _rope_lanes(_dot(h, wq_ref[...]), cos, sin)
    y = _dot(h, wkv_ref[...])
    for gi in range(N_KV_GROUPS):
        blk = y[:, gi * KV_WIDTH:(gi + 1) * KV_WIDTH]
        out_refs[gi][...] = _rope_lanes(blk, cos, sin) if gi % 2 == 0 else blk
    _proj_rest(h, wrest_ref, out_refs[N_KV_GROUPS:], ("gn", "u", "v", "ga", "gb"))


def _proj_seq_kernel(x_ref, g_ref, wq_ref, wkvt_ref, wrest_ref, cos_ref, sin_ref, cost_ref, sint_ref,
                     q_ref, kct_ref, vct_ref, kst_ref, vst_ref, kwt_ref, vwt_ref,
                     kst_b_ref, vst_b_ref, kwt_b_ref, vwt_b_ref, gn_ref, u_ref, v_ref, ga_ref, gb_ref):
    h = _rms(x_ref[...], g_ref[...]).astype(jnp.bfloat16)
    q_ref[...] = _rope_lanes(_dot(h, wq_ref[...]), cos_ref[...], sin_ref[...]) * (ATTN_SCALE * LOG2E)
    cos_t, sin_t = cost_ref[...], sint_ref[...]
    yt = _dot_nt(wkvt_ref[...], h)
    kv_refs = (kct_ref, vct_ref, kst_ref, vst_ref, kwt_ref, vwt_ref)
    kvb_refs = (None, None, kst_b_ref, vst_b_ref, kwt_b_ref, vwt_b_ref)
    for gi in range(N_KV_GROUPS):
        blk = yt[gi * KV_WIDTH:(gi + 1) * KV_WIDTH]
        if gi % 2 == 0:
            blk = _rope_sublanes(blk, cos_t, sin_t)
        kv_refs[gi][0] = blk
        if kvb_refs[gi] is not None:
            kvb_refs[gi][0] = blk.astype(jnp.bfloat16)
    _proj_rest(h, wrest_ref, (gn_ref, u_ref, v_ref, ga_ref, gb_ref), ("gn", "u", "v", "ga", "gb"))


def _split_w_in(w_in):
    sizes = (NSA_WIDTH,) + (KV_WIDTH,) * N_KV_GROUPS + (N_HEADS * N_BRANCH,)
    offs = np.cumsum((0,) + sizes)
    wq = w_in[:, offs[0]:offs[1]]
    wkv = w_in[:, offs[1]:offs[7]]
    wgn = w_in[:, offs[7]:offs[8]]
    wrest = w_in[:, offs[8]:]
    return wq, wkv, wgn, wrest


def mixer_proj_rows(x, g, w_in, pos):
    M = x.shape[0]
    tm = min(ROW_TILE, M)
    assert M % tm == 0
    wq, wkv, wgn, wrest = _split_w_in(w_in)
    wrest = jnp.concatenate([jnp.pad(wgn, ((0, 0), (0, LANES - wgn.shape[1]))), wrest], axis=1)
    cos, sin = rope_tables(pos)
    b16 = lambda t: t.astype(jnp.bfloat16)
    row = lambda i: (i, 0)
    cw = lambda shape: pl.BlockSpec(shape, lambda i: (0, 0), pipeline_mode=pl.Buffered(1))
    widths = (NSA_WIDTH,) + (KV_WIDTH,) * N_KV_GROUPS + (LANES, GMLP_WIDTH, GMLP_WIDTH, D_MODEL, D_MODEL)
    return pl.pallas_call(
        _proj_rows_kernel,
        grid=(M // tm,),
        in_specs=[pl.BlockSpec((tm, D_MODEL), row), cw((1, D_MODEL)), cw(wq.shape), cw(wkv.shape),
                  cw(wrest.shape), pl.BlockSpec((tm, LANES), row), pl.BlockSpec((tm, LANES), row)],
        out_specs=[pl.BlockSpec((tm, w), row) for w in widths],
        out_shape=[jax.ShapeDtypeStruct((M, w), jnp.float32) for w in widths],
        compiler_params=pltpu.CompilerParams(
            dimension_semantics=("arbitrary",), vmem_limit_bytes=VMEM_LIMIT_BYTES),
        name="mixer_proj_rows",
    )(x, g.reshape(1, -1), b16(wq), b16(wkv), b16(wrest), cos, sin)


def mixer_proj_seq(x, g, w_in, batch):
    M = x.shape[0]
    S = M // batch
    tm = min(ROW_TILE, S)
    assert S % tm == 0
    n_s = S // tm
    wq, wkv, wgn, wrest = _split_w_in(w_in)
    wrest = jnp.concatenate([jnp.pad(wgn, ((0, 0), (0, LANES - wgn.shape[1]))), wrest], axis=1)
    cos, sin = rope_tables(jnp.arange(S, dtype=jnp.int32))
    b16 = lambda t: t.astype(jnp.bfloat16)
    f32 = jnp.float32
    row = lambda i: (i, 0)
    tab = lambda i: (i % n_s, 0)
    tab_t = lambda i: (0, i % n_s)
    seq_t = lambda i: (i // n_s, 0, i % n_s)
    cw = lambda shape: pl.BlockSpec(shape, lambda i: (0, 0), pipeline_mode=pl.Buffered(1))
    t_spec = pl.BlockSpec((1, KV_WIDTH, tm), seq_t)
    t_shape = lambda dt: jax.ShapeDtypeStruct((batch, KV_WIDTH, S), dt)
    r_widths = (LANES, GMLP_WIDTH, GMLP_WIDTH, D_MODEL, D_MODEL)
    out_specs = ([pl.BlockSpec((tm, NSA_WIDTH), row)] + [t_spec] * (N_KV_GROUPS + 4)
                 + [pl.BlockSpec((tm, w), row) for w in r_widths])
    out_shape = ([jax.ShapeDtypeStruct((M, NSA_WIDTH), f32)] + [t_shape(f32)] * N_KV_GROUPS
                 + [t_shape(jnp.bfloat16)] * 4 + [jax.ShapeDtypeStruct((M, w), f32) for w in r_widths])
    return pl.pallas_call(
        _proj_seq_kernel,
        grid=(M // tm,),
        in_specs=[pl.BlockSpec((tm, D_MODEL), row), cw((1, D_MODEL)), cw(wq.shape),
                  cw((N_KV_GROUPS * KV_WIDTH, D_MODEL)), cw(wrest.shape),
                  pl.BlockSpec((tm, LANES), tab), pl.BlockSpec((tm, LANES), tab),
                  pl.BlockSpec((LANES, tm), tab_t), pl.BlockSpec((LANES, tm), tab_t)],
        out_specs=out_specs,
        out_shape=out_shape,
        compiler_params=pltpu.CompilerParams(
            dimension_semantics=("arbitrary",), vmem_limit_bytes=VMEM_LIMIT_BYTES),
        name="mixer_proj_seq",
    )(x, g.reshape(1, -1), b16(wq), b16(wkv.T), b16(wrest), cos, sin, cos.T, sin.T)


def _merge_kernel(oa_ref, u_ref, v_ref, ga_ref, gb_ref, x_ref, lng_ref, lnb_ref, wsp_ref, bsx_ref,
                  wa_ref, wb_ref, wout_ref, gc_ref, wcq_ref, x1_ref, hq_ref):
    tm = x_ref.shape[0]
    v = v_ref[...]
    vc = v - jnp.mean(v, axis=-1, keepdims=True)
    var = jnp.mean(vc * vc, axis=-1, keepdims=True)
    vn = vc * lax.rsqrt(var + NORM_EPS) * lng_ref[...] + lnb_ref[...]
    lane = _iota((GMLP_CHUNK, LANES), 1)
    chunks = []
    for c in range(tm // GMLP_CHUNK):
        cols = []
        for pr in range(GMLP_GROUPS // 2):
            pair = vn[c * GMLP_CHUNK:(c + 1) * GMLP_CHUNK, pr * LANES:(pr + 1) * LANES]
            rhs = jnp.concatenate([jnp.where(lane < GMLP_GROUP_DIM, pair, 0.0),
                                   jnp.where(lane >= GMLP_GROUP_DIM, pair, 0.0)], axis=0).astype(jnp.bfloat16)
            cols.append(_dot(wsp_ref[pr], rhs))
        chunks.append(jnp.concatenate(cols, axis=1) + bsx_ref[...])
    o_b = u_ref[...] * jnp.concatenate(chunks, axis=0)
    m = (jax.nn.sigmoid(ga_ref[...]) * _dot(oa_ref[...].astype(jnp.bfloat16), wa_ref[...])
         + jax.nn.sigmoid(gb_ref[...]) * _dot(o_b.astype(jnp.bfloat16), wb_ref[...]))
    x1 = x_ref[...] + _dot(m.astype(jnp.bfloat16), wout_ref[...])
    x1_ref[...] = x1
    hq_ref[...] = _dot(_rms(x1, gc_ref[...]).astype(jnp.bfloat16), wcq_ref[...])


def gmlp_weights(ws, bs, chunk_len):
    L = chunk_len
    w = jnp.tril(ws[:, :L, :L])
    reps = GMLP_CHUNK // L
    w = jnp.einsum('gij,ab->gaibj', w, jnp.eye(reps, dtype=w.dtype)).reshape(GMLP_GROUPS, GMLP_CHUNK, GMLP_CHUNK)
    pairs = w.reshape(GMLP_GROUPS // 2, 2, GMLP_CHUNK, GMLP_CHUNK)
    wsp = jnp.swapaxes(pairs, 1, 2).reshape(GMLP_GROUPS // 2, GMLP_CHUNK, 2 * GMLP_CHUNK)
    bsx = jnp.tile(jnp.repeat(bs[:, :L].T, GMLP_GROUP_DIM, axis=1), (reps, 1))
    return wsp.astype(jnp.bfloat16), bsx


def merge_and_cross_q(o_a, u, v, ga, gb, x, ln_g, ln_b, wsp, bsx, w_a, w_b, w_out, g_cross, w_cq):
    M = x.shape[0]
    tm = min(ROW_TILE, M)
    one = pl.Buffered(1)
    row = lambda i: (i, 0)
    cw = lambda shape: pl.BlockSpec(shape, lambda i: (0,) * len(shape), pipeline_mode=one)
    b16 = lambda t: t.astype(jnp.bfloat16)
    out = jax.ShapeDtypeStruct((M, D_MODEL), jnp.float32)
    return pl.pallas_call(
        _merge_kernel,
        grid=(M // tm,),
        in_specs=[
            pl.BlockSpec((tm, NSA_WIDTH), row), pl.BlockSpec((tm, GMLP_WIDTH), row),
            pl.BlockSpec((tm, GMLP_WIDTH), row), pl.BlockSpec((tm, D_MODEL), row),
            pl.BlockSpec((tm, D_MODEL), row), pl.BlockSpec((tm, D_MODEL), row),
            cw((1, GMLP_WIDTH)), cw((1, GMLP_WIDTH)), cw(wsp.shape), cw(bsx.shape),
            cw(w_a.shape), cw(w_b.shape), cw(w_out.shape), cw((1, D_MODEL)), cw(w_cq.shape),
        ],
        out_specs=[pl.BlockSpec((tm, D_MODEL), row)] * 2,
        out_shape=[out, out],
        compiler_params=pltpu.CompilerParams(
            dimension_semantics=("arbitrary",), vmem_limit_bytes=VMEM_LIMIT_BYTES),
        name="merge_branches",
    )(o_a, u, v, ga, gb, x, ln_g.reshape(1, -1), ln_b.reshape(1, -1), wsp, bsx,
      b16(w_a), b16(w_b), b16(w_out), g_cross.reshape(1, -1), b16(w_cq))


def _memkv_kernel(x_ref, g_ref, wk_ref, wv_ref, k_ref, v_ref):
    h = _rms(x_ref[...], g_ref[...]).astype(jnp.bfloat16)
    k_ref[...] = _dot(h, wk_ref[...])
    v_ref[...] = _dot(h, wv_ref[...])


def memory_kv(mem, g_mem, w_ck, w_cv):
    M = mem.shape[0]
    tm = min(ROW_TILE, M)
    one = pl.Buffered(1)
    row = lambda i: (i, 0)
    cw = lambda shape: pl.BlockSpec(shape, lambda i: (0, 0), pipeline_mode=one)
    out = jax.ShapeDtypeStruct((M, D_MODEL), jnp.float32)
    return pl.pallas_call(
        _memkv_kernel,
        grid=(M // tm,),
        in_specs=[pl.BlockSpec((tm, D_MODEL), row), cw((1, D_MODEL)), cw(w_ck.shape), cw(w_cv.shape)],
        out_specs=[pl.BlockSpec((tm, D_MODEL), row)] * 2,
        out_shape=[out, out],
        compiler_params=pltpu.CompilerParams(
            dimension_semantics=("arbitrary",), vmem_limit_bytes=VMEM_LIMIT_BYTES),
        name="memory_kv",
    )(mem, g_mem.reshape(1, -1), w_ck.astype(jnp.bfloat16), w_cv.astype(jnp.bfloat16))


def _cross_kernel(q_ref, mk_ref, mv_ref, o_ref):
    q = q_ref[0] * CROSS_SCALE
    outs = []
    for h in range(CROSS_HEADS):
        sl = slice(h * CROSS_HEAD_DIM, (h + 1) * CROSS_HEAD_DIM)
        s = _dot_nt(q[:, sl].astype(jnp.bfloat16), mk_ref[0, :, sl].astype(jnp.bfloat16))
        e = jnp.exp(s - jnp.max(s, axis=1, keepdims=True))
        p = e / jnp.sum(e, axis=1, keepdims=True)
        outs.append(_dot(p.astype(jnp.bfloat16), mv_ref[0, :, sl].astype(jnp.bfloat16)))
    o_ref[0] = jnp.concatenate(outs, axis=1)


def cross_attention(hq, mk, mv, *, rows):
    NB, R, _ = hq.shape
    return pl.pallas_call(
        _cross_kernel,
        grid=(NB, R // rows),
        in_specs=[pl.BlockSpec((1, rows, D_MODEL), lambda b, i: (b, i, 0)),
                  pl.BlockSpec((1, N_MEM, D_MODEL), lambda b, i: (b, 0, 0)),
                  pl.BlockSpec((1, N_MEM, D_MODEL), lambda b, i: (b, 0, 0))],
        out_specs=pl.BlockSpec((1, rows, D_MODEL), lambda b, i: (b, i, 0)),
        out_shape=jax.ShapeDtypeStruct((NB, R, D_MODEL), jnp.float32),
        compiler_params=pltpu.CompilerParams(
            dimension_semantics=("arbitrary", "arbitrary"), vmem_limit_bytes=VMEM_LIMIT_BYTES),
        name="cross_attention",
    )(hq, mk, mv)


def _cross_heads_kernel(q_ref, mk_hbm, mv_hbm, o_ref, buf, sem, *, rows, n_seq):
    b = pl.program_id(0)
    nb = pl.num_programs(0)
    slot = b % 2

    def head_copies(bb, sl):
        return [pltpu.make_async_copy(src.at[bb * n_seq + si, :, h, :], buf.at[sl, si, kv, h], sem.at[sl, kv])
                for si in range(n_seq) for kv, src in enumerate((mk_hbm, mv_hbm)) for h in range(CROSS_HEADS)]

    @pl.when(b == 0)
    def _():
        for c in head_copies(0, 0):
            c.start()

    @pl.when(b + 1 < nb)
    def _():
        for c in head_copies(b + 1, 1 - slot):
            c.start()

    pad = jnp.zeros((SUBLANES - rows, D_MODEL), jnp.float32)
    qs = [jnp.concatenate([q_ref[si] * CROSS_SCALE, pad], axis=0) for si in range(n_seq)]
    for c in head_copies(b, slot):
        c.wait()
    for si in range(n_seq):
        qh = [qs[si][:, h * CROSS_HEAD_DIM:(h + 1) * CROSS_HEAD_DIM].astype(jnp.bfloat16) for h in range(CROSS_HEADS)]
        s = jnp.concatenate([_dot_nt(qh[h], buf[slot, si, 0, h].astype(jnp.bfloat16)) for h in range(CROSS_HEADS)], axis=0)
        e = jnp.exp(s - jnp.max(s, axis=1, keepdims=True))
        p = (e / jnp.sum(e, axis=1, keepdims=True)).astype(jnp.bfloat16)
        o = jnp.concatenate([_dot(p[h * SUBLANES:(h + 1) * SUBLANES], buf[slot, si, 1, h].astype(jnp.bfloat16))
                             for h in range(CROSS_HEADS)], axis=1)
        o_ref[si] = o[:rows]


def cross_attention_cached(hq, mk, mv, *, n_seq=CROSS_SEQS_PER_STEP):
    NB, R, _ = hq.shape
    assert NB % n_seq == 0
    any_spec = pl.BlockSpec(memory_space=pl.ANY)
    return pl.pallas_call(
        functools.partial(_cross_heads_kernel, rows=R, n_seq=n_seq),
        grid=(NB // n_seq,),
        in_specs=[pl.BlockSpec((n_seq, R, D_MODEL), lambda b: (b, 0, 0)), any_spec, any_spec],
        out_specs=pl.BlockSpec((n_seq, R, D_MODEL), lambda b: (b, 0, 0)),
        out_shape=jax.ShapeDtypeStruct((NB, R, D_MODEL), jnp.float32),
        scratch_shapes=[pltpu.VMEM((2, n_seq, 2, CROSS_HEADS, N_MEM, CROSS_HEAD_DIM), jnp.float32),
                        pltpu.SemaphoreType.DMA((2, 2))],
        compiler_params=pltpu.CompilerParams(
            dimension_semantics=("arbitrary",), vmem_limit_bytes=VMEM_LIMIT_BYTES),
        name="cross_attention_cached",
    )(hq, mk, mv)


def _tail_kernel(x_ref, o_ref, wco_ref, gf_ref, wg_ref, wu_ref, wd_ref, gfin_ref, y_ref, acc_ref):
    x = x_ref[...] + _dot(o_ref[...].astype(jnp.bfloat16), wco_ref[...])
    h = _rms(x, gf_ref[...]).astype(jnp.bfloat16)
    acc_ref[...] = x
    for c in range(D_FF // FF_CHUNK):
        sl = slice(c * FF_CHUNK, (c + 1) * FF_CHUNK)
        a = _dot(h, wg_ref[:, sl])
        b = _dot(h, wu_ref[:, sl])
        t = (a * jax.nn.sigmoid(a) * b).astype(jnp.bfloat16)
        acc_ref[...] += _dot(t, wd_ref[sl, :])
    y_ref[...] = _rms(acc_ref[...], gfin_ref[...])


def tail(x1, o, w_co, g_ffn, w_gate, w_up, w_down, g_final):
    M = x1.shape[0]
    tm = min(ROW_TILE, M)
    row = lambda i: (i, 0)
    cw = lambda shape: pl.BlockSpec(shape, lambda i: (0, 0), pipeline_mode=pl.Buffered(1))
    b16 = lambda t: t.astype(jnp.bfloat16)
    return pl.pallas_call(
        _tail_kernel,
        grid=(M // tm,),
        in_specs=[pl.BlockSpec((tm, D_MODEL), row), pl.BlockSpec((tm, D_MODEL), row),
                  cw(w_co.shape), cw((1, D_MODEL)), cw(w_gate.shape), cw(w_up.shape), cw(w_down.shape),
                  cw((1, D_MODEL))],
        out_specs=pl.BlockSpec((tm, D_MODEL), row),
        out_shape=jax.ShapeDtypeStruct((M, D_MODEL), jnp.float32),
        scratch_shapes=[pltpu.VMEM((tm, D_MODEL), jnp.float32)],
        compiler_params=pltpu.CompilerParams(
            dimension_semantics=("arbitrary",), vmem_limit_bytes=VMEM_LIMIT_BYTES),
        name="tail",
    )(x1, o, b16(w_co), g_ffn.reshape(1, -1), b16(w_gate), b16(w_up), b16(w_down), g_final.reshape(1, -1))


def kernel(x_prompt, x_sample, mem_prompt, cache_cmp_k, cache_cmp_v, cache_slc_k, cache_slc_v,
           page_table, state_win_k, state_win_v, cache_mem_k, cache_mem_v,
           g_mix, w_in, cmp_pe, cmp_w1, cmp_w2, gmlp_ln_g, gmlp_ln_b, gmlp_ws, gmlp_bs,
           w_branch_a, w_branch_b, w_out, g_cross, g_mem, w_cq, w_ck, w_cv, w_co,
           g_ffn, w_gate, w_up, w_down, g_final):
    B, S, _ = x_prompt.shape
    DB, DS, _ = x_sample.shape
    n_pages = page_table.shape[1]
    assert g_mix.shape[0] == 1, "single-layer step"
    l = 0
    cmpw = cmp_weights(cmp_pe[l], cmp_w1[l], cmp_w2[l])
    merge_w = (gmlp_ln_g[l], gmlp_ln_b[l])
    merge_tail = (w_branch_a[l], w_branch_b[l], w_out[l], g_cross[l], w_cq[l])
    ffn_w = (w_co[l], g_ffn[l], w_gate[l], w_up[l], w_down[l], g_final)
    to_t = lambda t: jnp.transpose(t, (0, 2, 3, 1)).reshape(t.shape[0], KV_WIDTH, t.shape[1])
    from_t = lambda t: jnp.transpose(t.reshape(t.shape[0], N_KV_HEADS, HEAD_DIM, t.shape[2]), (0, 3, 1, 2))
    mem3 = lambda t: t.reshape(-1, N_MEM, D_MODEL)

    xp = x_prompt.reshape(B * S, D_MODEL)
    (q, kct, vct, kst, vst, kwt, vwt, kst_b, vst_b, kwt_b, vwt_b, gn, u, v, ga, gb) = mixer_proj_seq(
        xp, g_mix[l], w_in[l], B)
    ck, cv = cmp_blocks_dense(kct, vct, *cmpw)
    o_a = nsa_prompt_attention(q.reshape(B, S, NSA_WIDTH), gn.reshape(B, S, LANES), ck, cv,
                               kst_b, vst_b, kwt_b, vwt_b)
    x1, hq = merge_and_cross_q(o_a.reshape(B * S, NSA_WIDTH), u, v, ga, gb, xp, *merge_w,
                               *gmlp_weights(gmlp_ws[l], gmlp_bs[l], min(S, GMLP_CHUNK)), *merge_tail)
    mk, mv = memory_kv(mem_prompt.reshape(B * N_MEM, D_MODEL), g_mem[l], w_ck[l], w_cv[l])
    o = cross_attention(hq.reshape(B, S, D_MODEL), mem3(mk), mem3(mv), rows=min(S, ROW_TILE))
    y_prompt = tail(x1, o.reshape(B * S, D_MODEL), *ffn_w).reshape(B, S, D_MODEL)
    wb_p = min(WINDOW, S)
    mem5 = lambda t: t.reshape(B, N_MEM, CROSS_HEADS, CROSS_HEAD_DIM)
    prompt_new = (from_t(kct), from_t(vct), from_t(kst), from_t(vst),
                  from_t(kwt[:, :, S - wb_p:]), from_t(vwt[:, :, S - wb_p:]), mem5(mk), mem5(mv))

    xs = x_sample.reshape(DB * DS, D_MODEL)
    pos_s = n_pages * PAGE_SIZE + jnp.arange(DS, dtype=jnp.int32)
    (q, kc, vc, ks, vs, kw, vw, gn, u, v, ga, gb) = mixer_proj_rows(xs, g_mix[l], w_in[l], jnp.tile(pos_s, DB))
    ck, cv = cmp_blocks_paged(to_t(cache_cmp_k[l]), to_t(cache_cmp_v[l]), page_table, *cmpw)
    per_seq = lambda t: t.reshape(DB, DS, -1)
    o_a, win_kt, win_vt = nsa_sample_attention(
        per_seq(q), per_seq(gn), ck, cv, to_t(state_win_k[l]), to_t(state_win_v[l]),
        per_seq(ks), per_seq(vs), per_seq(kw), per_seq(vw),
        to_t(cache_slc_k[l]), to_t(cache_slc_v[l]), page_table)
    x1, hq = merge_and_cross_q(o_a.reshape(DB * DS, NSA_WIDTH), u, v, ga, gb, xs, *merge_w,
                               *gmlp_weights(gmlp_ws[l], gmlp_bs[l], min(DS, GMLP_CHUNK)), *merge_tail)
    o = cross_attention_cached(per_seq(hq), cache_mem_k[l], cache_mem_v[l])
    y_sample = tail(x1, o.reshape(DB * DS, D_MODEL), *ffn_w).reshape(DB, DS, D_MODEL)
    heads = lambda t: t.reshape(DB, DS, N_KV_HEADS, HEAD_DIM)
    sample_new = (heads(kc), heads(vc), heads(ks), heads(vs), from_t(win_kt), from_t(win_vt), per_seq(v))

    return (y_prompt, y_sample) + tuple(t[None] for t in prompt_new + sample_new)
```

```python
import functools

import jax
import jax.numpy as jnp
import numpy as np
from jax import lax
from jax.experimental import pallas as pl
from jax.experimental.pallas import tpu as pltpu

D_MODEL = 1024
N_HEADS = 8
HEAD_DIM = 64
N_KV_HEADS = 2
GROUP = N_HEADS // N_KV_HEADS
NSA_WIDTH = N_HEADS * HEAD_DIM
KV_WIDTH = N_KV_HEADS * HEAD_DIM
CMP_LEN = 32
CMP_STRIDE = 16
CMP_HIDDEN = 128
SLC_BLK = 64
SLC_SHIFT = 6
TOP_N = 16
WINDOW = 512
ROPE_THETA = 10000.0
ATTN_SCALE = HEAD_DIM ** -0.5
LOG2E = 1.4426950408889634
FORCE_BONUS = 1e4
NEG_INF = -1e30
N_BRANCH = 3
GMLP_WIDTH = 512
GMLP_GROUPS = 8
GMLP_GROUP_DIM = GMLP_WIDTH // GMLP_GROUPS
GMLP_CHUNK = 128
N_MEM = 256
CROSS_HEADS = 4
CROSS_HEAD_DIM = D_MODEL // CROSS_HEADS
CROSS_SCALE = CROSS_HEAD_DIM ** -0.5
D_FF = -(-8 * D_MODEL // (3 * 256)) * 256
NORM_EPS = 1e-6
PAGE_SIZE = 128
CHUNKS_PER_PAGE = PAGE_SIZE // CMP_STRIDE
N_KV_GROUPS = 6

LANES = 128
SUBLANES = 8
VMEM_LIMIT_BYTES = 48 * 1024 * 1024
PAGED_VMEM_LIMIT_BYTES = 56 * 1024 * 1024
SEL_TILE = 1024
SEL_TAIL_TILE = 512
SAMPLE_KEY_CHUNK = 2048
SAMPLE_SEQS_PER_STEP = 2
CROSS_SEQS_PER_STEP = 4
ROW_TILE = 512
FF_CHUNK = 256

_NT = (((1,), (1,)), ((), ()))


def _dot_nt(a, b):
    return lax.dot_general(a, b, _NT, preferred_element_type=jnp.float32)


def _dot(a, b):
    return jnp.dot(a, b, preferred_element_type=jnp.float32)


def _split_bf16(x):
    hi = x.astype(jnp.bfloat16)
    lo = (x - hi.astype(jnp.float32)).astype(jnp.bfloat16)
    return hi, lo


def _iota(shape, dim):
    return lax.broadcasted_iota(jnp.int32, shape, dim)


def _rms(x, g):
    return x * lax.rsqrt(jnp.mean(x * x, axis=-1, keepdims=True) + NORM_EPS) * g


def _cmp_from_transposed(xt_ref, p, xs, wbig_ref, w2big_ref, peterm_ref, out_ref, n_rows):
    n_chunks = n_rows // CMP_STRIDE
    pitch = xs.shape[0] // CMP_STRIDE
    per_blk = LANES // CMP_STRIDE
    for j in range(n_rows // LANES):
        blk = xt_ref[:, j * LANES:(j + 1) * LANES].T
        for k in range(LANES // SUBLANES):
            c, s0 = k // 2, (k % 2) * SUBLANES
            xs[pl.ds(s0 * pitch + j * per_blk + c, SUBLANES, stride=pitch), :] = blk[k * SUBLANES:(k + 1) * SUBLANES]
    acc = jnp.zeros((n_chunks, 4 * CMP_HIDDEN), jnp.float32)
    for sp in range(CMP_STRIDE // 2):
        a0 = xs[pl.ds(2 * sp * pitch, n_chunks), :]
        a1 = xs[pl.ds((2 * sp + 1) * pitch, n_chunks), :]
        lhs = jnp.concatenate([a0, a1], axis=1).astype(jnp.bfloat16)
        acc = acc + _dot(lhs, wbig_ref[p, sp])
    lead = acc[:, :2 * CMP_HIDDEN]
    trail_next = pltpu.roll(acc[:, 2 * CMP_HIDDEN:], n_chunks - 1, axis=0)
    pre = lead + trail_next + peterm_ref[p]
    out = _dot(jax.nn.gelu(pre).astype(jnp.bfloat16), w2big_ref[p])
    out_ref[0] = out.astype(out_ref.dtype)


def _regroup_scratch(n_rows):
    n_chunks = n_rows // CMP_STRIDE
    pitch = n_chunks + SUBLANES if n_chunks % 32 == 0 else n_chunks
    return pltpu.VMEM((CMP_STRIDE * pitch, KV_WIDTH), jnp.float32)


def _page_gather(pt_ref, pools, buf, sem, n_pages, n_seq=1):
    b = pl.program_id(0)
    nb = pl.num_programs(0)
    slot = b % 2

    def copies(bb, sl):
        return [pltpu.make_async_copy(pools[p].at[pt_ref[bb * n_seq + si, j]],
                                      buf.at[sl, si, p, :, pl.ds(j * PAGE_SIZE, PAGE_SIZE)], sem.at[sl, p])
                for si in range(n_seq) for p in range(len(pools)) for j in range(n_pages)]

    @pl.when(b == 0)
    def _():
        for c in copies(0, 0):
            c.start()

    @pl.when(b + 1 < nb)
    def _():
        for c in copies(b + 1, 1 - slot):
            c.start()

    def wait():
        for c in copies(b, slot):
            c.wait()

    return wait


def _cmp_paged_kernel(pt_ref, pool_k, pool_v, wbig_ref, w2big_ref, peterm_ref, ck_ref, cv_ref,
                      buf, xs, sem, *, n_pages):
    wait = _page_gather(pt_ref, (pool_k, pool_v), buf, sem, n_pages)
    wait()
    slot = pl.program_id(0) % 2
    for p, out_ref in ((0, ck_ref), (1, cv_ref)):
        _cmp_from_transposed(buf.at[slot, 0, p], p, xs, wbig_ref, w2big_ref, peterm_ref, out_ref,
                             n_pages * PAGE_SIZE)


def _cmp_dense_kernel(kt_ref, vt_ref, wbig_ref, w2big_ref, peterm_ref, ck_ref, cv_ref, xs):
    for p, (src, out_ref) in enumerate(((kt_ref, ck_ref), (vt_ref, cv_ref))):
        _cmp_from_transposed(src.at[0], p, xs, wbig_ref, w2big_ref, peterm_ref, out_ref, src.shape[2])


def cmp_weights(cmp_pe, cmp_w1, cmp_w2):
    eye = jnp.eye(N_KV_HEADS, dtype=jnp.float32)
    w1r = cmp_w1.reshape(2, 2, CMP_STRIDE // 2, 2, HEAD_DIM, CMP_HIDDEN)
    wbig = jnp.einsum('jtpsdf,hk->jpshdtkf', w1r, eye)
    wbig = wbig.reshape(2, CMP_STRIDE // 2, 2 * KV_WIDTH, 4 * CMP_HIDDEN).astype(jnp.bfloat16)
    w2big = jnp.einsum('jfd,hk->jhfkd', cmp_w2, eye).reshape(2, 2 * CMP_HIDDEN, KV_WIDTH)
    pe_term = jnp.einsum('jsd,jsdf->jf', cmp_pe, cmp_w1)
    peterm = jnp.tile(pe_term[:, None, :], (1, 1, N_KV_HEADS))
    return wbig, w2big.astype(jnp.bfloat16), peterm


def _cmp_weight_specs(wbig, w2big, peterm, nargs):
    c3 = (lambda b: (0, 0, 0)) if nargs == 1 else (lambda b, pt: (0, 0, 0))
    c4 = (lambda b: (0, 0, 0, 0)) if nargs == 1 else (lambda b, pt: (0, 0, 0, 0))
    one = pl.Buffered(1)
    return [pl.BlockSpec(wbig.shape, c4, pipeline_mode=one),
            pl.BlockSpec(w2big.shape, c3, pipeline_mode=one),
            pl.BlockSpec(peterm.shape, c3, pipeline_mode=one)]


def cmp_blocks_paged(pool_k, pool_v, page_table, wbig, w2big, peterm):
    n_seq, n_pages = page_table.shape
    n_rows = n_pages * PAGE_SIZE
    n_chunks = n_rows // CMP_STRIDE
    out = jax.ShapeDtypeStruct((n_seq, n_chunks, KV_WIDTH), jnp.bfloat16)
    grid_spec = pltpu.PrefetchScalarGridSpec(
        num_scalar_prefetch=1,
        grid=(n_seq,),
        in_specs=[pl.BlockSpec(memory_space=pl.ANY), pl.BlockSpec(memory_space=pl.ANY)]
        + _cmp_weight_specs(wbig, w2big, peterm, 2),
        out_specs=[pl.BlockSpec((1, n_chunks, KV_WIDTH), lambda b, pt: (b, 0, 0))] * 2,
        scratch_shapes=[
            pltpu.VMEM((2, 1, 2, KV_WIDTH, n_rows), jnp.float32),
            _regroup_scratch(n_rows),
            pltpu.SemaphoreType.DMA((2, 2)),
        ],
    )
    return pl.pallas_call(
        functools.partial(_cmp_paged_kernel, n_pages=n_pages),
        grid_spec=grid_spec,
        out_shape=[out, out],
        compiler_params=pltpu.CompilerParams(
            dimension_semantics=("arbitrary",), vmem_limit_bytes=PAGED_VMEM_LIMIT_BYTES),
        name="cmp_blocks_paged",
    )(page_table, pool_k, pool_v, wbig, w2big, peterm)


def cmp_blocks_dense(kt, vt, wbig, w2big, peterm):
    B, _, S = kt.shape
    n_chunks = S // CMP_STRIDE
    per_b = lambda b: (b, 0, 0)
    return pl.pallas_call(
        _cmp_dense_kernel,
        grid=(B,),
        in_specs=[pl.BlockSpec((1, KV_WIDTH, S), per_b), pl.BlockSpec((1, KV_WIDTH, S), per_b)]
        + _cmp_weight_specs(wbig, w2big, peterm, 1),
        out_specs=[pl.BlockSpec((1, n_chunks, KV_WIDTH), per_b)] * 2,
        out_shape=[jax.ShapeDtypeStruct((B, n_chunks, KV_WIDTH), jnp.bfloat16)] * 2,
        scratch_shapes=[_regroup_scratch(S)],
        compiler_params=pltpu.CompilerParams(
            dimension_semantics=("arbitrary",), vmem_limit_bytes=VMEM_LIMIT_BYTES),
        name="cmp_blocks_dense",
    )(kt, vt, wbig, w2big, peterm)


def cmp_to_slc_matrix(n_chunks, n_cols):
    cs = np.arange(n_chunks)[:, None] * CMP_STRIDE
    ss = np.arange(n_cols)[None, :] * SLC_BLK
    shared = np.minimum(cs + CMP_LEN, ss + SLC_BLK) - np.maximum(cs, ss)
    w = np.maximum(shared, 0).astype(np.float32) / CMP_LEN
    w[n_chunks - 1] = 0.0
    return w


def _nsa_prompt_kernel(q_ref, gn_ref, ck_ref, cv_ref, ks_ref, vs_ref, kw_ref, vw_ref,
                       wcst_ref, gexp_ref, o_ref, lhs_ref, m_ref, l_ref, acc_ref, *, tq, n_chunks):
    i = pl.program_id(1)
    q0 = i * tq
    qpb = _padded_queries(q_ref[0], tq).astype(jnp.bfloat16)

    def qpos_of(shape):
        return q0 + (_iota(shape, 0) & (tq - 1))

    s = _dot_nt(qpb, ck_ref[0])
    n_idx = _iota(s.shape, 1)
    valid = (n_idx * CMP_STRIDE + (CMP_LEN - 1) <= qpos_of(s.shape)) & (n_idx < n_chunks - 1)
    p_cmp = _masked_softmax2(s, valid)
    o_cmp = _dot(p_cmp.astype(jnp.bfloat16), cv_ref[0])

    blk = _iota((LANES, tq), 0)
    blkf = blk.astype(jnp.float32)
    qblk = (q0 + _iota((LANES, tq), 1)) >> SLC_SHIFT
    forced = (blk == 0) | (blk == qblk) | (blk == qblk - 1)
    wcst = wcst_ref[...]
    nsel = []
    for psum in _group_sum(p_cmp, tq):
        hi, lo = _split_bf16(psum)
        sc = _dot_nt(wcst, hi) + _dot_nt(wcst, lo)
        sc = jnp.where(blk <= qblk, jnp.where(forced, sc + FORCE_BONUS, sc), NEG_INF)
        sel = jnp.zeros_like(sc)
        for _ in range(TOP_N):
            mx = jnp.max(sc, axis=0, keepdims=True)
            first = jnp.min(jnp.where(sc == mx, blkf, float(LANES)), axis=0, keepdims=True)
            hit = blkf == first
            sel = jnp.where(hit, 1.0, sel)
            sc = jnp.where(hit, -jnp.inf, sc)
        nsel.append((1.0 - sel).T.astype(jnp.bfloat16))
    for h in range(N_KV_HEADS):
        r0 = h * GROUP * tq
        lhs_ref[r0:r0 + GROUP * tq, :LANES] = qpb[r0:r0 + GROUP * tq]
        for g in range(GROUP):
            lhs_ref[r0 + g * tq:r0 + (g + 1) * tq, LANES:] = nsel[h]

    m_ref[...] = jnp.full(m_ref.shape, -3e38, jnp.float32)
    l_ref[...] = jnp.zeros(l_ref.shape, jnp.float32)
    acc_ref[...] = jnp.zeros(acc_ref.shape, jnp.float32)

    def sel_tile(k0, size, causal):
        kblk = (k0 + _iota((LANES, size), 1)) >> SLC_SHIFT
        ebig = jnp.where(kblk == _iota((LANES, size), 0), NEG_INF, 0.0).astype(jnp.bfloat16)
        rhs = jnp.concatenate([ks_ref[0, :, pl.ds(k0, size)], ebig], axis=0)
        st = _dot(lhs_ref[...], rhs)
        if causal:
            st = jnp.where(k0 + _iota(st.shape, 1) <= qpos_of(st.shape), st, NEG_INF)
        cols = [st[:, c * LANES:(c + 1) * LANES] for c in range(size // LANES)]
        mt = functools.reduce(jnp.maximum, cols)
        m_prev = m_ref[...]
        m_new = jnp.maximum(m_prev, jnp.max(mt, axis=1, keepdims=True))
        alpha = jnp.exp2(m_prev - m_new)
        ps = [jnp.exp2(c - m_new) for c in cols]
        l_ref[...] = alpha * l_ref[...] + functools.reduce(jnp.add, ps)
        pb = jnp.concatenate(ps, axis=1).astype(jnp.bfloat16)
        acc_ref[...] = alpha * acc_ref[...] + _dot_nt(pb, vs_ref[0, :, pl.ds(k0, size)])
        m_ref[...] = m_new

    n_full = q0 // SEL_TILE
    tail0 = n_full * SEL_TILE
    n_tail = (q0 + tq - tail0 + SEL_TAIL_TILE - 1) // SEL_TAIL_TILE

    def full_body(kt, carry):
        sel_tile(pl.multiple_of(kt * SEL_TILE, SEL_TILE), SEL_TILE, False)
        return carry

    def tail_body(r, carry):
        sel_tile(pl.multiple_of(tail0 + r * SEL_TAIL_TILE, SEL_TAIL_TILE), SEL_TAIL_TILE, True)
        return carry

    lax.fori_loop(0, n_full, full_body, 0)
    lax.fori_loop(0, n_tail, tail_body, 0)
    o_sel = acc_ref[...] / jnp.sum(l_ref[...], axis=1, keepdims=True)

    w0 = pl.multiple_of(jnp.maximum(q0 - WINDOW, 0), tq)
    sw = _dot(qpb, kw_ref[0, :, pl.ds(w0, WINDOW + tq)])
    kpos = w0 + _iota(sw.shape, 1)
    qpos = qpos_of(sw.shape)
    p_win = _masked_softmax2(sw, (kpos <= qpos) & (kpos > qpos - WINDOW))
    o_win = _dot_nt(p_win.astype(jnp.bfloat16), vw_ref[0, :, pl.ds(w0, WINDOW + tq)])

    o_ref[0] = _combine((o_cmp, o_sel, o_win), gn_ref[0], gexp_ref[...], tq)


def nsa_prompt_attention(q, gn, ck, cv, ks, vs, kw, vw, *, tq=LANES):
    B, S, _ = q.shape
    n_chunks = ck.shape[1]
    assert S % SEL_TILE == 0 and SEL_TILE % SEL_TAIL_TILE == 0 and SEL_TAIL_TILE % tq == 0
    assert S // SLC_BLK <= LANES and tq == LANES and S >= WINDOW + tq
    wcst = jnp.asarray(cmp_to_slc_matrix(n_chunks, LANES).T, jnp.bfloat16)
    gexp = gate_expand_matrix()
    rows = N_HEADS * tq
    per_b = lambda b, i: (b, 0, 0)
    tile = lambda b, i: (b, i, 0)
    const = lambda b, i: (0, 0)
    return pl.pallas_call(
        functools.partial(_nsa_prompt_kernel, tq=tq, n_chunks=n_chunks),
        grid=(B, S // tq),
        in_specs=[
            pl.BlockSpec((1, tq, NSA_WIDTH), tile),
            pl.BlockSpec((1, tq, LANES), tile),
            pl.BlockSpec((1, n_chunks, KV_WIDTH), per_b),
            pl.BlockSpec((1, n_chunks, KV_WIDTH), per_b),
            pl.BlockSpec((1, KV_WIDTH, S), per_b),
            pl.BlockSpec((1, KV_WIDTH, S), per_b),
            pl.BlockSpec((1, KV_WIDTH, S), per_b),
            pl.BlockSpec((1, KV_WIDTH, S), per_b),
            pl.BlockSpec(wcst.shape, const),
            pl.BlockSpec(gexp.shape, const),
        ],
        out_specs=pl.BlockSpec((1, tq, NSA_WIDTH), tile),
        out_shape=jax.ShapeDtypeStruct((B, S, NSA_WIDTH), jnp.float32),
        scratch_shapes=[
            pltpu.VMEM((rows, 2 * LANES), jnp.bfloat16),
            pltpu.VMEM((rows, LANES), jnp.float32),
            pltpu.VMEM((rows, LANES), jnp.float32),
            pltpu.VMEM((rows, LANES), jnp.float32),
        ],
        compiler_params=pltpu.CompilerParams(
            dimension_semantics=("arbitrary", "arbitrary"), vmem_limit_bytes=VMEM_LIMIT_BYTES),
        name="nsa_prompt",
    )(q, gn, ck, cv, ks, vs, kw, vw, wcst, gexp)


def _padded_queries(q, tq):
    lane = _iota((tq, LANES), 1)
    blocks = []
    for h in range(N_KV_HEADS):
        for g in range(GROUP):
            c = h * GROUP + g
            pair = q[:, LANES * (c // 2):LANES * (c // 2 + 1)]
            if c % 2 != h:
                pair = pltpu.roll(pair, HEAD_DIM, axis=1)
            keep = (lane >= HEAD_DIM * h) & (lane < HEAD_DIM * (h + 1))
            blocks.append(jnp.where(keep, pair, 0.0))
    return jnp.concatenate(blocks, axis=0)


def _masked_softmax(s, valid):
    sm = jnp.where(valid, s, NEG_INF)
    mx = jnp.max(sm, axis=1, keepdims=True)
    e = jnp.where(valid, jnp.exp(sm - mx), 0.0)
    den = jnp.maximum(jnp.sum(e, axis=1, keepdims=True), 1e-30)
    return e / den


def _masked_softmax2(s2, valid):
    sm = jnp.where(valid, s2, NEG_INF)
    mx = jnp.max(sm, axis=1, keepdims=True)
    e = jnp.exp2(sm - mx)
    inv = jnp.where(mx > 0.5 * NEG_INF, 1.0 / jnp.sum(e, axis=1, keepdims=True), 0.0)
    return e * inv


def _group_sum(p, tq):
    out = []
    for h in range(N_KV_HEADS):
        r0 = h * GROUP * tq
        acc = p[r0:r0 + tq]
        for g in range(1, GROUP):
            acc = acc + p[r0 + g * tq:r0 + (g + 1) * tq]
        out.append(acc)
    return out


def _combine(branches, gn, gexp_w, tq):
    hi, lo = _split_bf16(gn)
    gexp = _dot(hi, gexp_w) + _dot(lo, gexp_w)
    lane = _iota((tq, LANES), 1)
    cols = []
    for h in range(N_KV_HEADS):
        blks = []
        for g in range(GROUP):
            j = h * GROUP + g
            rows = slice(j * tq, (j + 1) * tq)
            o = None
            for br in range(N_BRANCH):
                gate = gexp[:, (j * N_BRANCH + br) * LANES:(j * N_BRANCH + br + 1) * LANES]
                term = gate * branches[br][rows]
                o = term if o is None else o + term
            blks.append(o)
        for jj in range(GROUP // 2):
            a, b2 = blks[2 * jj], blks[2 * jj + 1]
            if h == 0:
                b2 = pltpu.roll(b2, HEAD_DIM, axis=1)
            else:
                a = pltpu.roll(a, HEAD_DIM, axis=1)
            cols.append(jnp.where(lane < HEAD_DIM, a, b2))
    return jnp.concatenate(cols, axis=1)


def gate_expand_matrix():
    e = np.zeros((LANES, N_HEADS * N_BRANCH * LANES), np.float32)
    for j in range(N_HEADS * N_BRANCH):
        e[j, j * LANES:(j + 1) * LANES] = 1.0
    return jnp.asarray(e, jnp.bfloat16)


def _nsa_sample_kernel(pt_ref, q_ref, gn_ref, ck_ref, cv_ref, wk_ref, wv_ref,
                       ksn_ref, vsn_ref, kwn_ref, vwn_ref, pool_k, pool_v, ebig_ref, wcst_ref, gexp_ref,
                       o_ref, nwk_ref, nwv_ref, buf, sem, *, n_pages, n_new, tq, n_seq):
    slot = pl.program_id(0) % 2
    past = n_pages * PAGE_SIZE
    n_chunks = n_pages * CHUNKS_PER_PAGE
    n_past_blk = past // SLC_BLK
    n_slc = -(-(past + n_new) // SLC_BLK)
    wb = wk_ref.shape[2]
    rows = N_HEADS * tq
    wait_pages = _page_gather(pt_ref, (pool_k, pool_v), buf, sem, n_pages, n_seq)

    def tok_of(shape):
        return _iota(shape, 0) & (tq - 1)

    def new_rows(ref, si):
        x = jnp.concatenate([ref[si], jnp.zeros((LANES - tq, KV_WIDTH), jnp.float32)], axis=0)
        return x.astype(jnp.bfloat16)

    def compressed(si):
        qpb = _padded_queries(q_ref[si] * ATTN_SCALE, tq).astype(jnp.bfloat16)
        s = _dot_nt(qpb, ck_ref[si])
        n_idx = _iota(s.shape, 1)
        valid = (n_idx * CMP_STRIDE + (CMP_LEN - 1) <= past + tok_of(s.shape)) & (n_idx < n_chunks - 1)
        p_cmp = _masked_softmax(s, valid)
        return qpb, p_cmp, _dot(p_cmp.astype(jnp.bfloat16), cv_ref[si])

    def select(p_cmps):
        ncol = wcst_ref.shape[0]
        groups = [g for p in p_cmps for g in _group_sum(p, tq)]
        psum = jnp.concatenate(groups + [jnp.zeros((LANES - len(groups) * tq, n_chunks), jnp.float32)], axis=0)
        hi, lo = _split_bf16(psum)
        wcst = wcst_ref[...]
        sc = _dot_nt(wcst, hi) + _dot_nt(wcst, lo)
        blk = _iota((ncol, LANES), 0)
        blkf = blk.astype(jnp.float32)
        qblk = (past + (_iota((ncol, LANES), 1) & (tq - 1))) >> SLC_SHIFT
        forced = (blk == 0) | (blk == qblk) | (blk == qblk - 1)
        sc = jnp.where(blk <= qblk, jnp.where(forced, sc + FORCE_BONUS, sc), NEG_INF)
        sc = jnp.where(blk < n_slc, sc, -jnp.inf)
        sel_t = jnp.zeros_like(sc)
        for _ in range(min(TOP_N, n_slc)):
            mx = jnp.max(sc, axis=0, keepdims=True)
            first = jnp.min(jnp.where(sc == mx, blkf, float(ncol)), axis=0, keepdims=True)
            hit = blkf == first
            sel_t = jnp.where(hit, 1.0, sel_t)
            sc = jnp.where(hit, -jnp.inf, sc)
        sel = sel_t.T
        return [[sel[(si * N_KV_HEADS + h) * tq:(si * N_KV_HEADS + h + 1) * tq] for h in range(N_KV_HEADS)]
                for si in range(n_seq)]

    def window_and_operands(si, qpb, sel):
        nsel = [jnp.concatenate([1.0 - sel[h][:, :LANES]] * GROUP, axis=0).astype(jnp.bfloat16) for h in range(N_KV_HEADS)]
        sel_new = [jnp.broadcast_to(sel[h][:, n_past_blk:n_past_blk + 1], (tq, LANES)) for h in range(N_KV_HEADS)]
        lhs = jnp.concatenate(
            [jnp.concatenate([qpb[h * GROUP * tq:(h + 1) * GROUP * tq], nsel[h]], axis=1)
             for h in range(N_KV_HEADS)], axis=0)
        sel_new_rows = jnp.concatenate([sel_new[h] for h in range(N_KV_HEADS) for _ in range(GROUP)], axis=0)

        tn = _iota((rows, LANES), 1)
        new_ok = (tn <= tok_of((rows, LANES))) & (tn < n_new)
        s_sn = jnp.where(new_ok & (sel_new_rows > 0.5), _dot_nt(qpb, new_rows(ksn_ref, si)), NEG_INF)
        s_wn = jnp.where(new_ok, _dot_nt(qpb, new_rows(kwn_ref, si)), NEG_INF)

        wk = wk_ref[si]
        wv = wv_ref[si]
        s_w = _dot(qpb, wk.astype(jnp.bfloat16))
        kwpos = past - wb + _iota(s_w.shape, 1)
        qpos = past + tok_of(s_w.shape)
        s_w = jnp.where((kwpos >= 0) & (kwpos <= qpos) & (kwpos > qpos - WINDOW), s_w, NEG_INF)
        mx = jnp.maximum(jnp.max(s_w, axis=1, keepdims=True), jnp.max(s_wn, axis=1, keepdims=True))
        e_w = jnp.exp(s_w - mx)
        e_wn = jnp.exp(s_wn - mx)
        den = jnp.sum(e_w, axis=1, keepdims=True) + jnp.sum(e_wn, axis=1, keepdims=True)
        o_win = (_dot_nt(e_w.astype(jnp.bfloat16), wv.astype(jnp.bfloat16))
                 + _dot(e_wn.astype(jnp.bfloat16), new_rows(vwn_ref, si))) / den

        col = _iota((KV_WIDTH, wb), 1)
        for src, new_ref, dst in ((wk, kwn_ref, nwk_ref), (wv, vwn_ref, nwv_ref)):
            new_t = jnp.concatenate([new_ref[si], jnp.zeros((LANES - tq, KV_WIDTH), jnp.float32)], axis=0).T
            tail = jnp.concatenate([jnp.zeros((KV_WIDTH, wb - LANES), jnp.float32), new_t], axis=1)
            dst[si] = jnp.where(col >= wb - n_new, pltpu.roll(tail, LANES - n_new, axis=1),
                                pltpu.roll(src, wb - n_new, axis=1))
        return lhs, s_sn, o_win

    def selected_and_combine(si, lhs, s_sn, o_cmp, o_win):
        ck_keys = min(SAMPLE_KEY_CHUNK, past)
        s_chunks = []
        for c in range(past // ck_keys):
            kc = buf[slot, si, 0, :, pl.ds(c * ck_keys, ck_keys)].astype(jnp.bfloat16)
            rhs = jnp.concatenate([kc, ebig_ref[:, pl.ds(c * ck_keys, ck_keys)]], axis=0)
            s_chunks.append(_dot(lhs, rhs))
        mx = jnp.max(s_sn, axis=1, keepdims=True)
        for sc in s_chunks:
            mx = jnp.maximum(mx, jnp.max(sc, axis=1, keepdims=True))
        e_sn = jnp.exp(s_sn - mx)
        den = jnp.sum(e_sn, axis=1, keepdims=True)
        o_sel = _dot(e_sn.astype(jnp.bfloat16), new_rows(vsn_ref, si))
        for c, sc in enumerate(s_chunks):
            e = jnp.exp(sc - mx)
            den = den + jnp.sum(e, axis=1, keepdims=True)
            vc = buf[slot, si, 1, :, pl.ds(c * ck_keys, ck_keys)].astype(jnp.bfloat16)
            o_sel = o_sel + _dot_nt(e.astype(jnp.bfloat16), vc)
        o_sel = o_sel / den

        o_ref[si] = _combine((o_cmp, o_sel, o_win), gn_ref[si], gexp_ref[...], tq)

    seqs = range(n_seq)
    qpb, p_cmp, o_cmp = zip(*[compressed(si) for si in seqs])
    sel = select(p_cmp)
    staged = [window_and_operands(si, qpb[si], sel[si]) for si in seqs]
    wait_pages()
    for si in seqs:
        lhs, s_sn, o_win = staged[si]
        selected_and_combine(si, lhs, s_sn, o_cmp[si], o_win)


def nsa_sample_attention(q, gn, ck, cv, win_k, win_v, ks_new, vs_new, kw_new, vw_new,
                         pool_k, pool_v, page_table, *, tq=SUBLANES, n_seq=SAMPLE_SEQS_PER_STEP):
    DB, DS, _ = q.shape
    n_pages = page_table.shape[1]
    past = n_pages * PAGE_SIZE
    n_chunks = ck.shape[1]
    wb = win_k.shape[2]
    n_slc = -(-(past + DS) // SLC_BLK)
    assert DS <= tq and DS < CMP_STRIDE and past // SLC_BLK <= LANES and wb % LANES == 0
    assert past % min(SAMPLE_KEY_CHUNK, past) == 0 and DB % n_seq == 0 and n_seq * N_KV_HEADS * tq <= LANES
    ncol = -(-n_slc // LANES) * LANES
    pad_t = lambda t: jnp.pad(t, ((0, 0), (0, tq - DS), (0, 0)))
    kb = np.arange(past)[None, :] // SLC_BLK
    ebig = jnp.asarray(np.where(kb == np.arange(LANES)[:, None], NEG_INF, 0.0), jnp.bfloat16)
    wcst = jnp.asarray(cmp_to_slc_matrix(n_chunks, ncol).T, jnp.bfloat16)
    gexp = gate_expand_matrix()
    per_b = lambda b, pt: (b, 0, 0)
    const = lambda b, pt: (0, 0)
    blk3 = lambda n, w: pl.BlockSpec((n_seq, n, w), per_b)
    grid_spec = pltpu.PrefetchScalarGridSpec(
        num_scalar_prefetch=1,
        grid=(DB // n_seq,),
        in_specs=[
            blk3(tq, NSA_WIDTH), blk3(tq, LANES), blk3(n_chunks, KV_WIDTH), blk3(n_chunks, KV_WIDTH),
            blk3(KV_WIDTH, wb), blk3(KV_WIDTH, wb),
            blk3(tq, KV_WIDTH), blk3(tq, KV_WIDTH), blk3(tq, KV_WIDTH), blk3(tq, KV_WIDTH),
            pl.BlockSpec(memory_space=pl.ANY), pl.BlockSpec(memory_space=pl.ANY),
            pl.BlockSpec(ebig.shape, const, pipeline_mode=pl.Buffered(1)),
            pl.BlockSpec(wcst.shape, const, pipeline_mode=pl.Buffered(1)),
            pl.BlockSpec(gexp.shape, const, pipeline_mode=pl.Buffered(1)),
        ],
        out_specs=[blk3(tq, NSA_WIDTH), blk3(KV_WIDTH, wb), blk3(KV_WIDTH, wb)],
        scratch_shapes=[
            pltpu.VMEM((2, n_seq, 2, KV_WIDTH, past), jnp.float32),
            pltpu.SemaphoreType.DMA((2, 2)),
        ],
    )
    o, nwk, nwv = pl.pallas_call(
        functools.partial(_nsa_sample_kernel, n_pages=n_pages, n_new=DS, tq=tq, n_seq=n_seq),
        grid_spec=grid_spec,
        out_shape=[jax.ShapeDtypeStruct((DB, tq, NSA_WIDTH), jnp.float32),
                   jax.ShapeDtypeStruct((DB, KV_WIDTH, wb), jnp.float32),
                   jax.ShapeDtypeStruct((DB, KV_WIDTH, wb), jnp.float32)],
        compiler_params=pltpu.CompilerParams(
            dimension_semantics=("arbitrary",), vmem_limit_bytes=PAGED_VMEM_LIMIT_BYTES),
        name="nsa_sample",
    )(page_table, pad_t(q), pad_t(gn), ck, cv, win_k, win_v, pad_t(ks_new), pad_t(vs_new), pad_t(kw_new),
      pad_t(vw_new), pool_k, pool_v, ebig, wcst, gexp)
    return o[:, :DS], nwk, nwv


def rope_tables(pos):
    half = HEAD_DIM // 2
    inv_freq = ROPE_THETA ** (-jnp.arange(half, dtype=jnp.float32) / half)
    ang = pos.astype(jnp.float32)[:, None] * inv_freq[None, :]
    cos, sin = jnp.cos(ang), jnp.sin(ang)
    cos_t = jnp.tile(cos, (1, LANES // half))
    sin_t = jnp.tile(jnp.concatenate([-sin, sin], axis=1), (1, LANES // HEAD_DIM))
    return cos_t, sin_t


def _rope_lanes(y, cos, sin):
    first_half = (_iota((y.shape[0], LANES), 1) & (HEAD_DIM - 1)) < HEAD_DIM // 2
    cols = []
    for c in range(y.shape[1] // LANES):
        yc = y[:, c * LANES:(c + 1) * LANES]
        rot = jnp.where(first_half, pltpu.roll(yc, LANES - HEAD_DIM // 2, axis=1),
                        pltpu.roll(yc, HEAD_DIM // 2, axis=1))
        cols.append(yc * cos + rot * sin)
    return cols[0] if len(cols) == 1 else jnp.concatenate(cols, axis=1)


def _rope_sublanes(y, cos_t, sin_t):
    hh = HEAD_DIM // 2
    out = []
    for c in range(y.shape[0] // LANES):
        yc = y[c * LANES:(c + 1) * LANES]
        rot = jnp.concatenate([yc[hh:2 * hh], yc[:hh], yc[3 * hh:], yc[2 * hh:3 * hh]], axis=0)
        out.append(yc * cos_t + rot * sin_t)
    return out[0] if len(out) == 1 else jnp.concatenate(out, axis=0)


_REST_ACTS = {"u": jax.nn.gelu, "v": jax.nn.gelu, "gn": jax.nn.sigmoid}


def _proj_rest(h, wrest_ref, refs, names):
    c0 = 0
    for ref, name in zip(refs, names):
        width = ref.shape[-1]
        act = _REST_ACTS.get(name)
        for c in range(0, width, GMLP_WIDTH):
            w = min(GMLP_WIDTH, width - c)
            y = _dot(h, wrest_ref[:, c0 + c:c0 + c + w])
            ref[:, c:c + w] = y if act is None else act(y)
        c0 += width


def _proj_rows_kernel(x_ref, g_ref, wq_ref, wkv_ref, wrest_ref, cos_ref, sin_ref, q_ref, *out_refs):
    h = _rms(x_ref[...], g_ref[...]).astype(jnp.bfloat16)
    cos, sin = cos_ref[...], sin_ref[...]
    q_ref[...] = _rope_lanes(_dot(h, wq_ref[...]), cos, sin)
    y = _dot(h, wkv_ref[...])
    for gi in range(N_KV_GROUPS):
        blk = y[:, gi * KV_WIDTH:(gi + 1) * KV_WIDTH]
        out_refs[gi][...] = _rope_lanes(blk, cos, sin) if gi % 2 == 0 else blk
    _proj_rest(h, wrest_ref, out_refs[N_KV_GROUPS:], ("gn", "u", "v", "ga", "gb"))


def _proj_seq_kernel(x_ref, g_ref, wq_ref, wkvt_ref, wrest_ref, cos_ref, sin_ref, cost_ref, sint_ref,
                     q_ref, kct_ref, vct_ref, kst_ref, vst_ref, kwt_ref, vwt_ref,
                     kst_b_ref, vst_b_ref, kwt_b_ref, vwt_b_ref, gn_ref, u_ref, v_ref, ga_ref, gb_ref):
    h = _rms(x_ref[...], g_ref[...]).astype(jnp.bfloat16)
    q_ref[...] = _rope_lanes(_dot(h, wq_ref[...]), cos_ref[...], sin_ref[...]) * (ATTN_SCALE * LOG2E)
    cos_t, sin_t = cost_ref[...], sint_ref[...]
    yt = _dot_nt(wkvt_ref[...], h)
    kv_refs = (kct_ref, vct_ref, kst_ref, vst_ref, kwt_ref, vwt_ref)
    kvb_refs = (None, None, kst_b_ref, vst_b_ref, kwt_b_ref, vwt_b_ref)
    for gi in range(N_KV_GROUPS):
        blk = yt[gi * KV_WIDTH:(gi + 1) * KV_WIDTH]
        if gi % 2 == 0:
            blk = _rope_sublanes(blk, cos_t, sin_t)
        kv_refs[gi][0] = blk
        if kvb_refs[gi] is not None:
            kvb_refs[gi][0] = blk.astype(jnp.bfloat16)
    _proj_rest(h, wrest_ref, (gn_ref, u_ref, v_ref, ga_ref, gb_ref), ("gn", "u", "v", "ga", "gb"))


def _split_w_in(w_in):
    sizes = (NSA_WIDTH,) + (KV_WIDTH,) * N_KV_GROUPS + (N_HEADS * N_BRANCH,)
    offs = np.cumsum((0,) + sizes)
    wq = w_in[:, offs[0]:offs[1]]
    wkv = w_in[:, offs[1]:offs[7]]
    wgn = w_in[:, offs[7]:offs[8]]
    wrest = w_in[:, offs[8]:]
    return wq, wkv, wgn, wrest


def mixer_proj_rows(x, g, w_in, pos):
    M = x.shape[0]
    tm = min(ROW_TILE, M)
    assert M % tm == 0
    wq, wkv, wgn, wrest = _split_w_in(w_in)
    wrest = jnp.concatenate([jnp.pad(wgn, ((0, 0), (0, LANES - wgn.shape[1]))), wrest], axis=1)
    cos, sin = rope_tables(pos)
    b16 = lambda t: t.astype(jnp.bfloat16)
    row = lambda i: (i, 0)
    cw = lambda shape: pl.BlockSpec(shape, lambda i: (0, 0), pipeline_mode=pl.Buffered(1))
    widths = (NSA_WIDTH,) + (KV_WIDTH,) * N_KV_GROUPS + (LANES, GMLP_WIDTH, GMLP_WIDTH, D_MODEL, D_MODEL)
    return pl.pallas_call(
        _proj_rows_kernel,
        grid=(M // tm,),
        in_specs=[pl.BlockSpec((tm, D_MODEL), row), cw((1, D_MODEL)), cw(wq.shape), cw(wkv.shape),
                  cw(wrest.shape), pl.BlockSpec((tm, LANES), row), pl.BlockSpec((tm, LANES), row)],
        out_specs=[pl.BlockSpec((tm, w), row) for w in widths],
        out_shape=[jax.ShapeDtypeStruct((M, w), jnp.float32) for w in widths],
        compiler_params=pltpu.CompilerParams(
            dimension_semantics=("arbitrary",), vmem_limit_bytes=VMEM_LIMIT_BYTES),
        name="mixer_proj_rows",
    )(x, g.reshape(1, -1), b16(wq), b16(wkv), b16(wrest), cos, sin)


def mixer_proj_seq(x, g, w_in, batch):
    M = x.shape[0]
    S = M // batch
    tm = min(ROW_TILE, S)
    assert S % tm == 0
    n_s = S // tm
    wq, wkv, wgn, wrest = _split_w_in(w_in)
    wrest = jnp.concatenate([jnp.pad(wgn, ((0, 0), (0, LANES - wgn.shape[1]))), wrest], axis=1)
    cos, sin = rope_tables(jnp.arange(S, dtype=jnp.int32))
    b16 = lambda t: t.astype(jnp.bfloat16)
    f32 = jnp.float32
    row = lambda i: (i, 0)
    tab = lambda i: (i % n_s, 0)
    tab_t = lambda i: (0, i % n_s)
    seq_t = lambda i: (i // n_s, 0, i % n_s)
    cw = lambda shape: pl.BlockSpec(shape, lambda i: (0, 0), pipeline_mode=pl.Buffered(1))
    t_spec = pl.BlockSpec((1, KV_WIDTH, tm), seq_t)
    t_shape = lambda dt: jax.ShapeDtypeStruct((batch, KV_WIDTH, S), dt)
    r_widths = (LANES, GMLP_WIDTH, GMLP_WIDTH, D_MODEL, D_MODEL)
    out_specs = ([pl.BlockSpec((tm, NSA_WIDTH), row)] + [t_spec] * (N_KV_GROUPS + 4)
                 + [pl.BlockSpec((tm, w), row) for w in r_widths])
    out_shape = ([jax.ShapeDtypeStruct((M, NSA_WIDTH), f32)] + [t_shape(f32)] * N_KV_GROUPS
                 + [t_shape(jnp.bfloat16)] * 4 + [jax.ShapeDtypeStruct((M, w), f32) for w in r_widths])
    return pl.pallas_call(
        _proj_seq_kernel,
        grid=(M // tm,),
        in_specs=[pl.BlockSpec((tm, D_MODEL), row), cw((1, D_MODEL)), cw(wq.shape),
                  cw((N_KV_GROUPS * KV_WIDTH, D_MODEL)), cw(wrest.shape),
                  pl.BlockSpec((tm, LANES), tab), pl.BlockSpec((tm, LANES), tab),
                  pl.BlockSpec((LANES, tm), tab_t), pl.BlockSpec((LANES, tm), tab_t)],
        out_specs=out_specs,
        out_shape=out_shape,
        compiler_params=pltpu.CompilerParams(
            dimension_semantics=("arbitrary",), vmem_limit_bytes=VMEM_LIMIT_BYTES),
        name="mixer_proj_seq",
    )(x, g.reshape(1, -1), b16(wq), b16(wkv.T), b16(wrest), cos, sin, cos.T, sin.T)


def _merge_kernel(oa_ref, u_ref, v_ref, ga_ref, gb_ref, x_ref, lng_ref, lnb_ref, wsp_ref, bsx_ref,
                  wa_ref, wb_ref, wout_ref, gc_ref, wcq_ref, x1_ref, hq_ref):
    tm = x_ref.shape[0]
    v = v_ref[...]
    vc = v - jnp.mean(v, axis=-1, keepdims=True)
    var = jnp.mean(vc * vc, axis=-1, keepdims=True)
    vn = vc * lax.rsqrt(var + NORM_EPS) * lng_ref[...] + lnb_ref[...]
    lane = _iota((GMLP_CHUNK, LANES), 1)
    chunks = []
    for c in range(tm // GMLP_CHUNK):
        cols = []
        for pr in range(GMLP_GROUPS // 2):
            pair = vn[c * GMLP_CHUNK:(c + 1) * GMLP_CHUNK, pr * LANES:(pr + 1) * LANES]
            rhs = jnp.concatenate([jnp.where(lane < GMLP_GROUP_DIM, pair, 0.0),
                                   jnp.where(lane >= GMLP_GROUP_DIM, pair, 0.0)], axis=0).astype(jnp.bfloat16)
            cols.append(_dot(wsp_ref[pr], rhs))
        chunks.append(jnp.concatenate(cols, axis=1) + bsx_ref[...])
    o_b = u_ref[...] * jnp.concatenate(chunks, axis=0)
    m = (jax.nn.sigmoid(ga_ref[...]) * _dot(oa_ref[...].astype(jnp.bfloat16), wa_ref[...])
         + jax.nn.sigmoid(gb_ref[...]) * _dot(o_b.astype(jnp.bfloat16), wb_ref[...]))
    x1 = x_ref[...] + _dot(m.astype(jnp.bfloat16), wout_ref[...])
    x1_ref[...] = x1
    hq_ref[...] = _dot(_rms(x1, gc_ref[...]).astype(jnp.bfloat16), wcq_ref[...])


def gmlp_weights(ws, bs, chunk_len):
    L = chunk_len
    w = jnp.tril(ws[:, :L, :L])
    reps = GMLP_CHUNK // L
    w = jnp.einsum('gij,ab->gaibj', w, jnp.eye(reps, dtype=w.dtype)).reshape(GMLP_GROUPS, GMLP_CHUNK, GMLP_CHUNK)
    pairs = w.reshape(GMLP_GROUPS // 2, 2, GMLP_CHUNK, GMLP_CHUNK)
    wsp = jnp.swapaxes(pairs, 1, 2).reshape(GMLP_GROUPS // 2, GMLP_CHUNK, 2 * GMLP_CHUNK)
    bsx = jnp.tile(jnp.repeat(bs[:, :L].T, GMLP_GROUP_DIM, axis=1), (reps, 1))
    return wsp.astype(jnp.bfloat16), bsx


def merge_and_cross_q(o_a, u, v, ga, gb, x, ln_g, ln_b, wsp, bsx, w_a, w_b, w_out, g_cross, w_cq):
    M = x.shape[0]
    tm = min(ROW_TILE, M)
    one = pl.Buffered(1)
    row = lambda i: (i, 0)
    cw = lambda shape: pl.BlockSpec(shape, lambda i: (0,) * len(shape), pipeline_mode=one)
    b16 = lambda t: t.astype(jnp.bfloat16)
    out = jax.ShapeDtypeStruct((M, D_MODEL), jnp.float32)
    return pl.pallas_call(
        _merge_kernel,
        grid=(M // tm,),
        in_specs=[
            pl.BlockSpec((tm, NSA_WIDTH), row), pl.BlockSpec((tm, GMLP_WIDTH), row),
            pl.BlockSpec((tm, GMLP_WIDTH), row), pl.BlockSpec((tm, D_MODEL), row),
            pl.BlockSpec((tm, D_MODEL), row), pl.BlockSpec((tm, D_MODEL), row),
            cw((1, GMLP_WIDTH)), cw((1, GMLP_WIDTH)), cw(wsp.shape), cw(bsx.shape),
            cw(w_a.shape), cw(w_b.shape), cw(w_out.shape), cw((1, D_MODEL)), cw(w_cq.shape),
        ],
        out_specs=[pl.BlockSpec((tm, D_MODEL), row)] * 2,
        out_shape=[out, out],
        compiler_params=pltpu.CompilerParams(
            dimension_semantics=("arbitrary",), vmem_limit_bytes=VMEM_LIMIT_BYTES),
        name="merge_branches",
    )(o_a, u, v, ga, gb, x, ln_g.reshape(1, -1), ln_b.reshape(1, -1), wsp, bsx,
      b16(w_a), b16(w_b), b16(w_out), g_cross.reshape(1, -1), b16(w_cq))


def _memkv_kernel(x_ref, g_ref, wk_ref, wv_ref, k_ref, v_ref):
    h = _rms(x_ref[...], g_ref[...]).astype(jnp.bfloat16)
    k_ref[...] = _dot(h, wk_ref[...])
    v_ref[...] = _dot(h, wv_ref[...])


def memory_kv(mem, g_mem, w_ck, w_cv):
    M = mem.shape[0]
    tm = min(ROW_TILE, M)
    one = pl.Buffered(1)
    row = lambda i: (i, 0)
    cw = lambda shape: pl.BlockSpec(shape, lambda i: (0, 0), pipeline_mode=one)
    out = jax.ShapeDtypeStruct((M, D_MODEL), jnp.float32)
    return pl.pallas_call(
        _memkv_kernel,
        grid=(M // tm,),
        in_specs=[pl.BlockSpec((tm, D_MODEL), row), cw((1, D_MODEL)), cw(w_ck.shape), cw(w_cv.shape)],
        out_specs=[pl.BlockSpec((tm, D_MODEL), row)] * 2,
        out_shape=[out, out],
        compiler_params=pltpu.CompilerParams(
            dimension_semantics=("arbitrary",), vmem_limit_bytes=VMEM_LIMIT_BYTES),
        name="memory_kv",
    )(mem, g_mem.reshape(1, -1), w_ck.astype(jnp.bfloat16), w_cv.astype(jnp.bfloat16))


def _cross_kernel(q_ref, mk_ref, mv_ref, o_ref):
    q = q_ref[0] * CROSS_SCALE
    outs = []
    for h in range(CROSS_HEADS):
        sl = slice(h * CROSS_HEAD_DIM, (h + 1) * CROSS_HEAD_DIM)
        s = _dot_nt(q[:, sl].astype(jnp.bfloat16), mk_ref[0, :, sl].astype(jnp.bfloat16))
        e = jnp.exp(s - jnp.max(s, axis=1, keepdims=True))
        p = e / jnp.sum(e, axis=1, keepdims=True)
        outs.append(_dot(p.astype(jnp.bfloat16), mv_ref[0, :, sl].astype(jnp.bfloat16)))
    o_ref[0] = jnp.concatenate(outs, axis=1)


def cross_attention(hq, mk, mv, *, rows):
    NB, R, _ = hq.shape
    return pl.pallas_call(
        _cross_kernel,
        grid=(NB, R // rows),
        in_specs=[pl.BlockSpec((1, rows, D_MODEL), lambda b, i: (b, i, 0)),
                  pl.BlockSpec((1, N_MEM, D_MODEL), lambda b, i: (b, 0, 0)),
                  pl.BlockSpec((1, N_MEM, D_MODEL), lambda b, i: (b, 0, 0))],
        out_specs=pl.BlockSpec((1, rows, D_MODEL), lambda b, i: (b, i, 0)),
        out_shape=jax.ShapeDtypeStruct((NB, R, D_MODEL), jnp.float32),
        compiler_params=pltpu.CompilerParams(
            dimension_semantics=("arbitrary", "arbitrary"), vmem_limit_bytes=VMEM_LIMIT_BYTES),
        name="cross_attention",
    )(hq, mk, mv)


def _cross_heads_kernel(q_ref, mk_hbm, mv_hbm, o_ref, buf, sem, *, rows, n_seq):
    b = pl.program_id(0)
    nb = pl.num_programs(0)
    slot = b % 2

    def head_copies(bb, sl):
        return [pltpu.make_async_copy(src.at[bb * n_seq + si, :, h, :], buf.at[sl, si, kv, h], sem.at[sl, kv])
                for si in range(n_seq) for kv, src in enumerate((mk_hbm, mv_hbm)) for h in range(CROSS_HEADS)]

    @pl.when(b == 0)
    def _():
        for c in head_copies(0, 0):
            c.start()

    @pl.when(b + 1 < nb)
    def _():
        for c in head_copies(b + 1, 1 - slot):
            c.start()

    pad = jnp.zeros((SUBLANES - rows, D_MODEL), jnp.float32)
    qs = [jnp.concatenate([q_ref[si] * CROSS_SCALE, pad], axis=0) for si in range(n_seq)]
    for c in head_copies(b, slot):
        c.wait()
    for si in range(n_seq):
        qh = [qs[si][:, h * CROSS_HEAD_DIM:(h + 1) * CROSS_HEAD_DIM].astype(jnp.bfloat16) for h in range(CROSS_HEADS)]
        s = jnp.concatenate([_dot_nt(qh[h], buf[slot, si, 0, h].astype(jnp.bfloat16)) for h in range(CROSS_HEADS)], axis=0)
        e = jnp.exp(s - jnp.max(s, axis=1, keepdims=True))
        p = (e / jnp.sum(e, axis=1, keepdims=True)).astype(jnp.bfloat16)
        o = jnp.concatenate([_dot(p[h * SUBLANES:(h + 1) * SUBLANES], buf[slot, si, 1, h].astype(jnp.bfloat16))
                             for h in range(CROSS_HEADS)], axis=1)
        o_ref[si] = o[:rows]


def cross_attention_cached(hq, mk, mv, *, n_seq=CROSS_SEQS_PER_STEP):
    NB, R, _ = hq.shape
    assert NB % n_seq == 0
    any_spec = pl.BlockSpec(memory_space=pl.ANY)
    return pl.pallas_call(
        functools.partial(_cross_heads_kernel, rows=R, n_seq=n_seq),
        grid=(NB // n_seq,),
        in_specs=[pl.BlockSpec((n_seq, R, D_MODEL), lambda b: (b, 0, 0)), any_spec, any_spec],
        out_specs=pl.BlockSpec((n_seq, R, D_MODEL), lambda b: (b, 0, 0)),
        out_shape=jax.ShapeDtypeStruct((NB, R, D_MODEL), jnp.float32),
        scratch_shapes=[pltpu.VMEM((2, n_seq, 2, CROSS_HEADS, N_MEM, CROSS_HEAD_DIM), jnp.float32),
                        pltpu.SemaphoreType.DMA((2, 2))],
        compiler_params=pltpu.CompilerParams(
            dimension_semantics=("arbitrary",), vmem_limit_bytes=VMEM_LIMIT_BYTES),
        name="cross_attention_cached",
    )(hq, mk, mv)


def _tail_kernel(x_ref, o_ref, wco_ref, gf_ref, wg_ref, wu_ref, wd_ref, gfin_ref, y_ref, acc_ref):
    x = x_ref[...] + _dot(o_ref[...].astype(jnp.bfloat16), wco_ref[...])
    h = _rms(x, gf_ref[...]).astype(jnp.bfloat16)
    acc_ref[...] = x
    for c in range(D_FF // FF_CHUNK):
        sl = slice(c * FF_CHUNK, (c + 1) * FF_CHUNK)
        a = _dot(h, wg_ref[:, sl])
        b = _dot(h, wu_ref[:, sl])
        t = (a * jax.nn.sigmoid(a) * b).astype(jnp.bfloat16)
        acc_ref[...] += _dot(t, wd_ref[sl, :])
    y_ref[...] = _rms(acc_ref[...], gfin_ref[...])


def tail(x1, o, w_co, g_ffn, w_gate, w_up, w_down, g_final):
    M = x1.shape[0]
    tm = min(ROW_TILE, M)
    row = lambda i: (i, 0)
    cw = lambda shape: pl.BlockSpec(shape, lambda i: (0, 0), pipeline_mode=pl.Buffered(1))
    b16 = lambda t: t.astype(jnp.bfloat16)
    return pl.pallas_call(
        _tail_kernel,
        grid=(M // tm,),
        in_specs=[pl.BlockSpec((tm, D_MODEL), row), pl.BlockSpec((tm, D_MODEL), row),
                  cw(w_co.shape), cw((1, D_MODEL)), cw(w_gate.shape), cw(w_up.shape), cw(w_down.shape),
                  cw((1, D_MODEL))],
        out_specs=pl.BlockSpec((tm, D_MODEL), row),
        out_shape=jax.ShapeDtypeStruct((M, D_MODEL), jnp.float32),
        scratch_shapes=[pltpu.VMEM((tm, D_MODEL), jnp.float32)],
        compiler_params=pltpu.CompilerParams(
            dimension_semantics=("arbitrary",), vmem_limit_bytes=VMEM_LIMIT_BYTES),
        name="tail",
    )(x1, o, b16(w_co), g_ffn.reshape(1, -1), b16(w_gate), b16(w_up), b16(w_down), g_final.reshape(1, -1))


def kernel(x_prompt, x_sample, mem_prompt, cache_cmp_k, cache_cmp_v, cache_slc_k, cache_slc_v,
           page_table, state_win_k, state_win_v, cache_mem_k, cache_mem_v,
           g_mix, w_in, cmp_pe, cmp_w1, cmp_w2, gmlp_ln_g, gmlp_ln_b, gmlp_ws, gmlp_bs,
           w_branch_a, w_branch_b, w_out, g_cross, g_mem, w_cq, w_ck, w_cv, w_co,
           g_ffn, w_gate, w_up, w_down, g_final):
    B, S, _ = x_prompt.shape
    DB, DS, _ = x_sample.shape
    n_pages = page_table.shape[1]
    assert g_mix.shape[0] == 1, "single-layer step"
    l = 0
    cmpw = cmp_weights(cmp_pe[l], cmp_w1[l], cmp_w2[l])
    merge_w = (gmlp_ln_g[l], gmlp_ln_b[l])
    merge_tail = (w_branch_a[l], w_branch_b[l], w_out[l], g_cross[l], w_cq[l])
    ffn_w = (w_co[l], g_ffn[l], w_gate[l], w_up[l], w_down[l], g_final)
    to_t = lambda t: jnp.transpose(t, (0, 2, 3, 1)).reshape(t.shape[0], KV_WIDTH, t.shape[1])
    from_t = lambda t: jnp.transpose(t.reshape(t.shape[0], N_KV_HEADS, HEAD_DIM, t.shape[2]), (0, 3, 1, 2))
    mem3 = lambda t: t.reshape(-1, N_MEM, D_MODEL)

    xp = x_prompt.reshape(B * S, D_MODEL)
    (q, kct, vct, kst, vst, kwt, vwt, kst_b, vst_b, kwt_b, vwt_b, gn, u, v, ga, gb) = mixer_proj_seq(
        xp, g_mix[l], w_in[l], B)
    ck, cv = cmp_blocks_dense(kct, vct, *cmpw)
    o_a = nsa_prompt_attention(q.reshape(B, S, NSA_WIDTH), gn.reshape(B, S, LANES), ck, cv,
                               kst_b, vst_b, kwt_b, vwt_b)
    x1, hq = merge_and_cross_q(o_a.reshape(B * S, NSA_WIDTH), u, v, ga, gb, xp, *merge_w,
                               *gmlp_weights(gmlp_ws[l], gmlp_bs[l], min(S, GMLP_CHUNK)), *merge_tail)
    mk, mv = memory_kv(mem_prompt.reshape(B * N_MEM, D_MODEL), g_mem[l], w_ck[l], w_cv[l])
    o = cross_attention(hq.reshape(B, S, D_MODEL), mem3(mk), mem3(mv), rows=min(S, ROW_TILE))
    y_prompt = tail(x1, o.reshape(B * S, D_MODEL), *ffn_w).reshape(B, S, D_MODEL)
    wb_p = min(WINDOW, S)
    mem5 = lambda t: t.reshape(B, N_MEM, CROSS_HEADS, CROSS_HEAD_DIM)
    prompt_new = (from_t(kct), from_t(vct), from_t(kst), from_t(vst),
                  from_t(kwt[:, :, S - wb_p:]), from_t(vwt[:, :, S - wb_p:]), mem5(mk), mem5(mv))

    xs = x_sample.reshape(DB * DS, D_MODEL)
    pos_s = n_pages * PAGE_SIZE + jnp.arange(DS, dtype=jnp.int32)
    (q, kc, vc, ks, vs, kw, vw, gn, u, v, ga, gb) = mixer_proj_rows(xs, g_mix[l], w_in[l], jnp.tile(pos_s, DB))
    ck, cv = cmp_blocks_paged(to_t(cache_cmp_k[l]), to_t(cache_cmp_v[l]), page_table, *cmpw)
    per_seq = lambda t: t.reshape(DB, DS, -1)
    o_a, win_kt, win_vt = nsa_sample_attention(
        per_seq(q), per_seq(gn), ck, cv, to_t(state_win_k[l]), to_t(state_win_v[l]),
        per_seq(ks), per_seq(vs), per_seq(kw), per_seq(vw),
        to_t(cache_slc_k[l]), to_t(cache_slc_v[l]), page_table)
    x1, hq = merge_and_cross_q(o_a.reshape(DB * DS, NSA_WIDTH), u, v, ga, gb, xs, *merge_w,
                               *gmlp_weights(gmlp_ws[l], gmlp_bs[l], min(DS, GMLP_CHUNK)), *merge_tail)
    o = cross_attention_cached(per_seq(hq), cache_mem_k[l], cache_mem_v[l])
    y_sample = tail(x1, o.reshape(DB * DS, D_MODEL), *ffn_w).reshape(DB, DS, D_MODEL)
    heads = lambda t: t.reshape(DB, DS, N_KV_HEADS, HEAD_DIM)
    sample_new = (heads(kc), heads(vc), heads(ks), heads(vs), from_t(win_kt), from_t(win_vt), per_seq(v))

    return (y_prompt, y_sample) + tuple(t[None] for t in prompt_new + sample_new)
```

```python
import functools

import jax
import jax.numpy as jnp
import numpy as np
from jax import lax
from jax.experimental import pallas as pl
from jax.experimental.pallas import tpu as pltpu

D_MODEL = 1024
N_HEADS = 8
HEAD_DIM = 64
N_KV_HEADS = 2
GROUP = N_HEADS // N_KV_HEADS
NSA_WIDTH = N_HEADS * HEAD_DIM
KV_WIDTH = N_KV_HEADS * HEAD_DIM
CMP_LEN = 32
CMP_STRIDE = 16
CMP_HIDDEN = 128
SLC_BLK = 64
SLC_SHIFT = 6
TOP_N = 16
WINDOW = 512
ROPE_THETA = 10000.0
ATTN_SCALE = HEAD_DIM ** -0.5
LOG2E = 1.4426950408889634
FORCE_BONUS = 1e4
NEG_INF = -1e30
N_BRANCH = 3
GMLP_WIDTH = 512
GMLP_GROUPS = 8
GMLP_GROUP_DIM = GMLP_WIDTH // GMLP_GROUPS
GMLP_CHUNK = 128
N_MEM = 256
CROSS_HEADS = 4
CROSS_HEAD_DIM = D_MODEL // CROSS_HEADS
CROSS_SCALE = CROSS_HEAD_DIM ** -0.5
D_FF = -(-8 * D_MODEL // (3 * 256)) * 256
NORM_EPS = 1e-6
PAGE_SIZE = 128
CHUNKS_PER_PAGE = PAGE_SIZE // CMP_STRIDE
N_KV_GROUPS = 6

LANES = 128
SUBLANES = 8
VMEM_LIMIT_BYTES = 48 * 1024 * 1024
PAGED_VMEM_LIMIT_BYTES = 56 * 1024 * 1024
SEL_TILE = 1024
SEL_TAIL_TILE = 512
SAMPLE_KEY_CHUNK = 2048
SAMPLE_SEQS_PER_STEP = 2
CROSS_SEQS_PER_STEP = 4
ROW_TILE = 512
FF_CHUNK = 256

_NT = (((1,), (1,)), ((), ()))


def _dot_nt(a, b):
    return lax.dot_general(a, b, _NT, preferred_element_type=jnp.float32)


def _dot(a, b):
    return jnp.dot(a, b, preferred_element_type=jnp.float32)


def _split_bf16(x):
    hi = x.astype(jnp.bfloat16)
    lo = (x - hi.astype(jnp.float32)).astype(jnp.bfloat16)
    return hi, lo


def _iota(shape, dim):
    return lax.broadcasted_iota(jnp.int32, shape, dim)


def _rms(x, g):
    return x * lax.rsqrt(jnp.mean(x * x, axis=-1, keepdims=True) + NORM_EPS) * g


def _cmp_from_transposed(xt_ref, p, xs, wbig_ref, w2big_ref, peterm_ref, out_ref, n_rows):
    n_chunks = n_rows // CMP_STRIDE
    pitch = xs.shape[0] // CMP_STRIDE
    per_blk = LANES // CMP_STRIDE
    for j in range(n_rows // LANES):
        blk = xt_ref[:, j * LANES:(j + 1) * LANES].T
        for k in range(LANES // SUBLANES):
            c, s0 = k // 2, (k % 2) * SUBLANES
            xs[pl.ds(s0 * pitch + j * per_blk + c, SUBLANES, stride=pitch), :] = blk[k * SUBLANES:(k + 1) * SUBLANES]
    acc = jnp.zeros((n_chunks, 4 * CMP_HIDDEN), jnp.float32)
    for sp in range(CMP_STRIDE // 2):
        a0 = xs[pl.ds(2 * sp * pitch, n_chunks), :]
        a1 = xs[pl.ds((2 * sp + 1) * pitch, n_chunks), :]
        lhs = jnp.concatenate([a0, a1], axis=1).astype(jnp.bfloat16)
        acc = acc + _dot(lhs, wbig_ref[p, sp])
    lead = acc[:, :2 * CMP_HIDDEN]
    trail_next = pltpu.roll(acc[:, 2 * CMP_HIDDEN:], n_chunks - 1, axis=0)
    pre = lead + trail_next + peterm_ref[p]
    out = _dot(jax.nn.gelu(pre).astype(jnp.bfloat16), w2big_ref[p])
    out_ref[0] = out.astype(out_ref.dtype)


def _regroup_scratch(n_rows):
    n_chunks = n_rows // CMP_STRIDE
    pitch = n_chunks + SUBLANES if n_chunks % 32 == 0 else n_chunks
    return pltpu.VMEM((CMP_STRIDE * pitch, KV_WIDTH), jnp.float32)


def _page_gather(pt_ref, pools, buf, sem, n_pages, n_seq=1):
    b = pl.program_id(0)
    nb = pl.num_programs(0)
    slot = b % 2

    def copies(bb, sl):
        return [pltpu.make_async_copy(pools[p].at[pt_ref[bb * n_seq + si, j]],
                                      buf.at[sl, si, p, :, pl.ds(j * PAGE_SIZE, PAGE_SIZE)], sem.at[sl, p])
                for si in range(n_seq) for p in range(len(pools)) for j in range(n_pages)]

    @pl.when(b == 0)
    def _():
        for c in copies(0, 0):
            c.start()

    @pl.when(b + 1 < nb)
    def _():
        for c in copies(b + 1, 1 - slot):
            c.start()

    def wait():
        for c in copies(b, slot):
            c.wait()

    return wait


def _cmp_paged_kernel(pt_ref, pool_k, pool_v, wbig_ref, w2big_ref, peterm_ref, ck_ref, cv_ref,
                      buf, xs, sem, *, n_pages):
    wait = _page_gather(pt_ref, (pool_k, pool_v), buf, sem, n_pages)
    wait()
    slot = pl.program_id(0) % 2
    for p, out_ref in ((0, ck_ref), (1, cv_ref)):
        _cmp_from_transposed(buf.at[slot, 0, p], p, xs, wbig_ref, w2big_ref, peterm_ref, out_ref,
                             n_pages * PAGE_SIZE)


def _cmp_dense_kernel(kt_ref, vt_ref, wbig_ref, w2big_ref, peterm_ref, ck_ref, cv_ref, xs):
    for p, (src, out_ref) in enumerate(((kt_ref, ck_ref), (vt_ref, cv_ref))):
        _cmp_from_transposed(src.at[0], p, xs, wbig_ref, w2big_ref, peterm_ref, out_ref, src.shape[2])


def cmp_weights(cmp_pe, cmp_w1, cmp_w2):
    eye = jnp.eye(N_KV_HEADS, dtype=jnp.float32)
    w1r = cmp_w1.reshape(2, 2, CMP_STRIDE // 2, 2, HEAD_DIM, CMP_HIDDEN)
    wbig = jnp.einsum('jtpsdf,hk->jpshdtkf', w1r, eye)
    wbig = wbig.reshape(2, CMP_STRIDE // 2, 2 * KV_WIDTH, 4 * CMP_HIDDEN).astype(jnp.bfloat16)
    w2big = jnp.einsum('jfd,hk->jhfkd', cmp_w2, eye).reshape(2, 2 * CMP_HIDDEN, KV_WIDTH)
    pe_term = jnp.einsum('jsd,jsdf->jf', cmp_pe, cmp_w1)
    peterm = jnp.tile(pe_term[:, None, :], (1, 1, N_KV_HEADS))
    return wbig, w2big.astype(jnp.bfloat16), peterm


def _cmp_weight_specs(wbig, w2big, peterm, nargs):
    c3 = (lambda b: (0, 0, 0)) if nargs == 1 else (lambda b, pt: (0, 0, 0))
    c4 = (lambda b: (0, 0, 0, 0)) if nargs == 1 else (lambda b, pt: (0, 0, 0, 0))
    one = pl.Buffered(1)
    return [pl.BlockSpec(wbig.shape, c4, pipeline_mode=one),
            pl.BlockSpec(w2big.shape, c3, pipeline_mode=one),
            pl.BlockSpec(peterm.shape, c3, pipeline_mode=one)]


def cmp_blocks_paged(pool_k, pool_v, page_table, wbig, w2big, peterm):
    n_seq, n_pages = page_table.shape
    n_rows = n_pages * PAGE_SIZE
    n_chunks = n_rows // CMP_STRIDE
    out = jax.ShapeDtypeStruct((n_seq, n_chunks, KV_WIDTH), jnp.bfloat16)
    grid_spec = pltpu.PrefetchScalarGridSpec(
        num_scalar_prefetch=1,
        grid=(n_seq,),
        in_specs=[pl.BlockSpec(memory_space=pl.ANY), pl.BlockSpec(memory_space=pl.ANY)]
        + _cmp_weight_specs(wbig, w2big, peterm, 2),
        out_specs=[pl.BlockSpec((1, n_chunks, KV_WIDTH), lambda b, pt: (b, 0, 0))] * 2,
        scratch_shapes=[
            pltpu.VMEM((2, 1, 2, KV_WIDTH, n_rows), jnp.float32),
            _regroup_scratch(n_rows),
            pltpu.SemaphoreType.DMA((2, 2)),
        ],
    )
    return pl.pallas_call(
        functools.partial(_cmp_paged_kernel, n_pages=n_pages),
        grid_spec=grid_spec,
        out_shape=[out, out],
        compiler_params=pltpu.CompilerParams(
            dimension_semantics=("arbitrary",), vmem_limit_bytes=PAGED_VMEM_LIMIT_BYTES),
        name="cmp_blocks_paged",
    )(page_table, pool_k, pool_v, wbig, w2big, peterm)


def cmp_blocks_dense(kt, vt, wbig, w2big, peterm):
    B, _, S = kt.shape
    n_chunks = S // CMP_STRIDE
    per_b = lambda b: (b, 0, 0)
    return pl.pallas_call(
        _cmp_dense_kernel,
        grid=(B,),
        in_specs=[pl.BlockSpec((1, KV_WIDTH, S), per_b), pl.BlockSpec((1, KV_WIDTH, S), per_b)]
        + _cmp_weight_specs(wbig, w2big, peterm, 1),
        out_specs=[pl.BlockSpec((1, n_chunks, KV_WIDTH), per_b)] * 2,
        out_shape=[jax.ShapeDtypeStruct((B, n_chunks, KV_WIDTH), jnp.bfloat16)] * 2,
        scratch_shapes=[_regroup_scratch(S)],
        compiler_params=pltpu.CompilerParams(
            dimension_semantics=("arbitrary",), vmem_limit_bytes=VMEM_LIMIT_BYTES),
        name="cmp_blocks_dense",
    )(kt, vt, wbig, w2big, peterm)


def cmp_to_slc_matrix(n_chunks, n_cols):
    cs = np.arange(n_chunks)[:, None] * CMP_STRIDE
    ss = np.arange(n_cols)[None, :] * SLC_BLK
    shared = np.minimum(cs + CMP_LEN, ss + SLC_BLK) - np.maximum(cs, ss)
    w = np.maximum(shared, 0).astype(np.float32) / CMP_LEN
    w[n_chunks - 1] = 0.0
    return w


def _nsa_prompt_kernel(q_ref, gn_ref, ck_ref, cv_ref, ks_ref, vs_ref, kw_ref, vw_ref,
                       wcst_ref, gexp_ref, o_ref, lhs_ref, m_ref, l_ref, acc_ref, *, tq, n_chunks):
    i = pl.program_id(1)
    q0 = i * tq
    qpb = _padded_queries(q_ref[0], tq).astype(jnp.bfloat16)

    def qpos_of(shape):
        return q0 + (_iota(shape, 0) & (tq - 1))

    s = _dot_nt(qpb, ck_ref[0])
    n_idx = _iota(s.shape, 1)
    valid = (n_idx * CMP_STRIDE + (CMP_LEN - 1) <= qpos_of(s.shape)) & (n_idx < n_chunks - 1)
    p_cmp = _masked_softmax2(s, valid)
    o_cmp = _dot(p_cmp.astype(jnp.bfloat16), cv_ref[0])

    blk = _iota((LANES, tq), 0)
    blkf = blk.astype(jnp.float32)
    qblk = (q0 + _iota((LANES, tq), 1)) >> SLC_SHIFT
    forced = (blk == 0) | (blk == qblk) | (blk == qblk - 1)
    wcst = wcst_ref[...]
    nsel = []
    for psum in _group_sum(p_cmp, tq):
        hi, lo = _split_bf16(psum)
        sc = _dot_nt(wcst, hi) + _dot_nt(wcst, lo)
        sc = jnp.where(blk <= qblk, jnp.where(forced, sc + FORCE_BONUS, sc), NEG_INF)
        sel = jnp.zeros_like(sc)
        for _ in range(TOP_N):
            mx = jnp.max(sc, axis=0, keepdims=True)
            first = jnp.min(jnp.where(sc == mx, blkf, float(LANES)), axis=0, keepdims=True)
            hit = blkf == first
            sel = jnp.where(hit, 1.0, sel)
            sc = jnp.where(hit, -jnp.inf, sc)
        nsel.append((1.0 - sel).T.astype(jnp.bfloat16))
    for h in range(N_KV_HEADS):
        r0 = h * GROUP * tq
        lhs_ref[r0:r0 + GROUP * tq, :LANES] = qpb[r0:r0 + GROUP * tq]
        for g in range(GROUP):
            lhs_ref[r0 + g * tq:r0 + (g + 1) * tq, LANES:] = nsel[h]

    m_ref[...] = jnp.full(m_ref.shape, -3e38, jnp.float32)
    l_ref[...] = jnp.zeros(l_ref.shape, jnp.float32)
    acc_ref[...] = jnp.zeros(acc_ref.shape, jnp.float32)

    def sel_tile(k0, size, causal):
        kblk = (k0 + _iota((LANES, size), 1)) >> SLC_SHIFT
        ebig = jnp.where(kblk == _iota((LANES, size), 0), NEG_INF, 0.0).astype(jnp.bfloat16)
        rhs = jnp.concatenate([ks_ref[0, :, pl.ds(k0, size)], ebig], axis=0)
        st = _dot(lhs_ref[...], rhs)
        if causal:
            st = jnp.where(k0 + _iota(st.shape, 1) <= qpos_of(st.shape), st, NEG_INF)
        cols = [st[:, c * LANES:(c + 1) * LANES] for c in range(size // LANES)]
        mt = functools.reduce(jnp.maximum, cols)
        m_prev = m_ref[...]
        m_new = jnp.maximum(m_prev, jnp.max(mt, axis=1, keepdims=True))
        alpha = jnp.exp2(m_prev - m_new)
        ps = [jnp.exp2(c - m_new) for c in cols]
        l_ref[...] = alpha * l_ref[...] + functools.reduce(jnp.add, ps)
        pb = jnp.concatenate(ps, axis=1).astype(jnp.bfloat16)
        acc_ref[...] = alpha * acc_ref[...] + _dot_nt(pb, vs_ref[0, :, pl.ds(k0, size)])
        m_ref[...] = m_new

    n_full = q0 // SEL_TILE
    tail0 = n_full * SEL_TILE
    n_tail = (q0 + tq - tail0 + SEL_TAIL_TILE - 1) // SEL_TAIL_TILE

    def full_body(kt, carry):
        sel_tile(pl.multiple_of(kt * SEL_TILE, SEL_TILE), SEL_TILE, False)
        return carry

    def tail_body(r, carry):
        sel_tile(pl.multiple_of(tail0 + r * SEL_TAIL_TILE, SEL_TAIL_TILE), SEL_TAIL_TILE, True)
        return carry

    lax.fori_loop(0, n_full, full_body, 0)
    lax.fori_loop(0, n_tail, tail_body, 0)
    o_sel = acc_ref[...] / jnp.sum(l_ref[...], axis=1, keepdims=True)

    w0 = pl.multiple_of(jnp.maximum(q0 - WINDOW, 0), tq)
    sw = _dot(qpb, kw_ref[0, :, pl.ds(w0, WINDOW + tq)])
    kpos = w0 + _iota(sw.shape, 1)
    qpos = qpos_of(sw.shape)
    p_win = _masked_softmax2(sw, (kpos <= qpos) & (kpos > qpos - WINDOW))
    o_win = _dot_nt(p_win.astype(jnp.bfloat16), vw_ref[0, :, pl.ds(w0, WINDOW + tq)])

    o_ref[0] = _combine((o_cmp, o_sel, o_win), gn_ref[0], gexp_ref[...], tq)


def nsa_prompt_attention(q, gn, ck, cv, ks, vs, kw, vw, *, tq=LANES):
    B, S, _ = q.shape
    n_chunks = ck.shape[1]
    assert S % SEL_TILE == 0 and SEL_TILE % SEL_TAIL_TILE == 0 and SEL_TAIL_TILE % tq == 0
    assert S // SLC_BLK <= LANES and tq == LANES and S >= WINDOW + tq
    wcst = jnp.asarray(cmp_to_slc_matrix(n_chunks, LANES).T, jnp.bfloat16)
    gexp = gate_expand_matrix()
    rows = N_HEADS * tq
    per_b = lambda b, i: (b, 0, 0)
    tile = lambda b, i: (b, i, 0)
    const = lambda b, i: (0, 0)
    return pl.pallas_call(
        functools.partial(_nsa_prompt_kernel, tq=tq, n_chunks=n_chunks),
        grid=(B, S // tq),
        in_specs=[
            pl.BlockSpec((1, tq, NSA_WIDTH), tile),
            pl.BlockSpec((1, tq, LANES), tile),
            pl.BlockSpec((1, n_chunks, KV_WIDTH), per_b),
            pl.BlockSpec((1, n_chunks, KV_WIDTH), per_b),
            pl.BlockSpec((1, KV_WIDTH, S), per_b),
            pl.BlockSpec((1, KV_WIDTH, S), per_b),
            pl.BlockSpec((1, KV_WIDTH, S), per_b),
            pl.BlockSpec((1, KV_WIDTH, S), per_b),
            pl.BlockSpec(wcst.shape, const),
            pl.BlockSpec(gexp.shape, const),
        ],
        out_specs=pl.BlockSpec((1, tq, NSA_WIDTH), tile),
        out_shape=jax.ShapeDtypeStruct((B, S, NSA_WIDTH), jnp.float32),
        scratch_shapes=[
            pltpu.VMEM((rows, 2 * LANES), jnp.bfloat16),
            pltpu.VMEM((rows, LANES), jnp.float32),
            pltpu.VMEM((rows, LANES), jnp.float32),
            pltpu.VMEM((rows, LANES), jnp.float32),
        ],
        compiler_params=pltpu.CompilerParams(
            dimension_semantics=("arbitrary", "arbitrary"), vmem_limit_bytes=VMEM_LIMIT_BYTES),
        name="nsa_prompt",
    )(q, gn, ck, cv, ks, vs, kw, vw, wcst, gexp)


def _padded_queries(q, tq):
    lane = _iota((tq, LANES), 1)
    blocks = []
    for h in range(N_KV_HEADS):
        for g in range(GROUP):
            c = h * GROUP + g
            pair = q[:, LANES * (c // 2):LANES * (c // 2 + 1)]
            if c % 2 != h:
                pair = pltpu.roll(pair, HEAD_DIM, axis=1)
            keep = (lane >= HEAD_DIM * h) & (lane < HEAD_DIM * (h + 1))
            blocks.append(jnp.where(keep, pair, 0.0))
    return jnp.concatenate(blocks, axis=0)


def _masked_softmax(s, valid):
    sm = jnp.where(valid, s, NEG_INF)
    mx = jnp.max(sm, axis=1, keepdims=True)
    e = jnp.where(valid, jnp.exp(sm - mx), 0.0)
    den = jnp.maximum(jnp.sum(e, axis=1, keepdims=True), 1e-30)
    return e / den


def _masked_softmax2(s2, valid):
    sm = jnp.where(valid, s2, NEG_INF)
    mx = jnp.max(sm, axis=1, keepdims=True)
    e = jnp.exp2(sm - mx)
    inv = jnp.where(mx > 0.5 * NEG_INF, 1.0 / jnp.sum(e, axis=1, keepdims=True), 0.0)
    return e * inv


def _group_sum(p, tq):
    out = []
    for h in range(N_KV_HEADS):
        r0 = h * GROUP * tq
        acc = p[r0:r0 + tq]
        for g in range(1, GROUP):
            acc = acc + p[r0 + g * tq:r0 + (g + 1) * tq]
        out.append(acc)
    return out


def _combine(branches, gn, gexp_w, tq):
    hi, lo = _split_bf16(gn)
    gexp = _dot(hi, gexp_w) + _dot(lo, gexp_w)
    lane = _iota((tq, LANES), 1)
    cols = []
    for h in range(N_KV_HEADS):
        blks = []
        for g in range(GROUP):
            j = h * GROUP + g
            rows = slice(j * tq, (j + 1) * tq)
            o = None
            for br in range(N_BRANCH):
                gate = gexp[:, (j * N_BRANCH + br) * LANES:(j * N_BRANCH + br + 1) * LANES]
                term = gate * branches[br][rows]
                o = term if o is None else o + term
            blks.append(o)
        for jj in range(GROUP // 2):
            a, b2 = blks[2 * jj], blks[2 * jj + 1]
            if h == 0:
                b2 = pltpu.roll(b2, HEAD_DIM, axis=1)
            else:
                a = pltpu.roll(a, HEAD_DIM, axis=1)
            cols.append(jnp.where(lane < HEAD_DIM, a, b2))
    return jnp.concatenate(cols, axis=1)


def gate_expand_matrix():
    e = np.zeros((LANES, N_HEADS * N_BRANCH * LANES), np.float32)
    for j in range(N_HEADS * N_BRANCH):
        e[j, j * LANES:(j + 1) * LANES] = 1.0
    return jnp.asarray(e, jnp.bfloat16)


def _nsa_sample_kernel(pt_ref, q_ref, gn_ref, ck_ref, cv_ref, wk_ref, wv_ref,
                       ksn_ref, vsn_ref, kwn_ref, vwn_ref, pool_k, pool_v, ebig_ref, wcst_ref, gexp_ref,
                       o_ref, nwk_ref, nwv_ref, buf, sem, *, n_pages, n_new, tq, n_seq):
    slot = pl.program_id(0) % 2
    past = n_pages * PAGE_SIZE
    n_chunks = n_pages * CHUNKS_PER_PAGE
    n_past_blk = past // SLC_BLK
    n_slc = -(-(past + n_new) // SLC_BLK)
    wb = wk_ref.shape[2]
    rows = N_HEADS * tq
    wait_pages = _page_gather(pt_ref, (pool_k, pool_v), buf, sem, n_pages, n_seq)

    def tok_of(shape):
        return _iota(shape, 0) & (tq - 1)

    def new_rows(ref, si):
        x = jnp.concatenate([ref[si], jnp.zeros((LANES - tq, KV_WIDTH), jnp.float32)], axis=0)
        return x.astype(jnp.bfloat16)

    def compressed(si):
        qpb = _padded_queries(q_ref[si] * ATTN_SCALE, tq).astype(jnp.bfloat16)
        s = _dot_nt(qpb, ck_ref[si])
        n_idx = _iota(s.shape, 1)
        valid = (n_idx * CMP_STRIDE + (CMP_LEN - 1) <= past + tok_of(s.shape)) & (n_idx < n_chunks - 1)
        p_cmp = _masked_softmax(s, valid)
        return qpb, p_cmp, _dot(p_cmp.astype(jnp.bfloat16), cv_ref[si])

    def select(p_cmps):
        ncol = wcst_ref.shape[0]
        groups = [g for p in p_cmps for g in _group_sum(p, tq)]
        psum = jnp.concatenate(groups + [jnp.zeros((LANES - len(groups) * tq, n_chunks), jnp.float32)], axis=0)
        hi, lo = _split_bf16(psum)
        wcst = wcst_ref[...]
        sc = _dot_nt(wcst, hi) + _dot_nt(wcst, lo)
        blk = _iota((ncol, LANES), 0)
        blkf = blk.astype(jnp.float32)
        qblk = (past + (_iota((ncol, LANES), 1) & (tq - 1))) >> SLC_SHIFT
        forced = (blk == 0) | (blk == qblk) | (blk == qblk - 1)
        sc = jnp.where(blk <= qblk, jnp.where(forced, sc + FORCE_BONUS, sc), NEG_INF)
        sc = jnp.where(blk < n_slc, sc, -jnp.inf)
        sel_t = jnp.zeros_like(sc)
        for _ in range(min(TOP_N, n_slc)):
            mx = jnp.max(sc, axis=0, keepdims=True)
            first = jnp.min(jnp.where(sc == mx, blkf, float(ncol)), axis=0, keepdims=True)
            hit = blkf == first
            sel_t = jnp.where(hit, 1.0, sel_t)
            sc = jnp.where(hit, -jnp.inf, sc)
        sel = sel_t.T
        return [[sel[(si * N_KV_HEADS + h) * tq:(si * N_KV_HEADS + h + 1) * tq] for h in range(N_KV_HEADS)]
                for si in range(n_seq)]

    def window_and_operands(si, qpb, sel):
        nsel = [jnp.concatenate([1.0 - sel[h][:, :LANES]] * GROUP, axis=0).astype(jnp.bfloat16) for h in range(N_KV_HEADS)]
        sel_new = [jnp.broadcast_to(sel[h][:, n_past_blk:n_past_blk + 1], (tq, LANES)) for h in range(N_KV_HEADS)]
        lhs = jnp.concatenate(
            [jnp.concatenate([qpb[h * GROUP * tq:(h + 1) * GROUP * tq], nsel[h]], axis=1)
             for h in range(N_KV_HEADS)], axis=0)
        sel_new_rows = jnp.concatenate([sel_new[h] for h in range(N_KV_HEADS) for _ in range(GROUP)], axis=0)

        tn = _iota((rows, LANES), 1)
        new_ok = (tn <= tok_of((rows, LANES))) & (tn < n_new)
        s_sn = jnp.where(new_ok & (sel_new_rows > 0.5), _dot_nt(qpb, new_rows(ksn_ref, si)), NEG_INF)
        s_wn = jnp.where(new_ok, _dot_nt(qpb, new_rows(kwn_ref, si)), NEG_INF)

        wk = wk_ref[si]
        wv = wv_ref[si]
        s_w = _dot(qpb, wk.astype(jnp.bfloat16))
        kwpos = past - wb + _iota(s_w.shape, 1)
        qpos = past + tok_of(s_w.shape)
        s_w = jnp.where((kwpos >= 0) & (kwpos <= qpos) & (kwpos > qpos - WINDOW), s_w, NEG_INF)
        mx = jnp.maximum(jnp.max(s_w, axis=1, keepdims=True), jnp.max(s_wn, axis=1, keepdims=True))
        e_w = jnp.exp(s_w - mx)
        e_wn = jnp.exp(s_wn - mx)
        den = jnp.sum(e_w, axis=1, keepdims=True) + jnp.sum(e_wn, axis=1, keepdims=True)
        o_win = (_dot_nt(e_w.astype(jnp.bfloat16), wv.astype(jnp.bfloat16))
                 + _dot(e_wn.astype(jnp.bfloat16), new_rows(vwn_ref, si))) / den

        col = _iota((KV_WIDTH, wb), 1)
        for src, new_ref, dst in ((wk, kwn_ref, nwk_ref), (wv, vwn_ref, nwv_ref)):
            new_t = jnp.concatenate([new_ref[si], jnp.zeros((LANES - tq, KV_WIDTH), jnp.float32)], axis=0).T
            tail = jnp.concatenate([jnp.zeros((KV_WIDTH, wb - LANES), jnp.float32), new_t], axis=1)
            dst[si] = jnp.where(col >= wb - n_new, pltpu.roll(tail, LANES - n_new, axis=1),
                                pltpu.roll(src, wb - n_new, axis=1))
        return lhs, s_sn, o_win

    def selected_and_combine(si, lhs, s_sn, o_cmp, o_win):
        ck_keys = min(SAMPLE_KEY_CHUNK, past)
        s_chunks = []
        for c in range(past // ck_keys):
            kc = buf[slot, si, 0, :, pl.ds(c * ck_keys, ck_keys)].astype(jnp.bfloat16)
            rhs = jnp.concatenate([kc, ebig_ref[:, pl.ds(c * ck_keys, ck_keys)]], axis=0)
            s_chunks.append(_dot(lhs, rhs))
        mx = jnp.max(s_sn, axis=1, keepdims=True)
        for sc in s_chunks:
            mx = jnp.maximum(mx, jnp.max(sc, axis=1, keepdims=True))
        e_sn = jnp.exp(s_sn - mx)
        den = jnp.sum(e_sn, axis=1, keepdims=True)
        o_sel = _dot(e_sn.astype(jnp.bfloat16), new_rows(vsn_ref, si))
        for c, sc in enumerate(s_chunks):
            e = jnp.exp(sc - mx)
            den = den + jnp.sum(e, axis=1, keepdims=True)
            vc = buf[slot, si, 1, :, pl.ds(c * ck_keys, ck_keys)].astype(jnp.bfloat16)
            o_sel = o_sel + _dot_nt(e.astype(jnp.bfloat16), vc)
        o_sel = o_sel / den

        o_ref[si] = _combine((o_cmp, o_sel, o_win), gn_ref[si], gexp_ref[...], tq)

    seqs = range(n_seq)
    qpb, p_cmp, o_cmp = zip(*[compressed(si) for si in seqs])
    sel = select(p_cmp)
    staged = [window_and_operands(si, qpb[si], sel[si]) for si in seqs]
    wait_pages()
    for si in seqs:
        lhs, s_sn, o_win = staged[si]
        selected_and_combine(si, lhs, s_sn, o_cmp[si], o_win)


def nsa_sample_attention(q, gn, ck, cv, win_k, win_v, ks_new, vs_new, kw_new, vw_new,
                         pool_k, pool_v, page_table, *, tq=SUBLANES, n_seq=SAMPLE_SEQS_PER_STEP):
    DB, DS, _ = q.shape
    n_pages = page_table.shape[1]
    past = n_pages * PAGE_SIZE
    n_chunks = ck.shape[1]
    wb = win_k.shape[2]
    n_slc = -(-(past + DS) // SLC_BLK)
    assert DS <= tq and DS < CMP_STRIDE and past // SLC_BLK <= LANES and wb % LANES == 0
    assert past % min(SAMPLE_KEY_CHUNK, past) == 0 and DB % n_seq == 0 and n_seq * N_KV_HEADS * tq <= LANES
    ncol = -(-n_slc // LANES) * LANES
    pad_t = lambda t: jnp.pad(t, ((0, 0), (0, tq - DS), (0, 0)))
    kb = np.arange(past)[None, :] // SLC_BLK
    ebig = jnp.asarray(np.where(kb == np.arange(LANES)[:, None], NEG_INF, 0.0), jnp.bfloat16)
    wcst = jnp.asarray(cmp_to_slc_matrix(n_chunks, ncol).T, jnp.bfloat16)
    gexp = gate_expand_matrix()
    per_b = lambda b, pt: (b, 0, 0)
    const = lambda b, pt: (0, 0)
    blk3 = lambda n, w: pl.BlockSpec((n_seq, n, w), per_b)
    grid_spec = pltpu.PrefetchScalarGridSpec(
        num_scalar_prefetch=1,
        grid=(DB // n_seq,),
        in_specs=[
            blk3(tq, NSA_WIDTH), blk3(tq, LANES), blk3(n_chunks, KV_WIDTH), blk3(n_chunks, KV_WIDTH),
            blk3(KV_WIDTH, wb), blk3(KV_WIDTH, wb),
            blk3(tq, KV_WIDTH), blk3(tq, KV_WIDTH), blk3(tq, KV_WIDTH), blk3(tq, KV_WIDTH),
            pl.BlockSpec(memory_space=pl.ANY), pl.BlockSpec(memory_space=pl.ANY),
            pl.BlockSpec(ebig.shape, const, pipeline_mode=pl.Buffered(1)),
            pl.BlockSpec(wcst.shape, const, pipeline_mode=pl.Buffered(1)),
            pl.BlockSpec(gexp.shape, const, pipeline_mode=pl.Buffered(1)),
        ],
        out_specs=[blk3(tq, NSA_WIDTH), blk3(KV_WIDTH, wb), blk3(KV_WIDTH, wb)],
        scratch_shapes=[
            pltpu.VMEM((2, n_seq, 2, KV_WIDTH, past), jnp.float32),
            pltpu.SemaphoreType.DMA((2, 2)),
        ],
    )
    o, nwk, nwv = pl.pallas_call(
        functools.partial(_nsa_sample_kernel, n_pages=n_pages, n_new=DS, tq=tq, n_seq=n_seq),
        grid_spec=grid_spec,
        out_shape=[jax.ShapeDtypeStruct((DB, tq, NSA_WIDTH), jnp.float32),
                   jax.ShapeDtypeStruct((DB, KV_WIDTH, wb), jnp.float32),
                   jax.ShapeDtypeStruct((DB, KV_WIDTH, wb), jnp.float32)],
        compiler_params=pltpu.CompilerParams(
            dimension_semantics=("arbitrary",), vmem_limit_bytes=PAGED_VMEM_LIMIT_BYTES),
        name="nsa_sample",
    )(page_table, pad_t(q), pad_t(gn), ck, cv, win_k, win_v, pad_t(ks_new), pad_t(vs_new), pad_t(kw_new),
      pad_t(vw_new), pool_k, pool_v, ebig, wcst, gexp)
    return o[:, :DS], nwk, nwv


def rope_tables(pos):
    half = HEAD_DIM // 2
    inv_freq = ROPE_THETA ** (-jnp.arange(half, dtype=jnp.float32) / half)
    ang = pos.astype(jnp.float32)[:, None] * inv_freq[None, :]
    cos, sin = jnp.cos(ang), jnp.sin(ang)
    cos_t = jnp.tile(cos, (1, LANES // half))
    sin_t = jnp.tile(jnp.concatenate([-sin, sin], axis=1), (1, LANES // HEAD_DIM))
    return cos_t, sin_t


def _rope_lanes(y, cos, sin):
    first_half = (_iota((y.shape[0], LANES), 1) & (HEAD_DIM - 1)) < HEAD_DIM // 2
    cols = []
    for c in range(y.shape[1] // LANES):
        yc = y[:, c * LANES:(c + 1) * LANES]
        rot = jnp.where(first_half, pltpu.roll(yc, LANES - HEAD_DIM // 2, axis=1),
                        pltpu.roll(yc, HEAD_DIM // 2, axis=1))
        cols.append(yc * cos + rot * sin)
    return cols[0] if len(cols) == 1 else jnp.concatenate(cols, axis=1)


def _rope_sublanes(y, cos_t, sin_t):
    hh = HEAD_DIM // 2
    out = []
    for c in range(y.shape[0] // LANES):
        yc = y[c * LANES:(c + 1) * LANES]
        rot = jnp.concatenate([yc[hh:2 * hh], yc[:hh], yc[3 * hh:], yc[2 * hh:3 * hh]], axis=0)
        out.append(yc * cos_t + rot * sin_t)
    return out[0] if len(out) == 1 else jnp.concatenate(out, axis=0)


_REST_ACTS = {"u": jax.nn.gelu, "v": jax.nn.gelu, "gn": jax.nn.sigmoid}


def _proj_rest(h, wrest_ref, refs, names):
    c0 = 0
    for ref, name in zip(refs, names):
        width = ref.shape[-1]
        act = _REST_ACTS.get(name)
        for c in range(0, width, GMLP_WIDTH):
            w = min(GMLP_WIDTH, width - c)
            y = _dot(h, wrest_ref[:, c0 + c:c0 + c + w])
            ref[:, c:c + w] = y if act is None else act(y)
        c0 += width


def _proj_rows_kernel(x_ref, g_ref, wq_ref, wkv_ref, wrest_ref, cos_ref, sin_ref, q_ref, *out_refs):
    h = _rms(x_ref[...], g_ref[...]).astype(jnp.bfloat16)
    cos, sin = cos_ref[...], sin_ref[...]
    q_ref[...] = _rope_lanes(_dot(h, wq_ref[...]), cos, sin)
    y = _dot(h, wkv_ref[...])
    for gi in range(N_KV_GROUPS):
        blk = y[:, gi * KV_WIDTH:(gi + 1) * KV_WIDTH]
        out_refs[gi][...] = _rope_lanes(blk, cos, sin) if gi % 2 == 0 else blk
    _proj_rest(h, wrest_ref, out_refs[N_KV_GROUPS:], ("gn", "u", "v", "ga", "gb"))


def _proj_seq_kernel(x_ref, g_ref, wq_ref, wkvt_ref, wrest_ref, cos_ref, sin_ref, cost_ref, sint_ref,
                     q_ref, kct_ref, vct_ref, kst_ref, vst_ref, kwt_ref, vwt_ref,
                     kst_b_ref, vst_b_ref, kwt_b_ref, vwt_b_ref, gn_ref, u_ref, v_ref, ga_ref, gb_ref):
    h = _rms(x_ref[...], g_ref[...]).astype(jnp.bfloat16)
    q_ref[...] = _rope_lanes(_dot(h, wq_ref[...]), cos_ref[...], sin_ref[...]) * (ATTN_SCALE * LOG2E)
    cos_t, sin_t = cost_ref[...], sint_ref[...]
    yt = _dot_nt(wkvt_ref[...], h)
    kv_refs = (kct_ref, vct_ref, kst_ref, vst_ref, kwt_ref, vwt_ref)
    kvb_refs = (None, None, kst_b_ref, vst_b_ref, kwt_b_ref, vwt_b_ref)
    for gi in range(N_KV_GROUPS):
        blk = yt[gi * KV_WIDTH:(gi + 1) * KV_WIDTH]
        if gi % 2 == 0:
            blk = _rope_sublanes(blk, cos_t, sin_t)
        kv_refs[gi][0] = blk
        if kvb_refs[gi] is not None:
            kvb_refs[gi][0] = blk.astype(jnp.bfloat16)
    _proj_rest(h, wrest_ref, (gn_ref, u_ref, v_ref, ga_ref, gb_ref), ("gn", "u", "v", "ga", "gb"))


def _split_w_in(w_in):
    sizes = (NSA_WIDTH,) + (KV_WIDTH,) * N_KV_GROUPS + (N_HEADS * N_BRANCH,)
    offs = np.cumsum((0,) + sizes)
    wq = w_in[:, offs[0]:offs[1]]
    wkv = w_in[:, offs[1]:offs[7]]
    wgn = w_in[:, offs[7]:offs[8]]
    wrest = w_in[:, offs[8]:]
    return wq, wkv, wgn, wrest


def mixer_proj_rows(x, g, w_in, pos):
    M = x.shape[0]
    tm = min(ROW_TILE, M)
    assert M % tm == 0
    wq, wkv, wgn, wrest = _split_w_in(w_in)
    wrest = jnp.concatenate([jnp.pad(wgn, ((0, 0), (0, LANES - wgn.shape[1]))), wrest], axis=1)
    cos, sin = rope_tables(pos)
    b16 = lambda t: t.astype(jnp.bfloat16)
    row = lambda i: (i, 0)
    cw = lambda shape: pl.BlockSpec(shape, lambda i: (0, 0), pipeline_mode=pl.Buffered(1))
    widths = (NSA_WIDTH,) + (KV_WIDTH,) * N_KV_GROUPS + (LANES, GMLP_WIDTH, GMLP_WIDTH, D_MODEL, D_MODEL)
    return pl.pallas_call(
        _proj_rows_kernel,
        grid=(M // tm,),
        in_specs=[pl.BlockSpec((tm, D_MODEL), row), cw((1, D_MODEL)), cw(wq.shape), cw(wkv.shape),
                  cw(wrest.shape), pl.BlockSpec((tm, LANES), row), pl.BlockSpec((tm, LANES), row)],
        out_specs=[pl.BlockSpec((tm, w), row) for w in widths],
        out_shape=[jax.ShapeDtypeStruct((M, w), jnp.float32) for w in widths],
        compiler_params=pltpu.CompilerParams(
            dimension_semantics=("arbitrary",), vmem_limit_bytes=VMEM_LIMIT_BYTES),
        name="mixer_proj_rows",
    )(x, g.reshape(1, -1), b16(wq), b16(wkv), b16(wrest), cos, sin)


def mixer_proj_seq(x, g, w_in, batch):
    M = x.shape[0]
    S = M // batch
    tm = min(ROW_TILE, S)
    assert S % tm == 0
    n_s = S // tm
    wq, wkv, wgn, wrest = _split_w_in(w_in)
    wrest = jnp.concatenate([jnp.pad(wgn, ((0, 0), (0, LANES - wgn.shape[1]))), wrest], axis=1)
    cos, sin = rope_tables(jnp.arange(S, dtype=jnp.int32))
    b16 = lambda t: t.astype(jnp.bfloat16)
    f32 = jnp.float32
    row = lambda i: (i, 0)
    tab = lambda i: (i % n_s, 0)
    tab_t = lambda i: (0, i % n_s)
    seq_t = lambda i: (i // n_s, 0, i % n_s)
    cw = lambda shape: pl.BlockSpec(shape, lambda i: (0, 0), pipeline_mode=pl.Buffered(1))
    t_spec = pl.BlockSpec((1, KV_WIDTH, tm), seq_t)
    t_shape = lambda dt: jax.ShapeDtypeStruct((batch, KV_WIDTH, S), dt)
    r_widths = (LANES, GMLP_WIDTH, GMLP_WIDTH, D_MODEL, D_MODEL)
    out_specs = ([pl.BlockSpec((tm, NSA_WIDTH), row)] + [t_spec] * (N_KV_GROUPS + 4)
                 + [pl.BlockSpec((tm, w), row) for w in r_widths])
    out_shape = ([jax.ShapeDtypeStruct((M, NSA_WIDTH), f32)] + [t_shape(f32)] * N_KV_GROUPS
                 + [t_shape(jnp.bfloat16)] * 4 + [jax.ShapeDtypeStruct((M, w), f32) for w in r_widths])
    return pl.pallas_call(
        _proj_seq_kernel,
        grid=(M // tm,),
        in_specs=[pl.BlockSpec((tm, D_MODEL), row), cw((1, D_MODEL)), cw(wq.shape),
                  cw((N_KV_GROUPS * KV_WIDTH, D_MODEL)), cw(wrest.shape),
                  pl.BlockSpec((tm, LANES), tab), pl.BlockSpec((tm, LANES), tab),
                  pl.BlockSpec((LANES, tm), tab_t), pl.BlockSpec((LANES, tm), tab_t)],
        out_specs=out_specs,
        out_shape=out_shape,
        compiler_params=pltpu.CompilerParams(
            dimension_semantics=("arbitrary",), vmem_limit_bytes=VMEM_LIMIT_BYTES),
        name="mixer_proj_seq",
    )(x, g.reshape(1, -1), b16(wq), b16(wkv.T), b16(wrest), cos, sin, cos.T, sin.T)


def _merge_kernel(oa_ref, u_ref, v_ref, ga_ref, gb_ref, x_ref, lng_ref, lnb_ref, wsp_ref, bsx_ref,
                  wa_ref, wb_ref, wout_ref, gc_ref, wcq_ref, x1_ref, hq_ref):
    tm = x_ref.shape[0]
    v = v_ref[...]
    vc = v - jnp.mean(v, axis=-1, keepdims=True)
    var = jnp.mean(vc * vc, axis=-1, keepdims=True)
    vn = vc * lax.rsqrt(var + NORM_EPS) * lng_ref[...] + lnb_ref[...]
    lane = _iota((GMLP_CHUNK, LANES), 1)
    chunks = []
    for c in range(tm // GMLP_CHUNK):
        cols = []
        for pr in range(GMLP_GROUPS // 2):
            pair = vn[c * GMLP_CHUNK:(c + 1) * GMLP_CHUNK, pr * LANES:(pr + 1) * LANES]
            rhs = jnp.concatenate([jnp.where(lane < GMLP_GROUP_DIM, pair, 0.0),
                                   jnp.where(lane >= GMLP_GROUP_DIM, pair, 0.0)], axis=0).astype(jnp.bfloat16)
            cols.append(_dot(wsp_ref[pr], rhs))
        chunks.append(jnp.concatenate(cols, axis=1) + bsx_ref[...])
    o_b = u_ref[...] * jnp.concatenate(chunks, axis=0)
    m = (jax.nn.sigmoid(ga_ref[...]) * _dot(oa_ref[...].astype(jnp.bfloat16), wa_ref[...])
         + jax.nn.sigmoid(gb_ref[...]) * _dot(o_b.astype(jnp.bfloat16), wb_ref[...]))
    x1 = x_ref[...] + _dot(m.astype(jnp.bfloat16), wout_ref[...])
    x1_ref[...] = x1
    hq_ref[...] = _dot(_rms(x1, gc_ref[...]).astype(jnp.bfloat16), wcq_ref[...])


def gmlp_weights(ws, bs, chunk_len):
    L = chunk_len
    w = jnp.tril(ws[:, :L, :L])
    bias = bs[:, :L]
    if L < GMLP_CHUNK:
        r = np.arange(GMLP_CHUNK)
        same_seq = (r[:, None] // L) == (r[None, :] // L)
        w = jnp.where(same_seq, w[:, r % L][:, :, r % L], 0.0)
        bias = bias[:, r % L]
    pairs = w.reshape(GMLP_GROUPS // 2, 2, GMLP_CHUNK, GMLP_CHUNK)
    wsp = jnp.swapaxes(pairs, 1, 2).reshape(GMLP_GROUPS // 2, GMLP_CHUNK, 2 * GMLP_CHUNK)
    bsx = jnp.repeat(bias.T, GMLP_GROUP_DIM, axis=1)
    return wsp.astype(jnp.bfloat16), bsx


def merge_and_cross_q(o_a, u, v, ga, gb, x, ln_g, ln_b, wsp, bsx, w_a, w_b, w_out, g_cross, w_cq):
    M = x.shape[0]
    tm = min(ROW_TILE, M)
    one = pl.Buffered(1)
    row = lambda i: (i, 0)
    cw = lambda shape: pl.BlockSpec(shape, lambda i: (0,) * len(shape), pipeline_mode=one)
    b16 = lambda t: t.astype(jnp.bfloat16)
    out = jax.ShapeDtypeStruct((M, D_MODEL), jnp.float32)
    return pl.pallas_call(
        _merge_kernel,
        grid=(M // tm,),
        in_specs=[
            pl.BlockSpec((tm, NSA_WIDTH), row), pl.BlockSpec((tm, GMLP_WIDTH), row),
            pl.BlockSpec((tm, GMLP_WIDTH), row), pl.BlockSpec((tm, D_MODEL), row),
            pl.BlockSpec((tm, D_MODEL), row), pl.BlockSpec((tm, D_MODEL), row),
            cw((1, GMLP_WIDTH)), cw((1, GMLP_WIDTH)), cw(wsp.shape), cw(bsx.shape),
            cw(w_a.shape), cw(w_b.shape), cw(w_out.shape), cw((1, D_MODEL)), cw(w_cq.shape),
        ],
        out_specs=[pl.BlockSpec((tm, D_MODEL), row)] * 2,
        out_shape=[out, out],
        compiler_params=pltpu.CompilerParams(
            dimension_semantics=("arbitrary",), vmem_limit_bytes=VMEM_LIMIT_BYTES),
        name="merge_branches",
    )(o_a, u, v, ga, gb, x, ln_g.reshape(1, -1), ln_b.reshape(1, -1), wsp, bsx,
      b16(w_a), b16(w_b), b16(w_out), g_cross.reshape(1, -1), b16(w_cq))


def _memkv_kernel(x_ref, g_ref, wk_ref, wv_ref, k_ref, v_ref):
    h = _rms(x_ref[...], g_ref[...]).astype(jnp.bfloat16)
    k_ref[...] = _dot(h, wk_ref[...])
    v_ref[...] = _dot(h, wv_ref[...])


def memory_kv(mem, g_mem, w_ck, w_cv):
    M = mem.shape[0]
    tm = min(ROW_TILE, M)
    one = pl.Buffered(1)
    row = lambda i: (i, 0)
    cw = lambda shape: pl.BlockSpec(shape, lambda i: (0, 0), pipeline_mode=one)
    out = jax.ShapeDtypeStruct((M, D_MODEL), jnp.float32)
    return pl.pallas_call(
        _memkv_kernel,
        grid=(M // tm,),
        in_specs=[pl.BlockSpec((tm, D_MODEL), row), cw((1, D_MODEL)), cw(w_ck.shape), cw(w_cv.shape)],
        out_specs=[pl.BlockSpec((tm, D_MODEL), row)] * 2,
        out_shape=[out, out],
        compiler_params=pltpu.CompilerParams(
            dimension_semantics=("arbitrary",), vmem_limit_bytes=VMEM_LIMIT_BYTES),
        name="memory_kv",
    )(mem, g_mem.reshape(1, -1), w_ck.astype(jnp.bfloat16), w_cv.astype(jnp.bfloat16))


def _cross_heads_kernel(q_ref, mk_hbm, mv_hbm, o_ref, buf, sem, *, rows, n_seq):
    b = pl.program_id(0)
    nb = pl.num_programs(0)
    slot = b % 2

    def head_copies(bb, sl):
        return [pltpu.make_async_copy(src.at[bb * n_seq + si, :, h, :], buf.at[sl, si, kv, h], sem.at[sl, kv])
                for si in range(n_seq) for kv, src in enumerate((mk_hbm, mv_hbm)) for h in range(CROSS_HEADS)]

    @pl.when(b == 0)
    def _():
        for c in head_copies(0, 0):
            c.start()

    @pl.when(b + 1 < nb)
    def _():
        for c in head_copies(b + 1, 1 - slot):
            c.start()

    pad = jnp.zeros((SUBLANES - rows, D_MODEL), jnp.float32)
    qs = [jnp.concatenate([q_ref[si] * CROSS_SCALE, pad], axis=0) for si in range(n_seq)]
    for c in head_copies(b, slot):
        c.wait()
    for si in range(n_seq):
        qh = [qs[si][:, h * CROSS_HEAD_DIM:(h + 1) * CROSS_HEAD_DIM].astype(jnp.bfloat16) for h in range(CROSS_HEADS)]
        s = jnp.concatenate([_dot_nt(qh[h], buf[slot, si, 0, h].astype(jnp.bfloat16)) for h in range(CROSS_HEADS)], axis=0)
        e = jnp.exp(s - jnp.max(s, axis=1, keepdims=True))
        p = (e / jnp.sum(e, axis=1, keepdims=True)).astype(jnp.bfloat16)
        o = jnp.concatenate([_dot(p[h * SUBLANES:(h + 1) * SUBLANES], buf[slot, si, 1, h].astype(jnp.bfloat16))
                             for h in range(CROSS_HEADS)], axis=1)
        o_ref[si] = o[:rows]


def cross_attention_cached(hq, mk, mv, *, n_seq=CROSS_SEQS_PER_STEP):
    NB, R, _ = hq.shape
    assert NB % n_seq == 0
    any_spec = pl.BlockSpec(memory_space=pl.ANY)
    return pl.pallas_call(
        functools.partial(_cross_heads_kernel, rows=R, n_seq=n_seq),
        grid=(NB // n_seq,),
        in_specs=[pl.BlockSpec((n_seq, R, D_MODEL), lambda b: (b, 0, 0)), any_spec, any_spec],
        out_specs=pl.BlockSpec((n_seq, R, D_MODEL), lambda b: (b, 0, 0)),
        out_shape=jax.ShapeDtypeStruct((NB, R, D_MODEL), jnp.float32),
        scratch_shapes=[pltpu.VMEM((2, n_seq, 2, CROSS_HEADS, N_MEM, CROSS_HEAD_DIM), jnp.float32),
                        pltpu.SemaphoreType.DMA((2, 2))],
        compiler_params=pltpu.CompilerParams(
            dimension_semantics=("arbitrary",), vmem_limit_bytes=VMEM_LIMIT_BYTES),
        name="cross_attention_cached",
    )(hq, mk, mv)


def _ffn_and_norm(x, h_gain, wg_ref, wu_ref, wd_ref, gfin_ref, y_ref, acc_ref):
    h = _rms(x, h_gain).astype(jnp.bfloat16)
    acc_ref[...] = x
    for c in range(D_FF // FF_CHUNK):
        sl = slice(c * FF_CHUNK, (c + 1) * FF_CHUNK)
        a = _dot(h, wg_ref[:, sl])
        b = _dot(h, wu_ref[:, sl])
        t = (a * jax.nn.sigmoid(a) * b).astype(jnp.bfloat16)
        acc_ref[...] += _dot(t, wd_ref[sl, :])
    y_ref[...] = _rms(acc_ref[...], gfin_ref[...])


def _memory_attention(q, mk, mv):
    q = q * CROSS_SCALE
    outs = []
    for h in range(CROSS_HEADS):
        sl = slice(h * CROSS_HEAD_DIM, (h + 1) * CROSS_HEAD_DIM)
        s = _dot_nt(q[:, sl].astype(jnp.bfloat16), mk[:, sl].astype(jnp.bfloat16))
        e = jnp.exp(s - jnp.max(s, axis=1, keepdims=True))
        p = e / jnp.sum(e, axis=1, keepdims=True)
        outs.append(_dot(p.astype(jnp.bfloat16), mv[:, sl].astype(jnp.bfloat16)))
    return jnp.concatenate(outs, axis=1)


def _tail_kernel(x_ref, o_ref, wco_ref, gf_ref, wg_ref, wu_ref, wd_ref, gfin_ref, y_ref, acc_ref):
    x = x_ref[...] + _dot(o_ref[...].astype(jnp.bfloat16), wco_ref[...])
    _ffn_and_norm(x, gf_ref[...], wg_ref, wu_ref, wd_ref, gfin_ref, y_ref, acc_ref)


def _tail_cross_kernel(x_ref, hq_ref, mk_ref, mv_ref, wco_ref, gf_ref, wg_ref, wu_ref, wd_ref, gfin_ref,
                       y_ref, acc_ref):
    o = _memory_attention(hq_ref[...], mk_ref[0], mv_ref[0])
    x = x_ref[...] + _dot(o.astype(jnp.bfloat16), wco_ref[...])
    _ffn_and_norm(x, gf_ref[...], wg_ref, wu_ref, wd_ref, gfin_ref, y_ref, acc_ref)


def _tail_call(kernel_fn, name, x1, row_inputs, batch_inputs, rows_per_batch, w_co, g_ffn, w_gate, w_up, w_down, g_final):
    M = x1.shape[0]
    tm = min(ROW_TILE, M)
    row = lambda i: (i, 0)
    cw = lambda shape: pl.BlockSpec(shape, lambda i: (0, 0), pipeline_mode=pl.Buffered(1))
    b16 = lambda t: t.astype(jnp.bfloat16)
    per_batch = lambda i: (i * tm // rows_per_batch, 0, 0)
    return pl.pallas_call(
        kernel_fn,
        grid=(M // tm,),
        in_specs=[pl.BlockSpec((tm, D_MODEL), row)] * (1 + len(row_inputs))
        + [pl.BlockSpec((1,) + t.shape[1:], per_batch) for t in batch_inputs]
        + [cw(w_co.shape), cw((1, D_MODEL)), cw(w_gate.shape), cw(w_up.shape), cw(w_down.shape), cw((1, D_MODEL))],
        out_specs=pl.BlockSpec((tm, D_MODEL), row),
        out_shape=jax.ShapeDtypeStruct((M, D_MODEL), jnp.float32),
        scratch_shapes=[pltpu.VMEM((tm, D_MODEL), jnp.float32)],
        compiler_params=pltpu.CompilerParams(
            dimension_semantics=("arbitrary",), vmem_limit_bytes=VMEM_LIMIT_BYTES),
        name=name,
    )(x1, *row_inputs, *batch_inputs, b16(w_co), g_ffn.reshape(1, -1), b16(w_gate), b16(w_up), b16(w_down),
      g_final.reshape(1, -1))


def tail(x1, o, *weights):
    return _tail_call(_tail_kernel, "tail", x1, (o,), (), x1.shape[0], *weights)


def tail_with_memory(x1, hq, mk, mv, *weights):
    rows_per_batch = x1.shape[0] // mk.shape[0]
    assert rows_per_batch % min(ROW_TILE, x1.shape[0]) == 0
    return _tail_call(_tail_cross_kernel, "tail_with_memory", x1, (hq,), (mk, mv), rows_per_batch, *weights)


def kernel(x_prompt, x_sample, mem_prompt, cache_cmp_k, cache_cmp_v, cache_slc_k, cache_slc_v,
           page_table, state_win_k, state_win_v, cache_mem_k, cache_mem_v,
           g_mix, w_in, cmp_pe, cmp_w1, cmp_w2, gmlp_ln_g, gmlp_ln_b, gmlp_ws, gmlp_bs,
           w_branch_a, w_branch_b, w_out, g_cross, g_mem, w_cq, w_ck, w_cv, w_co,
           g_ffn, w_gate, w_up, w_down, g_final):
    B, S, _ = x_prompt.shape
    DB, DS, _ = x_sample.shape
    n_pages = page_table.shape[1]
    assert g_mix.shape[0] == 1, "single-layer step"
    l = 0
    cmpw = cmp_weights(cmp_pe[l], cmp_w1[l], cmp_w2[l])
    merge_w = (gmlp_ln_g[l], gmlp_ln_b[l])
    merge_tail = (w_branch_a[l], w_branch_b[l], w_out[l], g_cross[l], w_cq[l])
    ffn_w = (w_co[l], g_ffn[l], w_gate[l], w_up[l], w_down[l], g_final)
    to_t = lambda t: jnp.transpose(t, (0, 2, 3, 1)).reshape(t.shape[0], KV_WIDTH, t.shape[1])
    from_t = lambda t: jnp.transpose(t.reshape(t.shape[0], N_KV_HEADS, HEAD_DIM, t.shape[2]), (0, 3, 1, 2))
    mem3 = lambda t: t.reshape(-1, N_MEM, D_MODEL)

    xp = x_prompt.reshape(B * S, D_MODEL)
    (q, kct, vct, kst, vst, kwt, vwt, kst_b, vst_b, kwt_b, vwt_b, gn, u, v, ga, gb) = mixer_proj_seq(
        xp, g_mix[l], w_in[l], B)
    ck, cv = cmp_blocks_dense(kct, vct, *cmpw)
    o_a = nsa_prompt_attention(q.reshape(B, S, NSA_WIDTH), gn.reshape(B, S, LANES), ck, cv,
                               kst_b, vst_b, kwt_b, vwt_b)
    x1, hq = merge_and_cross_q(o_a.reshape(B * S, NSA_WIDTH), u, v, ga, gb, xp, *merge_w,
                               *gmlp_weights(gmlp_ws[l], gmlp_bs[l], min(S, GMLP_CHUNK)), *merge_tail)
    mk, mv = memory_kv(mem_prompt.reshape(B * N_MEM, D_MODEL), g_mem[l], w_ck[l], w_cv[l])
    y_prompt = tail_with_memory(x1, hq, mem3(mk), mem3(mv), *ffn_w).reshape(B, S, D_MODEL)
    wb_p = min(WINDOW, S)
    mem5 = lambda t: t.reshape(B, N_MEM, CROSS_HEADS, CROSS_HEAD_DIM)
    prompt_new = (from_t(kct), from_t(vct), from_t(kst), from_t(vst),
                  from_t(kwt[:, :, S - wb_p:]), from_t(vwt[:, :, S - wb_p:]), mem5(mk), mem5(mv))

    xs = x_sample.reshape(DB * DS, D_MODEL)
    pos_s = n_pages * PAGE_SIZE + jnp.arange(DS, dtype=jnp.int32)
    (q, kc, vc, ks, vs, kw, vw, gn, u, v, ga, gb) = mixer_proj_rows(xs, g_mix[l], w_in[l], jnp.tile(pos_s, DB))
    ck, cv = cmp_blocks_paged(to_t(cache_cmp_k[l]), to_t(cache_cmp_v[l]), page_table, *cmpw)
    per_seq = lambda t: t.reshape(DB, DS, -1)
    o_a, win_kt, win_vt = nsa_sample_attention(
        per_seq(q), per_seq(gn), ck, cv, to_t(state_win_k[l]), to_t(state_win_v[l]),
        per_seq(ks), per_seq(vs), per_seq(kw), per_seq(vw),
        to_t(cache_slc_k[l]), to_t(cache_slc_v[l]), page_table)
    x1, hq = merge_and_cross_q(o_a.reshape(DB * DS, NSA_WIDTH), u, v, ga, gb, xs, *merge_w,
                               *gmlp_weights(gmlp_ws[l], gmlp_bs[l], min(DS, GMLP_CHUNK)), *merge_tail)
    o = cross_attention_cached(per_seq(hq), cache_mem_k[l], cache_mem_v[l])
    y_sample = tail(x1, o.reshape(DB * DS, D_MODEL), *ffn_w).reshape(DB, DS, D_MODEL)
    heads = lambda t: t.reshape(DB, DS, N_KV_HEADS, HEAD_DIM)
    sample_new = (heads(kc), heads(vc), heads(ks), heads(vs), from_t(win_kt), from_t(win_vt), per_seq(v))

    return (y_prompt, y_sample) + tuple(t[None] for t in prompt_new + sample_new)
```

```python
import functools

import jax
import jax.numpy as jnp
import numpy as np
from jax import lax
from jax.experimental import pallas as pl
from jax.experimental.pallas import tpu as pltpu

D_MODEL = 1024
N_HEADS = 8
HEAD_DIM = 64
N_KV_HEADS = 2
GROUP = N_HEADS // N_KV_HEADS
NSA_WIDTH = N_HEADS * HEAD_DIM
KV_WIDTH = N_KV_HEADS * HEAD_DIM
CMP_LEN = 32
CMP_STRIDE = 16
CMP_HIDDEN = 128
SLC_BLK = 64
SLC_SHIFT = 6
TOP_N = 16
WINDOW = 512
ROPE_THETA = 10000.0
ATTN_SCALE = HEAD_DIM ** -0.5
LOG2E = 1.4426950408889634
FORCE_BONUS = 1e4
NEG_INF = -1e30
N_BRANCH = 3
GMLP_WIDTH = 512
GMLP_GROUPS = 8
GMLP_GROUP_DIM = GMLP_WIDTH // GMLP_GROUPS
GMLP_CHUNK = 128
N_MEM = 256
CROSS_HEADS = 4
CROSS_HEAD_DIM = D_MODEL // CROSS_HEADS
CROSS_SCALE = CROSS_HEAD_DIM ** -0.5
D_FF = -(-8 * D_MODEL // (3 * 256)) * 256
NORM_EPS = 1e-6
PAGE_SIZE = 128
CHUNKS_PER_PAGE = PAGE_SIZE // CMP_STRIDE
N_KV_GROUPS = 6

LANES = 128
SUBLANES = 8
VMEM_LIMIT_BYTES = 48 * 1024 * 1024
PAGED_VMEM_LIMIT_BYTES = 56 * 1024 * 1024
SEL_TILE = 1024
SEL_TAIL_TILE = 512
SAMPLE_KEY_CHUNK = 2048
SAMPLE_SEQS_PER_STEP = 2
CROSS_SEQS_PER_STEP = 4
ROW_TILE = 512
FF_CHUNK = 256

_NT = (((1,), (1,)), ((), ()))


def _dot_nt(a, b):
    return lax.dot_general(a, b, _NT, preferred_element_type=jnp.float32)


def _dot(a, b):
    return jnp.dot(a, b, preferred_element_type=jnp.float32)


def _split_bf16(x):
    hi = x.astype(jnp.bfloat16)
    lo = (x - hi.astype(jnp.float32)).astype(jnp.bfloat16)
    return hi, lo


def _iota(shape, dim):
    return lax.broadcasted_iota(jnp.int32, shape, dim)


def _rms(x, g):
    return x * lax.rsqrt(jnp.mean(x * x, axis=-1, keepdims=True) + NORM_EPS) * g


def _cmp_from_transposed(xt_ref, p, xs, wbig_ref, w2big_ref, peterm_ref, out_ref, n_rows):
    n_chunks = n_rows // CMP_STRIDE
    pitch = xs.shape[0] // CMP_STRIDE
    per_blk = LANES // CMP_STRIDE
    for j in range(n_rows // LANES):
        blk = xt_ref[:, j * LANES:(j + 1) * LANES].T
        for k in range(LANES // SUBLANES):
            c, s0 = k // 2, (k % 2) * SUBLANES
            xs[pl.ds(s0 * pitch + j * per_blk + c, SUBLANES, stride=pitch), :] = blk[k * SUBLANES:(k + 1) * SUBLANES]
    acc = jnp.zeros((n_chunks, 4 * CMP_HIDDEN), jnp.float32)
    for sp in range(CMP_STRIDE // 2):
        a0 = xs[pl.ds(2 * sp * pitch, n_chunks), :]
        a1 = xs[pl.ds((2 * sp + 1) * pitch, n_chunks), :]
        lhs = jnp.concatenate([a0, a1], axis=1).astype(jnp.bfloat16)
        acc = acc + _dot(lhs, wbig_ref[p, sp])
    lead = acc[:, :2 * CMP_HIDDEN]
    trail_next = pltpu.roll(acc[:, 2 * CMP_HIDDEN:], n_chunks - 1, axis=0)
    pre = lead + trail_next + peterm_ref[p]
    out = _dot(jax.nn.gelu(pre).astype(jnp.bfloat16), w2big_ref[p])
    out_ref[0] = out.astype(out_ref.dtype)


def _regroup_scratch(n_rows):
    n_chunks = n_rows // CMP_STRIDE
    pitch = n_chunks + SUBLANES if n_chunks % 32 == 0 else n_chunks
    return pltpu.VMEM((CMP_STRIDE * pitch, KV_WIDTH), jnp.float32)


def _page_gather(pt_ref, pools, buf, sem, n_pages, n_seq=1):
    b = pl.program_id(0)
    nb = pl.num_programs(0)
    slot = b % 2

    def copies(bb, sl):
        return [pltpu.make_async_copy(pools[p].at[pt_ref[bb * n_seq + si, j]],
                                      buf.at[sl, si, p, :, pl.ds(j * PAGE_SIZE, PAGE_SIZE)], sem.at[sl, p])
                for si in range(n_seq) for p in range(len(pools)) for j in range(n_pages)]

    @pl.when(b == 0)
    def _():
        for c in copies(0, 0):
            c.start()

    @pl.when(b + 1 < nb)
    def _():
        for c in copies(b + 1, 1 - slot):
            c.start()

    def wait():
        for c in copies(b, slot):
            c.wait()

    return wait


def _cmp_paged_kernel(pt_ref, pool_k, pool_v, wbig_ref, w2big_ref, peterm_ref, ck_ref, cv_ref,
                      buf, xs, sem, *, n_pages):
    wait = _page_gather(pt_ref, (pool_k, pool_v), buf, sem, n_pages)
    wait()
    slot = pl.program_id(0) % 2
    for p, out_ref in ((0, ck_ref), (1, cv_ref)):
        _cmp_from_transposed(buf.at[slot, 0, p], p, xs, wbig_ref, w2big_ref, peterm_ref, out_ref,
                             n_pages * PAGE_SIZE)


def _cmp_dense_kernel(kt_ref, vt_ref, wbig_ref, w2big_ref, peterm_ref, ck_ref, cv_ref, xs):
    for p, (src, out_ref) in enumerate(((kt_ref, ck_ref), (vt_ref, cv_ref))):
        _cmp_from_transposed(src.at[0], p, xs, wbig_ref, w2big_ref, peterm_ref, out_ref, src.shape[2])


def cmp_weights(cmp_pe, cmp_w1, cmp_w2):
    eye = jnp.eye(N_KV_HEADS, dtype=jnp.float32)
    w1r = cmp_w1.reshape(2, 2, CMP_STRIDE // 2, 2, HEAD_DIM, CMP_HIDDEN)
    wbig = jnp.einsum('jtpsdf,hk->jpshdtkf', w1r, eye)
    wbig = wbig.reshape(2, CMP_STRIDE // 2, 2 * KV_WIDTH, 4 * CMP_HIDDEN).astype(jnp.bfloat16)
    w2big = jnp.einsum('jfd,hk->jhfkd', cmp_w2, eye).reshape(2, 2 * CMP_HIDDEN, KV_WIDTH)
    pe_term = jnp.einsum('jsd,jsdf->jf', cmp_pe, cmp_w1)
    peterm = jnp.tile(pe_term[:, None, :], (1, 1, N_KV_HEADS))
    return wbig, w2big.astype(jnp.bfloat16), peterm


def _cmp_weight_specs(wbig, w2big, peterm, nargs):
    c3 = (lambda b: (0, 0, 0)) if nargs == 1 else (lambda b, pt: (0, 0, 0))
    c4 = (lambda b: (0, 0, 0, 0)) if nargs == 1 else (lambda b, pt: (0, 0, 0, 0))
    one = pl.Buffered(1)
    return [pl.BlockSpec(wbig.shape, c4, pipeline_mode=one),
            pl.BlockSpec(w2big.shape, c3, pipeline_mode=one),
            pl.BlockSpec(peterm.shape, c3, pipeline_mode=one)]


def cmp_blocks_paged(pool_k, pool_v, page_table, wbig, w2big, peterm):
    n_seq, n_pages = page_table.shape
    n_rows = n_pages * PAGE_SIZE
    n_chunks = n_rows // CMP_STRIDE
    out = jax.ShapeDtypeStruct((n_seq, n_chunks, KV_WIDTH), jnp.bfloat16)
    grid_spec = pltpu.PrefetchScalarGridSpec(
        num_scalar_prefetch=1,
        grid=(n_seq,),
        in_specs=[pl.BlockSpec(memory_space=pl.ANY), pl.BlockSpec(memory_space=pl.ANY)]
        + _cmp_weight_specs(wbig, w2big, peterm, 2),
        out_specs=[pl.BlockSpec((1, n_chunks, KV_WIDTH), lambda b, pt: (b, 0, 0))] * 2,
        scratch_shapes=[
            pltpu.VMEM((2, 1, 2, KV_WIDTH, n_rows), jnp.float32),
            _regroup_scratch(n_rows),
            pltpu.SemaphoreType.DMA((2, 2)),
        ],
    )
    return pl.pallas_call(
        functools.partial(_cmp_paged_kernel, n_pages=n_pages),
        grid_spec=grid_spec,
        out_shape=[out, out],
        compiler_params=pltpu.CompilerParams(
            dimension_semantics=("arbitrary",), vmem_limit_bytes=PAGED_VMEM_LIMIT_BYTES),
        name="cmp_blocks_paged",
    )(page_table, pool_k, pool_v, wbig, w2big, peterm)


def cmp_blocks_dense(kt, vt, wbig, w2big, peterm):
    B, _, S = kt.shape
    n_chunks = S // CMP_STRIDE
    per_b = lambda b: (b, 0, 0)
    return pl.pallas_call(
        _cmp_dense_kernel,
        grid=(B,),
        in_specs=[pl.BlockSpec((1, KV_WIDTH, S), per_b), pl.BlockSpec((1, KV_WIDTH, S), per_b)]
        + _cmp_weight_specs(wbig, w2big, peterm, 1),
        out_specs=[pl.BlockSpec((1, n_chunks, KV_WIDTH), per_b)] * 2,
        out_shape=[jax.ShapeDtypeStruct((B, n_chunks, KV_WIDTH), jnp.bfloat16)] * 2,
        scratch_shapes=[_regroup_scratch(S)],
        compiler_params=pltpu.CompilerParams(
            dimension_semantics=("arbitrary",), vmem_limit_bytes=VMEM_LIMIT_BYTES),
        name="cmp_blocks_dense",
    )(kt, vt, wbig, w2big, peterm)


def cmp_to_slc_matrix(n_chunks, n_cols):
    cs = np.arange(n_chunks)[:, None] * CMP_STRIDE
    ss = np.arange(n_cols)[None, :] * SLC_BLK
    shared = np.minimum(cs + CMP_LEN, ss + SLC_BLK) - np.maximum(cs, ss)
    w = np.maximum(shared, 0).astype(np.float32) / CMP_LEN
    w[n_chunks - 1] = 0.0
    return w


def _nsa_prompt_kernel(q_ref, gn_ref, ck_ref, cv_ref, ks_ref, vs_ref, kw_ref, vw_ref,
                       wcst_ref, gexp_ref, o_ref, lhs_ref, m_ref, l_ref, acc_ref, *, tq, n_chunks):
    i = pl.program_id(1)
    q0 = i * tq
    qpb = _padded_queries(q_ref[0], tq).astype(jnp.bfloat16)

    def qpos_of(shape):
        return q0 + (_iota(shape, 0) & (tq - 1))

    s = _dot_nt(qpb, ck_ref[0])
    n_idx = _iota(s.shape, 1)
    valid = (n_idx * CMP_STRIDE + (CMP_LEN - 1) <= qpos_of(s.shape)) & (n_idx < n_chunks - 1)
    p_cmp = _masked_softmax2(s, valid)
    o_cmp = _dot(p_cmp.astype(jnp.bfloat16), cv_ref[0])

    blk = _iota((LANES, tq), 0)
    blkf = blk.astype(jnp.float32)
    qblk = (q0 + _iota((LANES, tq), 1)) >> SLC_SHIFT
    forced = (blk == 0) | (blk == qblk) | (blk == qblk - 1)
    wcst = wcst_ref[...]
    nsel = []
    for psum in _group_sum(p_cmp, tq):
        hi, lo = _split_bf16(psum)
        sc = _dot_nt(wcst, hi) + _dot_nt(wcst, lo)
        sc = jnp.where(blk <= qblk, jnp.where(forced, sc + FORCE_BONUS, sc), NEG_INF)
        for _ in range(TOP_N):
            mx = jnp.max(sc, axis=0, keepdims=True)
            first = jnp.min(jnp.where(sc == mx, blkf, float(LANES)), axis=0, keepdims=True)
            sc = jnp.where(blkf == first, -jnp.inf, sc)
        nsel.append(jnp.where(sc == -jnp.inf, 0.0, 1.0).T.astype(jnp.bfloat16))
    for h in range(N_KV_HEADS):
        r0 = h * GROUP * tq
        lhs_ref[r0:r0 + GROUP * tq, :LANES] = qpb[r0:r0 + GROUP * tq]
        for g in range(GROUP):
            lhs_ref[r0 + g * tq:r0 + (g + 1) * tq, LANES:] = nsel[h]

    m_ref[...] = jnp.full(m_ref.shape, -3e38, jnp.float32)
    l_ref[...] = jnp.zeros(l_ref.shape, jnp.float32)
    acc_ref[...] = jnp.zeros(acc_ref.shape, jnp.float32)

    def sel_tile(k0, size, causal):
        kblk = (k0 + _iota((LANES, size), 1)) >> SLC_SHIFT
        ebig = jnp.where(kblk == _iota((LANES, size), 0), NEG_INF, 0.0).astype(jnp.bfloat16)
        rhs = jnp.concatenate([ks_ref[0, :, pl.ds(k0, size)], ebig], axis=0)
        st = _dot(lhs_ref[...], rhs)
        if causal:
            st = jnp.where(k0 + _iota(st.shape, 1) <= qpos_of(st.shape), st, NEG_INF)
        cols = [st[:, c * LANES:(c + 1) * LANES] for c in range(size // LANES)]
        mt = functools.reduce(jnp.maximum, cols)
        m_prev = m_ref[...]
        m_new = jnp.maximum(m_prev, jnp.max(mt, axis=1, keepdims=True))
        alpha = jnp.exp2(m_prev - m_new)
        ps = [jnp.exp2(c - m_new) for c in cols]
        l_ref[...] = alpha * l_ref[...] + functools.reduce(jnp.add, ps)
        pb = jnp.concatenate(ps, axis=1).astype(jnp.bfloat16)
        acc_ref[...] = alpha * acc_ref[...] + _dot_nt(pb, vs_ref[0, :, pl.ds(k0, size)])
        m_ref[...] = m_new

    n_full = q0 // SEL_TILE
    tail0 = n_full * SEL_TILE
    n_tail = (q0 + tq - tail0 + SEL_TAIL_TILE - 1) // SEL_TAIL_TILE

    def full_body(kt, carry):
        sel_tile(pl.multiple_of(kt * SEL_TILE, SEL_TILE), SEL_TILE, False)
        return carry

    def tail_body(r, carry):
        sel_tile(pl.multiple_of(tail0 + r * SEL_TAIL_TILE, SEL_TAIL_TILE), SEL_TAIL_TILE, True)
        return carry

    lax.fori_loop(0, n_full, full_body, 0)
    lax.fori_loop(0, n_tail, tail_body, 0)
    o_sel = acc_ref[...] / jnp.sum(l_ref[...], axis=1, keepdims=True)

    w0 = pl.multiple_of(jnp.maximum(q0 - WINDOW, 0), tq)
    sw = _dot(qpb, kw_ref[0, :, pl.ds(w0, WINDOW + tq)])
    kpos = w0 + _iota(sw.shape, 1)
    qpos = qpos_of(sw.shape)
    e_win, inv_win = _masked_exp2(sw, (kpos <= qpos) & (kpos > qpos - WINDOW))
    o_win = _dot_nt(e_win.astype(jnp.bfloat16), vw_ref[0, :, pl.ds(w0, WINDOW + tq)]) * inv_win

    o_ref[0] = _combine((o_cmp, o_sel, o_win), gn_ref[0], gexp_ref[...], tq)


def nsa_prompt_attention(q, gn, ck, cv, ks, vs, kw, vw, *, tq=LANES):
    B, S, _ = q.shape
    n_chunks = ck.shape[1]
    assert S % SEL_TILE == 0 and SEL_TILE % SEL_TAIL_TILE == 0 and SEL_TAIL_TILE % tq == 0
    assert S // SLC_BLK <= LANES and tq == LANES and S >= WINDOW + tq
    wcst = jnp.asarray(cmp_to_slc_matrix(n_chunks, LANES).T, jnp.bfloat16)
    gexp = gate_expand_matrix()
    rows = N_HEADS * tq
    per_b = lambda b, i: (b, 0, 0)
    tile = lambda b, i: (b, i, 0)
    const = lambda b, i: (0, 0)
    return pl.pallas_call(
        functools.partial(_nsa_prompt_kernel, tq=tq, n_chunks=n_chunks),
        grid=(B, S // tq),
        in_specs=[
            pl.BlockSpec((1, tq, NSA_WIDTH), tile),
            pl.BlockSpec((1, tq, LANES), tile),
            pl.BlockSpec((1, n_chunks, KV_WIDTH), per_b),
            pl.BlockSpec((1, n_chunks, KV_WIDTH), per_b),
            pl.BlockSpec((1, KV_WIDTH, S), per_b),
            pl.BlockSpec((1, KV_WIDTH, S), per_b),
            pl.BlockSpec((1, KV_WIDTH, S), per_b),
            pl.BlockSpec((1, KV_WIDTH, S), per_b),
            pl.BlockSpec(wcst.shape, const),
            pl.BlockSpec(gexp.shape, const),
        ],
        out_specs=pl.BlockSpec((1, tq, NSA_WIDTH), tile),
        out_shape=jax.ShapeDtypeStruct((B, S, NSA_WIDTH), jnp.float32),
        scratch_shapes=[
            pltpu.VMEM((rows, 2 * LANES), jnp.bfloat16),
            pltpu.VMEM((rows, LANES), jnp.float32),
            pltpu.VMEM((rows, LANES), jnp.float32),
            pltpu.VMEM((rows, LANES), jnp.float32),
        ],
        compiler_params=pltpu.CompilerParams(
            dimension_semantics=("arbitrary", "arbitrary"), vmem_limit_bytes=VMEM_LIMIT_BYTES),
        name="nsa_prompt",
    )(q, gn, ck, cv, ks, vs, kw, vw, wcst, gexp)


def _padded_queries(q, tq):
    lane = _iota((tq, LANES), 1)
    blocks = []
    for h in range(N_KV_HEADS):
        for g in range(GROUP):
            c = h * GROUP + g
            pair = q[:, LANES * (c // 2):LANES * (c // 2 + 1)]
            if c % 2 != h:
                pair = pltpu.roll(pair, HEAD_DIM, axis=1)
            keep = (lane >= HEAD_DIM * h) & (lane < HEAD_DIM * (h + 1))
            blocks.append(jnp.where(keep, pair, 0.0))
    return jnp.concatenate(blocks, axis=0)


def _masked_softmax(s, valid):
    sm = jnp.where(valid, s, NEG_INF)
    mx = jnp.max(sm, axis=1, keepdims=True)
    e = jnp.where(valid, jnp.exp(sm - mx), 0.0)
    den = jnp.maximum(jnp.sum(e, axis=1, keepdims=True), 1e-30)
    return e / den


def _masked_softmax2(s2, valid):
    e, inv = _masked_exp2(s2, valid)
    return e * inv


def _masked_exp2(s2, valid):
    sm = jnp.where(valid, s2, NEG_INF)
    mx = jnp.max(sm, axis=1, keepdims=True)
    e = jnp.exp2(sm - mx)
    return e, jnp.where(mx > 0.5 * NEG_INF, 1.0 / jnp.sum(e, axis=1, keepdims=True), 0.0)


def _group_sum(p, tq):
    out = []
    for h in range(N_KV_HEADS):
        r0 = h * GROUP * tq
        acc = p[r0:r0 + tq]
        for g in range(1, GROUP):
            acc = acc + p[r0 + g * tq:r0 + (g + 1) * tq]
        out.append(acc)
    return out


def _combine(branches, gn, gexp_w, tq):
    hi, lo = _split_bf16(gn)
    gexp = _dot(hi, gexp_w) + _dot(lo, gexp_w)
    lane = _iota((tq, LANES), 1)
    cols = []
    for h in range(N_KV_HEADS):
        blks = []
        for g in range(GROUP):
            j = h * GROUP + g
            rows = slice(j * tq, (j + 1) * tq)
            o = None
            for br in range(N_BRANCH):
                gate = gexp[:, (j * N_BRANCH + br) * LANES:(j * N_BRANCH + br + 1) * LANES]
                term = gate * branches[br][rows]
                o = term if o is None else o + term
            blks.append(o)
        for jj in range(GROUP // 2):
            a, b2 = blks[2 * jj], blks[2 * jj + 1]
            if h == 0:
                b2 = pltpu.roll(b2, HEAD_DIM, axis=1)
            else:
                a = pltpu.roll(a, HEAD_DIM, axis=1)
            cols.append(jnp.where(lane < HEAD_DIM, a, b2))
    return jnp.concatenate(cols, axis=1)


def gate_expand_matrix():
    e = np.zeros((LANES, N_HEADS * N_BRANCH * LANES), np.float32)
    for j in range(N_HEADS * N_BRANCH):
        e[j, j * LANES:(j + 1) * LANES] = 1.0
    return jnp.asarray(e, jnp.bfloat16)


def _nsa_sample_kernel(pt_ref, q_ref, gn_ref, ck_ref, cv_ref, wk_ref, wv_ref,
                       ksn_ref, vsn_ref, kwn_ref, vwn_ref, pool_k, pool_v, ebig_ref, wcst_ref, gexp_ref,
                       o_ref, nwk_ref, nwv_ref, buf, sem, *, n_pages, n_new, tq, n_seq):
    slot = pl.program_id(0) % 2
    past = n_pages * PAGE_SIZE
    n_chunks = n_pages * CHUNKS_PER_PAGE
    n_past_blk = past // SLC_BLK
    n_slc = -(-(past + n_new) // SLC_BLK)
    wb = wk_ref.shape[2]
    rows = N_HEADS * tq
    wait_pages = _page_gather(pt_ref, (pool_k, pool_v), buf, sem, n_pages, n_seq)

    def tok_of(shape):
        return _iota(shape, 0) & (tq - 1)

    def new_rows(ref, si):
        x = jnp.concatenate([ref[si], jnp.zeros((LANES - tq, KV_WIDTH), jnp.float32)], axis=0)
        return x.astype(jnp.bfloat16)

    def compressed(si):
        qpb = _padded_queries(q_ref[si] * ATTN_SCALE, tq).astype(jnp.bfloat16)
        s = _dot_nt(qpb, ck_ref[si])
        n_idx = _iota(s.shape, 1)
        valid = (n_idx * CMP_STRIDE + (CMP_LEN - 1) <= past + tok_of(s.shape)) & (n_idx < n_chunks - 1)
        p_cmp = _masked_softmax(s, valid)
        return qpb, p_cmp, _dot(p_cmp.astype(jnp.bfloat16), cv_ref[si])

    def select(p_cmps):
        ncol = wcst_ref.shape[0]
        groups = [g for p in p_cmps for g in _group_sum(p, tq)]
        psum = jnp.concatenate(groups + [jnp.zeros((LANES - len(groups) * tq, n_chunks), jnp.float32)], axis=0)
        hi, lo = _split_bf16(psum)
        wcst = wcst_ref[...]
        sc = _dot_nt(wcst, hi) + _dot_nt(wcst, lo)
        blk = _iota((ncol, LANES), 0)
        blkf = blk.astype(jnp.float32)
        qblk = (past + (_iota((ncol, LANES), 1) & (tq - 1))) >> SLC_SHIFT
        forced = (blk == 0) | (blk == qblk) | (blk == qblk - 1)
        sc = jnp.where(blk <= qblk, jnp.where(forced, sc + FORCE_BONUS, sc), NEG_INF)
        sc = jnp.where(blk < n_slc, sc, -jnp.inf)
        sel_t = jnp.zeros_like(sc)
        for _ in range(min(TOP_N, n_slc)):
            mx = jnp.max(sc, axis=0, keepdims=True)
            first = jnp.min(jnp.where(sc == mx, blkf, float(ncol)), axis=0, keepdims=True)
            hit = blkf == first
            sel_t = jnp.where(hit, 1.0, sel_t)
            sc = jnp.where(hit, -jnp.inf, sc)
        sel = sel_t.T
        return [[sel[(si * N_KV_HEADS + h) * tq:(si * N_KV_HEADS + h + 1) * tq] for h in range(N_KV_HEADS)]
                for si in range(n_seq)]

    def window_and_operands(si, qpb, sel):
        nsel = [jnp.concatenate([1.0 - sel[h][:, :LANES]] * GROUP, axis=0).astype(jnp.bfloat16) for h in range(N_KV_HEADS)]
        sel_new = [jnp.broadcast_to(sel[h][:, n_past_blk:n_past_blk + 1], (tq, LANES)) for h in range(N_KV_HEADS)]
        lhs = jnp.concatenate(
            [jnp.concatenate([qpb[h * GROUP * tq:(h + 1) * GROUP * tq], nsel[h]], axis=1)
             for h in range(N_KV_HEADS)], axis=0)
        sel_new_rows = jnp.concatenate([sel_new[h] for h in range(N_KV_HEADS) for _ in range(GROUP)], axis=0)

        tn = _iota((rows, LANES), 1)
        new_ok = (tn <= tok_of((rows, LANES))) & (tn < n_new)
        s_sn = jnp.where(new_ok & (sel_new_rows > 0.5), _dot_nt(qpb, new_rows(ksn_ref, si)), NEG_INF)
        s_wn = jnp.where(new_ok, _dot_nt(qpb, new_rows(kwn_ref, si)), NEG_INF)

        wk = wk_ref[si]
        wv = wv_ref[si]
        s_w = _dot(qpb, wk.astype(jnp.bfloat16))
        kwpos = past - wb + _iota(s_w.shape, 1)
        qpos = past + tok_of(s_w.shape)
        s_w = jnp.where((kwpos >= 0) & (kwpos <= qpos) & (kwpos > qpos - WINDOW), s_w, NEG_INF)
        mx = jnp.maximum(jnp.max(s_w, axis=1, keepdims=True), jnp.max(s_wn, axis=1, keepdims=True))
        e_w = jnp.exp(s_w - mx)
        e_wn = jnp.exp(s_wn - mx)
        den = jnp.sum(e_w, axis=1, keepdims=True) + jnp.sum(e_wn, axis=1, keepdims=True)
        o_win = (_dot_nt(e_w.astype(jnp.bfloat16), wv.astype(jnp.bfloat16))
                 + _dot(e_wn.astype(jnp.bfloat16), new_rows(vwn_ref, si))) / den

        col = _iota((KV_WIDTH, wb), 1)
        for src, new_ref, dst in ((wk, kwn_ref, nwk_ref), (wv, vwn_ref, nwv_ref)):
            new_t = jnp.concatenate([new_ref[si], jnp.zeros((LANES - tq, KV_WIDTH), jnp.float32)], axis=0).T
            tail = jnp.concatenate([jnp.zeros((KV_WIDTH, wb - LANES), jnp.float32), new_t], axis=1)
            dst[si] = jnp.where(col >= wb - n_new, pltpu.roll(tail, LANES - n_new, axis=1),
                                pltpu.roll(src, wb - n_new, axis=1))
        return lhs, s_sn, o_win

    def selected_and_combine(si, lhs, s_sn, o_cmp, o_win):
        ck_keys = min(SAMPLE_KEY_CHUNK, past)
        s_chunks = []
        for c in range(past // ck_keys):
            kc = buf[slot, si, 0, :, pl.ds(c * ck_keys, ck_keys)].astype(jnp.bfloat16)
            rhs = jnp.concatenate([kc, ebig_ref[:, pl.ds(c * ck_keys, ck_keys)]], axis=0)
            s_chunks.append(_dot(lhs, rhs))
        mx = jnp.max(s_sn, axis=1, keepdims=True)
        for sc in s_chunks:
            mx = jnp.maximum(mx, jnp.max(sc, axis=1, keepdims=True))
        e_sn = jnp.exp(s_sn - mx)
        den = jnp.sum(e_sn, axis=1, keepdims=True)
        o_sel = _dot(e_sn.astype(jnp.bfloat16), new_rows(vsn_ref, si))
        for c, sc in enumerate(s_chunks):
            e = jnp.exp(sc - mx)
            den = den + jnp.sum(e, axis=1, keepdims=True)
            vc = buf[slot, si, 1, :, pl.ds(c * ck_keys, ck_keys)].astype(jnp.bfloat16)
            o_sel = o_sel + _dot_nt(e.astype(jnp.bfloat16), vc)
        o_sel = o_sel / den

        o_ref[si] = _combine((o_cmp, o_sel, o_win), gn_ref[si], gexp_ref[...], tq)

    seqs = range(n_seq)
    qpb, p_cmp, o_cmp = zip(*[compressed(si) for si in seqs])
    sel = select(p_cmp)
    staged = [window_and_operands(si, qpb[si], sel[si]) for si in seqs]
    wait_pages()
    for si in seqs:
        lhs, s_sn, o_win = staged[si]
        selected_and_combine(si, lhs, s_sn, o_cmp[si], o_win)


def nsa_sample_attention(q, gn, ck, cv, win_k, win_v, ks_new, vs_new, kw_new, vw_new,
                         pool_k, pool_v, page_table, *, tq=SUBLANES, n_seq=SAMPLE_SEQS_PER_STEP):
    DB, DS, _ = q.shape
    n_pages = page_table.shape[1]
    past = n_pages * PAGE_SIZE
    n_chunks = ck.shape[1]
    wb = win_k.shape[2]
    n_slc = -(-(past + DS) // SLC_BLK)
    assert DS <= tq and DS < CMP_STRIDE and past // SLC_BLK <= LANES and wb % LANES == 0
    assert past % min(SAMPLE_KEY_CHUNK, past) == 0 and DB % n_seq == 0 and n_seq * N_KV_HEADS * tq <= LANES
    ncol = -(-n_slc // LANES) * LANES
    pad_t = lambda t: jnp.pad(t, ((0, 0), (0, tq - DS), (0, 0)))
    kb = np.arange(past)[None, :] // SLC_BLK
    ebig = jnp.asarray(np.where(kb == np.arange(LANES)[:, None], NEG_INF, 0.0), jnp.bfloat16)
    wcst = jnp.asarray(cmp_to_slc_matrix(n_chunks, ncol).T, jnp.bfloat16)
    gexp = gate_expand_matrix()
    per_b = lambda b, pt: (b, 0, 0)
    const = lambda b, pt: (0, 0)
    blk3 = lambda n, w: pl.BlockSpec((n_seq, n, w), per_b)
    grid_spec = pltpu.PrefetchScalarGridSpec(
        num_scalar_prefetch=1,
        grid=(DB // n_seq,),
        in_specs=[
            blk3(tq, NSA_WIDTH), blk3(tq, LANES), blk3(n_chunks, KV_WIDTH), blk3(n_chunks, KV_WIDTH),
            blk3(KV_WIDTH, wb), blk3(KV_WIDTH, wb),
            blk3(tq, KV_WIDTH), blk3(tq, KV_WIDTH), blk3(tq, KV_WIDTH), blk3(tq, KV_WIDTH),
            pl.BlockSpec(memory_space=pl.ANY), pl.BlockSpec(memory_space=pl.ANY),
            pl.BlockSpec(ebig.shape, const, pipeline_mode=pl.Buffered(1)),
            pl.BlockSpec(wcst.shape, const, pipeline_mode=pl.Buffered(1)),
            pl.BlockSpec(gexp.shape, const, pipeline_mode=pl.Buffered(1)),
        ],
        out_specs=[blk3(tq, NSA_WIDTH), blk3(KV_WIDTH, wb), blk3(KV_WIDTH, wb)],
        scratch_shapes=[
            pltpu.VMEM((2, n_seq, 2, KV_WIDTH, past), jnp.float32),
            pltpu.SemaphoreType.DMA((2, 2)),
        ],
    )
    o, nwk, nwv = pl.pallas_call(
        functools.partial(_nsa_sample_kernel, n_pages=n_pages, n_new=DS, tq=tq, n_seq=n_seq),
        grid_spec=grid_spec,
        out_shape=[jax.ShapeDtypeStruct((DB, tq, NSA_WIDTH), jnp.float32),
                   jax.ShapeDtypeStruct((DB, KV_WIDTH, wb), jnp.float32),
                   jax.ShapeDtypeStruct((DB, KV_WIDTH, wb), jnp.float32)],
        compiler_params=pltpu.CompilerParams(
            dimension_semantics=("arbitrary",), vmem_limit_bytes=PAGED_VMEM_LIMIT_BYTES),
        name="nsa_sample",
    )(page_table, pad_t(q), pad_t(gn), ck, cv, win_k, win_v, pad_t(ks_new), pad_t(vs_new), pad_t(kw_new),
      pad_t(vw_new), pool_k, pool_v, ebig, wcst, gexp)
    return o[:, :DS], nwk, nwv


def rope_tables(pos):
    half = HEAD_DIM // 2
    inv_freq = ROPE_THETA ** (-jnp.arange(half, dtype=jnp.float32) / half)
    ang = pos.astype(jnp.float32)[:, None] * inv_freq[None, :]
    cos, sin = jnp.cos(ang), jnp.sin(ang)
    cos_t = jnp.tile(cos, (1, LANES // half))
    sin_t = jnp.tile(jnp.concatenate([-sin, sin], axis=1), (1, LANES // HEAD_DIM))
    return cos_t, sin_t


def _rope_lanes(y, cos, sin):
    first_half = (_iota((y.shape[0], LANES), 1) & (HEAD_DIM - 1)) < HEAD_DIM // 2
    cols = []
    for c in range(y.shape[1] // LANES):
        yc = y[:, c * LANES:(c + 1) * LANES]
        rot = jnp.where(first_half, pltpu.roll(yc, LANES - HEAD_DIM // 2, axis=1),
                        pltpu.roll(yc, HEAD_DIM // 2, axis=1))
        cols.append(yc * cos + rot * sin)
    return cols[0] if len(cols) == 1 else jnp.concatenate(cols, axis=1)


def _rope_sublanes(y, cos_t, sin_t):
    hh = HEAD_DIM // 2
    out = []
    for c in range(y.shape[0] // LANES):
        yc = y[c * LANES:(c + 1) * LANES]
        rot = jnp.concatenate([yc[hh:2 * hh], yc[:hh], yc[3 * hh:], yc[2 * hh:3 * hh]], axis=0)
        out.append(yc * cos_t + rot * sin_t)
    return out[0] if len(out) == 1 else jnp.concatenate(out, axis=0)


_REST_ACTS = {"u": jax.nn.gelu, "v": jax.nn.gelu, "gn": jax.nn.sigmoid}


def _proj_rest(h, wrest_ref, refs, names):
    c0 = 0
    for ref, name in zip(refs, names):
        width = ref.shape[-1]
        act = _REST_ACTS.get(name)
        for c in range(0, width, GMLP_WIDTH):
            w = min(GMLP_WIDTH, width - c)
            y = _dot(h, wrest_ref[:, c0 + c:c0 + c + w])
            ref[:, c:c + w] = y if act is None else act(y)
        c0 += width


def _proj_rows_kernel(x_ref, g_ref, wq_ref, wkv_ref, wrest_ref, cos_ref, sin_ref, q_ref, *out_refs):
    h = _rms(x_ref[...], g_ref[...]).astype(jnp.bfloat16)
    cos, sin = cos_ref[...], sin_ref[...]
    q_ref[...] = _rope_lanes(_dot(h, wq_ref[...]), cos, sin)
    y = _dot(h, wkv_ref[...])
    for gi in range(N_KV_GROUPS):
        blk = y[:, gi * KV_WIDTH:(gi + 1) * KV_WIDTH]
        out_refs[gi][...] = _rope_lanes(blk, cos, sin) if gi % 2 == 0 else blk
    _proj_rest(h, wrest_ref, out_refs[N_KV_GROUPS:], ("gn", "u", "v", "ga", "gb"))


def _proj_seq_kernel(x_ref, g_ref, wq_ref, wkvt_ref, wrest_ref, cos_ref, sin_ref, cost_ref, sint_ref,
                     q_ref, kct_ref, vct_ref, kst_ref, vst_ref, kwt_ref, vwt_ref,
                     kst_b_ref, vst_b_ref, kwt_b_ref, vwt_b_ref, gn_ref, u_ref, v_ref, ga_ref, gb_ref):
    h = _rms(x_ref[...], g_ref[...]).astype(jnp.bfloat16)
    q_ref[...] = _rope_lanes(_dot(h, wq_ref[...]), cos_ref[...], sin_ref[...]) * (ATTN_SCALE * LOG2E)
    cos_t, sin_t = cost_ref[...], sint_ref[...]
    yt = _dot_nt(wkvt_ref[...], h)
    kv_refs = (kct_ref, vct_ref, kst_ref, vst_ref, kwt_ref, vwt_ref)
    kvb_refs = (None, None, kst_b_ref, vst_b_ref, kwt_b_ref, vwt_b_ref)
    for gi in range(N_KV_GROUPS):
        blk = yt[gi * KV_WIDTH:(gi + 1) * KV_WIDTH]
        if gi % 2 == 0:
            blk = _rope_sublanes(blk, cos_t, sin_t)
        kv_refs[gi][0] = blk
        if kvb_refs[gi] is not None:
            kvb_refs[gi][0] = blk.astype(jnp.bfloat16)
    _proj_rest(h, wrest_ref, (gn_ref, u_ref, v_ref, ga_ref, gb_ref), ("gn", "u", "v", "ga", "gb"))


def _split_w_in(w_in):
    sizes = (NSA_WIDTH,) + (KV_WIDTH,) * N_KV_GROUPS + (N_HEADS * N_BRANCH,)
    offs = np.cumsum((0,) + sizes)
    wq = w_in[:, offs[0]:offs[1]]
    wkv = w_in[:, offs[1]:offs[7]]
    wgn = w_in[:, offs[7]:offs[8]]
    wrest = w_in[:, offs[8]:]
    return wq, wkv, wgn, wrest


def mixer_proj_rows(x, g, w_in, pos):
    M = x.shape[0]
    tm = min(ROW_TILE, M)
    assert M % tm == 0
    wq, wkv, wgn, wrest = _split_w_in(w_in)
    wrest = jnp.concatenate([jnp.pad(wgn, ((0, 0), (0, LANES - wgn.shape[1]))), wrest], axis=1)
    cos, sin = rope_tables(pos)
    b16 = lambda t: t.astype(jnp.bfloat16)
    row = lambda i: (i, 0)
    cw = lambda shape: pl.BlockSpec(shape, lambda i: (0, 0), pipeline_mode=pl.Buffered(1))
    widths = (NSA_WIDTH,) + (KV_WIDTH,) * N_KV_GROUPS + (LANES, GMLP_WIDTH, GMLP_WIDTH, D_MODEL, D_MODEL)
    return pl.pallas_call(
        _proj_rows_kernel,
        grid=(M // tm,),
        in_specs=[pl.BlockSpec((tm, D_MODEL), row), cw((1, D_MODEL)), cw(wq.shape), cw(wkv.shape),
                  cw(wrest.shape), pl.BlockSpec((tm, LANES), row), pl.BlockSpec((tm, LANES), row)],
        out_specs=[pl.BlockSpec((tm, w), row) for w in widths],
        out_shape=[jax.ShapeDtypeStruct((M, w), jnp.float32) for w in widths],
        compiler_params=pltpu.CompilerParams(
            dimension_semantics=("arbitrary",), vmem_limit_bytes=VMEM_LIMIT_BYTES),
        name="mixer_proj_rows",
    )(x, g.reshape(1, -1), b16(wq), b16(wkv), b16(wrest), cos, sin)


def mixer_proj_seq(x, g, w_in, batch):
    M = x.shape[0]
    S = M // batch
    tm = min(ROW_TILE, S)
    assert S % tm == 0
    n_s = S // tm
    wq, wkv, wgn, wrest = _split_w_in(w_in)
    wrest = jnp.concatenate([jnp.pad(wgn, ((0, 0), (0, LANES - wgn.shape[1]))), wrest], axis=1)
    cos, sin = rope_tables(jnp.arange(S, dtype=jnp.int32))
    b16 = lambda t: t.astype(jnp.bfloat16)
    f32 = jnp.float32
    row = lambda i: (i, 0)
    tab = lambda i: (i % n_s, 0)
    tab_t = lambda i: (0, i % n_s)
    seq_t = lambda i: (i // n_s, 0, i % n_s)
    cw = lambda shape: pl.BlockSpec(shape, lambda i: (0, 0), pipeline_mode=pl.Buffered(1))
    t_spec = pl.BlockSpec((1, KV_WIDTH, tm), seq_t)
    t_shape = lambda dt: jax.ShapeDtypeStruct((batch, KV_WIDTH, S), dt)
    r_widths = (LANES, GMLP_WIDTH, GMLP_WIDTH, D_MODEL, D_MODEL)
    out_specs = ([pl.BlockSpec((tm, NSA_WIDTH), row)] + [t_spec] * (N_KV_GROUPS + 4)
                 + [pl.BlockSpec((tm, w), row) for w in r_widths])
    out_shape = ([jax.ShapeDtypeStruct((M, NSA_WIDTH), f32)] + [t_shape(f32)] * N_KV_GROUPS
                 + [t_shape(jnp.bfloat16)] * 4 + [jax.ShapeDtypeStruct((M, w), f32) for w in r_widths])
    return pl.pallas_call(
        _proj_seq_kernel,
        grid=(M // tm,),
        in_specs=[pl.BlockSpec((tm, D_MODEL), row), cw((1, D_MODEL)), cw(wq.shape),
                  cw((N_KV_GROUPS * KV_WIDTH, D_MODEL)), cw(wrest.shape),
                  pl.BlockSpec((tm, LANES), tab), pl.BlockSpec((tm, LANES), tab),
                  pl.BlockSpec((LANES, tm), tab_t), pl.BlockSpec((LANES, tm), tab_t)],
        out_specs=out_specs,
        out_shape=out_shape,
        compiler_params=pltpu.CompilerParams(
            dimension_semantics=("arbitrary",), vmem_limit_bytes=VMEM_LIMIT_BYTES),
        name="mixer_proj_seq",
    )(x, g.reshape(1, -1), b16(wq), b16(wkv.T), b16(wrest), cos, sin, cos.T, sin.T)


def _merge_kernel(oa_ref, u_ref, v_ref, ga_ref, gb_ref, x_ref, lng_ref, lnb_ref, wsp_ref, bsx_ref,
                  wa_ref, wb_ref, wout_ref, gc_ref, wcq_ref, x1_ref, hq_ref):
    tm = x_ref.shape[0]
    v = v_ref[...]
    vc = v - jnp.mean(v, axis=-1, keepdims=True)
    var = jnp.mean(vc * vc, axis=-1, keepdims=True)
    vn = vc * lax.rsqrt(var + NORM_EPS) * lng_ref[...] + lnb_ref[...]
    lane = _iota((GMLP_CHUNK, LANES), 1)
    chunks = []
    for c in range(tm // GMLP_CHUNK):
        cols = []
        for pr in range(GMLP_GROUPS // 2):
            pair = vn[c * GMLP_CHUNK:(c + 1) * GMLP_CHUNK, pr * LANES:(pr + 1) * LANES]
            rhs = jnp.concatenate([jnp.where(lane < GMLP_GROUP_DIM, pair, 0.0),
                                   jnp.where(lane >= GMLP_GROUP_DIM, pair, 0.0)], axis=0).astype(jnp.bfloat16)
            cols.append(_dot(wsp_ref[pr], rhs))
        chunks.append(jnp.concatenate(cols, axis=1) + bsx_ref[...])
    o_b = u_ref[...] * jnp.concatenate(chunks, axis=0)
    m = (jax.nn.sigmoid(ga_ref[...]) * _dot(oa_ref[...].astype(jnp.bfloat16), wa_ref[...])
         + jax.nn.sigmoid(gb_ref[...]) * _dot(o_b.astype(jnp.bfloat16), wb_ref[...]))
    x1 = x_ref[...] + _dot(m.astype(jnp.bfloat16), wout_ref[...])
    x1_ref[...] = x1
    hq_ref[...] = _dot(_rms(x1, gc_ref[...]).astype(jnp.bfloat16), wcq_ref[...])


def gmlp_weights(ws, bs, chunk_len):
    L = chunk_len
    w = jnp.tril(ws[:, :L, :L])
    bias = bs[:, :L]
    if L < GMLP_CHUNK:
        r = np.arange(GMLP_CHUNK)
        same_seq = (r[:, None] // L) == (r[None, :] // L)
        w = jnp.where(same_seq, w[:, r % L][:, :, r % L], 0.0)
        bias = bias[:, r % L]
    pairs = w.reshape(GMLP_GROUPS // 2, 2, GMLP_CHUNK, GMLP_CHUNK)
    wsp = jnp.swapaxes(pairs, 1, 2).reshape(GMLP_GROUPS // 2, GMLP_CHUNK, 2 * GMLP_CHUNK)
    bsx = jnp.repeat(bias.T, GMLP_GROUP_DIM, axis=1)
    return wsp.astype(jnp.bfloat16), bsx


def merge_and_cross_q(o_a, u, v, ga, gb, x, ln_g, ln_b, wsp, bsx, w_a, w_b, w_out, g_cross, w_cq):
    M = x.shape[0]
    tm = min(ROW_TILE, M)
    one = pl.Buffered(1)
    row = lambda i: (i, 0)
    cw = lambda shape: pl.BlockSpec(shape, lambda i: (0,) * len(shape), pipeline_mode=one)
    b16 = lambda t: t.astype(jnp.bfloat16)
    out = jax.ShapeDtypeStruct((M, D_MODEL), jnp.float32)
    return pl.pallas_call(
        _merge_kernel,
        grid=(M // tm,),
        in_specs=[
            pl.BlockSpec((tm, NSA_WIDTH), row), pl.BlockSpec((tm, GMLP_WIDTH), row),
            pl.BlockSpec((tm, GMLP_WIDTH), row), pl.BlockSpec((tm, D_MODEL), row),
            pl.BlockSpec((tm, D_MODEL), row), pl.BlockSpec((tm, D_MODEL), row),
            cw((1, GMLP_WIDTH)), cw((1, GMLP_WIDTH)), cw(wsp.shape), cw(bsx.shape),
            cw(w_a.shape), cw(w_b.shape), cw(w_out.shape), cw((1, D_MODEL)), cw(w_cq.shape),
        ],
        out_specs=[pl.BlockSpec((tm, D_MODEL), row)] * 2,
        out_shape=[out, out],
        compiler_params=pltpu.CompilerParams(
            dimension_semantics=("arbitrary",), vmem_limit_bytes=VMEM_LIMIT_BYTES),
        name="merge_branches",
    )(o_a, u, v, ga, gb, x, ln_g.reshape(1, -1), ln_b.reshape(1, -1), wsp, bsx,
      b16(w_a), b16(w_b), b16(w_out), g_cross.reshape(1, -1), b16(w_cq))


def _memkv_kernel(x_ref, g_ref, wk_ref, wv_ref, k_ref, v_ref):
    h = _rms(x_ref[...], g_ref[...]).astype(jnp.bfloat16)
    k_ref[...] = _dot(h, wk_ref[...])
    v_ref[...] = _dot(h, wv_ref[...])


def memory_kv(mem, g_mem, w_ck, w_cv):
    M = mem.shape[0]
    tm = min(ROW_TILE, M)
    one = pl.Buffered(1)
    row = lambda i: (i, 0)
    cw = lambda shape: pl.BlockSpec(shape, lambda i: (0, 0), pipeline_mode=one)
    out = jax.ShapeDtypeStruct((M, D_MODEL), jnp.float32)
    return pl.pallas_call(
        _memkv_kernel,
        grid=(M // tm,),
        in_specs=[pl.BlockSpec((tm, D_MODEL), row), cw((1, D_MODEL)), cw(w_ck.shape), cw(w_cv.shape)],
        out_specs=[pl.BlockSpec((tm, D_MODEL), row)] * 2,
        out_shape=[out, out],
        compiler_params=pltpu.CompilerParams(
            dimension_semantics=("arbitrary",), vmem_limit_bytes=VMEM_LIMIT_BYTES),
        name="memory_kv",
    )(mem, g_mem.reshape(1, -1), w_ck.astype(jnp.bfloat16), w_cv.astype(jnp.bfloat16))


def _cross_heads_kernel(q_ref, mk_hbm, mv_hbm, o_ref, buf, sem, *, rows, n_seq):
    b = pl.program_id(0)
    nb = pl.num_programs(0)
    slot = b % 2

    def head_copies(bb, sl):
        return [pltpu.make_async_copy(src.at[bb * n_seq + si, :, h, :], buf.at[sl, si, kv, h], sem.at[sl, kv])
                for si in range(n_seq) for kv, src in enumerate((mk_hbm, mv_hbm)) for h in range(CROSS_HEADS)]

    @pl.when(b == 0)
    def _():
        for c in head_copies(0, 0):
            c.start()

    @pl.when(b + 1 < nb)
    def _():
        for c in head_copies(b + 1, 1 - slot):
            c.start()

    pad = jnp.zeros((SUBLANES - rows, D_MODEL), jnp.float32)
    qs = [jnp.concatenate([q_ref[si] * CROSS_SCALE, pad], axis=0) for si in range(n_seq)]
    for c in head_copies(b, slot):
        c.wait()
    for si in range(n_seq):
        qh = [qs[si][:, h * CROSS_HEAD_DIM:(h + 1) * CROSS_HEAD_DIM].astype(jnp.bfloat16) for h in range(CROSS_HEADS)]
        s = jnp.concatenate([_dot_nt(qh[h], buf[slot, si, 0, h].astype(jnp.bfloat16)) for h in range(CROSS_HEADS)], axis=0)
        e = jnp.exp(s - jnp.max(s, axis=1, keepdims=True))
        p = (e / jnp.sum(e, axis=1, keepdims=True)).astype(jnp.bfloat16)
        o = jnp.concatenate([_dot(p[h * SUBLANES:(h + 1) * SUBLANES], buf[slot, si, 1, h].astype(jnp.bfloat16))
                             for h in range(CROSS_HEADS)], axis=1)
        o_ref[si] = o[:rows]


def cross_attention_cached(hq, mk, mv, *, n_seq=CROSS_SEQS_PER_STEP):
    NB, R, _ = hq.shape
    assert NB % n_seq == 0
    any_spec = pl.BlockSpec(memory_space=pl.ANY)
    return pl.pallas_call(
        functools.partial(_cross_heads_kernel, rows=R, n_seq=n_seq),
        grid=(NB // n_seq,),
        in_specs=[pl.BlockSpec((n_seq, R, D_MODEL), lambda b: (b, 0, 0)), any_spec, any_spec],
        out_specs=pl.BlockSpec((n_seq, R, D_MODEL), lambda b: (b, 0, 0)),
        out_shape=jax.ShapeDtypeStruct((NB, R, D_MODEL), jnp.float32),
        scratch_shapes=[pltpu.VMEM((2, n_seq, 2, CROSS_HEADS, N_MEM, CROSS_HEAD_DIM), jnp.float32),
                        pltpu.SemaphoreType.DMA((2, 2))],
        compiler_params=pltpu.CompilerParams(
            dimension_semantics=("arbitrary",), vmem_limit_bytes=VMEM_LIMIT_BYTES),
        name="cross_attention_cached",
    )(hq, mk, mv)


def _ffn_and_norm(x, h_gain, wg_ref, wu_ref, wd_ref, gfin_ref, y_ref, acc_ref):
    h = _rms(x, h_gain).astype(jnp.bfloat16)
    acc_ref[...] = x
    for c in range(D_FF // FF_CHUNK):
        sl = slice(c * FF_CHUNK, (c + 1) * FF_CHUNK)
        a = _dot(h, wg_ref[:, sl])
        b = _dot(h, wu_ref[:, sl])
        t = (a * jax.nn.sigmoid(a) * b).astype(jnp.bfloat16)
        acc_ref[...] += _dot(t, wd_ref[sl, :])
    y_ref[...] = _rms(acc_ref[...], gfin_ref[...])


def _memory_attention(q, mk, mv):
    q = q * CROSS_SCALE
    outs = []
    for h in range(CROSS_HEADS):
        sl = slice(h * CROSS_HEAD_DIM, (h + 1) * CROSS_HEAD_DIM)
        s = _dot_nt(q[:, sl].astype(jnp.bfloat16), mk[:, sl].astype(jnp.bfloat16))
        e = jnp.exp(s - jnp.max(s, axis=1, keepdims=True))
        p = e / jnp.sum(e, axis=1, keepdims=True)
        outs.append(_dot(p.astype(jnp.bfloat16), mv[:, sl].astype(jnp.bfloat16)))
    return jnp.concatenate(outs, axis=1)


def _tail_kernel(x_ref, o_ref, wco_ref, gf_ref, wg_ref, wu_ref, wd_ref, gfin_ref, y_ref, acc_ref):
    x = x_ref[...] + _dot(o_ref[...].astype(jnp.bfloat16), wco_ref[...])
    _ffn_and_norm(x, gf_ref[...], wg_ref, wu_ref, wd_ref, gfin_ref, y_ref, acc_ref)


def _tail_cross_kernel(x_ref, hq_ref, mk_ref, mv_ref, wco_ref, gf_ref, wg_ref, wu_ref, wd_ref, gfin_ref,
                       y_ref, acc_ref):
    o = _memory_attention(hq_ref[...], mk_ref[0], mv_ref[0])
    x = x_ref[...] + _dot(o.astype(jnp.bfloat16), wco_ref[...])
    _ffn_and_norm(x, gf_ref[...], wg_ref, wu_ref, wd_ref, gfin_ref, y_ref, acc_ref)


def _tail_call(kernel_fn, name, x1, row_inputs, batch_inputs, rows_per_batch, w_co, g_ffn, w_gate, w_up, w_down, g_final):
    M = x1.shape[0]
    tm = min(ROW_TILE, M)
    row = lambda i: (i, 0)
    cw = lambda shape: pl.BlockSpec(shape, lambda i: (0, 0), pipeline_mode=pl.Buffered(1))
    b16 = lambda t: t.astype(jnp.bfloat16)
    per_batch = lambda i: (i * tm // rows_per_batch, 0, 0)
    return pl.pallas_call(
        kernel_fn,
        grid=(M // tm,),
        in_specs=[pl.BlockSpec((tm, D_MODEL), row)] * (1 + len(row_inputs))
        + [pl.BlockSpec((1,) + t.shape[1:], per_batch) for t in batch_inputs]
        + [cw(w_co.shape), cw((1, D_MODEL)), cw(w_gate.shape), cw(w_up.shape), cw(w_down.shape), cw((1, D_MODEL))],
        out_specs=pl.BlockSpec((tm, D_MODEL), row),
        out_shape=jax.ShapeDtypeStruct((M, D_MODEL), jnp.float32),
        scratch_shapes=[pltpu.VMEM((tm, D_MODEL), jnp.float32)],
        compiler_params=pltpu.CompilerParams(
            dimension_semantics=("arbitrary",), vmem_limit_bytes=VMEM_LIMIT_BYTES),
        name=name,
    )(x1, *row_inputs, *batch_inputs, b16(w_co), g_ffn.reshape(1, -1), b16(w_gate), b16(w_up), b16(w_down),
      g_final.reshape(1, -1))


def tail(x1, o, *weights):
    return _tail_call(_tail_kernel, "tail", x1, (o,), (), x1.shape[0], *weights)


def tail_with_memory(x1, hq, mk, mv, *weights):
    rows_per_batch = x1.shape[0] // mk.shape[0]
    assert rows_per_batch % min(ROW_TILE, x1.shape[0]) == 0
    return _tail_call(_tail_cross_kernel, "tail_with_memory", x1, (hq,), (mk, mv), rows_per_batch, *weights)


def kernel(x_prompt, x_sample, mem_prompt, cache_cmp_k, cache_cmp_v, cache_slc_k, cache_slc_v,
           page_table, state_win_k, state_win_v, cache_mem_k, cache_mem_v,
           g_mix, w_in, cmp_pe, cmp_w1, cmp_w2, gmlp_ln_g, gmlp_ln_b, gmlp_ws, gmlp_bs,
           w_branch_a, w_branch_b, w_out, g_cross, g_mem, w_cq, w_ck, w_cv, w_co,
           g_ffn, w_gate, w_up, w_down, g_final):
    B, S, _ = x_prompt.shape
    DB, DS, _ = x_sample.shape
    n_pages = page_table.shape[1]
    assert g_mix.shape[0] == 1, "single-layer step"
    l = 0
    cmpw = cmp_weights(cmp_pe[l], cmp_w1[l], cmp_w2[l])
    merge_w = (gmlp_ln_g[l], gmlp_ln_b[l])
    merge_tail = (w_branch_a[l], w_branch_b[l], w_out[l], g_cross[l], w_cq[l])
    ffn_w = (w_co[l], g_ffn[l], w_gate[l], w_up[l], w_down[l], g_final)
    to_t = lambda t: jnp.transpose(t, (0, 2, 3, 1)).reshape(t.shape[0], KV_WIDTH, t.shape[1])
    from_t = lambda t: jnp.transpose(t.reshape(t.shape[0], N_KV_HEADS, HEAD_DIM, t.shape[2]), (0, 3, 1, 2))
    mem3 = lambda t: t.reshape(-1, N_MEM, D_MODEL)

    xp = x_prompt.reshape(B * S, D_MODEL)
    (q, kct, vct, kst, vst, kwt, vwt, kst_b, vst_b, kwt_b, vwt_b, gn, u, v, ga, gb) = mixer_proj_seq(
        xp, g_mix[l], w_in[l], B)
    ck, cv = cmp_blocks_dense(kct, vct, *cmpw)
    o_a = nsa_prompt_attention(q.reshape(B, S, NSA_WIDTH), gn.reshape(B, S, LANES), ck, cv,
                               kst_b, vst_b, kwt_b, vwt_b)
    x1, hq = merge_and_cross_q(o_a.reshape(B * S, NSA_WIDTH), u, v, ga, gb, xp, *merge_w,
                               *gmlp_weights(gmlp_ws[l], gmlp_bs[l], min(S, GMLP_CHUNK)), *merge_tail)
    mk, mv = memory_kv(mem_prompt.reshape(B * N_MEM, D_MODEL), g_mem[l], w_ck[l], w_cv[l])
    y_prompt = tail_with_memory(x1, hq, mem3(mk), mem3(mv), *ffn_w).reshape(B, S, D_MODEL)
    wb_p = min(WINDOW, S)
    mem5 = lambda t: t.reshape(B, N_MEM, CROSS_HEADS, CROSS_HEAD_DIM)
    prompt_new = (from_t(kct), from_t(vct), from_t(kst), from_t(vst),
                  from_t(kwt[:, :, S - wb_p:]), from_t(vwt[:, :, S - wb_p:]), mem5(mk), mem5(mv))

    xs = x_sample.reshape(DB * DS, D_MODEL)
    pos_s = n_pages * PAGE_SIZE + jnp.arange(DS, dtype=jnp.int32)
    (q, kc, vc, ks, vs, kw, vw, gn, u, v, ga, gb) = mixer_proj_rows(xs, g_mix[l], w_in[l], jnp.tile(pos_s, DB))
    ck, cv = cmp_blocks_paged(to_t(cache_cmp_k[l]), to_t(cache_cmp_v[l]), page_table, *cmpw)
    per_seq = lambda t: t.reshape(DB, DS, -1)
    o_a, win_kt, win_vt = nsa_sample_attention(
        per_seq(q), per_seq(gn), ck, cv, to_t(state_win_k[l]), to_t(state_win_v[l]),
        per_seq(ks), per_seq(vs), per_seq(kw), per_seq(vw),
        to_t(cache_slc_k[l]), to_t(cache_slc_v[l]), page_table)
    x1, hq = merge_and_cross_q(o_a.reshape(DB * DS, NSA_WIDTH), u, v, ga, gb, xs, *merge_w,
                               *gmlp_weights(gmlp_ws[l], gmlp_bs[l], min(DS, GMLP_CHUNK)), *merge_tail)
    o = cross_attention_cached(per_seq(hq), cache_mem_k[l], cache_mem_v[l])
    y_sample = tail(x1, o.reshape(DB * DS, D_MODEL), *ffn_w).reshape(DB, DS, D_MODEL)
    heads = lambda t: t.reshape(DB, DS, N_KV_HEADS, HEAD_DIM)
    sample_new = (heads(kc), heads(vc), heads(ks), heads(vs), from_t(win_kt), from_t(win_vt), per_seq(v))

    return (y_prompt, y_sample) + tuple(t[None] for t in prompt_new + sample_new)
```

```python
import functools

import jax
import jax.numpy as jnp
import numpy as np
from jax import lax
from jax.experimental import pallas as pl
from jax.experimental.pallas import tpu as pltpu

D_MODEL = 1024
N_HEADS = 8
HEAD_DIM = 64
N_KV_HEADS = 2
GROUP = N_HEADS // N_KV_HEADS
NSA_WIDTH = N_HEADS * HEAD_DIM
KV_WIDTH = N_KV_HEADS * HEAD_DIM
CMP_LEN = 32
CMP_STRIDE = 16
CMP_HIDDEN = 128
SLC_BLK = 64
SLC_SHIFT = 6
TOP_N = 16
WINDOW = 512
ROPE_THETA = 10000.0
ATTN_SCALE = HEAD_DIM ** -0.5
LOG2E = 1.4426950408889634
FORCE_BONUS = 1e4
NEG_INF = -1e30
N_BRANCH = 3
GMLP_WIDTH = 512
GMLP_GROUPS = 8
GMLP_GROUP_DIM = GMLP_WIDTH // GMLP_GROUPS
GMLP_CHUNK = 128
N_MEM = 256
CROSS_HEADS = 4
CROSS_HEAD_DIM = D_MODEL // CROSS_HEADS
CROSS_SCALE = CROSS_HEAD_DIM ** -0.5
D_FF = -(-8 * D_MODEL // (3 * 256)) * 256
NORM_EPS = 1e-6
PAGE_SIZE = 128
CHUNKS_PER_PAGE = PAGE_SIZE // CMP_STRIDE
N_KV_GROUPS = 6

LANES = 128
SUBLANES = 8
VMEM_LIMIT_BYTES = 48 * 1024 * 1024
PAGED_VMEM_LIMIT_BYTES = 56 * 1024 * 1024
SEL_TILE = 1024
SEL_TAIL_TILE = 512
SAMPLE_KEY_CHUNK = 2048
SAMPLE_SEQS_PER_STEP = 2
CROSS_SEQS_PER_STEP = 4
ROW_TILE = 512
FF_CHUNK = 256

_NT = (((1,), (1,)), ((), ()))


def _dot_nt(a, b):
    return lax.dot_general(a, b, _NT, preferred_element_type=jnp.float32)


def _dot(a, b):
    return jnp.dot(a, b, preferred_element_type=jnp.float32)


def _split_bf16(x):
    hi = x.astype(jnp.bfloat16)
    lo = (x - hi.astype(jnp.float32)).astype(jnp.bfloat16)
    return hi, lo


def _iota(shape, dim):
    return lax.broadcasted_iota(jnp.int32, shape, dim)


def _rms(x, g):
    return x * lax.rsqrt(jnp.mean(x * x, axis=-1, keepdims=True) + NORM_EPS) * g


def _cmp_from_transposed(xt_ref, p, xs, wbig_ref, w2big_ref, peterm_ref, out_ref, n_rows):
    n_chunks = n_rows // CMP_STRIDE
    pitch = xs.shape[0] // CMP_STRIDE
    per_blk = LANES // CMP_STRIDE
    for j in range(n_rows // LANES):
        blk = xt_ref[:, j * LANES:(j + 1) * LANES].T
        for k in range(LANES // SUBLANES):
            c, s0 = k // 2, (k % 2) * SUBLANES
            xs[pl.ds(s0 * pitch + j * per_blk + c, SUBLANES, stride=pitch), :] = blk[k * SUBLANES:(k + 1) * SUBLANES]
    acc = jnp.zeros((n_chunks, 4 * CMP_HIDDEN), jnp.float32)
    for sp in range(CMP_STRIDE // 2):
        a0 = xs[pl.ds(2 * sp * pitch, n_chunks), :]
        a1 = xs[pl.ds((2 * sp + 1) * pitch, n_chunks), :]
        lhs = jnp.concatenate([a0, a1], axis=1).astype(jnp.bfloat16)
        acc = acc + _dot(lhs, wbig_ref[p, sp])
    lead = acc[:, :2 * CMP_HIDDEN]
    trail_next = pltpu.roll(acc[:, 2 * CMP_HIDDEN:], n_chunks - 1, axis=0)
    pre = lead + trail_next + peterm_ref[p]
    out = _dot(jax.nn.gelu(pre).astype(jnp.bfloat16), w2big_ref[p])
    out_ref[0] = out.astype(out_ref.dtype)


def _regroup_scratch(n_rows):
    n_chunks = n_rows // CMP_STRIDE
    pitch = n_chunks + SUBLANES if n_chunks % 32 == 0 else n_chunks
    return pltpu.VMEM((CMP_STRIDE * pitch, KV_WIDTH), jnp.float32)


def _page_gather(pt_ref, pools, buf, sem, n_pages, n_seq=1):
    b = pl.program_id(0)
    nb = pl.num_programs(0)
    slot = b % 2

    def copies(bb, sl):
        return [pltpu.make_async_copy(pools[p].at[pt_ref[bb * n_seq + si, j]],
                                      buf.at[sl, si, p, :, pl.ds(j * PAGE_SIZE, PAGE_SIZE)], sem.at[sl, p])
                for si in range(n_seq) for p in range(len(pools)) for j in range(n_pages)]

    @pl.when(b == 0)
    def _():
        for c in copies(0, 0):
            c.start()

    @pl.when(b + 1 < nb)
    def _():
        for c in copies(b + 1, 1 - slot):
            c.start()

    def wait():
        for c in copies(b, slot):
            c.wait()

    return wait


def _cmp_paged_kernel(pt_ref, pool_k, pool_v, wbig_ref, w2big_ref, peterm_ref, ck_ref, cv_ref,
                      buf, xs, sem, *, n_pages):
    wait = _page_gather(pt_ref, (pool_k, pool_v), buf, sem, n_pages)
    wait()
    slot = pl.program_id(0) % 2
    for p, out_ref in ((0, ck_ref), (1, cv_ref)):
        _cmp_from_transposed(buf.at[slot, 0, p], p, xs, wbig_ref, w2big_ref, peterm_ref, out_ref,
                             n_pages * PAGE_SIZE)


def _cmp_dense_kernel(kt_ref, vt_ref, wbig_ref, w2big_ref, peterm_ref, ck_ref, cv_ref, xs):
    for p, (src, out_ref) in enumerate(((kt_ref, ck_ref), (vt_ref, cv_ref))):
        _cmp_from_transposed(src.at[0], p, xs, wbig_ref, w2big_ref, peterm_ref, out_ref, src.shape[2])


def cmp_weights(cmp_pe, cmp_w1, cmp_w2):
    eye = jnp.eye(N_KV_HEADS, dtype=jnp.float32)
    w1r = cmp_w1.reshape(2, 2, CMP_STRIDE // 2, 2, HEAD_DIM, CMP_HIDDEN)
    wbig = jnp.einsum('jtpsdf,hk->jpshdtkf', w1r, eye)
    wbig = wbig.reshape(2, CMP_STRIDE // 2, 2 * KV_WIDTH, 4 * CMP_HIDDEN).astype(jnp.bfloat16)
    w2big = jnp.einsum('jfd,hk->jhfkd', cmp_w2, eye).reshape(2, 2 * CMP_HIDDEN, KV_WIDTH)
    pe_term = jnp.einsum('jsd,jsdf->jf', cmp_pe, cmp_w1)
    peterm = jnp.tile(pe_term[:, None, :], (1, 1, N_KV_HEADS))
    return wbig, w2big.astype(jnp.bfloat16), peterm


def _cmp_weight_specs(wbig, w2big, peterm, nargs):
    c3 = (lambda b: (0, 0, 0)) if nargs == 1 else (lambda b, pt: (0, 0, 0))
    c4 = (lambda b: (0, 0, 0, 0)) if nargs == 1 else (lambda b, pt: (0, 0, 0, 0))
    one = pl.Buffered(1)
    return [pl.BlockSpec(wbig.shape, c4, pipeline_mode=one),
            pl.BlockSpec(w2big.shape, c3, pipeline_mode=one),
            pl.BlockSpec(peterm.shape, c3, pipeline_mode=one)]


def cmp_blocks_paged(pool_k, pool_v, page_table, wbig, w2big, peterm):
    n_seq, n_pages = page_table.shape
    n_rows = n_pages * PAGE_SIZE
    n_chunks = n_rows // CMP_STRIDE
    out = jax.ShapeDtypeStruct((n_seq, n_chunks, KV_WIDTH), jnp.bfloat16)
    grid_spec = pltpu.PrefetchScalarGridSpec(
        num_scalar_prefetch=1,
        grid=(n_seq,),
        in_specs=[pl.BlockSpec(memory_space=pl.ANY), pl.BlockSpec(memory_space=pl.ANY)]
        + _cmp_weight_specs(wbig, w2big, peterm, 2),
        out_specs=[pl.BlockSpec((1, n_chunks, KV_WIDTH), lambda b, pt: (b, 0, 0))] * 2,
        scratch_shapes=[
            pltpu.VMEM((2, 1, 2, KV_WIDTH, n_rows), jnp.float32),
            _regroup_scratch(n_rows),
            pltpu.SemaphoreType.DMA((2, 2)),
        ],
    )
    return pl.pallas_call(
        functools.partial(_cmp_paged_kernel, n_pages=n_pages),
        grid_spec=grid_spec,
        out_shape=[out, out],
        compiler_params=pltpu.CompilerParams(
            dimension_semantics=("arbitrary",), vmem_limit_bytes=PAGED_VMEM_LIMIT_BYTES),
        name="cmp_blocks_paged",
    )(page_table, pool_k, pool_v, wbig, w2big, peterm)


def cmp_blocks_dense(kt, vt, wbig, w2big, peterm):
    B, _, S = kt.shape
    n_chunks = S // CMP_STRIDE
    per_b = lambda b: (b, 0, 0)
    return pl.pallas_call(
        _cmp_dense_kernel,
        grid=(B,),
        in_specs=[pl.BlockSpec((1, KV_WIDTH, S), per_b), pl.BlockSpec((1, KV_WIDTH, S), per_b)]
        + _cmp_weight_specs(wbig, w2big, peterm, 1),
        out_specs=[pl.BlockSpec((1, n_chunks, KV_WIDTH), per_b)] * 2,
        out_shape=[jax.ShapeDtypeStruct((B, n_chunks, KV_WIDTH), jnp.bfloat16)] * 2,
        scratch_shapes=[_regroup_scratch(S)],
        compiler_params=pltpu.CompilerParams(
            dimension_semantics=("arbitrary",), vmem_limit_bytes=VMEM_LIMIT_BYTES),
        name="cmp_blocks_dense",
    )(kt, vt, wbig, w2big, peterm)


def cmp_to_slc_matrix(n_chunks, n_cols):
    cs = np.arange(n_chunks)[:, None] * CMP_STRIDE
    ss = np.arange(n_cols)[None, :] * SLC_BLK
    shared = np.minimum(cs + CMP_LEN, ss + SLC_BLK) - np.maximum(cs, ss)
    w = np.maximum(shared, 0).astype(np.float32) / CMP_LEN
    w[n_chunks - 1] = 0.0
    return w


def _nsa_prompt_kernel(q_ref, gn_ref, ck_ref, cv_ref, ks_ref, vs_ref, kw_ref, vw_ref,
                       wcst_ref, o_ref, lhs_ref, m_ref, l_ref, acc_ref, *, tq, n_chunks):
    i = pl.program_id(1)
    q0 = i * tq
    qpb = _padded_queries(q_ref[0], tq).astype(jnp.bfloat16)

    def qpos_of(shape):
        return q0 + (_iota(shape, 0) & (tq - 1))

    s = _dot_nt(qpb, ck_ref[0])
    n_idx = _iota(s.shape, 1)
    valid = (n_idx * CMP_STRIDE + (CMP_LEN - 1) <= qpos_of(s.shape)) & (n_idx < n_chunks - 1)
    p_cmp = _masked_softmax2(s, valid)
    o_cmp = _dot(p_cmp.astype(jnp.bfloat16), cv_ref[0])

    blk = _iota((LANES, tq), 0)
    blkf = blk.astype(jnp.float32)
    qblk = (q0 + _iota((LANES, tq), 1)) >> SLC_SHIFT
    forced = (blk == 0) | (blk == qblk) | (blk == qblk - 1)
    wcst = wcst_ref[...]
    nsel = []
    for psum in _group_sum(p_cmp, tq):
        hi, lo = _split_bf16(psum)
        sc = _dot_nt(wcst, hi) + _dot_nt(wcst, lo)
        sc = jnp.where(blk <= qblk, jnp.where(forced, sc + FORCE_BONUS, sc), NEG_INF)
        for _ in range(TOP_N):
            mx = jnp.max(sc, axis=0, keepdims=True)
            first = jnp.min(jnp.where(sc == mx, blkf, float(LANES)), axis=0, keepdims=True)
            sc = jnp.where(blkf == first, -jnp.inf, sc)
        nsel.append(jnp.where(sc == -jnp.inf, 0.0, 1.0).T.astype(jnp.bfloat16))
    for h in range(N_KV_HEADS):
        r0 = h * GROUP * tq
        lhs_ref[r0:r0 + GROUP * tq, :LANES] = qpb[r0:r0 + GROUP * tq]
        for g in range(GROUP):
            lhs_ref[r0 + g * tq:r0 + (g + 1) * tq, LANES:] = nsel[h]

    m_ref[...] = jnp.full(m_ref.shape, -3e38, jnp.float32)
    l_ref[...] = jnp.zeros(l_ref.shape, jnp.float32)
    acc_ref[...] = jnp.zeros(acc_ref.shape, jnp.float32)

    def sel_tile(k0, size, causal):
        kblk = (k0 + _iota((LANES, size), 1)) >> SLC_SHIFT
        ebig = jnp.where(kblk == _iota((LANES, size), 0), NEG_INF, 0.0).astype(jnp.bfloat16)
        rhs = jnp.concatenate([ks_ref[0, :, pl.ds(k0, size)], ebig], axis=0)
        st = _dot(lhs_ref[...], rhs)
        if causal:
            st = jnp.where(k0 + _iota(st.shape, 1) <= qpos_of(st.shape), st, NEG_INF)
        cols = [st[:, c * LANES:(c + 1) * LANES] for c in range(size // LANES)]
        mt = functools.reduce(jnp.maximum, cols)
        m_prev = m_ref[...]
        m_new = jnp.maximum(m_prev, jnp.max(mt, axis=1, keepdims=True))
        alpha = jnp.exp2(m_prev - m_new)
        ps = [jnp.exp2(c - m_new) for c in cols]
        l_ref[...] = alpha * l_ref[...] + functools.reduce(jnp.add, ps)
        pb = jnp.concatenate(ps, axis=1).astype(jnp.bfloat16)
        acc_ref[...] = alpha * acc_ref[...] + _dot_nt(pb, vs_ref[0, :, pl.ds(k0, size)])
        m_ref[...] = m_new

    n_full = q0 // SEL_TILE
    tail0 = n_full * SEL_TILE
    n_tail = (q0 + tq - tail0 + SEL_TAIL_TILE - 1) // SEL_TAIL_TILE

    def full_body(kt, carry):
        sel_tile(pl.multiple_of(kt * SEL_TILE, SEL_TILE), SEL_TILE, False)
        return carry

    def tail_body(r, carry):
        sel_tile(pl.multiple_of(tail0 + r * SEL_TAIL_TILE, SEL_TAIL_TILE), SEL_TAIL_TILE, True)
        return carry

    lax.fori_loop(0, n_full, full_body, 0)
    lax.fori_loop(0, n_tail, tail_body, 0)
    o_sel = acc_ref[...] / jnp.sum(l_ref[...], axis=1, keepdims=True)

    w0 = pl.multiple_of(jnp.maximum(q0 - WINDOW, 0), tq)
    sw = _dot(qpb, kw_ref[0, :, pl.ds(w0, WINDOW + tq)])
    kpos = w0 + _iota(sw.shape, 1)
    qpos = qpos_of(sw.shape)
    e_win, inv_win = _masked_exp2(sw, (kpos <= qpos) & (kpos > qpos - WINDOW))
    o_win = _dot_nt(e_win.astype(jnp.bfloat16), vw_ref[0, :, pl.ds(w0, WINDOW + tq)]) * inv_win

    o_ref[0] = _combine((o_cmp, o_sel, o_win), gn_ref[0], tq)


def nsa_prompt_attention(q, gn, ck, cv, ks, vs, kw, vw, *, tq=LANES):
    B, S, _ = q.shape
    n_chunks = ck.shape[1]
    assert S % SEL_TILE == 0 and SEL_TILE % SEL_TAIL_TILE == 0 and SEL_TAIL_TILE % tq == 0
    assert S // SLC_BLK <= LANES and tq == LANES and S >= WINDOW + tq
    wcst = jnp.asarray(cmp_to_slc_matrix(n_chunks, LANES).T, jnp.bfloat16)
    rows = N_HEADS * tq
    per_b = lambda b, i: (b, 0, 0)
    tile = lambda b, i: (b, i, 0)
    const = lambda b, i: (0, 0)
    return pl.pallas_call(
        functools.partial(_nsa_prompt_kernel, tq=tq, n_chunks=n_chunks),
        grid=(B, S // tq),
        in_specs=[
            pl.BlockSpec((1, tq, NSA_WIDTH), tile),
            pl.BlockSpec((1, tq, LANES), tile),
            pl.BlockSpec((1, n_chunks, KV_WIDTH), per_b),
            pl.BlockSpec((1, n_chunks, KV_WIDTH), per_b),
            pl.BlockSpec((1, KV_WIDTH, S), per_b),
            pl.BlockSpec((1, KV_WIDTH, S), per_b),
            pl.BlockSpec((1, KV_WIDTH, S), per_b),
            pl.BlockSpec((1, KV_WIDTH, S), per_b),
            pl.BlockSpec(wcst.shape, const),
        ],
        out_specs=pl.BlockSpec((1, tq, NSA_WIDTH), tile),
        out_shape=jax.ShapeDtypeStruct((B, S, NSA_WIDTH), jnp.float32),
        scratch_shapes=[
            pltpu.VMEM((rows, 2 * LANES), jnp.bfloat16),
            pltpu.VMEM((rows, LANES), jnp.float32),
            pltpu.VMEM((rows, LANES), jnp.float32),
            pltpu.VMEM((rows, LANES), jnp.float32),
        ],
        compiler_params=pltpu.CompilerParams(
            dimension_semantics=("arbitrary", "arbitrary"), vmem_limit_bytes=VMEM_LIMIT_BYTES),
        name="nsa_prompt",
    )(q, gn, ck, cv, ks, vs, kw, vw, wcst)


def _padded_queries(q, tq):
    lane = _iota((tq, LANES), 1)
    blocks = []
    for h in range(N_KV_HEADS):
        for g in range(GROUP):
            c = h * GROUP + g
            pair = q[:, LANES * (c // 2):LANES * (c // 2 + 1)]
            if c % 2 != h:
                pair = pltpu.roll(pair, HEAD_DIM, axis=1)
            keep = (lane >= HEAD_DIM * h) & (lane < HEAD_DIM * (h + 1))
            blocks.append(jnp.where(keep, pair, 0.0))
    return jnp.concatenate(blocks, axis=0)


def _masked_softmax(s, valid):
    sm = jnp.where(valid, s, NEG_INF)
    mx = jnp.max(sm, axis=1, keepdims=True)
    e = jnp.where(valid, jnp.exp(sm - mx), 0.0)
    den = jnp.maximum(jnp.sum(e, axis=1, keepdims=True), 1e-30)
    return e / den


def _masked_softmax2(s2, valid):
    e, inv = _masked_exp2(s2, valid)
    return e * inv


def _masked_exp2(s2, valid):
    sm = jnp.where(valid, s2, NEG_INF)
    mx = jnp.max(sm, axis=1, keepdims=True)
    e = jnp.exp2(sm - mx)
    return e, jnp.where(mx > 0.5 * NEG_INF, 1.0 / jnp.sum(e, axis=1, keepdims=True), 0.0)


def _group_sum(p, tq):
    out = []
    for h in range(N_KV_HEADS):
        r0 = h * GROUP * tq
        acc = p[r0:r0 + tq]
        for g in range(1, GROUP):
            acc = acc + p[r0 + g * tq:r0 + (g + 1) * tq]
        out.append(acc)
    return out


def _combine(branches, gn, tq):
    lane = _iota((tq, LANES), 1)
    cols = []
    for h in range(N_KV_HEADS):
        blks = []
        for g in range(GROUP):
            j = h * GROUP + g
            rows = slice(j * tq, (j + 1) * tq)
            o = None
            for br in range(N_BRANCH):
                gate = jnp.broadcast_to(gn[:, j * N_BRANCH + br:j * N_BRANCH + br + 1], (tq, LANES))
                term = gate * branches[br][rows]
                o = term if o is None else o + term
            blks.append(o)
        for jj in range(GROUP // 2):
            a, b2 = blks[2 * jj], blks[2 * jj + 1]
            if h == 0:
                b2 = pltpu.roll(b2, HEAD_DIM, axis=1)
            else:
                a = pltpu.roll(a, HEAD_DIM, axis=1)
            cols.append(jnp.where(lane < HEAD_DIM, a, b2))
    return jnp.concatenate(cols, axis=1)


def _nsa_sample_kernel(pt_ref, q_ref, gn_ref, ck_ref, cv_ref, wk_ref, wv_ref,
                       ksn_ref, vsn_ref, kwn_ref, vwn_ref, pool_k, pool_v, ebig_ref, wcst_ref,
                       o_ref, nwk_ref, nwv_ref, buf, sem, *, n_pages, n_new, tq, n_seq):
    slot = pl.program_id(0) % 2
    past = n_pages * PAGE_SIZE
    n_chunks = n_pages * CHUNKS_PER_PAGE
    n_past_blk = past // SLC_BLK
    n_slc = -(-(past + n_new) // SLC_BLK)
    wb = wk_ref.shape[2]
    rows = N_HEADS * tq
    wait_pages = _page_gather(pt_ref, (pool_k, pool_v), buf, sem, n_pages, n_seq)

    def tok_of(shape):
        return _iota(shape, 0) & (tq - 1)

    def new_rows(ref, si):
        x = jnp.concatenate([ref[si], jnp.zeros((LANES - tq, KV_WIDTH), jnp.float32)], axis=0)
        return x.astype(jnp.bfloat16)

    def compressed(si):
        qpb = _padded_queries(q_ref[si] * ATTN_SCALE, tq).astype(jnp.bfloat16)
        s = _dot_nt(qpb, ck_ref[si])
        n_idx = _iota(s.shape, 1)
        valid = (n_idx * CMP_STRIDE + (CMP_LEN - 1) <= past + tok_of(s.shape)) & (n_idx < n_chunks - 1)
        p_cmp = _masked_softmax(s, valid)
        return qpb, p_cmp, _dot(p_cmp.astype(jnp.bfloat16), cv_ref[si])

    def select(p_cmps):
        ncol = wcst_ref.shape[0]
        groups = [g for p in p_cmps for g in _group_sum(p, tq)]
        psum = jnp.concatenate(groups + [jnp.zeros((LANES - len(groups) * tq, n_chunks), jnp.float32)], axis=0)
        hi, lo = _split_bf16(psum)
        wcst = wcst_ref[...]
        sc = _dot_nt(wcst, hi) + _dot_nt(wcst, lo)
        blk = _iota((ncol, LANES), 0)
        blkf = blk.astype(jnp.float32)
        qblk = (past + (_iota((ncol, LANES), 1) & (tq - 1))) >> SLC_SHIFT
        forced = (blk == 0) | (blk == qblk) | (blk == qblk - 1)
        sc = jnp.where(blk <= qblk, jnp.where(forced, sc + FORCE_BONUS, sc), NEG_INF)
        sc = jnp.where(blk < n_slc, sc, -jnp.inf)
        sel_t = jnp.zeros_like(sc)
        for _ in range(min(TOP_N, n_slc)):
            mx = jnp.max(sc, axis=0, keepdims=True)
            first = jnp.min(jnp.where(sc == mx, blkf, float(ncol)), axis=0, keepdims=True)
            hit = blkf == first
            sel_t = jnp.where(hit, 1.0, sel_t)
            sc = jnp.where(hit, -jnp.inf, sc)
        sel = sel_t.T
        return [[sel[(si * N_KV_HEADS + h) * tq:(si * N_KV_HEADS + h + 1) * tq] for h in range(N_KV_HEADS)]
                for si in range(n_seq)]

    def window_and_operands(si, qpb, sel):
        nsel = [jnp.concatenate([1.0 - sel[h][:, :LANES]] * GROUP, axis=0).astype(jnp.bfloat16) for h in range(N_KV_HEADS)]
        sel_new = [jnp.broadcast_to(sel[h][:, n_past_blk:n_past_blk + 1], (tq, LANES)) for h in range(N_KV_HEADS)]
        lhs = jnp.concatenate(
            [jnp.concatenate([qpb[h * GROUP * tq:(h + 1) * GROUP * tq], nsel[h]], axis=1)
             for h in range(N_KV_HEADS)], axis=0)
        sel_new_rows = jnp.concatenate([sel_new[h] for h in range(N_KV_HEADS) for _ in range(GROUP)], axis=0)

        tn = _iota((rows, LANES), 1)
        new_ok = (tn <= tok_of((rows, LANES))) & (tn < n_new)
        s_sn = jnp.where(new_ok & (sel_new_rows > 0.5), _dot_nt(qpb, new_rows(ksn_ref, si)), NEG_INF)
        s_wn = jnp.where(new_ok, _dot_nt(qpb, new_rows(kwn_ref, si)), NEG_INF)

        wk = wk_ref[si]
        wv = wv_ref[si]
        s_w = _dot(qpb, wk.astype(jnp.bfloat16))
        kwpos = past - wb + _iota(s_w.shape, 1)
        qpos = past + tok_of(s_w.shape)
        s_w = jnp.where((kwpos >= 0) & (kwpos <= qpos) & (kwpos > qpos - WINDOW), s_w, NEG_INF)
        mx = jnp.maximum(jnp.max(s_w, axis=1, keepdims=True), jnp.max(s_wn, axis=1, keepdims=True))
        e_w = jnp.exp(s_w - mx)
        e_wn = jnp.exp(s_wn - mx)
        den = jnp.sum(e_w, axis=1, keepdims=True) + jnp.sum(e_wn, axis=1, keepdims=True)
        o_win = (_dot_nt(e_w.astype(jnp.bfloat16), wv.astype(jnp.bfloat16))
                 + _dot(e_wn.astype(jnp.bfloat16), new_rows(vwn_ref, si))) / den

        col = _iota((KV_WIDTH, wb), 1)
        for src, new_ref, dst in ((wk, kwn_ref, nwk_ref), (wv, vwn_ref, nwv_ref)):
            new_t = jnp.concatenate([new_ref[si], jnp.zeros((LANES - tq, KV_WIDTH), jnp.float32)], axis=0).T
            tail = jnp.concatenate([jnp.zeros((KV_WIDTH, wb - LANES), jnp.float32), new_t], axis=1)
            dst[si] = jnp.where(col >= wb - n_new, pltpu.roll(tail, LANES - n_new, axis=1),
                                pltpu.roll(src, wb - n_new, axis=1))
        return lhs, s_sn, o_win

    def selected_and_combine(si, lhs, s_sn, o_cmp, o_win):
        ck_keys = min(SAMPLE_KEY_CHUNK, past)
        s_chunks = []
        for c in range(past // ck_keys):
            kc = buf[slot, si, 0, :, pl.ds(c * ck_keys, ck_keys)].astype(jnp.bfloat16)
            rhs = jnp.concatenate([kc, ebig_ref[:, pl.ds(c * ck_keys, ck_keys)]], axis=0)
            s_chunks.append(_dot(lhs, rhs))
        mx = jnp.max(s_sn, axis=1, keepdims=True)
        for sc in s_chunks:
            mx = jnp.maximum(mx, jnp.max(sc, axis=1, keepdims=True))
        e_sn = jnp.exp(s_sn - mx)
        den = jnp.sum(e_sn, axis=1, keepdims=True)
        o_sel = _dot(e_sn.astype(jnp.bfloat16), new_rows(vsn_ref, si))
        for c, sc in enumerate(s_chunks):
            e = jnp.exp(sc - mx)
            den = den + jnp.sum(e, axis=1, keepdims=True)
            vc = buf[slot, si, 1, :, pl.ds(c * ck_keys, ck_keys)].astype(jnp.bfloat16)
            o_sel = o_sel + _dot_nt(e.astype(jnp.bfloat16), vc)
        o_sel = o_sel / den

        o_ref[si] = _combine((o_cmp, o_sel, o_win), gn_ref[si], tq)

    seqs = range(n_seq)
    qpb, p_cmp, o_cmp = zip(*[compressed(si) for si in seqs])
    sel = select(p_cmp)
    staged = [window_and_operands(si, qpb[si], sel[si]) for si in seqs]
    wait_pages()
    for si in seqs:
        lhs, s_sn, o_win = staged[si]
        selected_and_combine(si, lhs, s_sn, o_cmp[si], o_win)


def nsa_sample_attention(q, gn, ck, cv, win_k, win_v, ks_new, vs_new, kw_new, vw_new,
                         pool_k, pool_v, page_table, *, tq=SUBLANES, n_seq=SAMPLE_SEQS_PER_STEP):
    DB, DS, _ = q.shape
    n_pages = page_table.shape[1]
    past = n_pages * PAGE_SIZE
    n_chunks = ck.shape[1]
    wb = win_k.shape[2]
    n_slc = -(-(past + DS) // SLC_BLK)
    assert DS <= tq and DS < CMP_STRIDE and past // SLC_BLK <= LANES and wb % LANES == 0
    assert past % min(SAMPLE_KEY_CHUNK, past) == 0 and DB % n_seq == 0 and n_seq * N_KV_HEADS * tq <= LANES
    ncol = -(-n_slc // LANES) * LANES
    pad_t = lambda t: jnp.pad(t, ((0, 0), (0, tq - DS), (0, 0)))
    kb = np.arange(past)[None, :] // SLC_BLK
    ebig = jnp.asarray(np.where(kb == np.arange(LANES)[:, None], NEG_INF, 0.0), jnp.bfloat16)
    wcst = jnp.asarray(cmp_to_slc_matrix(n_chunks, ncol).T, jnp.bfloat16)
    per_b = lambda b, pt: (b, 0, 0)
    const = lambda b, pt: (0, 0)
    blk3 = lambda n, w: pl.BlockSpec((n_seq, n, w), per_b)
    grid_spec = pltpu.PrefetchScalarGridSpec(
        num_scalar_prefetch=1,
        grid=(DB // n_seq,),
        in_specs=[
            blk3(tq, NSA_WIDTH), blk3(tq, LANES), blk3(n_chunks, KV_WIDTH), blk3(n_chunks, KV_WIDTH),
            blk3(KV_WIDTH, wb), blk3(KV_WIDTH, wb),
            blk3(tq, KV_WIDTH), blk3(tq, KV_WIDTH), blk3(tq, KV_WIDTH), blk3(tq, KV_WIDTH),
            pl.BlockSpec(memory_space=pl.ANY), pl.BlockSpec(memory_space=pl.ANY),
            pl.BlockSpec(ebig.shape, const, pipeline_mode=pl.Buffered(1)),
            pl.BlockSpec(wcst.shape, const, pipeline_mode=pl.Buffered(1)),
        ],
        out_specs=[blk3(tq, NSA_WIDTH), blk3(KV_WIDTH, wb), blk3(KV_WIDTH, wb)],
        scratch_shapes=[
            pltpu.VMEM((2, n_seq, 2, KV_WIDTH, past), jnp.float32),
            pltpu.SemaphoreType.DMA((2, 2)),
        ],
    )
    o, nwk, nwv = pl.pallas_call(
        functools.partial(_nsa_sample_kernel, n_pages=n_pages, n_new=DS, tq=tq, n_seq=n_seq),
        grid_spec=grid_spec,
        out_shape=[jax.ShapeDtypeStruct((DB, tq, NSA_WIDTH), jnp.float32),
                   jax.ShapeDtypeStruct((DB, KV_WIDTH, wb), jnp.float32),
                   jax.ShapeDtypeStruct((DB, KV_WIDTH, wb), jnp.float32)],
        compiler_params=pltpu.CompilerParams(
            dimension_semantics=("arbitrary",), vmem_limit_bytes=PAGED_VMEM_LIMIT_BYTES),
        name="nsa_sample",
    )(page_table, pad_t(q), pad_t(gn), ck, cv, win_k, win_v, pad_t(ks_new), pad_t(vs_new), pad_t(kw_new),
      pad_t(vw_new), pool_k, pool_v, ebig, wcst)
    return o[:, :DS], nwk, nwv


def rope_tables(pos):
    half = HEAD_DIM // 2
    inv_freq = ROPE_THETA ** (-jnp.arange(half, dtype=jnp.float32) / half)
    ang = pos.astype(jnp.float32)[:, None] * inv_freq[None, :]
    cos, sin = jnp.cos(ang), jnp.sin(ang)
    cos_t = jnp.tile(cos, (1, LANES // half))
    sin_t = jnp.tile(jnp.concatenate([-sin, sin], axis=1), (1, LANES // HEAD_DIM))
    return cos_t, sin_t


def _rope_lanes(y, cos, sin):
    first_half = (_iota((y.shape[0], LANES), 1) & (HEAD_DIM - 1)) < HEAD_DIM // 2
    cols = []
    for c in range(y.shape[1] // LANES):
        yc = y[:, c * LANES:(c + 1) * LANES]
        rot = jnp.where(first_half, pltpu.roll(yc, LANES - HEAD_DIM // 2, axis=1),
                        pltpu.roll(yc, HEAD_DIM // 2, axis=1))
        cols.append(yc * cos + rot * sin)
    return cols[0] if len(cols) == 1 else jnp.concatenate(cols, axis=1)


def _rope_sublanes(y, cos_t, sin_t):
    hh = HEAD_DIM // 2
    out = []
    for c in range(y.shape[0] // LANES):
        yc = y[c * LANES:(c + 1) * LANES]
        rot = jnp.concatenate([yc[hh:2 * hh], yc[:hh], yc[3 * hh:], yc[2 * hh:3 * hh]], axis=0)
        out.append(yc * cos_t + rot * sin_t)
    return out[0] if len(out) == 1 else jnp.concatenate(out, axis=0)


_REST_ACTS = {"u": jax.nn.gelu, "v": jax.nn.gelu, "gn": jax.nn.sigmoid}


def _proj_rest(h, wrest_ref, refs, names):
    c0 = 0
    for ref, name in zip(refs, names):
        width = ref.shape[-1]
        act = _REST_ACTS.get(name)
        for c in range(0, width, GMLP_WIDTH):
            w = min(GMLP_WIDTH, width - c)
            y = _dot(h, wrest_ref[:, c0 + c:c0 + c + w])
            ref[:, c:c + w] = y if act is None else act(y)
        c0 += width


def _proj_rows_kernel(x_ref, g_ref, wq_ref, wkv_ref, wrest_ref, cos_ref, sin_ref, q_ref, *out_refs):
    h = _rms(x_ref[...], g_ref[...]).astype(jnp.bfloat16)
    cos, sin = cos_ref[...], sin_ref[...]
    q_ref[...] = _rope_lanes(_dot(h, wq_ref[...]), cos, sin)
    y = _dot(h, wkv_ref[...])
    for gi in range(N_KV_GROUPS):
        blk = y[:, gi * KV_WIDTH:(gi + 1) * KV_WIDTH]
        out_refs[gi][...] = _rope_lanes(blk, cos, sin) if gi % 2 == 0 else blk
    _proj_rest(h, wrest_ref, out_refs[N_KV_GROUPS:], ("gn", "u", "v", "ga", "gb"))


def _proj_seq_kernel(x_ref, g_ref, wq_ref, wkvt_ref, wrest_ref, cos_ref, sin_ref, cost_ref, sint_ref,
                     q_ref, kct_ref, vct_ref, kst_ref, vst_ref, kwt_ref, vwt_ref,
                     kst_b_ref, vst_b_ref, kwt_b_ref, vwt_b_ref, gn_ref, u_ref, v_ref, ga_ref, gb_ref):
    h = _rms(x_ref[...], g_ref[...]).astype(jnp.bfloat16)
    q_ref[...] = _rope_lanes(_dot(h, wq_ref[...]), cos_ref[...], sin_ref[...]) * (ATTN_SCALE * LOG2E)
    cos_t, sin_t = cost_ref[...], sint_ref[...]
    yt = _dot_nt(wkvt_ref[...], h)
    kv_refs = (kct_ref, vct_ref, kst_ref, vst_ref, kwt_ref, vwt_ref)
    kvb_refs = (None, None, kst_b_ref, vst_b_ref, kwt_b_ref, vwt_b_ref)
    for gi in range(N_KV_GROUPS):
        blk = yt[gi * KV_WIDTH:(gi + 1) * KV_WIDTH]
        if gi % 2 == 0:
            blk = _rope_sublanes(blk, cos_t, sin_t)
        kv_refs[gi][0] = blk
        if kvb_refs[gi] is not None:
            kvb_refs[gi][0] = blk.astype(jnp.bfloat16)
    _proj_rest(h, wrest_ref, (gn_ref, u_ref, v_ref, ga_ref, gb_ref), ("gn", "u", "v", "ga", "gb"))


def _split_w_in(w_in):
    sizes = (NSA_WIDTH,) + (KV_WIDTH,) * N_KV_GROUPS + (N_HEADS * N_BRANCH,)
    offs = np.cumsum((0,) + sizes)
    wq = w_in[:, offs[0]:offs[1]]
    wkv = w_in[:, offs[1]:offs[7]]
    wgn = w_in[:, offs[7]:offs[8]]
    wrest = w_in[:, offs[8]:]
    return wq, wkv, wgn, wrest


def mixer_proj_rows(x, g, w_in, pos):
    M = x.shape[0]
    tm = min(ROW_TILE, M)
    assert M % tm == 0
    wq, wkv, wgn, wrest = _split_w_in(w_in)
    wrest = jnp.concatenate([jnp.pad(wgn, ((0, 0), (0, LANES - wgn.shape[1]))), wrest], axis=1)
    cos, sin = rope_tables(pos)
    b16 = lambda t: t.astype(jnp.bfloat16)
    row = lambda i: (i, 0)
    cw = lambda shape: pl.BlockSpec(shape, lambda i: (0, 0), pipeline_mode=pl.Buffered(1))
    widths = (NSA_WIDTH,) + (KV_WIDTH,) * N_KV_GROUPS + (LANES, GMLP_WIDTH, GMLP_WIDTH, D_MODEL, D_MODEL)
    return pl.pallas_call(
        _proj_rows_kernel,
        grid=(M // tm,),
        in_specs=[pl.BlockSpec((tm, D_MODEL), row), cw((1, D_MODEL)), cw(wq.shape), cw(wkv.shape),
                  cw(wrest.shape), pl.BlockSpec((tm, LANES), row), pl.BlockSpec((tm, LANES), row)],
        out_specs=[pl.BlockSpec((tm, w), row) for w in widths],
        out_shape=[jax.ShapeDtypeStruct((M, w), jnp.float32) for w in widths],
        compiler_params=pltpu.CompilerParams(
            dimension_semantics=("arbitrary",), vmem_limit_bytes=VMEM_LIMIT_BYTES),
        name="mixer_proj_rows",
    )(x, g.reshape(1, -1), b16(wq), b16(wkv), b16(wrest), cos, sin)


def mixer_proj_seq(x, g, w_in, batch):
    M = x.shape[0]
    S = M // batch
    tm = min(ROW_TILE, S)
    assert S % tm == 0
    n_s = S // tm
    wq, wkv, wgn, wrest = _split_w_in(w_in)
    wrest = jnp.concatenate([jnp.pad(wgn, ((0, 0), (0, LANES - wgn.shape[1]))), wrest], axis=1)
    cos, sin = rope_tables(jnp.arange(S, dtype=jnp.int32))
    b16 = lambda t: t.astype(jnp.bfloat16)
    f32 = jnp.float32
    row = lambda i: (i, 0)
    tab = lambda i: (i % n_s, 0)
    tab_t = lambda i: (0, i % n_s)
    seq_t = lambda i: (i // n_s, 0, i % n_s)
    cw = lambda shape: pl.BlockSpec(shape, lambda i: (0, 0), pipeline_mode=pl.Buffered(1))
    t_spec = pl.BlockSpec((1, KV_WIDTH, tm), seq_t)
    t_shape = lambda dt: jax.ShapeDtypeStruct((batch, KV_WIDTH, S), dt)
    r_widths = (LANES, GMLP_WIDTH, GMLP_WIDTH, D_MODEL, D_MODEL)
    out_specs = ([pl.BlockSpec((tm, NSA_WIDTH), row)] + [t_spec] * (N_KV_GROUPS + 4)
                 + [pl.BlockSpec((tm, w), row) for w in r_widths])
    out_shape = ([jax.ShapeDtypeStruct((M, NSA_WIDTH), f32)] + [t_shape(f32)] * N_KV_GROUPS
                 + [t_shape(jnp.bfloat16)] * 4 + [jax.ShapeDtypeStruct((M, w), f32) for w in r_widths])
    return pl.pallas_call(
        _proj_seq_kernel,
        grid=(M // tm,),
        in_specs=[pl.BlockSpec((tm, D_MODEL), row), cw((1, D_MODEL)), cw(wq.shape),
                  cw((N_KV_GROUPS * KV_WIDTH, D_MODEL)), cw(wrest.shape),
                  pl.BlockSpec((tm, LANES), tab), pl.BlockSpec((tm, LANES), tab),
                  pl.BlockSpec((LANES, tm), tab_t), pl.BlockSpec((LANES, tm), tab_t)],
        out_specs=out_specs,
        out_shape=out_shape,
        compiler_params=pltpu.CompilerParams(
            dimension_semantics=("arbitrary",), vmem_limit_bytes=VMEM_LIMIT_BYTES),
        name="mixer_proj_seq",
    )(x, g.reshape(1, -1), b16(wq), b16(wkv.T), b16(wrest), cos, sin, cos.T, sin.T)


def _merge_kernel(oa_ref, u_ref, v_ref, ga_ref, gb_ref, x_ref, lng_ref, lnb_ref, wsp_ref, bsx_ref,
                  wa_ref, wb_ref, wout_ref, gc_ref, wcq_ref, x1_ref, hq_ref):
    tm = x_ref.shape[0]
    v = v_ref[...]
    vc = v - jnp.mean(v, axis=-1, keepdims=True)
    var = jnp.mean(vc * vc, axis=-1, keepdims=True)
    vn = vc * lax.rsqrt(var + NORM_EPS) * lng_ref[...] + lnb_ref[...]
    lane = _iota((GMLP_CHUNK, LANES), 1)
    chunks = []
    for c in range(tm // GMLP_CHUNK):
        cols = []
        for pr in range(GMLP_GROUPS // 2):
            pair = vn[c * GMLP_CHUNK:(c + 1) * GMLP_CHUNK, pr * LANES:(pr + 1) * LANES]
            rhs = jnp.concatenate([jnp.where(lane < GMLP_GROUP_DIM, pair, 0.0),
                                   jnp.where(lane >= GMLP_GROUP_DIM, pair, 0.0)], axis=0).astype(jnp.bfloat16)
            cols.append(_dot(wsp_ref[pr], rhs))
        chunks.append(jnp.concatenate(cols, axis=1) + bsx_ref[...])
    o_b = u_ref[...] * jnp.concatenate(chunks, axis=0)
    m = (jax.nn.sigmoid(ga_ref[...]) * _dot(oa_ref[...].astype(jnp.bfloat16), wa_ref[...])
         + jax.nn.sigmoid(gb_ref[...]) * _dot(o_b.astype(jnp.bfloat16), wb_ref[...]))
    x1 = x_ref[...] + _dot(m.astype(jnp.bfloat16), wout_ref[...])
    x1_ref[...] = x1
    hq_ref[...] = _dot(_rms(x1, gc_ref[...]).astype(jnp.bfloat16), wcq_ref[...])


def gmlp_weights(ws, bs, chunk_len):
    L = chunk_len
    w = jnp.tril(ws[:, :L, :L])
    bias = bs[:, :L]
    if L < GMLP_CHUNK:
        r = np.arange(GMLP_CHUNK)
        same_seq = (r[:, None] // L) == (r[None, :] // L)
        w = jnp.where(same_seq, w[:, r % L][:, :, r % L], 0.0)
        bias = bias[:, r % L]
    pairs = w.reshape(GMLP_GROUPS // 2, 2, GMLP_CHUNK, GMLP_CHUNK)
    wsp = jnp.swapaxes(pairs, 1, 2).reshape(GMLP_GROUPS // 2, GMLP_CHUNK, 2 * GMLP_CHUNK)
    bsx = jnp.repeat(bias.T, GMLP_GROUP_DIM, axis=1)
    return wsp.astype(jnp.bfloat16), bsx


def merge_and_cross_q(o_a, u, v, ga, gb, x, ln_g, ln_b, wsp, bsx, w_a, w_b, w_out, g_cross, w_cq):
    M = x.shape[0]
    tm = min(ROW_TILE, M)
    one = pl.Buffered(1)
    row = lambda i: (i, 0)
    cw = lambda shape: pl.BlockSpec(shape, lambda i: (0,) * len(shape), pipeline_mode=one)
    b16 = lambda t: t.astype(jnp.bfloat16)
    out = jax.ShapeDtypeStruct((M, D_MODEL), jnp.float32)
    return pl.pallas_call(
        _merge_kernel,
        grid=(M // tm,),
        in_specs=[
            pl.BlockSpec((tm, NSA_WIDTH), row), pl.BlockSpec((tm, GMLP_WIDTH), row),
            pl.BlockSpec((tm, GMLP_WIDTH), row), pl.BlockSpec((tm, D_MODEL), row),
            pl.BlockSpec((tm, D_MODEL), row), pl.BlockSpec((tm, D_MODEL), row),
            cw((1, GMLP_WIDTH)), cw((1, GMLP_WIDTH)), cw(wsp.shape), cw(bsx.shape),
            cw(w_a.shape), cw(w_b.shape), cw(w_out.shape), cw((1, D_MODEL)), cw(w_cq.shape),
        ],
        out_specs=[pl.BlockSpec((tm, D_MODEL), row)] * 2,
        out_shape=[out, out],
        compiler_params=pltpu.CompilerParams(
            dimension_semantics=("arbitrary",), vmem_limit_bytes=VMEM_LIMIT_BYTES),
        name="merge_branches",
    )(o_a, u, v, ga, gb, x, ln_g.reshape(1, -1), ln_b.reshape(1, -1), wsp, bsx,
      b16(w_a), b16(w_b), b16(w_out), g_cross.reshape(1, -1), b16(w_cq))


def _memkv_kernel(x_ref, g_ref, wk_ref, wv_ref, k_ref, v_ref):
    h = _rms(x_ref[...], g_ref[...]).astype(jnp.bfloat16)
    k_ref[...] = _dot(h, wk_ref[...])
    v_ref[...] = _dot(h, wv_ref[...])


def memory_kv(mem, g_mem, w_ck, w_cv):
    M = mem.shape[0]
    tm = min(ROW_TILE, M)
    one = pl.Buffered(1)
    row = lambda i: (i, 0)
    cw = lambda shape: pl.BlockSpec(shape, lambda i: (0, 0), pipeline_mode=one)
    out = jax.ShapeDtypeStruct((M, D_MODEL), jnp.float32)
    return pl.pallas_call(
        _memkv_kernel,
        grid=(M // tm,),
        in_specs=[pl.BlockSpec((tm, D_MODEL), row), cw((1, D_MODEL)), cw(w_ck.shape), cw(w_cv.shape)],
        out_specs=[pl.BlockSpec((tm, D_MODEL), row)] * 2,
        out_shape=[out, out],
        compiler_params=pltpu.CompilerParams(
            dimension_semantics=("arbitrary",), vmem_limit_bytes=VMEM_LIMIT_BYTES),
        name="memory_kv",
    )(mem, g_mem.reshape(1, -1), w_ck.astype(jnp.bfloat16), w_cv.astype(jnp.bfloat16))


def _cross_heads_kernel(q_ref, mk_hbm, mv_hbm, o_ref, buf, sem, *, rows, n_seq):
    b = pl.program_id(0)
    nb = pl.num_programs(0)
    slot = b % 2

    def head_copies(bb, sl):
        return [pltpu.make_async_copy(src.at[bb * n_seq + si, :, h, :], buf.at[sl, si, kv, h], sem.at[sl, kv])
                for si in range(n_seq) for kv, src in enumerate((mk_hbm, mv_hbm)) for h in range(CROSS_HEADS)]

    @pl.when(b == 0)
    def _():
        for c in head_copies(0, 0):
            c.start()

    @pl.when(b + 1 < nb)
    def _():
        for c in head_copies(b + 1, 1 - slot):
            c.start()

    pad = jnp.zeros((SUBLANES - rows, D_MODEL), jnp.float32)
    qs = [jnp.concatenate([q_ref[si] * CROSS_SCALE, pad], axis=0) for si in range(n_seq)]
    for c in head_copies(b, slot):
        c.wait()
    for si in range(n_seq):
        qh = [qs[si][:, h * CROSS_HEAD_DIM:(h + 1) * CROSS_HEAD_DIM].astype(jnp.bfloat16) for h in range(CROSS_HEADS)]
        s = jnp.concatenate([_dot_nt(qh[h], buf[slot, si, 0, h].astype(jnp.bfloat16)) for h in range(CROSS_HEADS)], axis=0)
        e = jnp.exp(s - jnp.max(s, axis=1, keepdims=True))
        p = (e / jnp.sum(e, axis=1, keepdims=True)).astype(jnp.bfloat16)
        o = jnp.concatenate([_dot(p[h * SUBLANES:(h + 1) * SUBLANES], buf[slot, si, 1, h].astype(jnp.bfloat16))
                             for h in range(CROSS_HEADS)], axis=1)
        o_ref[si] = o[:rows]


def cross_attention_cached(hq, mk, mv, *, n_seq=CROSS_SEQS_PER_STEP):
    NB, R, _ = hq.shape
    assert NB % n_seq == 0
    any_spec = pl.BlockSpec(memory_space=pl.ANY)
    return pl.pallas_call(
        functools.partial(_cross_heads_kernel, rows=R, n_seq=n_seq),
        grid=(NB // n_seq,),
        in_specs=[pl.BlockSpec((n_seq, R, D_MODEL), lambda b: (b, 0, 0)), any_spec, any_spec],
        out_specs=pl.BlockSpec((n_seq, R, D_MODEL), lambda b: (b, 0, 0)),
        out_shape=jax.ShapeDtypeStruct((NB, R, D_MODEL), jnp.float32),
        scratch_shapes=[pltpu.VMEM((2, n_seq, 2, CROSS_HEADS, N_MEM, CROSS_HEAD_DIM), jnp.float32),
                        pltpu.SemaphoreType.DMA((2, 2))],
        compiler_params=pltpu.CompilerParams(
            dimension_semantics=("arbitrary",), vmem_limit_bytes=VMEM_LIMIT_BYTES),
        name="cross_attention_cached",
    )(hq, mk, mv)


def _ffn_and_norm(x, h_gain, wg_ref, wu_ref, wd_ref, gfin_ref, y_ref, acc_ref):
    h = _rms(x, h_gain).astype(jnp.bfloat16)
    acc_ref[...] = x
    for c in range(D_FF // FF_CHUNK):
        sl = slice(c * FF_CHUNK, (c + 1) * FF_CHUNK)
        a = _dot(h, wg_ref[:, sl])
        b = _dot(h, wu_ref[:, sl])
        t = (a * jax.nn.sigmoid(a) * b).astype(jnp.bfloat16)
        acc_ref[...] += _dot(t, wd_ref[sl, :])
    y_ref[...] = _rms(acc_ref[...], gfin_ref[...])


def _memory_attention(q, mk, mv):
    q = q * CROSS_SCALE
    outs = []
    for h in range(CROSS_HEADS):
        sl = slice(h * CROSS_HEAD_DIM, (h + 1) * CROSS_HEAD_DIM)
        s = _dot_nt(q[:, sl].astype(jnp.bfloat16), mk[:, sl].astype(jnp.bfloat16))
        e = jnp.exp(s - jnp.max(s, axis=1, keepdims=True))
        p = e / jnp.sum(e, axis=1, keepdims=True)
        outs.append(_dot(p.astype(jnp.bfloat16), mv[:, sl].astype(jnp.bfloat16)))
    return jnp.concatenate(outs, axis=1)


def _tail_kernel(x_ref, o_ref, wco_ref, gf_ref, wg_ref, wu_ref, wd_ref, gfin_ref, y_ref, acc_ref):
    x = x_ref[...] + _dot(o_ref[...].astype(jnp.bfloat16), wco_ref[...])
    _ffn_and_norm(x, gf_ref[...], wg_ref, wu_ref, wd_ref, gfin_ref, y_ref, acc_ref)


def _tail_cross_kernel(x_ref, hq_ref, mk_ref, mv_ref, wco_ref, gf_ref, wg_ref, wu_ref, wd_ref, gfin_ref,
                       y_ref, acc_ref):
    o = _memory_attention(hq_ref[...], mk_ref[0], mv_ref[0])
    x = x_ref[...] + _dot(o.astype(jnp.bfloat16), wco_ref[...])
    _ffn_and_norm(x, gf_ref[...], wg_ref, wu_ref, wd_ref, gfin_ref, y_ref, acc_ref)


def _tail_call(kernel_fn, name, x1, row_inputs, batch_inputs, rows_per_batch, w_co, g_ffn, w_gate, w_up, w_down, g_final):
    M = x1.shape[0]
    tm = min(ROW_TILE, M)
    row = lambda i: (i, 0)
    cw = lambda shape: pl.BlockSpec(shape, lambda i: (0, 0), pipeline_mode=pl.Buffered(1))
    b16 = lambda t: t.astype(jnp.bfloat16)
    per_batch = lambda i: (i * tm // rows_per_batch, 0, 0)
    return pl.pallas_call(
        kernel_fn,
        grid=(M // tm,),
        in_specs=[pl.BlockSpec((tm, D_MODEL), row)] * (1 + len(row_inputs))
        + [pl.BlockSpec((1,) + t.shape[1:], per_batch) for t in batch_inputs]
        + [cw(w_co.shape), cw((1, D_MODEL)), cw(w_gate.shape), cw(w_up.shape), cw(w_down.shape), cw((1, D_MODEL))],
        out_specs=pl.BlockSpec((tm, D_MODEL), row),
        out_shape=jax.ShapeDtypeStruct((M, D_MODEL), jnp.float32),
        scratch_shapes=[pltpu.VMEM((tm, D_MODEL), jnp.float32)],
        compiler_params=pltpu.CompilerParams(
            dimension_semantics=("arbitrary",), vmem_limit_bytes=VMEM_LIMIT_BYTES),
        name=name,
    )(x1, *row_inputs, *batch_inputs, b16(w_co), g_ffn.reshape(1, -1), b16(w_gate), b16(w_up), b16(w_down),
      g_final.reshape(1, -1))


def tail(x1, o, *weights):
    return _tail_call(_tail_kernel, "tail", x1, (o,), (), x1.shape[0], *weights)


def tail_with_memory(x1, hq, mk, mv, *weights):
    rows_per_batch = x1.shape[0] // mk.shape[0]
    assert rows_per_batch % min(ROW_TILE, x1.shape[0]) == 0
    return _tail_call(_tail_cross_kernel, "tail_with_memory", x1, (hq,), (mk, mv), rows_per_batch, *weights)


def kernel(x_prompt, x_sample, mem_prompt, cache_cmp_k, cache_cmp_v, cache_slc_k, cache_slc_v,
           page_table, state_win_k, state_win_v, cache_mem_k, cache_mem_v,
           g_mix, w_in, cmp_pe, cmp_w1, cmp_w2, gmlp_ln_g, gmlp_ln_b, gmlp_ws, gmlp_bs,
           w_branch_a, w_branch_b, w_out, g_cross, g_mem, w_cq, w_ck, w_cv, w_co,
           g_ffn, w_gate, w_up, w_down, g_final):
    B, S, _ = x_prompt.shape
    DB, DS, _ = x_sample.shape
    n_pages = page_table.shape[1]
    assert g_mix.shape[0] == 1, "single-layer step"
    l = 0
    cmpw = cmp_weights(cmp_pe[l], cmp_w1[l], cmp_w2[l])
    merge_w = (gmlp_ln_g[l], gmlp_ln_b[l])
    merge_tail = (w_branch_a[l], w_branch_b[l], w_out[l], g_cross[l], w_cq[l])
    ffn_w = (w_co[l], g_ffn[l], w_gate[l], w_up[l], w_down[l], g_final)
    to_t = lambda t: jnp.transpose(t, (0, 2, 3, 1)).reshape(t.shape[0], KV_WIDTH, t.shape[1])
    from_t = lambda t: jnp.transpose(t.reshape(t.shape[0], N_KV_HEADS, HEAD_DIM, t.shape[2]), (0, 3, 1, 2))
    mem3 = lambda t: t.reshape(-1, N_MEM, D_MODEL)

    xp = x_prompt.reshape(B * S, D_MODEL)
    (q, kct, vct, kst, vst, kwt, vwt, kst_b, vst_b, kwt_b, vwt_b, gn, u, v, ga, gb) = mixer_proj_seq(
        xp, g_mix[l], w_in[l], B)
    ck, cv = cmp_blocks_dense(kct, vct, *cmpw)
    o_a = nsa_prompt_attention(q.reshape(B, S, NSA_WIDTH), gn.reshape(B, S, LANES), ck, cv,
                               kst_b, vst_b, kwt_b, vwt_b)
    x1, hq = merge_and_cross_q(o_a.reshape(B * S, NSA_WIDTH), u, v, ga, gb, xp, *merge_w,
                               *gmlp_weights(gmlp_ws[l], gmlp_bs[l], min(S, GMLP_CHUNK)), *merge_tail)
    mk, mv = memory_kv(mem_prompt.reshape(B * N_MEM, D_MODEL), g_mem[l], w_ck[l], w_cv[l])
    y_prompt = tail_with_memory(x1, hq, mem3(mk), mem3(mv), *ffn_w).reshape(B, S, D_MODEL)
    wb_p = min(WINDOW, S)
    mem5 = lambda t: t.reshape(B, N_MEM, CROSS_HEADS, CROSS_HEAD_DIM)
    prompt_new = (from_t(kct), from_t(vct), from_t(kst), from_t(vst),
                  from_t(kwt[:, :, S - wb_p:]), from_t(vwt[:, :, S - wb_p:]), mem5(mk), mem5(mv))

    xs = x_sample.reshape(DB * DS, D_MODEL)
    pos_s = n_pages * PAGE_SIZE + jnp.arange(DS, dtype=jnp.int32)
    (q, kc, vc, ks, vs, kw, vw, gn, u, v, ga, gb) = mixer_proj_rows(xs, g_mix[l], w_in[l], jnp.tile(pos_s, DB))
    ck, cv = cmp_blocks_paged(to_t(cache_cmp_k[l]), to_t(cache_cmp_v[l]), page_table, *cmpw)
    per_seq = lambda t: t.reshape(DB, DS, -1)
    o_a, win_kt, win_vt = nsa_sample_attention(
        per_seq(q), per_seq(gn), ck, cv, to_t(state_win_k[l]), to_t(state_win_v[l]),
        per_seq(ks), per_seq(vs), per_seq(kw), per_seq(vw),
        to_t(cache_slc_k[l]), to_t(cache_slc_v[l]), page_table)
    x1, hq = merge_and_cross_q(o_a.reshape(DB * DS, NSA_WIDTH), u, v, ga, gb, xs, *merge_w,
                               *gmlp_weights(gmlp_ws[l], gmlp_bs[l], min(DS, GMLP_CHUNK)), *merge_tail)
    o = cross_attention_cached(per_seq(hq), cache_mem_k[l], cache_mem_v[l])
    y_sample = tail(x1, o.reshape(DB * DS, D_MODEL), *ffn_w).reshape(DB, DS, D_MODEL)
    heads = lambda t: t.reshape(DB, DS, N_KV_HEADS, HEAD_DIM)
    sample_new = (heads(kc), heads(vc), heads(ks), heads(vs), from_t(win_kt), from_t(win_vt), per_seq(v))

    return (y_prompt, y_sample) + tuple(t[None] for t in prompt_new + sample_new)
```

```python
import functools

import jax
import jax.numpy as jnp
import numpy as np
from jax import lax
from jax.experimental import pallas as pl
from jax.experimental.pallas import tpu as pltpu

D_MODEL = 1024
N_HEADS = 8
HEAD_DIM = 64
N_KV_HEADS = 2
GROUP = N_HEADS // N_KV_HEADS
NSA_WIDTH = N_HEADS * HEAD_DIM
KV_WIDTH = N_KV_HEADS * HEAD_DIM
CMP_LEN = 32
CMP_STRIDE = 16
CMP_HIDDEN = 128
SLC_BLK = 64
SLC_SHIFT = 6
TOP_N = 16
WINDOW = 512
ROPE_THETA = 10000.0
ATTN_SCALE = HEAD_DIM ** -0.5
LOG2E = 1.4426950408889634
FORCE_BONUS = 1e4
NEG_INF = -1e30
N_BRANCH = 3
GMLP_WIDTH = 512
GMLP_GROUPS = 8
GMLP_GROUP_DIM = GMLP_WIDTH // GMLP_GROUPS
GMLP_CHUNK = 128
N_MEM = 256
CROSS_HEADS = 4
CROSS_HEAD_DIM = D_MODEL // CROSS_HEADS
CROSS_SCALE = CROSS_HEAD_DIM ** -0.5
D_FF = -(-8 * D_MODEL // (3 * 256)) * 256
NORM_EPS = 1e-6
PAGE_SIZE = 128
CHUNKS_PER_PAGE = PAGE_SIZE // CMP_STRIDE
N_KV_GROUPS = 6

LANES = 128
SUBLANES = 8
VMEM_LIMIT_BYTES = 48 * 1024 * 1024
PAGED_VMEM_LIMIT_BYTES = 56 * 1024 * 1024
SEL_TILE = 1024
SEL_TAIL_TILE = 512
SAMPLE_KEY_CHUNK = 4096
SAMPLE_SEQS_PER_STEP = 2
CROSS_SEQS_PER_STEP = 4
ROW_TILE = 512
FF_CHUNK = 256

_NT = (((1,), (1,)), ((), ()))


def _dot_nt(a, b):
    return lax.dot_general(a, b, _NT, preferred_element_type=jnp.float32)


def _dot(a, b):
    return jnp.dot(a, b, preferred_element_type=jnp.float32)


def _split_bf16(x):
    hi = x.astype(jnp.bfloat16)
    lo = (x - hi.astype(jnp.float32)).astype(jnp.bfloat16)
    return hi, lo


def _iota(shape, dim):
    return lax.broadcasted_iota(jnp.int32, shape, dim)


def _rms(x, g):
    return x * lax.rsqrt(jnp.mean(x * x, axis=-1, keepdims=True) + NORM_EPS) * g


def _cmp_from_transposed(xt_ref, p, xs, wbig_ref, w2big_ref, peterm_ref, out_ref, n_rows):
    n_chunks = n_rows // CMP_STRIDE
    pitch = xs.shape[0] // CMP_STRIDE
    per_blk = LANES // CMP_STRIDE
    for j in range(n_rows // LANES):
        blk = xt_ref[:, j * LANES:(j + 1) * LANES].T
        for k in range(LANES // SUBLANES):
            c, s0 = k // 2, (k % 2) * SUBLANES
            xs[pl.ds(s0 * pitch + j * per_blk + c, SUBLANES, stride=pitch), :] = blk[k * SUBLANES:(k + 1) * SUBLANES]
    acc = jnp.zeros((n_chunks, 4 * CMP_HIDDEN), jnp.float32)
    for sp in range(CMP_STRIDE // 2):
        a0 = xs[pl.ds(2 * sp * pitch, n_chunks), :]
        a1 = xs[pl.ds((2 * sp + 1) * pitch, n_chunks), :]
        lhs = jnp.concatenate([a0, a1], axis=1).astype(jnp.bfloat16)
        acc = acc + _dot(lhs, wbig_ref[p, sp])
    lead = acc[:, :2 * CMP_HIDDEN]
    trail_next = pltpu.roll(acc[:, 2 * CMP_HIDDEN:], n_chunks - 1, axis=0)
    pre = lead + trail_next + peterm_ref[p]
    out = _dot(jax.nn.gelu(pre).astype(jnp.bfloat16), w2big_ref[p])
    out_ref[0] = out.astype(out_ref.dtype)


def _regroup_scratch(n_rows):
    n_chunks = n_rows // CMP_STRIDE
    pitch = n_chunks + SUBLANES if n_chunks % 32 == 0 else n_chunks
    return pltpu.VMEM((CMP_STRIDE * pitch, KV_WIDTH), jnp.float32)


def _page_gather(pt_ref, pools, buf, sem, n_pages, n_seq=1):
    b = pl.program_id(0)
    nb = pl.num_programs(0)
    slot = b % 2

    def copies(bb, sl):
        return [pltpu.make_async_copy(pools[p].at[pt_ref[bb * n_seq + si, j]],
                                      buf.at[sl, si, p, :, pl.ds(j * PAGE_SIZE, PAGE_SIZE)], sem.at[sl, p])
                for si in range(n_seq) for p in range(len(pools)) for j in range(n_pages)]

    @pl.when(b == 0)
    def _():
        for c in copies(0, 0):
            c.start()

    @pl.when(b + 1 < nb)
    def _():
        for c in copies(b + 1, 1 - slot):
            c.start()

    def wait():
        for c in copies(b, slot):
            c.wait()

    return wait


def _cmp_paged_kernel(pt_ref, pool_k, pool_v, wbig_ref, w2big_ref, peterm_ref, ck_ref, cv_ref,
                      buf, xs, sem, *, n_pages):
    wait = _page_gather(pt_ref, (pool_k, pool_v), buf, sem, n_pages)
    wait()
    slot = pl.program_id(0) % 2
    for p, out_ref in ((0, ck_ref), (1, cv_ref)):
        _cmp_from_transposed(buf.at[slot, 0, p], p, xs, wbig_ref, w2big_ref, peterm_ref, out_ref,
                             n_pages * PAGE_SIZE)


def _cmp_dense_kernel(kt_ref, vt_ref, wbig_ref, w2big_ref, peterm_ref, ck_ref, cv_ref, xs):
    for p, (src, out_ref) in enumerate(((kt_ref, ck_ref), (vt_ref, cv_ref))):
        _cmp_from_transposed(src.at[0], p, xs, wbig_ref, w2big_ref, peterm_ref, out_ref, src.shape[2])


def cmp_weights(cmp_pe, cmp_w1, cmp_w2):
    eye = jnp.eye(N_KV_HEADS, dtype=jnp.float32)
    w1r = cmp_w1.reshape(2, 2, CMP_STRIDE // 2, 2, HEAD_DIM, CMP_HIDDEN)
    wbig = jnp.einsum('jtpsdf,hk->jpshdtkf', w1r, eye)
    wbig = wbig.reshape(2, CMP_STRIDE // 2, 2 * KV_WIDTH, 4 * CMP_HIDDEN).astype(jnp.bfloat16)
    w2big = jnp.einsum('jfd,hk->jhfkd', cmp_w2, eye).reshape(2, 2 * CMP_HIDDEN, KV_WIDTH)
    pe_term = jnp.einsum('jsd,jsdf->jf', cmp_pe, cmp_w1)
    peterm = jnp.tile(pe_term[:, None, :], (1, 1, N_KV_HEADS))
    return wbig, w2big.astype(jnp.bfloat16), peterm


def _cmp_weight_specs(wbig, w2big, peterm, nargs):
    c3 = (lambda b: (0, 0, 0)) if nargs == 1 else (lambda b, pt: (0, 0, 0))
    c4 = (lambda b: (0, 0, 0, 0)) if nargs == 1 else (lambda b, pt: (0, 0, 0, 0))
    one = pl.Buffered(1)
    return [pl.BlockSpec(wbig.shape, c4, pipeline_mode=one),
            pl.BlockSpec(w2big.shape, c3, pipeline_mode=one),
            pl.BlockSpec(peterm.shape, c3, pipeline_mode=one)]


def cmp_blocks_paged(pool_k, pool_v, page_table, wbig, w2big, peterm):
    n_seq, n_pages = page_table.shape
    n_rows = n_pages * PAGE_SIZE
    n_chunks = n_rows // CMP_STRIDE
    out = jax.ShapeDtypeStruct((n_seq, n_chunks, KV_WIDTH), jnp.bfloat16)
    grid_spec = pltpu.PrefetchScalarGridSpec(
        num_scalar_prefetch=1,
        grid=(n_seq,),
        in_specs=[pl.BlockSpec(memory_space=pl.ANY), pl.BlockSpec(memory_space=pl.ANY)]
        + _cmp_weight_specs(wbig, w2big, peterm, 2),
        out_specs=[pl.BlockSpec((1, n_chunks, KV_WIDTH), lambda b, pt: (b, 0, 0))] * 2,
        scratch_shapes=[
            pltpu.VMEM((2, 1, 2, KV_WIDTH, n_rows), jnp.float32),
            _regroup_scratch(n_rows),
            pltpu.SemaphoreType.DMA((2, 2)),
        ],
    )
    return pl.pallas_call(
        functools.partial(_cmp_paged_kernel, n_pages=n_pages),
        grid_spec=grid_spec,
        out_shape=[out, out],
        compiler_params=pltpu.CompilerParams(
            dimension_semantics=("arbitrary",), vmem_limit_bytes=PAGED_VMEM_LIMIT_BYTES),
        name="cmp_blocks_paged",
    )(page_table, pool_k, pool_v, wbig, w2big, peterm)


def cmp_blocks_dense(kt, vt, wbig, w2big, peterm):
    B, _, S = kt.shape
    n_chunks = S // CMP_STRIDE
    per_b = lambda b: (b, 0, 0)
    return pl.pallas_call(
        _cmp_dense_kernel,
        grid=(B,),
        in_specs=[pl.BlockSpec((1, KV_WIDTH, S), per_b), pl.BlockSpec((1, KV_WIDTH, S), per_b)]
        + _cmp_weight_specs(wbig, w2big, peterm, 1),
        out_specs=[pl.BlockSpec((1, n_chunks, KV_WIDTH), per_b)] * 2,
        out_shape=[jax.ShapeDtypeStruct((B, n_chunks, KV_WIDTH), jnp.bfloat16)] * 2,
        scratch_shapes=[_regroup_scratch(S)],
        compiler_params=pltpu.CompilerParams(
            dimension_semantics=("arbitrary",), vmem_limit_bytes=VMEM_LIMIT_BYTES),
        name="cmp_blocks_dense",
    )(kt, vt, wbig, w2big, peterm)


def cmp_to_slc_matrix(n_chunks, n_cols):
    cs = np.arange(n_chunks)[:, None] * CMP_STRIDE
    ss = np.arange(n_cols)[None, :] * SLC_BLK
    shared = np.minimum(cs + CMP_LEN, ss + SLC_BLK) - np.maximum(cs, ss)
    w = np.maximum(shared, 0).astype(np.float32) / CMP_LEN
    w[n_chunks - 1] = 0.0
    return w


def _nsa_prompt_kernel(q_ref, gn_ref, ck_ref, cv_ref, ks_ref, vs_ref, kw_ref, vw_ref,
                       wcst_ref, o_ref, lhs_ref, m_ref, l_ref, acc_ref, *, tq, n_chunks):
    i = pl.program_id(1)
    q0 = i * tq
    qpb = _padded_queries(q_ref[0], tq).astype(jnp.bfloat16)

    def qpos_of(shape):
        return q0 + (_iota(shape, 0) & (tq - 1))

    s = _dot_nt(qpb, ck_ref[0])
    n_idx = _iota(s.shape, 1)
    valid = (n_idx * CMP_STRIDE + (CMP_LEN - 1) <= qpos_of(s.shape)) & (n_idx < n_chunks - 1)
    p_cmp = _masked_softmax2(s, valid)
    o_cmp = _dot(p_cmp.astype(jnp.bfloat16), cv_ref[0])

    blk = _iota((LANES, tq), 0)
    blkf = blk.astype(jnp.float32)
    qblk = (q0 + _iota((LANES, tq), 1)) >> SLC_SHIFT
    forced = (blk == 0) | (blk == qblk) | (blk == qblk - 1)
    wcst = wcst_ref[...]
    nsel = []
    for psum in _group_sum(p_cmp, tq):
        hi, lo = _split_bf16(psum)
        sc = _dot_nt(wcst, hi) + _dot_nt(wcst, lo)
        sc = jnp.where(blk <= qblk, jnp.where(forced, sc + FORCE_BONUS, sc), NEG_INF)
        for _ in range(TOP_N):
            mx = jnp.max(sc, axis=0, keepdims=True)
            first = jnp.min(jnp.where(sc == mx, blkf, float(LANES)), axis=0, keepdims=True)
            sc = jnp.where(blkf == first, -jnp.inf, sc)
        nsel.append(jnp.where(sc == -jnp.inf, 0.0, 1.0).T.astype(jnp.bfloat16))
    for h in range(N_KV_HEADS):
        r0 = h * GROUP * tq
        lhs_ref[r0:r0 + GROUP * tq, :LANES] = qpb[r0:r0 + GROUP * tq]
        for g in range(GROUP):
            lhs_ref[r0 + g * tq:r0 + (g + 1) * tq, LANES:] = nsel[h]

    m_ref[...] = jnp.full(m_ref.shape, -3e38, jnp.float32)
    l_ref[...] = jnp.zeros(l_ref.shape, jnp.float32)
    acc_ref[...] = jnp.zeros(acc_ref.shape, jnp.float32)

    def sel_tile(k0, size, causal):
        kblk = (k0 + _iota((LANES, size), 1)) >> SLC_SHIFT
        ebig = jnp.where(kblk == _iota((LANES, size), 0), NEG_INF, 0.0).astype(jnp.bfloat16)
        rhs = jnp.concatenate([ks_ref[0, :, pl.ds(k0, size)], ebig], axis=0)
        st = _dot(lhs_ref[...], rhs)
        if causal:
            st = jnp.where(k0 + _iota(st.shape, 1) <= qpos_of(st.shape), st, NEG_INF)
        cols = [st[:, c * LANES:(c + 1) * LANES] for c in range(size // LANES)]
        mt = functools.reduce(jnp.maximum, cols)
        m_prev = m_ref[...]
        m_new = jnp.maximum(m_prev, jnp.max(mt, axis=1, keepdims=True))
        alpha = jnp.exp2(m_prev - m_new)
        ps = [jnp.exp2(c - m_new) for c in cols]
        l_ref[...] = alpha * l_ref[...] + functools.reduce(jnp.add, ps)
        pb = jnp.concatenate(ps, axis=1).astype(jnp.bfloat16)
        acc_ref[...] = alpha * acc_ref[...] + _dot_nt(pb, vs_ref[0, :, pl.ds(k0, size)])
        m_ref[...] = m_new

    n_full = q0 // SEL_TILE
    tail0 = n_full * SEL_TILE
    n_tail = (q0 + tq - tail0 + SEL_TAIL_TILE - 1) // SEL_TAIL_TILE

    def full_body(kt, carry):
        sel_tile(pl.multiple_of(kt * SEL_TILE, SEL_TILE), SEL_TILE, False)
        return carry

    def tail_body(r, carry):
        sel_tile(pl.multiple_of(tail0 + r * SEL_TAIL_TILE, SEL_TAIL_TILE), SEL_TAIL_TILE, True)
        return carry

    lax.fori_loop(0, n_full, full_body, 0)
    lax.fori_loop(0, n_tail, tail_body, 0)
    o_sel = acc_ref[...] / jnp.sum(l_ref[...], axis=1, keepdims=True)

    w0 = pl.multiple_of(jnp.maximum(q0 - WINDOW, 0), tq)
    sw = _dot(qpb, kw_ref[0, :, pl.ds(w0, WINDOW + tq)])
    kpos = w0 + _iota(sw.shape, 1)
    qpos = qpos_of(sw.shape)
    e_win, inv_win = _masked_exp2(sw, (kpos <= qpos) & (kpos > qpos - WINDOW))
    o_win = _dot_nt(e_win.astype(jnp.bfloat16), vw_ref[0, :, pl.ds(w0, WINDOW + tq)]) * inv_win

    o_ref[0] = _combine((o_cmp, o_sel, o_win), gn_ref[0], tq)


def nsa_prompt_attention(q, gn, ck, cv, ks, vs, kw, vw, *, tq=LANES):
    B, S, _ = q.shape
    n_chunks = ck.shape[1]
    assert S % SEL_TILE == 0 and SEL_TILE % SEL_TAIL_TILE == 0 and SEL_TAIL_TILE % tq == 0
    assert S // SLC_BLK <= LANES and tq == LANES and S >= WINDOW + tq
    wcst = jnp.asarray(cmp_to_slc_matrix(n_chunks, LANES).T, jnp.bfloat16)
    rows = N_HEADS * tq
    per_b = lambda b, i: (b, 0, 0)
    tile = lambda b, i: (b, i, 0)
    const = lambda b, i: (0, 0)
    return pl.pallas_call(
        functools.partial(_nsa_prompt_kernel, tq=tq, n_chunks=n_chunks),
        grid=(B, S // tq),
        in_specs=[
            pl.BlockSpec((1, tq, NSA_WIDTH), tile),
            pl.BlockSpec((1, tq, LANES), tile),
            pl.BlockSpec((1, n_chunks, KV_WIDTH), per_b),
            pl.BlockSpec((1, n_chunks, KV_WIDTH), per_b),
            pl.BlockSpec((1, KV_WIDTH, S), per_b),
            pl.BlockSpec((1, KV_WIDTH, S), per_b),
            pl.BlockSpec((1, KV_WIDTH, S), per_b),
            pl.BlockSpec((1, KV_WIDTH, S), per_b),
            pl.BlockSpec(wcst.shape, const),
        ],
        out_specs=pl.BlockSpec((1, tq, NSA_WIDTH), tile),
        out_shape=jax.ShapeDtypeStruct((B, S, NSA_WIDTH), jnp.float32),
        scratch_shapes=[
            pltpu.VMEM((rows, 2 * LANES), jnp.bfloat16),
            pltpu.VMEM((rows, LANES), jnp.float32),
            pltpu.VMEM((rows, LANES), jnp.float32),
            pltpu.VMEM((rows, LANES), jnp.float32),
        ],
        compiler_params=pltpu.CompilerParams(
            dimension_semantics=("arbitrary", "arbitrary"), vmem_limit_bytes=VMEM_LIMIT_BYTES),
        name="nsa_prompt",
    )(q, gn, ck, cv, ks, vs, kw, vw, wcst)


def _padded_queries(q, tq):
    lane = _iota((tq, LANES), 1)
    blocks = []
    for h in range(N_KV_HEADS):
        for g in range(GROUP):
            c = h * GROUP + g
            pair = q[:, LANES * (c // 2):LANES * (c // 2 + 1)]
            if c % 2 != h:
                pair = pltpu.roll(pair, HEAD_DIM, axis=1)
            keep = (lane >= HEAD_DIM * h) & (lane < HEAD_DIM * (h + 1))
            blocks.append(jnp.where(keep, pair, 0.0))
    return jnp.concatenate(blocks, axis=0)


def _masked_softmax(s, valid):
    sm = jnp.where(valid, s, NEG_INF)
    mx = jnp.max(sm, axis=1, keepdims=True)
    e = jnp.where(valid, jnp.exp(sm - mx), 0.0)
    den = jnp.maximum(jnp.sum(e, axis=1, keepdims=True), 1e-30)
    return e / den


def _masked_softmax2(s2, valid):
    e, inv = _masked_exp2(s2, valid)
    return e * inv


def _masked_exp2(s2, valid):
    sm = jnp.where(valid, s2, NEG_INF)
    mx = jnp.max(sm, axis=1, keepdims=True)
    e = jnp.exp2(sm - mx)
    return e, jnp.where(mx > 0.5 * NEG_INF, 1.0 / jnp.sum(e, axis=1, keepdims=True), 0.0)


def _group_sum(p, tq):
    out = []
    for h in range(N_KV_HEADS):
        r0 = h * GROUP * tq
        acc = p[r0:r0 + tq]
        for g in range(1, GROUP):
            acc = acc + p[r0 + g * tq:r0 + (g + 1) * tq]
        out.append(acc)
    return out


def _combine(branches, gn, tq):
    lane = _iota((tq, LANES), 1)
    cols = []
    for h in range(N_KV_HEADS):
        blks = []
        for g in range(GROUP):
            j = h * GROUP + g
            rows = slice(j * tq, (j + 1) * tq)
            o = None
            for br in range(N_BRANCH):
                gate = jnp.broadcast_to(gn[:, j * N_BRANCH + br:j * N_BRANCH + br + 1], (tq, LANES))
                term = gate * branches[br][rows]
                o = term if o is None else o + term
            blks.append(o)
        for jj in range(GROUP // 2):
            a, b2 = blks[2 * jj], blks[2 * jj + 1]
            if h == 0:
                b2 = pltpu.roll(b2, HEAD_DIM, axis=1)
            else:
                a = pltpu.roll(a, HEAD_DIM, axis=1)
            cols.append(jnp.where(lane < HEAD_DIM, a, b2))
    return jnp.concatenate(cols, axis=1)


def _nsa_sample_kernel(pt_ref, q_ref, gn_ref, ck_ref, cv_ref, wk_ref, wv_ref,
                       ksn_ref, vsn_ref, kwn_ref, vwn_ref, pool_k, pool_v, ebig_ref, wcst_ref,
                       o_ref, nwk_ref, nwv_ref, buf, sem, *, n_pages, n_new, tq, n_seq):
    slot = pl.program_id(0) % 2
    past = n_pages * PAGE_SIZE
    n_chunks = n_pages * CHUNKS_PER_PAGE
    n_past_blk = past // SLC_BLK
    n_slc = -(-(past + n_new) // SLC_BLK)
    wb = wk_ref.shape[2]
    rows = N_HEADS * tq
    wait_pages = _page_gather(pt_ref, (pool_k, pool_v), buf, sem, n_pages, n_seq)

    def tok_of(shape):
        return _iota(shape, 0) & (tq - 1)

    def new_rows(ref, si):
        x = jnp.concatenate([ref[si], jnp.zeros((LANES - tq, KV_WIDTH), jnp.float32)], axis=0)
        return x.astype(jnp.bfloat16)

    def compressed(si):
        qpb = _padded_queries(q_ref[si] * ATTN_SCALE, tq).astype(jnp.bfloat16)
        s = _dot_nt(qpb, ck_ref[si])
        n_idx = _iota(s.shape, 1)
        valid = (n_idx * CMP_STRIDE + (CMP_LEN - 1) <= past + tok_of(s.shape)) & (n_idx < n_chunks - 1)
        p_cmp = _masked_softmax(s, valid)
        return qpb, p_cmp, _dot(p_cmp.astype(jnp.bfloat16), cv_ref[si])

    def select(p_cmps):
        ncol = wcst_ref.shape[0]
        groups = [g for p in p_cmps for g in _group_sum(p, tq)]
        psum = jnp.concatenate(groups + [jnp.zeros((LANES - len(groups) * tq, n_chunks), jnp.float32)], axis=0)
        hi, lo = _split_bf16(psum)
        wcst = wcst_ref[...]
        sc = _dot_nt(wcst, hi) + _dot_nt(wcst, lo)
        blk = _iota((ncol, LANES), 0)
        blkf = blk.astype(jnp.float32)
        qblk = (past + (_iota((ncol, LANES), 1) & (tq - 1))) >> SLC_SHIFT
        forced = (blk == 0) | (blk == qblk) | (blk == qblk - 1)
        sc = jnp.where(blk <= qblk, jnp.where(forced, sc + FORCE_BONUS, sc), NEG_INF)
        sc = jnp.where(blk < n_slc, sc, -jnp.inf)
        sel_t = jnp.zeros_like(sc)
        for _ in range(min(TOP_N, n_slc)):
            mx = jnp.max(sc, axis=0, keepdims=True)
            first = jnp.min(jnp.where(sc == mx, blkf, float(ncol)), axis=0, keepdims=True)
            hit = blkf == first
            sel_t = jnp.where(hit, 1.0, sel_t)
            sc = jnp.where(hit, -jnp.inf, sc)
        sel = sel_t.T
        return [[sel[(si * N_KV_HEADS + h) * tq:(si * N_KV_HEADS + h + 1) * tq] for h in range(N_KV_HEADS)]
                for si in range(n_seq)]

    def window_and_operands(si, qpb, sel):
        nsel = [jnp.concatenate([1.0 - sel[h][:, :LANES]] * GROUP, axis=0).astype(jnp.bfloat16) for h in range(N_KV_HEADS)]
        sel_new = [jnp.broadcast_to(sel[h][:, n_past_blk:n_past_blk + 1], (tq, LANES)) for h in range(N_KV_HEADS)]
        lhs = jnp.concatenate(
            [jnp.concatenate([qpb[h * GROUP * tq:(h + 1) * GROUP * tq], nsel[h]], axis=1)
             for h in range(N_KV_HEADS)], axis=0)
        sel_new_rows = jnp.concatenate([sel_new[h] for h in range(N_KV_HEADS) for _ in range(GROUP)], axis=0)

        tn = _iota((rows, LANES), 1)
        new_ok = (tn <= tok_of((rows, LANES))) & (tn < n_new)
        s_sn = jnp.where(new_ok & (sel_new_rows > 0.5), _dot_nt(qpb, new_rows(ksn_ref, si)), NEG_INF)
        s_wn = jnp.where(new_ok, _dot_nt(qpb, new_rows(kwn_ref, si)), NEG_INF)

        wk = wk_ref[si]
        wv = wv_ref[si]
        s_w = _dot(qpb, wk.astype(jnp.bfloat16))
        kwpos = past - wb + _iota(s_w.shape, 1)
        qpos = past + tok_of(s_w.shape)
        s_w = jnp.where((kwpos >= 0) & (kwpos <= qpos) & (kwpos > qpos - WINDOW), s_w, NEG_INF)
        mx = jnp.maximum(jnp.max(s_w, axis=1, keepdims=True), jnp.max(s_wn, axis=1, keepdims=True))
        e_w = jnp.exp(s_w - mx)
        e_wn = jnp.exp(s_wn - mx)
        den = jnp.sum(e_w, axis=1, keepdims=True) + jnp.sum(e_wn, axis=1, keepdims=True)
        o_win = (_dot_nt(e_w.astype(jnp.bfloat16), wv.astype(jnp.bfloat16))
                 + _dot(e_wn.astype(jnp.bfloat16), new_rows(vwn_ref, si))) / den

        col = _iota((KV_WIDTH, wb), 1)
        for src, new_ref, dst in ((wk, kwn_ref, nwk_ref), (wv, vwn_ref, nwv_ref)):
            new_t = jnp.concatenate([new_ref[si], jnp.zeros((LANES - tq, KV_WIDTH), jnp.float32)], axis=0).T
            tail = jnp.concatenate([jnp.zeros((KV_WIDTH, wb - LANES), jnp.float32), new_t], axis=1)
            dst[si] = jnp.where(col >= wb - n_new, pltpu.roll(tail, LANES - n_new, axis=1),
                                pltpu.roll(src, wb - n_new, axis=1))
        return lhs, s_sn, o_win

    def selected_and_combine(si, lhs, s_sn, o_cmp, o_win):
        ck_keys = min(SAMPLE_KEY_CHUNK, past)
        s_chunks = []
        for c in range(past // ck_keys):
            kc = buf[slot, si, 0, :, pl.ds(c * ck_keys, ck_keys)].astype(jnp.bfloat16)
            rhs = jnp.concatenate([kc, ebig_ref[:, pl.ds(c * ck_keys, ck_keys)]], axis=0)
            s_chunks.append(_dot(lhs, rhs))
        mx = jnp.max(s_sn, axis=1, keepdims=True)
        for sc in s_chunks:
            mx = jnp.maximum(mx, jnp.max(sc, axis=1, keepdims=True))
        e_sn = jnp.exp(s_sn - mx)
        den = jnp.sum(e_sn, axis=1, keepdims=True)
        o_sel = _dot(e_sn.astype(jnp.bfloat16), new_rows(vsn_ref, si))
        for c, sc in enumerate(s_chunks):
            e = jnp.exp(sc - mx)
            den = den + jnp.sum(e, axis=1, keepdims=True)
            vc = buf[slot, si, 1, :, pl.ds(c * ck_keys, ck_keys)].astype(jnp.bfloat16)
            o_sel = o_sel + _dot_nt(e.astype(jnp.bfloat16), vc)
        o_sel = o_sel / den

        o_ref[si] = _combine((o_cmp, o_sel, o_win), gn_ref[si], tq)

    seqs = range(n_seq)
    qpb, p_cmp, o_cmp = zip(*[compressed(si) for si in seqs])
    sel = select(p_cmp)
    staged = [window_and_operands(si, qpb[si], sel[si]) for si in seqs]
    wait_pages()
    for si in seqs:
        lhs, s_sn, o_win = staged[si]
        selected_and_combine(si, lhs, s_sn, o_cmp[si], o_win)


def nsa_sample_attention(q, gn, ck, cv, win_k, win_v, ks_new, vs_new, kw_new, vw_new,
                         pool_k, pool_v, page_table, *, tq=SUBLANES, n_seq=SAMPLE_SEQS_PER_STEP):
    DB, DS, _ = q.shape
    n_pages = page_table.shape[1]
    past = n_pages * PAGE_SIZE
    n_chunks = ck.shape[1]
    wb = win_k.shape[2]
    n_slc = -(-(past + DS) // SLC_BLK)
    assert DS <= tq and DS < CMP_STRIDE and past // SLC_BLK <= LANES and wb % LANES == 0
    assert past % min(SAMPLE_KEY_CHUNK, past) == 0 and DB % n_seq == 0 and n_seq * N_KV_HEADS * tq <= LANES
    ncol = -(-n_slc // LANES) * LANES
    pad_t = lambda t: jnp.pad(t, ((0, 0), (0, tq - DS), (0, 0)))
    kb = np.arange(past)[None, :] // SLC_BLK
    ebig = jnp.asarray(np.where(kb == np.arange(LANES)[:, None], NEG_INF, 0.0), jnp.bfloat16)
    wcst = jnp.asarray(cmp_to_slc_matrix(n_chunks, ncol).T, jnp.bfloat16)
    per_b = lambda b, pt: (b, 0, 0)
    const = lambda b, pt: (0, 0)
    blk3 = lambda n, w: pl.BlockSpec((n_seq, n, w), per_b)
    grid_spec = pltpu.PrefetchScalarGridSpec(
        num_scalar_prefetch=1,
        grid=(DB // n_seq,),
        in_specs=[
            blk3(tq, NSA_WIDTH), blk3(tq, LANES), blk3(n_chunks, KV_WIDTH), blk3(n_chunks, KV_WIDTH),
            blk3(KV_WIDTH, wb), blk3(KV_WIDTH, wb),
            blk3(tq, KV_WIDTH), blk3(tq, KV_WIDTH), blk3(tq, KV_WIDTH), blk3(tq, KV_WIDTH),
            pl.BlockSpec(memory_space=pl.ANY), pl.BlockSpec(memory_space=pl.ANY),
            pl.BlockSpec(ebig.shape, const, pipeline_mode=pl.Buffered(1)),
            pl.BlockSpec(wcst.shape, const, pipeline_mode=pl.Buffered(1)),
        ],
        out_specs=[blk3(tq, NSA_WIDTH), blk3(KV_WIDTH, wb), blk3(KV_WIDTH, wb)],
        scratch_shapes=[
            pltpu.VMEM((2, n_seq, 2, KV_WIDTH, past), jnp.float32),
            pltpu.SemaphoreType.DMA((2, 2)),
        ],
    )
    o, nwk, nwv = pl.pallas_call(
        functools.partial(_nsa_sample_kernel, n_pages=n_pages, n_new=DS, tq=tq, n_seq=n_seq),
        grid_spec=grid_spec,
        out_shape=[jax.ShapeDtypeStruct((DB, tq, NSA_WIDTH), jnp.float32),
                   jax.ShapeDtypeStruct((DB, KV_WIDTH, wb), jnp.float32),
                   jax.ShapeDtypeStruct((DB, KV_WIDTH, wb), jnp.float32)],
        compiler_params=pltpu.CompilerParams(
            dimension_semantics=("arbitrary",), vmem_limit_bytes=PAGED_VMEM_LIMIT_BYTES),
        name="nsa_sample",
    )(page_table, pad_t(q), pad_t(gn), ck, cv, win_k, win_v, pad_t(ks_new), pad_t(vs_new), pad_t(kw_new),
      pad_t(vw_new), pool_k, pool_v, ebig, wcst)
    return o[:, :DS], nwk, nwv


def rope_tables(pos):
    half = HEAD_DIM // 2
    inv_freq = ROPE_THETA ** (-jnp.arange(half, dtype=jnp.float32) / half)
    ang = pos.astype(jnp.float32)[:, None] * inv_freq[None, :]
    cos, sin = jnp.cos(ang), jnp.sin(ang)
    cos_t = jnp.tile(cos, (1, LANES // half))
    sin_t = jnp.tile(jnp.concatenate([-sin, sin], axis=1), (1, LANES // HEAD_DIM))
    return cos_t, sin_t


def _rope_lanes(y, cos, sin):
    first_half = (_iota((y.shape[0], LANES), 1) & (HEAD_DIM - 1)) < HEAD_DIM // 2
    cols = []
    for c in range(y.shape[1] // LANES):
        yc = y[:, c * LANES:(c + 1) * LANES]
        rot = jnp.where(first_half, pltpu.roll(yc, LANES - HEAD_DIM // 2, axis=1),
                        pltpu.roll(yc, HEAD_DIM // 2, axis=1))
        cols.append(yc * cos + rot * sin)
    return cols[0] if len(cols) == 1 else jnp.concatenate(cols, axis=1)


def _rope_sublanes(y, cos_t, sin_t):
    hh = HEAD_DIM // 2
    out = []
    for c in range(y.shape[0] // LANES):
        yc = y[c * LANES:(c + 1) * LANES]
        rot = jnp.concatenate([yc[hh:2 * hh], yc[:hh], yc[3 * hh:], yc[2 * hh:3 * hh]], axis=0)
        out.append(yc * cos_t + rot * sin_t)
    return out[0] if len(out) == 1 else jnp.concatenate(out, axis=0)


_REST_ACTS = {"u": jax.nn.gelu, "v": jax.nn.gelu, "gn": jax.nn.sigmoid}


def _proj_rest(h, wrest_ref, refs, names):
    c0 = 0
    for ref, name in zip(refs, names):
        width = ref.shape[-1]
        act = _REST_ACTS.get(name)
        for c in range(0, width, GMLP_WIDTH):
            w = min(GMLP_WIDTH, width - c)
            y = _dot(h, wrest_ref[:, c0 + c:c0 + c + w])
            ref[:, c:c + w] = y if act is None else act(y)
        c0 += width


def _proj_rows_kernel(x_ref, g_ref, wq_ref, wkv_ref, wrest_ref, cos_ref, sin_ref, q_ref, *out_refs):
    h = _rms(x_ref[...], g_ref[...]).astype(jnp.bfloat16)
    cos, sin = cos_ref[...], sin_ref[...]
    q_ref[...] = _rope_lanes(_dot(h, wq_ref[...]), cos, sin)
    y = _dot(h, wkv_ref[...])
    for gi in range(N_KV_GROUPS):
        blk = y[:, gi * KV_WIDTH:(gi + 1) * KV_WIDTH]
        out_refs[gi][...] = _rope_lanes(blk, cos, sin) if gi % 2 == 0 else blk
    _proj_rest(h, wrest_ref, out_refs[N_KV_GROUPS:], ("gn", "u", "v", "ga", "gb"))


def _proj_seq_kernel(x_ref, g_ref, wq_ref, wkvt_ref, wrest_ref, cos_ref, sin_ref, cost_ref, sint_ref,
                     q_ref, kct_ref, vct_ref, kst_ref, vst_ref, kwt_ref, vwt_ref,
                     kst_b_ref, vst_b_ref, kwt_b_ref, vwt_b_ref, gn_ref, u_ref, v_ref, ga_ref, gb_ref):
    h = _rms(x_ref[...], g_ref[...]).astype(jnp.bfloat16)
    q_ref[...] = _rope_lanes(_dot(h, wq_ref[...]), cos_ref[...], sin_ref[...]) * (ATTN_SCALE * LOG2E)
    cos_t, sin_t = cost_ref[...], sint_ref[...]
    yt = _dot_nt(wkvt_ref[...], h)
    kv_refs = (kct_ref, vct_ref, kst_ref, vst_ref, kwt_ref, vwt_ref)
    kvb_refs = (None, None, kst_b_ref, vst_b_ref, kwt_b_ref, vwt_b_ref)
    for gi in range(N_KV_GROUPS):
        blk = yt[gi * KV_WIDTH:(gi + 1) * KV_WIDTH]
        if gi % 2 == 0:
            blk = _rope_sublanes(blk, cos_t, sin_t)
        kv_refs[gi][0] = blk
        if kvb_refs[gi] is not None:
            kvb_refs[gi][0] = blk.astype(jnp.bfloat16)
    _proj_rest(h, wrest_ref, (gn_ref, u_ref, v_ref, ga_ref, gb_ref), ("gn", "u", "v", "ga", "gb"))


def _split_w_in(w_in):
    sizes = (NSA_WIDTH,) + (KV_WIDTH,) * N_KV_GROUPS + (N_HEADS * N_BRANCH,)
    offs = np.cumsum((0,) + sizes)
    wq = w_in[:, offs[0]:offs[1]]
    wkv = w_in[:, offs[1]:offs[7]]
    wgn = w_in[:, offs[7]:offs[8]]
    wrest = w_in[:, offs[8]:]
    return wq, wkv, wgn, wrest


def mixer_proj_rows(x, g, w_in, pos):
    M = x.shape[0]
    tm = min(ROW_TILE, M)
    assert M % tm == 0
    wq, wkv, wgn, wrest = _split_w_in(w_in)
    wrest = jnp.concatenate([jnp.pad(wgn, ((0, 0), (0, LANES - wgn.shape[1]))), wrest], axis=1)
    cos, sin = rope_tables(pos)
    b16 = lambda t: t.astype(jnp.bfloat16)
    row = lambda i: (i, 0)
    cw = lambda shape: pl.BlockSpec(shape, lambda i: (0, 0), pipeline_mode=pl.Buffered(1))
    widths = (NSA_WIDTH,) + (KV_WIDTH,) * N_KV_GROUPS + (LANES, GMLP_WIDTH, GMLP_WIDTH, D_MODEL, D_MODEL)
    return pl.pallas_call(
        _proj_rows_kernel,
        grid=(M // tm,),
        in_specs=[pl.BlockSpec((tm, D_MODEL), row), cw((1, D_MODEL)), cw(wq.shape), cw(wkv.shape),
                  cw(wrest.shape), pl.BlockSpec((tm, LANES), row), pl.BlockSpec((tm, LANES), row)],
        out_specs=[pl.BlockSpec((tm, w), row) for w in widths],
        out_shape=[jax.ShapeDtypeStruct((M, w), jnp.float32) for w in widths],
        compiler_params=pltpu.CompilerParams(
            dimension_semantics=("arbitrary",), vmem_limit_bytes=VMEM_LIMIT_BYTES),
        name="mixer_proj_rows",
    )(x, g.reshape(1, -1), b16(wq), b16(wkv), b16(wrest), cos, sin)


def mixer_proj_seq(x, g, w_in, batch):
    M = x.shape[0]
    S = M // batch
    tm = min(ROW_TILE, S)
    assert S % tm == 0
    n_s = S // tm
    wq, wkv, wgn, wrest = _split_w_in(w_in)
    wrest = jnp.concatenate([jnp.pad(wgn, ((0, 0), (0, LANES - wgn.shape[1]))), wrest], axis=1)
    cos, sin = rope_tables(jnp.arange(S, dtype=jnp.int32))
    b16 = lambda t: t.astype(jnp.bfloat16)
    f32 = jnp.float32
    row = lambda i: (i, 0)
    tab = lambda i: (i % n_s, 0)
    tab_t = lambda i: (0, i % n_s)
    seq_t = lambda i: (i // n_s, 0, i % n_s)
    cw = lambda shape: pl.BlockSpec(shape, lambda i: (0, 0), pipeline_mode=pl.Buffered(1))
    t_spec = pl.BlockSpec((1, KV_WIDTH, tm), seq_t)
    t_shape = lambda dt: jax.ShapeDtypeStruct((batch, KV_WIDTH, S), dt)
    r_widths = (LANES, GMLP_WIDTH, GMLP_WIDTH, D_MODEL, D_MODEL)
    out_specs = ([pl.BlockSpec((tm, NSA_WIDTH), row)] + [t_spec] * (N_KV_GROUPS + 4)
                 + [pl.BlockSpec((tm, w), row) for w in r_widths])
    out_shape = ([jax.ShapeDtypeStruct((M, NSA_WIDTH), f32)] + [t_shape(f32)] * N_KV_GROUPS
                 + [t_shape(jnp.bfloat16)] * 4 + [jax.ShapeDtypeStruct((M, w), f32) for w in r_widths])
    return pl.pallas_call(
        _proj_seq_kernel,
        grid=(M // tm,),
        in_specs=[pl.BlockSpec((tm, D_MODEL), row), cw((1, D_MODEL)), cw(wq.shape),
                  cw((N_KV_GROUPS * KV_WIDTH, D_MODEL)), cw(wrest.shape),
                  pl.BlockSpec((tm, LANES), tab), pl.BlockSpec((tm, LANES), tab),
                  pl.BlockSpec((LANES, tm), tab_t), pl.BlockSpec((LANES, tm), tab_t)],
        out_specs=out_specs,
        out_shape=out_shape,
        compiler_params=pltpu.CompilerParams(
            dimension_semantics=("arbitrary",), vmem_limit_bytes=VMEM_LIMIT_BYTES),
        name="mixer_proj_seq",
    )(x, g.reshape(1, -1), b16(wq), b16(wkv.T), b16(wrest), cos, sin, cos.T, sin.T)


def _merge_kernel(oa_ref, u_ref, v_ref, ga_ref, gb_ref, x_ref, lng_ref, lnb_ref, wsp_ref, bsx_ref,
                  wa_ref, wb_ref, wout_ref, gc_ref, wcq_ref, x1_ref, hq_ref):
    tm = x_ref.shape[0]
    v = v_ref[...]
    vc = v - jnp.mean(v, axis=-1, keepdims=True)
    var = jnp.mean(vc * vc, axis=-1, keepdims=True)
    vn = vc * lax.rsqrt(var + NORM_EPS) * lng_ref[...] + lnb_ref[...]
    lane = _iota((GMLP_CHUNK, LANES), 1)
    chunks = []
    for c in range(tm // GMLP_CHUNK):
        cols = []
        for pr in range(GMLP_GROUPS // 2):
            pair = vn[c * GMLP_CHUNK:(c + 1) * GMLP_CHUNK, pr * LANES:(pr + 1) * LANES]
            rhs = jnp.concatenate([jnp.where(lane < GMLP_GROUP_DIM, pair, 0.0),
                                   jnp.where(lane >= GMLP_GROUP_DIM, pair, 0.0)], axis=0).astype(jnp.bfloat16)
            cols.append(_dot(wsp_ref[pr], rhs))
        chunks.append(jnp.concatenate(cols, axis=1) + bsx_ref[...])
    o_b = u_ref[...] * jnp.concatenate(chunks, axis=0)
    m = (jax.nn.sigmoid(ga_ref[...]) * _dot(oa_ref[...].astype(jnp.bfloat16), wa_ref[...])
         + jax.nn.sigmoid(gb_ref[...]) * _dot(o_b.astype(jnp.bfloat16), wb_ref[...]))
    x1 = x_ref[...] + _dot(m.astype(jnp.bfloat16), wout_ref[...])
    x1_ref[...] = x1
    hq_ref[...] = _dot(_rms(x1, gc_ref[...]).astype(jnp.bfloat16), wcq_ref[...])


def gmlp_weights(ws, bs, chunk_len):
    L = chunk_len
    w = jnp.tril(ws[:, :L, :L])
    bias = bs[:, :L]
    if L < GMLP_CHUNK:
        r = np.arange(GMLP_CHUNK)
        same_seq = (r[:, None] // L) == (r[None, :] // L)
        w = jnp.where(same_seq, w[:, r % L][:, :, r % L], 0.0)
        bias = bias[:, r % L]
    pairs = w.reshape(GMLP_GROUPS // 2, 2, GMLP_CHUNK, GMLP_CHUNK)
    wsp = jnp.swapaxes(pairs, 1, 2).reshape(GMLP_GROUPS // 2, GMLP_CHUNK, 2 * GMLP_CHUNK)
    bsx = jnp.repeat(bias.T, GMLP_GROUP_DIM, axis=1)
    return wsp.astype(jnp.bfloat16), bsx


def merge_and_cross_q(o_a, u, v, ga, gb, x, ln_g, ln_b, wsp, bsx, w_a, w_b, w_out, g_cross, w_cq):
    M = x.shape[0]
    tm = min(ROW_TILE, M)
    one = pl.Buffered(1)
    row = lambda i: (i, 0)
    cw = lambda shape: pl.BlockSpec(shape, lambda i: (0,) * len(shape), pipeline_mode=one)
    b16 = lambda t: t.astype(jnp.bfloat16)
    out = jax.ShapeDtypeStruct((M, D_MODEL), jnp.float32)
    return pl.pallas_call(
        _merge_kernel,
        grid=(M // tm,),
        in_specs=[
            pl.BlockSpec((tm, NSA_WIDTH), row), pl.BlockSpec((tm, GMLP_WIDTH), row),
            pl.BlockSpec((tm, GMLP_WIDTH), row), pl.BlockSpec((tm, D_MODEL), row),
            pl.BlockSpec((tm, D_MODEL), row), pl.BlockSpec((tm, D_MODEL), row),
            cw((1, GMLP_WIDTH)), cw((1, GMLP_WIDTH)), cw(wsp.shape), cw(bsx.shape),
            cw(w_a.shape), cw(w_b.shape), cw(w_out.shape), cw((1, D_MODEL)), cw(w_cq.shape),
        ],
        out_specs=[pl.BlockSpec((tm, D_MODEL), row)] * 2,
        out_shape=[out, out],
        compiler_params=pltpu.CompilerParams(
            dimension_semantics=("arbitrary",), vmem_limit_bytes=VMEM_LIMIT_BYTES),
        name="merge_branches",
    )(o_a, u, v, ga, gb, x, ln_g.reshape(1, -1), ln_b.reshape(1, -1), wsp, bsx,
      b16(w_a), b16(w_b), b16(w_out), g_cross.reshape(1, -1), b16(w_cq))


def _memkv_kernel(x_ref, g_ref, wk_ref, wv_ref, k_ref, v_ref):
    h = _rms(x_ref[...], g_ref[...]).astype(jnp.bfloat16)
    k_ref[...] = _dot(h, wk_ref[...])
    v_ref[...] = _dot(h, wv_ref[...])


def memory_kv(mem, g_mem, w_ck, w_cv):
    M = mem.shape[0]
    tm = min(ROW_TILE, M)
    one = pl.Buffered(1)
    row = lambda i: (i, 0)
    cw = lambda shape: pl.BlockSpec(shape, lambda i: (0, 0), pipeline_mode=one)
    out = jax.ShapeDtypeStruct((M, D_MODEL), jnp.float32)
    return pl.pallas_call(
        _memkv_kernel,
        grid=(M // tm,),
        in_specs=[pl.BlockSpec((tm, D_MODEL), row), cw((1, D_MODEL)), cw(w_ck.shape), cw(w_cv.shape)],
        out_specs=[pl.BlockSpec((tm, D_MODEL), row)] * 2,
        out_shape=[out, out],
        compiler_params=pltpu.CompilerParams(
            dimension_semantics=("arbitrary",), vmem_limit_bytes=VMEM_LIMIT_BYTES),
        name="memory_kv",
    )(mem, g_mem.reshape(1, -1), w_ck.astype(jnp.bfloat16), w_cv.astype(jnp.bfloat16))


def _cross_heads_kernel(q_ref, mk_hbm, mv_hbm, o_ref, buf, sem, *, rows, n_seq):
    b = pl.program_id(0)
    nb = pl.num_programs(0)
    slot = b % 2

    def head_copies(bb, sl):
        return [pltpu.make_async_copy(src.at[bb * n_seq + si, :, h, :], buf.at[sl, si, kv, h], sem.at[sl, kv])
                for si in range(n_seq) for kv, src in enumerate((mk_hbm, mv_hbm)) for h in range(CROSS_HEADS)]

    @pl.when(b == 0)
    def _():
        for c in head_copies(0, 0):
            c.start()

    @pl.when(b + 1 < nb)
    def _():
        for c in head_copies(b + 1, 1 - slot):
            c.start()

    pad = jnp.zeros((SUBLANES - rows, D_MODEL), jnp.float32)
    qs = [jnp.concatenate([q_ref[si] * CROSS_SCALE, pad], axis=0) for si in range(n_seq)]
    for c in head_copies(b, slot):
        c.wait()
    for si in range(n_seq):
        qh = [qs[si][:, h * CROSS_HEAD_DIM:(h + 1) * CROSS_HEAD_DIM].astype(jnp.bfloat16) for h in range(CROSS_HEADS)]
        s = jnp.concatenate([_dot_nt(qh[h], buf[slot, si, 0, h].astype(jnp.bfloat16)) for h in range(CROSS_HEADS)], axis=0)
        e = jnp.exp(s - jnp.max(s, axis=1, keepdims=True))
        p = (e / jnp.sum(e, axis=1, keepdims=True)).astype(jnp.bfloat16)
        o = jnp.concatenate([_dot(p[h * SUBLANES:(h + 1) * SUBLANES], buf[slot, si, 1, h].astype(jnp.bfloat16))
                             for h in range(CROSS_HEADS)], axis=1)
        o_ref[si] = o[:rows]


def cross_attention_cached(hq, mk, mv, *, n_seq=CROSS_SEQS_PER_STEP):
    NB, R, _ = hq.shape
    assert NB % n_seq == 0
    any_spec = pl.BlockSpec(memory_space=pl.ANY)
    return pl.pallas_call(
        functools.partial(_cross_heads_kernel, rows=R, n_seq=n_seq),
        grid=(NB // n_seq,),
        in_specs=[pl.BlockSpec((n_seq, R, D_MODEL), lambda b: (b, 0, 0)), any_spec, any_spec],
        out_specs=pl.BlockSpec((n_seq, R, D_MODEL), lambda b: (b, 0, 0)),
        out_shape=jax.ShapeDtypeStruct((NB, R, D_MODEL), jnp.float32),
        scratch_shapes=[pltpu.VMEM((2, n_seq, 2, CROSS_HEADS, N_MEM, CROSS_HEAD_DIM), jnp.float32),
                        pltpu.SemaphoreType.DMA((2, 2))],
        compiler_params=pltpu.CompilerParams(
            dimension_semantics=("arbitrary",), vmem_limit_bytes=VMEM_LIMIT_BYTES),
        name="cross_attention_cached",
    )(hq, mk, mv)


def _ffn_and_norm(x, h_gain, wg_ref, wu_ref, wd_ref, gfin_ref, y_ref, acc_ref):
    h = _rms(x, h_gain).astype(jnp.bfloat16)
    acc_ref[...] = x
    for c in range(D_FF // FF_CHUNK):
        sl = slice(c * FF_CHUNK, (c + 1) * FF_CHUNK)
        a = _dot(h, wg_ref[:, sl])
        b = _dot(h, wu_ref[:, sl])
        t = (a * jax.nn.sigmoid(a) * b).astype(jnp.bfloat16)
        acc_ref[...] += _dot(t, wd_ref[sl, :])
    y_ref[...] = _rms(acc_ref[...], gfin_ref[...])


def _memory_attention(q, mk, mv):
    q = q * CROSS_SCALE
    outs = []
    for h in range(CROSS_HEADS):
        sl = slice(h * CROSS_HEAD_DIM, (h + 1) * CROSS_HEAD_DIM)
        s = _dot_nt(q[:, sl].astype(jnp.bfloat16), mk[:, sl].astype(jnp.bfloat16))
        e = jnp.exp(s - jnp.max(s, axis=1, keepdims=True))
        p = e / jnp.sum(e, axis=1, keepdims=True)
        outs.append(_dot(p.astype(jnp.bfloat16), mv[:, sl].astype(jnp.bfloat16)))
    return jnp.concatenate(outs, axis=1)


def _tail_kernel(x_ref, o_ref, wco_ref, gf_ref, wg_ref, wu_ref, wd_ref, gfin_ref, y_ref, acc_ref):
    x = x_ref[...] + _dot(o_ref[...].astype(jnp.bfloat16), wco_ref[...])
    _ffn_and_norm(x, gf_ref[...], wg_ref, wu_ref, wd_ref, gfin_ref, y_ref, acc_ref)


def _tail_cross_kernel(x_ref, hq_ref, mk_ref, mv_ref, wco_ref, gf_ref, wg_ref, wu_ref, wd_ref, gfin_ref,
                       y_ref, acc_ref):
    o = _memory_attention(hq_ref[...], mk_ref[0], mv_ref[0])
    x = x_ref[...] + _dot(o.astype(jnp.bfloat16), wco_ref[...])
    _ffn_and_norm(x, gf_ref[...], wg_ref, wu_ref, wd_ref, gfin_ref, y_ref, acc_ref)


def _tail_call(kernel_fn, name, x1, row_inputs, batch_inputs, rows_per_batch, w_co, g_ffn, w_gate, w_up, w_down, g_final):
    M = x1.shape[0]
    tm = min(ROW_TILE, M)
    row = lambda i: (i, 0)
    cw = lambda shape: pl.BlockSpec(shape, lambda i: (0, 0), pipeline_mode=pl.Buffered(1))
    b16 = lambda t: t.astype(jnp.bfloat16)
    per_batch = lambda i: (i * tm // rows_per_batch, 0, 0)
    return pl.pallas_call(
        kernel_fn,
        grid=(M // tm,),
        in_specs=[pl.BlockSpec((tm, D_MODEL), row)] * (1 + len(row_inputs))
        + [pl.BlockSpec((1,) + t.shape[1:], per_batch) for t in batch_inputs]
        + [cw(w_co.shape), cw((1, D_MODEL)), cw(w_gate.shape), cw(w_up.shape), cw(w_down.shape), cw((1, D_MODEL))],
        out_specs=pl.BlockSpec((tm, D_MODEL), row),
        out_shape=jax.ShapeDtypeStruct((M, D_MODEL), jnp.float32),
        scratch_shapes=[pltpu.VMEM((tm, D_MODEL), jnp.float32)],
        compiler_params=pltpu.CompilerParams(
            dimension_semantics=("arbitrary",), vmem_limit_bytes=VMEM_LIMIT_BYTES),
        name=name,
    )(x1, *row_inputs, *batch_inputs, b16(w_co), g_ffn.reshape(1, -1), b16(w_gate), b16(w_up), b16(w_down),
      g_final.reshape(1, -1))


def tail(x1, o, *weights):
    return _tail_call(_tail_kernel, "tail", x1, (o,), (), x1.shape[0], *weights)


def tail_with_memory(x1, hq, mk, mv, *weights):
    rows_per_batch = x1.shape[0] // mk.shape[0]
    assert rows_per_batch % min(ROW_TILE, x1.shape[0]) == 0
    return _tail_call(_tail_cross_kernel, "tail_with_memory", x1, (hq,), (mk, mv), rows_per_batch, *weights)


def kernel(x_prompt, x_sample, mem_prompt, cache_cmp_k, cache_cmp_v, cache_slc_k, cache_slc_v,
           page_table, state_win_k, state_win_v, cache_mem_k, cache_mem_v,
           g_mix, w_in, cmp_pe, cmp_w1, cmp_w2, gmlp_ln_g, gmlp_ln_b, gmlp_ws, gmlp_bs,
           w_branch_a, w_branch_b, w_out, g_cross, g_mem, w_cq, w_ck, w_cv, w_co,
           g_ffn, w_gate, w_up, w_down, g_final):
    B, S, _ = x_prompt.shape
    DB, DS, _ = x_sample.shape
    n_pages = page_table.shape[1]
    assert g_mix.shape[0] == 1, "single-layer step"
    l = 0
    cmpw = cmp_weights(cmp_pe[l], cmp_w1[l], cmp_w2[l])
    merge_w = (gmlp_ln_g[l], gmlp_ln_b[l])
    merge_tail = (w_branch_a[l], w_branch_b[l], w_out[l], g_cross[l], w_cq[l])
    ffn_w = (w_co[l], g_ffn[l], w_gate[l], w_up[l], w_down[l], g_final)
    to_t = lambda t: jnp.transpose(t, (0, 2, 3, 1)).reshape(t.shape[0], KV_WIDTH, t.shape[1])
    from_t = lambda t: jnp.transpose(t.reshape(t.shape[0], N_KV_HEADS, HEAD_DIM, t.shape[2]), (0, 3, 1, 2))
    mem3 = lambda t: t.reshape(-1, N_MEM, D_MODEL)

    xp = x_prompt.reshape(B * S, D_MODEL)
    (q, kct, vct, kst, vst, kwt, vwt, kst_b, vst_b, kwt_b, vwt_b, gn, u, v, ga, gb) = mixer_proj_seq(
        xp, g_mix[l], w_in[l], B)
    ck, cv = cmp_blocks_dense(kct, vct, *cmpw)
    o_a = nsa_prompt_attention(q.reshape(B, S, NSA_WIDTH), gn.reshape(B, S, LANES), ck, cv,
                               kst_b, vst_b, kwt_b, vwt_b)
    x1, hq = merge_and_cross_q(o_a.reshape(B * S, NSA_WIDTH), u, v, ga, gb, xp, *merge_w,
                               *gmlp_weights(gmlp_ws[l], gmlp_bs[l], min(S, GMLP_CHUNK)), *merge_tail)
    mk, mv = memory_kv(mem_prompt.reshape(B * N_MEM, D_MODEL), g_mem[l], w_ck[l], w_cv[l])
    y_prompt = tail_with_memory(x1, hq, mem3(mk), mem3(mv), *ffn_w).reshape(B, S, D_MODEL)
    wb_p = min(WINDOW, S)
    mem5 = lambda t: t.reshape(B, N_MEM, CROSS_HEADS, CROSS_HEAD_DIM)
    prompt_new = (from_t(kct), from_t(vct), from_t(kst), from_t(vst),
                  from_t(kwt[:, :, S - wb_p:]), from_t(vwt[:, :, S - wb_p:]), mem5(mk), mem5(mv))

    xs = x_sample.reshape(DB * DS, D_MODEL)
    pos_s = n_pages * PAGE_SIZE + jnp.arange(DS, dtype=jnp.int32)
    (q, kc, vc, ks, vs, kw, vw, gn, u, v, ga, gb) = mixer_proj_rows(xs, g_mix[l], w_in[l], jnp.tile(pos_s, DB))
    ck, cv = cmp_blocks_paged(to_t(cache_cmp_k[l]), to_t(cache_cmp_v[l]), page_table, *cmpw)
    per_seq = lambda t: t.reshape(DB, DS, -1)
    o_a, win_kt, win_vt = nsa_sample_attention(
        per_seq(q), per_seq(gn), ck, cv, to_t(state_win_k[l]), to_t(state_win_v[l]),
        per_seq(ks), per_seq(vs), per_seq(kw), per_seq(vw),
        to_t(cache_slc_k[l]), to_t(cache_slc_v[l]), page_table)
    x1, hq = merge_and_cross_q(o_a.reshape(DB * DS, NSA_WIDTH), u, v, ga, gb, xs, *merge_w,
                               *gmlp_weights(gmlp_ws[l], gmlp_bs[l], min(DS, GMLP_CHUNK)), *merge_tail)
    o = cross_attention_cached(per_seq(hq), cache_mem_k[l], cache_mem_v[l])
    y_sample = tail(x1, o.reshape(DB * DS, D_MODEL), *ffn_w).reshape(DB, DS, D_MODEL)
    heads = lambda t: t.reshape(DB, DS, N_KV_HEADS, HEAD_DIM)
    sample_new = (heads(kc), heads(vc), heads(ks), heads(vs), from_t(win_kt), from_t(win_vt), per_seq(v))

    return (y_prompt, y_sample) + tuple(t[None] for t in prompt_new + sample_new)
```
